```python
import math
import jax
import jax.numpy as jnp
from jax import lax
import numpy as np

D_MODEL = 2048
BATCH = 2
SEQ = 4096
DEPTH = 4

CTX_LEN = 256
GRID_W = 64
EPS = 1e-6
N_DIR = 2

GLA_HEADS = 4
GLA_DK = 128
GLA_DV = 256
GLA_KEY = GLA_HEADS * GLA_DK
GLA_WIDTH = GLA_HEADS * GLA_DV
GLA_RANK = 16
GLA_TAU = 16.0
GLA_CHUNK = 64

S5_WIDTH = 512
S5_GROUP = 16
S5_GROUPS = S5_WIDTH // S5_GROUP
S5_STATE = 64

HY_WIDTH = 512
HY_ORDER = 2
HY_EMB = 33
HY_BANDS = (HY_EMB - 1) // 2
HY_FFN = 64
HY_SHORT = 3
HY_MIN_DECAY = math.log(1e-2) / 0.3
HY_MAX_DECAY = math.log(1e-2) / 1.5

MIX_WIDTH = GLA_WIDTH + S5_WIDTH + HY_WIDTH
IN_SPLITS = (GLA_KEY, GLA_KEY, GLA_WIDTH, N_DIR * GLA_RANK, S5_WIDTH,
             GLA_WIDTH, S5_WIDTH, 3 * HY_WIDTH, HY_WIDTH)
N_STATE_SPLITS = 5
IN_WIDTH = sum(IN_SPLITS)
STATE_WIDTH = sum(IN_SPLITS[:N_STATE_SPLITS])

kernel_name = "hybrid_gla_s5_hyena_prefix_trunk"


def rmsnorm(x, g):
    xf = x.astype(jnp.float32)
    y = xf * lax.rsqrt(jnp.mean(xf * xf, axis=-1, keepdims=True) + EPS)
    return (y * g.astype(jnp.float32)).astype(x.dtype)


def split_cols(p, sizes):
    idx = np.cumsum(sizes)[:-1].tolist()
    return jnp.split(p, idx, axis=-1)


def grid_transpose(h, a, b):
    return h.reshape(h.shape[0], a, b, h.shape[-1]).transpose(0, 2, 1, 3).reshape(h.shape)


def gla_chunked(q, k, v, g, s0, with_out):
    bn, L, H, dk = q.shape
    n = L // GLA_CHUNK
    C = GLA_CHUNK
    rs = lambda t: t.reshape(bn, n, C, H, t.shape[-1])
    q, k, v, g = rs(q), rs(k), rs(v), rs(g)
    G = jnp.cumsum(g, axis=2)
    G_last = G[:, :, -1:]
    dS = jnp.einsum('bnshk,bnshv->bnhkv', k * jnp.exp(G_last - G), v)
    a = jnp.exp(G_last[:, :, 0])

    def step(S, inp):
        a_n, dS_n = inp
        return a_n[..., None] * S + dS_n, (S if with_out else None)

    S_fin, S_prev = lax.scan(step, s0, (jnp.moveaxis(a, 1, 0), jnp.moveaxis(dS, 1, 0)))
    if not with_out:
        return None, S_fin
    S_prev = jnp.moveaxis(S_prev, 0, 1)
    qg = q * jnp.exp(G)
    att = jnp.einsum('bnthk,bnshk->bnhts', qg, k * jnp.exp(-G))
    att = jnp.where(jnp.tril(jnp.ones((C, C), dtype=bool)), att, 0.0)
    o = jnp.einsum('bnhts,bnshv->bnthv', att, v) + jnp.einsum('bnthk,bnhkv->bnthv', qg, S_prev)
    return o.reshape(bn, L, H, v.shape[-1]), S_fin


def gla_branch(pc, pl, w_gate, b_gate, norm_g, ctx_out):
    f32 = jnp.float32

    def prep(q, k, v, lr):
        bn, L, _ = q.shape
        qh = q.astype(f32).reshape(bn, L, GLA_HEADS, GLA_DK) * (GLA_DK ** -0.5)
        kh = k.astype(f32).reshape(bn, L, GLA_HEADS, GLA_DK)
        vh = v.astype(f32).reshape(bn, L, GLA_HEADS, GLA_DV)
        lr = lr.astype(f32).reshape(bn, L, N_DIR, GLA_RANK)
        z = jnp.einsum('bldr,drk->bldk', lr, w_gate.astype(f32)) + b_gate.astype(f32)
        glog = (jax.nn.log_sigmoid(z) / GLA_TAU).reshape(bn, L, N_DIR, GLA_HEADS, GLA_DK)
        return qh, kh, vh, glog

    flip = lambda t: jnp.flip(t, axis=1)
    qc, kc, vc, gc = prep(*pc)
    ql, kl, vl, gl = prep(*pl)
    zero = jnp.zeros((qc.shape[0], GLA_HEADS, GLA_DK, GLA_DV), f32)
    oc_f, s_f = gla_chunked(qc, kc, vc, gc[:, :, 0], zero, ctx_out)
    oc_b, s_b = gla_chunked(flip(qc), flip(kc), flip(vc), flip(gc[:, :, 1]), zero, ctx_out)
    ol_f, _ = gla_chunked(ql, kl, vl, gl[:, :, 0], s_f, True)
    ol_b, _ = gla_chunked(flip(ql), flip(kl), flip(vl), flip(gl[:, :, 1]), s_b, True)
    yl = ol_f + flip(ol_b) - jnp.sum(ql * kl, -1, keepdims=True) * vl
    out_l = rmsnorm(yl, norm_g).reshape(yl.shape[0], yl.shape[1], GLA_WIDTH)
    if not ctx_out:
        return out_l, None
    yc = oc_f + flip(oc_b) - jnp.sum(qc * kc, -1, keepdims=True) * vc
    out_c = rmsnorm(yc, norm_g).reshape(yc.shape[0], yc.shape[1], GLA_WIDTH)
    return out_l, out_c


def _lin_op(e1, e2):
    a1, b1 = e1
    a2, b2 = e2
    return a1 * a2, a2 * b1 + b2


def s5_scan(bu, lam_bar, s0, reverse):
    if s0 is not None:
        bu = bu.at[:, -1 if reverse else 0].add(lam_bar * s0)
    a = jnp.broadcast_to(lam_bar, bu.shape)
    _, xs = lax.associative_scan(_lin_op, (a, bu), reverse=reverse, axis=1)
    return xs


def s5_branch(uc, ul, lam_re, lam_im, log_step, b_re, b_im, c_re, c_im, d_skip, w_glu, b_glu, ctx_out):
    f32 = jnp.float32
    lam = lax.complex(lam_re.astype(f32), lam_im.astype(f32))
    lam_bar = jnp.exp(lam * jnp.exp(log_step.astype(f32))[..., None])
    b_bar = ((lam_bar - 1.0) / lam)[..., None] * lax.complex(b_re.astype(f32), b_im.astype(f32))
    cmat = lax.complex(c_re.astype(f32), c_im.astype(f32))
    d = d_skip.astype(f32)
    wg = w_glu.astype(f32)
    bg = b_glu.astype(f32)

    def grouped(u):
        u = u.astype(f32)
        return u, u.reshape(u.shape[0], u.shape[1], S5_GROUPS, S5_GROUP)

    def readout(xs, cm):
        y = jnp.einsum('blgp,ghp->blgh', xs, cm).real
        return y.reshape(y.shape[0], y.shape[1], S5_WIDTH)

    def glu(y):
        yg = jax.nn.gelu(y)
        return yg * jax.nn.sigmoid(yg @ wg + bg)

    uc, ucg = grouped(uc)
    ul, ulg = grouped(ul)
    yl = d * ul
    yc = d * uc if ctx_out else None
    for di, rev in enumerate((False, True)):
        xs_c = s5_scan(jnp.einsum('blgh,gph->blgp', ucg, b_bar[di]), lam_bar[di], None, rev)
        s_c = xs_c[:, 0] if rev else xs_c[:, -1]
        xs_l = s5_scan(jnp.einsum('blgh,gph->blgp', ulg, b_bar[di]), lam_bar[di], s_c, rev)
        yl = yl + readout(xs_l, cmat[di])
        if ctx_out:
            yc = yc + readout(xs_c, cmat[di])
    return glu(yl), (glu(yc) if ctx_out else None)


def hyena_spectra(L, w1, b1, f1, w2, b2, f2, w3):
    f32 = jnp.float32
    t = jnp.linspace(0.0, 1.0, L, dtype=f32)[:, None]
    ang = (2.0 * math.pi / L) * jnp.arange(L, dtype=f32)[:, None]
    bands = jnp.linspace(1e-4, HY_BANDS - 1, HY_BANDS, dtype=f32)[None, :]
    feats = jnp.concatenate([t, jnp.cos(bands * ang), -jnp.sin(bands * ang)], axis=-1)
    h = jnp.sin(f1.astype(f32) * (feats @ w1.astype(f32) + b1.astype(f32)))
    h = jnp.sin(f2.astype(f32) * (h @ w2.astype(f32) + b2.astype(f32)))
    h = (h @ w3.astype(f32)).reshape(L, HY_ORDER, N_DIR, HY_WIDTH)
    deltas = jnp.abs(jnp.linspace(HY_MIN_DECAY, HY_MAX_DECAY, HY_WIDTH, dtype=f32))
    h = h * jnp.exp(-t * deltas)[:, None, None, :]
    h_fwd, h_bwd = h[:, :, 0], h[:, :, 1]
    taps = jnp.concatenate([h_fwd, jnp.zeros_like(h_fwd[:1]), h_bwd[:0:-1]], axis=0)
    return jnp.fft.rfft(taps, axis=0)


def short_conv(x, w, b):
    pad = HY_SHORT // 2
    L = x.shape[1]
    xp = jnp.pad(x, ((0, 0), (pad, pad), (0, 0)))
    return sum(xp[:, j:j + L] * w[j] for j in range(HY_SHORT)) + b


def long_conv(u, spec, d):
    L = u.shape[1]
    U = jnp.fft.rfft(u, n=2 * L, axis=1)
    y = jnp.fft.irfft(U * spec[None], n=2 * L, axis=1)[:, :L]
    return y + d * u


def hyena_seq(p, conv_w, conv_b, spec, hy_d):
    f32 = jnp.float32
    s = short_conv(p.astype(f32), conv_w.astype(f32), conv_b.astype(f32))
    v, x1, x2 = jnp.split(s, 3, axis=-1)
    d = hy_d.astype(f32)
    z = x1 * long_conv(v, spec[:, 0], d[0])
    return x2 * long_conv(z, spec[:, 1], d[1])


def setup_inputs(seed: int = 0) -> dict:
    key = jax.random.key(seed)
    ks = iter(jax.random.split(key, 40))
    f32 = jnp.float32
    nrm = lambda shape, s: s * jax.random.normal(next(ks), shape, f32)
    x = nrm((BATCH, SEQ, D_MODEL), 1.0)
    c = nrm((BATCH, D_MODEL), 1.0)
    ctx = nrm((BATCH, CTX_LEN, D_MODEL), 1.0)
    c_ctx = nrm((D_MODEL,), 1.0)
    w_mod = nrm((DEPTH, D_MODEL, 3 * D_MODEL), 0.5 * D_MODEL ** -0.5)
    b_mod = nrm((DEPTH, 3 * D_MODEL), 0.01)
    g_pre = 1.0 + nrm((DEPTH, D_MODEL), 0.01)
    g_post = 1.0 + nrm((DEPTH, D_MODEL), 0.01)
    w_in = nrm((DEPTH, D_MODEL, IN_WIDTH), D_MODEL ** -0.5)
    w_out = nrm((DEPTH, MIX_WIDTH, D_MODEL), MIX_WIDTH ** -0.5)
    gla_w_gate = nrm((DEPTH, N_DIR, GLA_RANK, GLA_KEY), GLA_RANK ** -0.5)
    gla_b_gate = nrm((DEPTH, N_DIR, GLA_KEY), 0.1)
    gla_norm = 1.0 + nrm((DEPTH, GLA_DV), 0.01)
    s5_lam_re = -0.5 + nrm((DEPTH, N_DIR, S5_GROUPS, S5_STATE), 0.01)
    s5_lam_im = math.pi * jnp.arange(S5_STATE, dtype=f32) + nrm((DEPTH, N_DIR, S5_GROUPS, S5_STATE), 0.01)
    s5_log_step = jax.random.uniform(next(ks), (DEPTH, N_DIR, S5_GROUPS), f32, math.log(1e-3), math.log(1e-1))
    s5_b_re = nrm((DEPTH, N_DIR, S5_GROUPS, S5_STATE, S5_GROUP), (2 * S5_GROUP) ** -0.5)
    s5_b_im = nrm((DEPTH, N_DIR, S5_GROUPS, S5_STATE, S5_GROUP), (2 * S5_GROUP) ** -0.5)
    s5_c_re = nrm((DEPTH, N_DIR, S5_GROUPS, S5_GROUP, S5_STATE), S5_STATE ** -0.5)
    s5_c_im = nrm((DEPTH, N_DIR, S5_GROUPS, S5_GROUP, S5_STATE), S5_STATE ** -0.5)
    s5_d = nrm((DEPTH, S5_WIDTH), 1.0)
    s5_w_glu = nrm((DEPTH, S5_WIDTH, S5_WIDTH), S5_WIDTH ** -0.5)
    s5_b_glu = nrm((DEPTH, S5_WIDTH), 0.01)
    hy_conv_w = nrm((DEPTH, HY_SHORT, 3 * HY_WIDTH), HY_SHORT ** -0.5)
    hy_conv_b = nrm((DEPTH, 3 * HY_WIDTH), 0.01)
    hy_w1 = nrm((DEPTH, HY_EMB, HY_FFN), HY_EMB ** -0.5)
    hy_b1 = nrm((DEPTH, HY_FFN), 0.01)
    hy_f1 = 1.0 + nrm((DEPTH, HY_FFN), 0.01)
    hy_w2 = nrm((DEPTH, HY_FFN, HY_FFN), HY_FFN ** -0.5)
    hy_b2 = nrm((DEPTH, HY_FFN), 0.01)
    hy_f2 = 1.0 + nrm((DEPTH, HY_FFN), 0.01)
    hy_w3 = nrm((DEPTH, HY_FFN, HY_ORDER * N_DIR * HY_WIDTH), 0.02)
    hy_d = nrm((DEPTH, HY_ORDER, HY_WIDTH), 1.0)
    return {"x": x, "c": c, "ctx": ctx, "c_ctx": c_ctx, "w_mod": w_mod, "b_mod": b_mod,
            "g_pre": g_pre, "g_post": g_post, "w_in": w_in, "w_out": w_out,
            "gla_w_gate": gla_w_gate, "gla_b_gate": gla_b_gate, "gla_norm": gla_norm,
            "s5_lam_re": s5_lam_re, "s5_lam_im": s5_lam_im, "s5_log_step": s5_log_step,
            "s5_b_re": s5_b_re, "s5_b_im": s5_b_im, "s5_c_re": s5_c_re, "s5_c_im": s5_c_im,
            "s5_d": s5_d, "s5_w_glu": s5_w_glu, "s5_b_glu": s5_b_glu,
            "hy_conv_w": hy_conv_w, "hy_conv_b": hy_conv_b, "hy_w1": hy_w1, "hy_b1": hy_b1,
            "hy_f1": hy_f1, "hy_w2": hy_w2, "hy_b2": hy_b2, "hy_f2": hy_f2, "hy_w3": hy_w3,
            "hy_d": hy_d}


def reference(x, c, ctx, c_ctx, w_mod, b_mod, g_pre, g_post, w_in, w_out,
              gla_w_gate, gla_b_gate, gla_norm,
              s5_lam_re, s5_lam_im, s5_log_step, s5_b_re, s5_b_im, s5_c_re, s5_c_im,
              s5_d, s5_w_glu, s5_b_glu,
              hy_conv_w, hy_conv_b, hy_w1, hy_b1, hy_f1, hy_w2, hy_b2, hy_f2, hy_w3, hy_d):
    f32 = jnp.float32
    L = x.shape[1]
    Lc = ctx.shape[1]
    rows = L // GRID_W
    silu_c = jax.nn.silu(c)
    silu_cc = jax.nn.silu(c_ctx)
    xc = ctx
    for l in range(DEPTH):
        last = l == DEPTH - 1
        ctx_out = not last
        col_major = l % 2 == 1
        shift, scale, gate = jnp.split(silu_c @ w_mod[l] + b_mod[l], 3, axis=-1)
        h = rmsnorm(x, g_pre[l]) * (1 + scale[:, None]) + shift[:, None]
        if col_major:
            h = grid_transpose(h, rows, GRID_W)
        cshift, cscale, cgate = jnp.split(silu_cc @ w_mod[l] + b_mod[l], 3, axis=-1)
        hc = rmsnorm(xc, g_pre[l]) * (1 + cscale) + cshift
        lat = split_cols(h @ w_in[l], IN_SPLITS)
        if last:
            cp = split_cols(hc @ w_in[l][:, :STATE_WIDTH], IN_SPLITS[:N_STATE_SPLITS])
        else:
            cp = split_cols(hc @ w_in[l], IN_SPLITS)
        gla_l, gla_c = gla_branch(cp[:4], lat[:4], gla_w_gate[l], gla_b_gate[l], gla_norm[l], ctx_out)
        s5_l, s5_c = s5_branch(cp[4], lat[4], s5_lam_re[l], s5_lam_im[l], s5_log_step[l],
                               s5_b_re[l], s5_b_im[l], s5_c_re[l], s5_c_im[l],
                               s5_d[l], s5_w_glu[l], s5_b_glu[l], ctx_out)
        hyp = (hy_w1[l], hy_b1[l], hy_f1[l], hy_w2[l], hy_b2[l], hy_f2[l], hy_w3[l])
        hy_l = hyena_seq(lat[7], hy_conv_w[l], hy_conv_b[l], hyena_spectra(L, *hyp), hy_d[l])
        sg = lambda t: jax.nn.silu(t.astype(f32))
        y = jnp.concatenate([gla_l * sg(lat[5]), s5_l * sg(lat[6]), hy_l * sg(lat[8])], axis=-1).astype(x.dtype)
        if col_major:
            y = grid_transpose(y, GRID_W, rows)
        x_new = x + gate[:, None] * rmsnorm(y @ w_out[l], g_post[l])
        if ctx_out:
            hy_c = hyena_seq(cp[7], hy_conv_w[l], hy_conv_b[l], hyena_spectra(Lc, *hyp), hy_d[l])
            yc = jnp.concatenate([gla_c * sg(cp[5]), s5_c * sg(cp[6]), hy_c * sg(cp[8])], axis=-1).astype(xc.dtype)
            xc = xc + cgate * rmsnorm(yc @ w_out[l], g_post[l])
        x = x_new
    return x
```

```python
import functools
import math

import numpy as np
import jax
import jax.numpy as jnp
from jax import lax
from jax.experimental import pallas as pl
from jax.experimental.pallas import tpu as pltpu

F32 = jnp.float32
BF16 = jnp.bfloat16

D_MODEL = 2048
GRID_W = 64
EPS = 1e-6

GLA_HEADS = 4
GLA_DK = 128
GLA_DV = 256
GLA_TAU = 16.0
GLA_CHUNK = 64

S5_WIDTH = 512
S5_GROUP = 16
S5_GROUPS = 32
S5_STATE = 64
S5_T = 16
S5_COLS = 640

HY_WIDTH = 512
HY_EMB = 33
HY_BANDS = 16
HY_FFN = 64
HY_MIN_DECAY = math.log(1e-2) / 0.3
HY_MAX_DECAY = math.log(1e-2) / 1.5
FFT_N1 = 64
FFT_N2 = 128

N_MAIN = 6144
COL_S5U, COL_S5G = 4, 7

VMEM_LIMIT_BYTES = 56 * 1024 * 1024


def _cparams(*sem):
    return pltpu.CompilerParams(dimension_semantics=sem, vmem_limit_bytes=VMEM_LIMIT_BYTES)


def _bf(x):
    return x.astype(BF16)


def _split(x):
    hi = _bf(x)
    return hi, _bf(x - hi.astype(F32))


def _split3(x):
    a = _bf(x)
    r = x - a.astype(F32)
    b = _bf(r)
    return a, b, _bf(r - b.astype(F32))


_NN = (((1,), (0,)), ((), ()))
_NT = (((1,), (1,)), ((), ()))
_TN = (((0,), (0,)), ((), ()))


def _mm(a, b, dims=_NN):
    return lax.dot_general(a, b, dims, preferred_element_type=F32)


def _mm3(a_hi, a_lo, b, dims=_NN):
    b_hi, b_lo = _split(b)
    return _mm(a_hi, b_hi, dims) + _mm(a_hi, b_lo, dims) + _mm(a_lo, b_hi, dims)


def _mod_kernel(s_ref, w_ref, b_ref, o_ref):
    s = s_ref[...]
    s = s * jax.nn.sigmoid(s)
    s_hi, s_lo = _split(s)
    o_ref[0] = _mm3(s_hi, s_lo, w_ref[0]) + b_ref[0]


def _modulation(cvec, w_mod, b_mod):
    depth, d, n = w_mod.shape
    tn = 512
    return pl.pallas_call(
        _mod_kernel,
        grid=(depth, n // tn),
        in_specs=[
            pl.BlockSpec((8, d), lambda l, j: (0, 0)),
            pl.BlockSpec((1, d, tn), lambda l, j: (l, 0, j)),
            pl.BlockSpec((1, 1, tn), lambda l, j: (l, 0, j)),
        ],
        out_specs=pl.BlockSpec((1, 8, tn), lambda l, j: (l, 0, j)),
        out_shape=jax.ShapeDtypeStruct((depth, 8, n), F32),
        compiler_params=_cparams("parallel", "parallel"),
        name="modulation",
    )(cvec, w_mod, b_mod.reshape(depth, 1, n))


def _inproj_kernel(x_ref, sc_ref, sh_ref, w_ref, wlr_ref, p_ref, lr_ref, h_scr, *, colmajor, tm):
    d = D_MODEL

    @pl.when(pl.program_id(2) == 0)
    def _():
        def norm(xb):
            ms = jnp.mean(xb * xb, axis=-1, keepdims=True)
            return _bf(xb * lax.rsqrt(ms + EPS) * sc_ref[0] + sh_ref[0])

        if colmajor:
            for j in range(tm // GRID_W):
                h_scr[j * GRID_W:(j + 1) * GRID_W, :] = norm(x_ref[0, :, j * d:(j + 1) * d])
        else:
            h_scr[...] = norm(x_ref[0])
        lr_ref[0] = _mm(h_scr[...], wlr_ref[...])

    p_ref[0] = _mm(h_scr[...], w_ref[...])


def _inproj(x, scale, shift, w_main, w_lr, *, colmajor, tm, tn):
    bsz, seq, d = x.shape
    n = w_main.shape[1]
    if colmajor:
        x_in = x.reshape(bsz, GRID_W, (seq // GRID_W) * d)
        x_spec = pl.BlockSpec((1, GRID_W, (tm // GRID_W) * d), lambda b, m, j: (b, 0, m))
    else:
        x_in = x
        x_spec = pl.BlockSpec((1, tm, d), lambda b, m, j: (b, m, 0))
    return pl.pallas_call(
        functools.partial(_inproj_kernel, colmajor=colmajor, tm=tm),
        grid=(bsz, seq // tm, n // tn),
        in_specs=[
            x_spec,
            pl.BlockSpec((1, 1, d), lambda b, m, j: (b, 0, 0)),
            pl.BlockSpec((1, 1, d), lambda b, m, j: (b, 0, 0)),
            pl.BlockSpec((d, tn), lambda b, m, j: (0, j)),
            pl.BlockSpec((d, 128), lambda b, m, j: (0, 0)),
        ],
        out_specs=[
            pl.BlockSpec((1, tm, tn), lambda b, m, j: (b, m, j)),
            pl.BlockSpec((1, tm, 128), lambda b, m, j: (b, m, 0)),
        ],
        out_shape=[jax.ShapeDtypeStruct((bsz, seq, n), F32),
                   jax.ShapeDtypeStruct((bsz, seq, 128), F32)],
        scratch_shapes=[pltpu.VMEM((tm, d), BF16)],
        compiler_params=_cparams("parallel", "parallel", "arbitrary"),
        name="inproj",
    )(x_in, scale, shift, w_main, w_lr)


def _gla_chunk(q, k, v, lr, st, wg_hi, wg_lo, bg, tri, mask):
    c = GLA_CHUNK
    lr_hi, lr_lo = _split(lr)
    z = _mm(lr_hi, wg_hi) + _mm(lr_hi, wg_lo) + _mm(lr_lo, wg_hi) + bg
    g = (jnp.minimum(z, 0.0) - jnp.log1p(jnp.exp(-jnp.abs(z)))) * (1.0 / GLA_TAU)
    g1, g2, g3 = _split3(g)
    cum = _mm(tri, g1) + _mm(tri, g2) + _mm(tri, g3)
    tot = jnp.sum(g, axis=0, keepdims=True)
    qg = q * jnp.exp(cum)
    kg = k * jnp.exp(-cum)
    kd = k * jnp.exp(tot - cum)
    att = _mm(_bf(qg), _bf(kg), _NT)
    att = jnp.where(mask, att, 0.0)
    vb = _bf(v)
    o = _mm(_bf(att), vb) + _mm(_bf(qg), _bf(st), _NT)
    st_new = st * jnp.exp(tot) + _mm(vb, _bf(kd), _TN)
    del c
    return o, st_new


def _gla_kernel(*refs, ctx_out, seq, seq_c):
    if ctx_out:
        (q_ref, k_ref, v_ref, gt_ref, lr_ref, qc_ref, kc_ref, vc_ref, gtc_ref, lrc_ref,
         wgh_ref, wgl_ref, bg_ref, gn_ref, y_ref, yc_ref) = refs
    else:
        (q_ref, k_ref, v_ref, gt_ref, lr_ref, qc_ref, kc_ref, vc_ref, lrc_ref,
         wgh_ref, wgl_ref, bg_ref, gn_ref, y_ref) = refs
        gtc_ref = yc_ref = None
    c = GLA_CHUNK
    scale = GLA_DK ** -0.5
    row = lax.broadcasted_iota(jnp.int32, (c, c), 0)
    col = lax.broadcasted_iota(jnp.int32, (c, c), 1)
    masks = (col <= row, col >= row)
    tris = tuple(_bf(m.astype(F32)) for m in masks)
    gn = gn_ref[...]

    def run(refs5, n_chunks, st, direction, store):
        qr, kr, vr, lrr, _ = refs5
        wg_hi = wgh_ref[direction, 0]
        wg_lo = wgl_ref[direction, 0]
        bg = bg_ref[direction, 0]

        def body(i, st):
            ci = i if direction == 0 else n_chunks - 1 - i
            rows = pl.ds(pl.multiple_of(ci * c, c), c)
            q = qr[0, rows, :] * scale
            k = kr[0, rows, :]
            v = vr[0, rows, :]
            o, st = _gla_chunk(q, k, v, lrr[0, rows, :], st, wg_hi, wg_lo, bg,
                               tris[direction], masks[direction])
            store(rows, q, k, v, o)
            return st

        return lax.fori_loop(0, n_chunks, body, st)

    def store_fwd(out_ref):
        def f(rows, q, k, v, o):
            if out_ref is not None:
                out_ref[0, rows, :] = o
        return f

    def store_bwd(out_ref, gate_ref):
        def f(rows, q, k, v, o):
            if out_ref is None:
                return
            y = out_ref[0, rows, :] + o - jnp.sum(q * k, axis=-1, keepdims=True) * v
            ms = jnp.mean(y * y, axis=-1, keepdims=True)
            y = y * lax.rsqrt(ms + EPS) * gn
            gt = gate_ref[0, rows, :]
            out_ref[0, rows, :] = y * (gt * jax.nn.sigmoid(gt))
        return f

    lat = (q_ref, k_ref, v_ref, lr_ref, None)
    ctx = (qc_ref, kc_ref, vc_ref, lrc_ref, None)
    zero = jnp.zeros((GLA_DV, GLA_DK), F32)
    st = run(ctx, seq_c // c, zero, 0, store_fwd(yc_ref))
    run(lat, seq // c, st, 0, store_fwd(y_ref))
    st = run(ctx, seq_c // c, zero, 1, store_bwd(yc_ref, gtc_ref))
    run(lat, seq // c, st, 1, store_bwd(y_ref, gt_ref))


def _gla(p, lr, pc, lrc, wg_hi, wg_lo, bg, gnorm, *, ctx_out):
    bsz, seq, _ = p.shape
    seq_c = pc.shape[1]
    h = GLA_HEADS

    def specs(n):
        return [
            pl.BlockSpec((1, n, 128), lambda b, i: (b, 0, i)),
            pl.BlockSpec((1, n, 128), lambda b, i: (b, 0, 4 + i)),
            pl.BlockSpec((1, n, 256), lambda b, i: (b, 0, 4 + i)),
            pl.BlockSpec((1, n, 256), lambda b, i: (b, 0, 10 + i)),
            pl.BlockSpec((1, n, 128), lambda b, i: (b, 0, 0)),
        ]

    lat_specs = specs(seq)
    ctx_specs = specs(seq_c)
    lat_args = [p, p, p, p, lr]
    ctx_args = [pc, pc, pc, pc, lrc]
    if not ctx_out:
        del ctx_specs[3], ctx_args[3]
    w_specs = [
        pl.BlockSpec((2, 1, 128, 128), lambda b, i: (0, i, 0, 0)),
        pl.BlockSpec((2, 1, 128, 128), lambda b, i: (0, i, 0, 0)),
        pl.BlockSpec((2, 1, 1, 128), lambda b, i: (0, i, 0, 0)),
        pl.BlockSpec((1, GLA_DV), lambda b, i: (0, 0)),
    ]
    out_specs = [pl.BlockSpec((1, seq, 256), lambda b, i: (b, 0, i))]
    out_shape = [jax.ShapeDtypeStruct((bsz, seq, h * GLA_DV), F32)]
    if ctx_out:
        out_specs.append(pl.BlockSpec((1, seq_c, 256), lambda b, i: (b, 0, i)))
        out_shape.append(jax.ShapeDtypeStruct((bsz, seq_c, h * GLA_DV), F32))
    outs = pl.pallas_call(
        functools.partial(_gla_kernel, ctx_out=ctx_out, seq=seq, seq_c=seq_c),
        grid=(bsz, h),
        in_specs=lat_specs + ctx_specs + w_specs,
        out_specs=out_specs,
        out_shape=out_shape,
        compiler_params=_cparams("parallel", "parallel"),
        name="gla",
    )(*lat_args, *ctx_args, wg_hi, wg_lo, bg, gnorm)
    return (outs[0], outs[1]) if ctx_out else (outs[0], None)


def _s5_operators(lam_re, lam_im, log_step, b_re, b_im, c_re, c_im):
    t_len, g_n, p_n, h_n = S5_T, S5_GROUPS, S5_STATE, S5_GROUP
    lam = lax.complex(lam_re, lam_im)
    dt = jnp.exp(log_step)[..., None]
    lam_bar = jnp.exp(lam * dt)
    b_bar = ((lam_bar - 1.0) / lam)[..., None] * lax.complex(b_re, b_im)
    cm = lax.complex(c_re, c_im)
    ks = jnp.arange(t_len + 1, dtype=F32)
    pw = jnp.exp((lam * dt)[..., None] * ks)
    taps = jnp.einsum('dghp,dgpt,dgpk->dgthk', cm, pw[..., :t_len], b_bar).real
    tt = np.arange(t_len)
    diff = tt[:, None] - tt[None, :]
    ktoep = []
    for direction in range(2):
        lag = diff if direction == 0 else -diff
        valid = jnp.asarray(lag >= 0)
        blk = taps[direction][:, np.clip(lag, 0, t_len - 1)]
        blk = jnp.where(valid[None, :, :, None, None], blk, 0.0)
        ktoep.append(blk.transpose(0, 1, 3, 2, 4).reshape(g_n, t_len * h_n, t_len * h_n))
    ktoep = jnp.stack(ktoep)
    pw_in = jnp.stack([pw[0][..., t_len - 1 - tt], pw[1][..., tt]])
    bpow = pw_in[..., None] * b_bar[..., None, :]
    bpow = bpow.reshape(2, g_n // 2, 2, p_n, t_len * h_n)
    eye2 = jnp.eye(2, dtype=F32)
    def pair_rows(m):
        return jnp.einsum('dqepc,ef->dqepfc', m, eye2).reshape(2, g_n // 2, 2 * p_n, 2 * t_len * h_n)
    b_pair_re, b_pair_im = pair_rows(bpow.real), pair_rows(bpow.imag)
    pw_out = jnp.stack([pw[0][..., tt + 1], pw[1][..., t_len - tt]])
    cpow = cm[:, :, None, :, :] * pw_out.transpose(0, 1, 3, 2)[:, :, :, None, :]
    cpow = cpow.reshape(2, g_n // 2, 2, t_len * h_n, p_n)
    def pair_cols(m):
        return jnp.einsum('dqerp,ef->dqerfp', m, eye2).reshape(2, g_n, t_len * h_n, 2 * p_n)
    cpad = jnp.concatenate([pair_cols(cpow.real), -pair_cols(cpow.imag)], axis=-1)
    lam_t = pw[..., t_len].reshape(2, 1, g_n * p_n)
    return (_bf(ktoep), _bf(b_pair_re), _bf(b_pair_im), _bf(cpad), lam_t.real, lam_t.imag)


def _s5_in_kernel(u_ref, uc_ref, o_ref):
    slab = jnp.concatenate(
        [u_ref[0], u_ref[1], uc_ref[0], uc_ref[1],
         jnp.zeros((S5_COLS - 2 * u_ref.shape[1] - 2 * uc_ref.shape[1], S5_WIDTH), F32)], axis=0)
    o_ref[...] = _bf(slab.T.reshape(S5_GROUPS, S5_GROUP, S5_COLS))


def _s5_relayout_in(p, pc):
    bsz, seq, n = p.shape
    seq_c, n_c = pc.shape[1], pc.shape[2]
    nb, nbc = n // 512, n_c // 512
    return pl.pallas_call(
        _s5_in_kernel,
        grid=(S5_T,),
        in_specs=[
            pl.BlockSpec((bsz, seq // S5_T, 512), lambda t: (0, 0, t * nb + COL_S5U)),
            pl.BlockSpec((bsz, seq_c // S5_T, 512), lambda t: (0, 0, t * nbc + COL_S5U)),
        ],
        out_specs=pl.BlockSpec((S5_GROUPS, S5_GROUP, S5_COLS), lambda t: (0, t, 0)),
        out_shape=jax.ShapeDtypeStruct((S5_GROUPS, S5_T * S5_GROUP, S5_COLS), BF16),
        compiler_params=_cparams("parallel"),
        name="s5_relayout_in",
    )(p.reshape(bsz, seq // S5_T, S5_T * n), pc.reshape(bsz, seq_c // S5_T, S5_T * n_c))


def _s5_state_in_kernel(ug_ref, bre_ref, bim_ref, vre_ref, vim_ref):
    u = ug_ref[...].reshape(2 * S5_T * S5_GROUP, S5_COLS)
    for direction in range(2):
        vre_ref[direction] = _mm(bre_ref[direction, 0], u)
        vim_ref[direction] = _mm(bim_ref[direction, 0], u)


def _s5_state_in(ug, b_pair_re, b_pair_im):
    pairs = S5_GROUPS // 2
    rows = S5_T * S5_GROUP
    lanes = S5_GROUPS * S5_STATE
    out = jax.ShapeDtypeStruct((2, lanes, S5_COLS), F32)
    return pl.pallas_call(
        _s5_state_in_kernel,
        grid=(pairs,),
        in_specs=[
            pl.BlockSpec((2, rows, S5_COLS), lambda q: (q, 0, 0)),
            pl.BlockSpec((2, 1, 128, 2 * rows), lambda q: (0, q, 0, 0)),
            pl.BlockSpec((2, 1, 128, 2 * rows), lambda q: (0, q, 0, 0)),
        ],
        out_specs=[pl.BlockSpec((2, 128, S5_COLS), lambda q: (0, q, 0))] * 2,
        out_shape=[out, out],
        compiler_params=_cparams("parallel"),
        name="s5_state_in",
    )(ug, b_pair_re, b_pair_im)


def _s5_scan_kernel(vre_ref, vim_ref, lre_ref, lim_ref, xre_ref, xim_ref, tre_scr, tim_scr, *, n_lat, n_ctx):
    direction = pl.program_id(0)
    a_re = lre_ref[0]
    a_im = lim_ref[0]
    lanes = a_re.shape[-1]
    tre_scr[...] = vre_ref[0].T
    tim_scr[...] = vim_ref[0].T

    def step(row, xr, xi):
        sl = pl.ds(row, 1)
        xre_ref[0, sl, :] = xr
        xim_ref[0, sl, :] = xi
        return (a_re * xr - a_im * xi + tre_scr[sl, :],
                a_re * xi + a_im * xr + tim_scr[sl, :])

    def run(base0, base1, n, carry):
        def body(i, carry):
            j = i + direction * (n - 1 - 2 * i)
            x0r, x0i, x1r, x1i = carry
            x0r, x0i = step(base0 + j, x0r, x0i)
            x1r, x1i = step(base1 + j, x1r, x1i)
            return x0r, x0i, x1r, x1i
        return lax.fori_loop(0, n, body, carry)

    zero = jnp.zeros((1, lanes), F32)
    carry = run(2 * n_lat, 2 * n_lat + n_ctx, n_ctx, (zero, zero, zero, zero))
    run(0, n_lat, n_lat, carry)
    pad0 = 2 * n_lat + 2 * n_ctx
    xre_ref[0, pad0:, :] = jnp.zeros((S5_COLS - pad0, lanes), F32)
    xim_ref[0, pad0:, :] = jnp.zeros((S5_COLS - pad0, lanes), F32)


def _s5_scan(vre, vim, lam_re, lam_im, *, n_lat, n_ctx):
    lanes = vre.shape[1]
    tl = 512
    vspec = pl.BlockSpec((1, tl, S5_COLS), lambda d, j: (d, j, 0))
    xspec = pl.BlockSpec((1, S5_COLS, tl), lambda d, j: (d, 0, j))
    lspec = pl.BlockSpec((1, 1, tl), lambda d, j: (d, 0, j))
    out = jax.ShapeDtypeStruct((2, S5_COLS, lanes), F32)
    return pl.pallas_call(
        functools.partial(_s5_scan_kernel, n_lat=n_lat, n_ctx=n_ctx),
        grid=(2, lanes // tl),
        in_specs=[vspec, vspec, lspec, lspec],
        out_specs=[xspec, xspec],
        out_shape=[out, out],
        scratch_shapes=[pltpu.VMEM((S5_COLS, tl), F32), pltpu.VMEM((S5_COLS, tl), F32)],
        compiler_params=_cparams("parallel", "parallel"),
        name="s5_scan",
    )(vre, vim, lam_re, lam_im)


def _s5_readout_kernel(ug_ref, kt_ref, cp_ref, xre_ref, xim_ref, y_ref):
    for e in range(2):
        u = ug_ref[e]
        acc = None
        for direction in range(2):
            xcat = _bf(jnp.concatenate([xre_ref[direction], xim_ref[direction]], axis=1))
            term = _mm(kt_ref[direction, e], u) + _mm(cp_ref[direction, e], xcat, _NT)
            acc = term if acc is None else acc + term
        y_ref[e] = acc


def _s5_readout(ug, ktoep, cpad, xre, xim):
    pairs = S5_GROUPS // 2
    rows = S5_T * S5_GROUP
    return pl.pallas_call(
        _s5_readout_kernel,
        grid=(pairs,),
        in_specs=[
            pl.BlockSpec((2, rows, S5_COLS), lambda q: (q, 0, 0)),
            pl.BlockSpec((2, 2, rows, rows), lambda q: (0, q, 0, 0)),
            pl.BlockSpec((2, 2, rows, rows), lambda q: (0, q, 0, 0)),
            pl.BlockSpec((2, S5_COLS, 128), lambda q: (0, 0, q)),
            pl.BlockSpec((2, S5_COLS, 128), lambda q: (0, 0, q)),
        ],
        out_specs=pl.BlockSpec((2, rows, S5_COLS), lambda q: (q, 0, 0)),
        out_shape=jax.ShapeDtypeStruct((S5_GROUPS, rows, S5_COLS), F32),
        compiler_params=_cparams("parallel"),
        name="s5_readout",
    )(ug, ktoep, cpad, xre, xim)


def _s5_out_kernel(*refs, ctx_out, n_lat, n_ctx):
    if ctx_out:
        yg_ref, u_ref, g_ref, uc_ref, gc_ref, d_ref, w_ref, b_ref, o_ref, oc_ref = refs
    else:
        yg_ref, u_ref, g_ref, d_ref, w_ref, b_ref, o_ref = refs
    y = yg_ref[...].reshape(S5_WIDTH, S5_COLS).T
    d = d_ref[...]
    w = w_ref[...]
    bias = b_ref[...]

    def finish(yy, u, gate):
        yy = yy + d * u
        yg = jax.nn.gelu(yy)
        out = yg * jax.nn.sigmoid(_mm(_bf(yg), w) + bias)
        return out * (gate * jax.nn.sigmoid(gate))

    for b in range(2):
        o_ref[b] = finish(y[b * n_lat:(b + 1) * n_lat], u_ref[b], g_ref[b])
        if ctx_out:
            r0 = 2 * n_lat + b * n_ctx
            oc_ref[b] = finish(y[r0:r0 + n_ctx], uc_ref[b], gc_ref[b])


def _s5_relayout_out(yg, p, pc, d_skip, w_glu, b_glu, *, ctx_out):
    bsz, seq, n = p.shape
    seq_c, n_c = pc.shape[1], pc.shape[2]
    nb, nbc = n // 512, n_c // 512
    n_lat, n_ctx = seq // S5_T, seq_c // S5_T
    pv = p.reshape(bsz, n_lat, S5_T * n)
    in_specs = [
        pl.BlockSpec((S5_GROUPS, S5_GROUP, S5_COLS), lambda t: (0, t, 0)),
        pl.BlockSpec((bsz, n_lat, 512), lambda t: (0, 0, t * nb + COL_S5U)),
        pl.BlockSpec((bsz, n_lat, 512), lambda t: (0, 0, t * nb + COL_S5G)),
    ]
    args = [yg, pv, pv]
    out_specs = [pl.BlockSpec((bsz, n_lat, 512), lambda t: (0, 0, t))]
    out_shape = [jax.ShapeDtypeStruct((bsz, n_lat, S5_T * S5_WIDTH), F32)]
    if ctx_out:
        pcv = pc.reshape(bsz, n_ctx, S5_T * n_c)
        in_specs += [
            pl.BlockSpec((bsz, n_ctx, 512), lambda t: (0, 0, t * nbc + COL_S5U)),
            pl.BlockSpec((bsz, n_ctx, 512), lambda t: (0, 0, t * nbc + COL_S5G)),
        ]
        args += [pcv, pcv]
        out_specs.append(pl.BlockSpec((bsz, n_ctx, 512), lambda t: (0, 0, t)))
        out_shape.append(jax.ShapeDtypeStruct((bsz, n_ctx, S5_T * S5_WIDTH), F32))
    in_specs += [
        pl.BlockSpec((1, S5_WIDTH), lambda t: (0, 0)),
        pl.BlockSpec((S5_WIDTH, S5_WIDTH), lambda t: (0, 0)),
        pl.BlockSpec((1, S5_WIDTH), lambda t: (0, 0)),
    ]
    args += [d_skip, w_glu, b_glu]
    outs = pl.pallas_call(
        functools.partial(_s5_out_kernel, ctx_out=ctx_out, n_lat=n_lat, n_ctx=n_ctx),
        grid=(S5_T,),
        in_specs=in_specs,
        out_specs=out_specs,
        out_shape=out_shape,
        compiler_params=_cparams("parallel"),
        name="s5_relayout_out",
    )(*args)
    y = outs[0].reshape(bsz, seq, S5_WIDTH)
    yc = outs[1].reshape(bsz, seq_c, S5_WIDTH) if ctx_out else None
    return y, yc


def _s5(p, pc, ops, d_skip, w_glu, b_glu, *, ctx_out):
    ktoep, b_pair_re, b_pair_im, cpad, lam_re, lam_im = ops
    ug = _s5_relayout_in(p, pc)
    vre, vim = _s5_state_in(ug, b_pair_re, b_pair_im)
    xre, xim = _s5_scan(vre, vim, lam_re, lam_im, n_lat=p.shape[1] // S5_T, n_ctx=pc.shape[1] // S5_T)
    yg = _s5_readout(ug, ktoep, cpad, xre, xim)
    return _s5_relayout_out(yg, p, pc, d_skip, w_glu, b_glu, ctx_out=ctx_out)


def _dft_tables():
    n1, n2 = FFT_N1, FFT_N2
    n = n1 * n2
    a = jnp.arange(n1, dtype=jnp.int32)
    ph1 = (a[:, None] * a[None, :]) % n1
    ang1 = ph1.astype(F32) * (2.0 * math.pi / n1)
    c1, s1 = jnp.cos(ang1), -jnp.sin(ang1)
    h = n1 // 2
    m1 = jnp.concatenate([jnp.concatenate([c1[:, :h], -s1[:, :h]], 1),
                          jnp.concatenate([s1[:, :h], c1[:, :h]], 1)], 0)
    m1_real = jnp.concatenate([c1, s1], 0)
    m3 = m1.T / n
    k1 = jnp.arange(n1, dtype=jnp.int32)[:, None, None]
    k2 = jnp.arange(n2, dtype=jnp.int32)[None, :, None]
    m2 = jnp.arange(n2, dtype=jnp.int32)[None, None, :]
    ph = (n1 * m2 * k2 + m2 * k1) % n
    ang = ph.astype(F32) * (2.0 * math.pi / n)
    gr, gi = jnp.cos(ang), -jnp.sin(ang)
    g = jnp.concatenate([jnp.concatenate([gr, -gi], 2), jnp.concatenate([gi, gr], 2)], 1)
    gt = jnp.swapaxes(g, 1, 2)
    return tuple(_split(t) for t in (m1, m1_real, m3, g, gt))


def _filter_features(length):
    t = jnp.linspace(0.0, 1.0, length, dtype=F32)[:, None]
    ang = (2.0 * math.pi / length) * jnp.arange(length, dtype=F32)[:, None]
    bands = jnp.linspace(1e-4, HY_BANDS - 1, HY_BANDS, dtype=F32)[None, :]
    feats = jnp.concatenate([t, jnp.cos(bands * ang), -jnp.sin(bands * ang)], axis=-1)
    feats = jnp.pad(feats, ((0, 0), (0, 128 - HY_EMB)))
    rev = jnp.roll(feats[::-1], 1, axis=0)
    return jnp.stack([feats, rev])


def _filter_kernel(f_ref, w1_ref, b1_ref, f1_ref, w2_ref, b2_ref, f2_ref, w3_ref, dl_ref, o_ref):
    half = pl.program_id(1)
    tile = pl.program_id(2)
    x = f_ref[0]
    x_hi, x_lo = _split(x)
    h = jnp.sin(f1_ref[0] * (_mm3(x_hi, x_lo, w1_ref[0]) + b1_ref[0]))
    h_hi, h_lo = _split(h)
    h = jnp.sin(f2_ref[0] * (_mm3(h_hi, h_lo, w2_ref[0]) + b2_ref[0]))
    h_hi, h_lo = _split(h)
    y = _mm3(h_hi, h_lo, w3_ref[0, 0]) * jnp.exp(-x[:, 0:1] * dl_ref[...])
    rows = lax.broadcasted_iota(jnp.int32, y.shape, 0)
    drop = jnp.logical_and(jnp.logical_and(half == 1, tile == 0), rows == 0)
    o_ref[0] = jnp.where(drop, 0.0, y)


def _filter_taps(feats, w1, b1, f1, w2, b2, f2, w3, deltas, tmf):
    depth = w1.shape[0]
    length = feats.shape[1]
    nt = length // tmf
    wide = w3.shape[-1]
    vec = pl.BlockSpec((1, 1, 128), lambda l, s, i: (l, 0, 0))
    sq = pl.BlockSpec((1, 128, 128), lambda l, s, i: (l, 0, 0))
    return pl.pallas_call(
        _filter_kernel,
        grid=(depth, 2, nt),
        in_specs=[
            pl.BlockSpec((1, tmf, 128), lambda l, s, i: (s, i, 0)),
            sq, vec, vec, sq, vec, vec,
            pl.BlockSpec((1, 1, 128, wide), lambda l, s, i: (l, s, 0, 0)),
            pl.BlockSpec((1, wide), lambda l, s, i: (0, 0)),
        ],
        out_specs=pl.BlockSpec((1, tmf, wide), lambda l, s, i: (l, s * nt + i, 0)),
        out_shape=jax.ShapeDtypeStruct((depth, 2 * length, wide), F32),
        compiler_params=_cparams("parallel", "parallel", "parallel"),
        name="hyena_filter",
    )(feats, w1, b1, f1, w2, b2, f2, w3, deltas)


def _dft_rows_kernel(mh_ref, ml_ref, x_ref, o_ref):
    o_ref[0] = _mm3(mh_ref[...], ml_ref[...], x_ref[0])


def _dft_rows(m, x, tc=2048):
    m_hi, m_lo = m
    batch, rin, cols = x.shape
    rout = m_hi.shape[0]
    return pl.pallas_call(
        _dft_rows_kernel,
        grid=(batch, cols // tc),
        in_specs=[
            pl.BlockSpec((rout, rin), lambda l, j: (0, 0)),
            pl.BlockSpec((rout, rin), lambda l, j: (0, 0)),
            pl.BlockSpec((1, rin, tc), lambda l, j: (l, 0, j)),
        ],
        out_specs=pl.BlockSpec((1, rout, tc), lambda l, j: (l, 0, j)),
        out_shape=jax.ShapeDtypeStruct((batch, rout, cols), F32),
        compiler_params=_cparams("parallel", "parallel"),
        name="dft_rows",
    )(m_hi, m_lo, x)


def _spectrum_kernel(gh_ref, gl_ref, a_ref, o_ref):
    a = a_ref[0, :, 0].reshape(2 * FFT_N2, a_ref.shape[-1])
    z = _mm3(gh_ref[0], gl_ref[0], a)
    o_ref[0, 0] = z.reshape(2, FFT_N2, a_ref.shape[-1])


def _filter_spectrum(g, a):
    g_hi, g_lo = g
    depth, _, n1, n2, ch = a.shape
    gspec = pl.BlockSpec((1, 2 * n2, 2 * n2), lambda l, k: (k, 0, 0))
    return pl.pallas_call(
        _spectrum_kernel,
        grid=(depth, n1),
        in_specs=[gspec, gspec, pl.BlockSpec((1, 2, 1, n2, ch), lambda l, k: (l, 0, k, 0, 0))],
        out_specs=pl.BlockSpec((1, 1, 2, n2, ch), lambda l, k: (l, k, 0, 0, 0)),
        out_shape=jax.ShapeDtypeStruct((depth, n1, 2, n2, ch), F32),
        compiler_params=_cparams("parallel", "parallel"),
        name="hyena_spectrum",
    )(g_hi, g_lo, a)


def _hy_pre_kernel(pv_ref, p1_ref, p2_ref, pg_ref, wv_ref, w1_ref, w2_ref, bv_ref, b1_ref, b2_ref,
                   v_ref, x1_ref, x2_ref):
    n = pv_ref.shape[1]
    rows = lax.broadcasted_iota(jnp.int32, (n, 128), 0)

    def conv(p, w_ref, b_ref):
        up = jnp.where(rows == 0, 0.0, pltpu.roll(p, 1, 0))
        dn = jnp.where(rows == n - 1, 0.0, pltpu.roll(p, n - 1, 0))
        return up * w_ref[0:1, :] + p * w_ref[1:2, :] + dn * w_ref[2:3, :] + b_ref[...]

    v_ref[0] = conv(pv_ref[0], wv_ref, bv_ref)
    x1_ref[0] = conv(p1_ref[0], w1_ref, b1_ref)
    gate = pg_ref[0]
    x2_ref[0] = conv(p2_ref[0], w2_ref, b2_ref) * (gate * jax.nn.sigmoid(gate))


def _hy_pre(p, conv_w, conv_b):
    bsz, seq, _ = p.shape
    def pspec(off):
        return pl.BlockSpec((1, seq, 128), lambda b, j: (b, 0, off + j))
    def wspec(off):
        return pl.BlockSpec((3, 128), lambda b, j: (0, off + j))
    def bspec(off):
        return pl.BlockSpec((1, 128), lambda b, j: (0, off + j))
    out = jax.ShapeDtypeStruct((bsz, seq, HY_WIDTH), F32)
    ospec = pl.BlockSpec((1, seq, 128), lambda b, j: (b, 0, j))
    return pl.pallas_call(
        _hy_pre_kernel,
        grid=(bsz, HY_WIDTH // 128),
        in_specs=[pspec(32), pspec(36), pspec(40), pspec(44),
                  wspec(0), wspec(4), wspec(8), bspec(0), bspec(4), bspec(8)],
        out_specs=[ospec, ospec, ospec],
        out_shape=[out, out, out],
        compiler_params=_cparams("parallel", "parallel"),
        name="hyena_pre",
    )(p, p, p, p, conv_w, conv_w, conv_w, conv_b, conv_b, conv_b)


def _hy_mid_kernel(gh_ref, gl_ref, th_ref, tl_ref, a_ref, h_ref, o_ref):
    ch = a_ref.shape[-1]
    a = a_ref[:, 0].reshape(2 * FFT_N2, ch)
    z = _mm3(gh_ref[0], gl_ref[0], a)
    zr, zi = z[:FFT_N2], z[FFT_N2:]
    hr, hi = h_ref[0, 0, 0], h_ref[0, 0, 1]
    z2 = jnp.concatenate([zr * hr - zi * hi, zr * hi + zi * hr], axis=0)
    o_ref[:, 0] = _mm3(th_ref[0], tl_ref[0], z2).reshape(2, FFT_N2, ch)


def _hy_mid(g, gt, a, spec, layer, order):
    (g_hi, g_lo), (t_hi, t_lo) = g, gt
    _, n1, n2, ch = a.shape
    gspec = pl.BlockSpec((1, 2 * n2, 2 * n2), lambda k: (k, 0, 0))
    aspec = pl.BlockSpec((2, 1, n2, ch), lambda k: (0, k, 0, 0))
    return pl.pallas_call(
        _hy_mid_kernel,
        grid=(n1,),
        in_specs=[gspec, gspec, gspec, gspec, aspec,
                  pl.BlockSpec((1, 1, 2, n2, ch), lambda k: (layer, k, 0, 0, order))],
        out_specs=aspec,
        out_shape=jax.ShapeDtypeStruct(a.shape, F32),
        compiler_params=_cparams("parallel"),
        name="hyena_mid",
    )(g_hi, g_lo, t_hi, t_lo, a, spec)


def _hy_post_kernel(mh_ref, ml_ref, a_ref, u_ref, g_ref, d_ref, o_ref):
    y = _mm3(mh_ref[...], ml_ref[...], a_ref[...])
    o_ref[...] = g_ref[...] * (y + d_ref[...] * u_ref[...])


def _hy_post(m3, a, u, gate, d_tiled, tc=2048):
    m_hi, m_lo = m3
    rin, cols = a.shape
    rout = m_hi.shape[0]
    cspec = pl.BlockSpec((rout, tc), lambda j: (0, j))
    return pl.pallas_call(
        _hy_post_kernel,
        grid=(cols // tc,),
        in_specs=[
            pl.BlockSpec((rout, rin), lambda j: (0, 0)),
            pl.BlockSpec((rout, rin), lambda j: (0, 0)),
            pl.BlockSpec((rin, tc), lambda j: (0, j)),
            cspec, cspec,
            pl.BlockSpec((1, tc), lambda j: (0, 0)),
        ],
        out_specs=cspec,
        out_shape=jax.ShapeDtypeStruct((rout, cols), F32),
        compiler_params=_cparams("parallel"),
        name="hyena_post",
    )(m_hi, m_lo, a, u, gate, d_tiled)


def _hy_long_conv(tables, spec, layer, order, u, gate, d):
    m1, _, m3, g, gt = tables
    bsz, seq, ch = u.shape
    rows = bsz * seq // FFT_N2
    cols = FFT_N2 * ch
    a = _dft_rows(m1, u.reshape(1, rows, cols))
    a = _hy_mid(g, gt, a.reshape(2, FFT_N1, FFT_N2, ch), spec, layer, order)
    d_tiled = jnp.tile(d.reshape(1, ch), (1, 2048 // ch))
    y = _hy_post(m3, a.reshape(2 * FFT_N1, cols), u.reshape(rows, cols), gate.reshape(rows, cols), d_tiled)
    return y.reshape(bsz, seq, ch)


def _ctx_dft_tables(length):
    n = 2 * length
    k = jnp.arange(n, dtype=jnp.int32)
    ang = ((k[:, None] * k[None, :]) % n).astype(F32) * (2.0 * math.pi / n)
    fr, fi = jnp.cos(ang), -jnp.sin(ang)
    fwd = jnp.concatenate([jnp.concatenate([fr[:, :length], -fi[:, :length]], 1),
                           jnp.concatenate([fi[:, :length], fr[:, :length]], 1)], 0)
    real = jnp.concatenate([fr, fi], 0)
    inv = fwd.T / n
    return tuple(_split(t) for t in (fwd, real, inv))


def _hyc_kernel(fh_ref, fl_ref, rh_ref, rl_ref, ih_ref, il_ref, taps_ref, u_ref, g_ref, d_ref, o_ref):
    n = u_ref.shape[1]
    x = jnp.concatenate([u_ref[0], u_ref[1]], axis=0)
    z = _mm3(fh_ref[...], fl_ref[...], x)
    h = _mm3(rh_ref[...], rl_ref[...], taps_ref[0])
    m = 2 * n
    zr, zi, hr, hi = z[:m], z[m:], h[:m], h[m:]
    z2 = jnp.concatenate([zr * hr - zi * hi, zr * hi + zi * hr], axis=0)
    y = _mm3(ih_ref[...], il_ref[...], z2)
    for b in range(2):
        o_ref[b] = g_ref[b] * (y[b * n:(b + 1) * n] + d_ref[...] * u_ref[b])


def _hyc_long_conv(ctabs, taps, layer, order, u, gate, d):
    (f_hi, f_lo), (r_hi, r_lo), (i_hi, i_lo) = ctabs
    bsz, n, ch = u.shape
    full = lambda t: pl.BlockSpec(t.shape, lambda i: (0,) * t.ndim)
    uspec = pl.BlockSpec((bsz, n, ch), lambda i: (0, 0, 0))
    return pl.pallas_call(
        _hyc_kernel,
        grid=(1,),
        in_specs=[full(f_hi), full(f_lo), full(r_hi), full(r_lo), full(i_hi), full(i_lo),
                  pl.BlockSpec((1, 2 * n, ch), lambda i: (layer, 0, order)),
                  uspec, uspec, pl.BlockSpec((1, ch), lambda i: (0, 0))],
        out_specs=uspec,
        out_shape=jax.ShapeDtypeStruct((bsz, n, ch), F32),
        compiler_params=_cparams("arbitrary"),
        name="hyena_ctx",
    )(f_hi, f_lo, r_hi, r_lo, i_hi, i_lo, taps, u, gate, d.reshape(1, ch))


def _outproj_kernel(x_ref, ya_ref, yb_ref, yc_ref, w_ref, gp_ref, gate_ref, o_ref, *, colmajor, tm):
    d = D_MODEL
    na, nb = ya_ref.shape[-1], yb_ref.shape[-1]
    acc = (_mm(_bf(ya_ref[0]), w_ref[0:na, :]) + _mm(_bf(yb_ref[0]), w_ref[na:na + nb, :])
           + _mm(_bf(yc_ref[0]), w_ref[na + nb:, :]))
    ms = jnp.mean(acc * acc, axis=-1, keepdims=True)
    upd = gate_ref[0] * (acc * lax.rsqrt(ms + EPS) * gp_ref[...])
    if colmajor:
        for j in range(tm // GRID_W):
            sl = slice(j * d, (j + 1) * d)
            o_ref[0, :, sl] = x_ref[0, :, sl] + upd[j * GRID_W:(j + 1) * GRID_W]
    else:
        o_ref[0] = x_ref[0] + upd


def _outproj(x, ya, yb, yc, w_out, g_post, gate, *, colmajor, tm):
    bsz, seq, d = x.shape
    if colmajor:
        x_in = x.reshape(bsz, GRID_W, (seq // GRID_W) * d)
        x_spec = pl.BlockSpec((1, GRID_W, (tm // GRID_W) * d), lambda b, m: (b, 0, m))
    else:
        x_in = x
        x_spec = pl.BlockSpec((1, tm, d), lambda b, m: (b, m, 0))
    def yspec(t):
        return pl.BlockSpec((1, tm, t.shape[-1]), lambda b, m: (b, m, 0))
    out = pl.pallas_call(
        functools.partial(_outproj_kernel, colmajor=colmajor, tm=tm),
        grid=(bsz, seq // tm),
        in_specs=[x_spec, yspec(ya), yspec(yb), yspec(yc),
                  pl.BlockSpec(w_out.shape, lambda b, m: (0, 0)),
                  pl.BlockSpec((1, d), lambda b, m: (0, 0)),
                  pl.BlockSpec((1, 1, d), lambda b, m: (b, 0, 0))],
        out_specs=x_spec,
        out_shape=jax.ShapeDtypeStruct(x_in.shape, F32),
        compiler_params=_cparams("parallel", "parallel"),
        name="outproj",
    )(x_in, ya, yb, yc, w_out, g_post, gate)
    return out.reshape(bsz, seq, d)


def kernel(x, c, ctx, c_ctx, w_mod, b_mod, g_pre, g_post, w_in, w_out, gla_w_gate, gla_b_gate, gla_norm, s5_lam_re, s5_lam_im, s5_log_step, s5_b_re, s5_b_im, s5_c_re, s5_c_im, s5_d, s5_w_glu, s5_b_glu, hy_conv_w, hy_conv_b, hy_w1, hy_b1, hy_f1, hy_w2, hy_b2, hy_f2, hy_w3, hy_d):
    bsz, seq, d = x.shape
    seq_c = ctx.shape[1]
    depth = w_in.shape[0]
    assert (bsz, d, seq // GRID_W) == (2, D_MODEL, GRID_W) and seq == FFT_N1 * FFT_N2 // 2

    cvec = jnp.concatenate([c, c_ctx[None], jnp.zeros((8 - bsz - 1, d), F32)], axis=0)
    mod = _modulation(cvec, w_mod, b_mod)
    shift, scale, gate = mod[..., :d], mod[..., d:2 * d], mod[..., 2 * d:]

    w_main = _bf(jnp.concatenate([w_in[..., :2048], w_in[..., 2080:]], axis=-1))
    w_lr = _bf(jnp.pad(w_in[..., 2048:2080], ((0, 0), (0, 0), (0, 96))))
    w_out_b = _bf(w_out)
    w_glu_b = _bf(s5_w_glu)

    wg = gla_w_gate.reshape(depth, 2, 16, GLA_HEADS, GLA_DK).transpose(0, 1, 3, 2, 4)
    wg_pad = jnp.stack([jnp.pad(wg[:, 0], ((0, 0), (0, 0), (0, 112), (0, 0))),
                        jnp.pad(wg[:, 1], ((0, 0), (0, 0), (16, 96), (0, 0)))], axis=1)
    wg_hi, wg_lo = _split(wg_pad)
    bg = gla_b_gate.reshape(depth, 2, GLA_HEADS, 1, GLA_DK)

    tables = _dft_tables()
    deltas = jnp.abs(jnp.linspace(HY_MIN_DECAY, HY_MAX_DECAY, HY_WIDTH, dtype=F32))
    deltas2 = jnp.tile(deltas, 2).reshape(1, 2 * HY_WIDTH)
    pad_k = lambda w: jnp.pad(w, ((0, 0), (0, 128 - w.shape[1]), (0, 128 - w.shape[2])))
    pad_v = lambda v: jnp.pad(v, ((0, 0), (0, 128 - v.shape[1]))).reshape(depth, 1, 128)
    w3 = hy_w3.reshape(depth, HY_FFN, 2, 2, HY_WIDTH).transpose(0, 3, 1, 2, 4)
    w3 = jnp.pad(w3.reshape(depth, 2, HY_FFN, 2 * HY_WIDTH), ((0, 0), (0, 0), (0, 128 - HY_FFN), (0, 0)))
    filt_w = (pad_k(hy_w1), pad_v(hy_b1), pad_v(hy_f1), pad_k(hy_w2), pad_v(hy_b2), pad_v(hy_f2), w3, deltas2)
    taps = _filter_taps(_filter_features(seq), *filt_w, tmf=512)
    fa = _dft_rows(tables[1], taps.reshape(depth, FFT_N1, FFT_N2 * 2 * HY_WIDTH))
    spec = _filter_spectrum(tables[3], fa.reshape(depth, 2, FFT_N1, FFT_N2, 2 * HY_WIDTH))
    taps_c = _filter_taps(_filter_features(seq_c), *filt_w, tmf=seq_c)
    ctabs = _ctx_dft_tables(seq_c)

    xc = ctx
    for l in range(depth):
        last = l == depth - 1
        ctx_out = not last
        colmajor = l % 2 == 1
        pre = g_pre[l][None, :] * (1.0 + scale[l])
        sc_l, sh_l = pre[:bsz, None, :], shift[l, :bsz, None, :]
        sc_c = jnp.broadcast_to(pre[bsz][None, None, :], (bsz, 1, d))
        sh_c = jnp.broadcast_to(shift[l, bsz][None, None, :], (bsz, 1, d))
        p, lr = _inproj(x, sc_l, sh_l, w_main[l], w_lr[l], colmajor=colmajor, tm=512, tn=1024)
        wc = w_main[l] if ctx_out else w_main[l][:, :2560]
        pc, lrc = _inproj(xc, sc_c, sh_c, wc, w_lr[l], colmajor=False, tm=seq_c, tn=512)

        y_gla, yc_gla = _gla(p, lr, pc, lrc, wg_hi[l], wg_lo[l], bg[l], gla_norm[l][None, :], ctx_out=ctx_out)

        ops = _s5_operators(s5_lam_re[l], s5_lam_im[l], s5_log_step[l], s5_b_re[l], s5_b_im[l],
                            s5_c_re[l], s5_c_im[l])
        y_s5, yc_s5 = _s5(p, pc, ops, s5_d[l][None, :], w_glu_b[l], s5_b_glu[l][None, :], ctx_out=ctx_out)

        v, x1, x2 = _hy_pre(p, hy_conv_w[l], hy_conv_b[l][None, :])
        z = _hy_long_conv(tables, spec, l, 0, v, x1, hy_d[l, 0])
        y_hy = _hy_long_conv(tables, spec, l, 1, z, x2, hy_d[l, 1])

        x_new = _outproj(x, y_gla, y_s5, y_hy, w_out_b[l], g_post[l][None, :], gate[l, :bsz, None, :],
                         colmajor=colmajor, tm=512)
        if ctx_out:
            vc, x1c, x2c = _hy_pre(pc, hy_conv_w[l], hy_conv_b[l][None, :])
            zc = _hyc_long_conv(ctabs, taps_c, l, 0, vc, x1c, hy_d[l, 0])
            yc_hy = _hyc_long_conv(ctabs, taps_c, l, 1, zc, x2c, hy_d[l, 1])
            gate_c = jnp.broadcast_to(gate[l, bsz][None, None, :], (bsz, 1, d))
            xc = _outproj(xc, yc_gla, yc_s5, yc_hy, w_out_b[l], g_post[l][None, :], gate_c,
                          colmajor=False, tm=seq_c)
        x = x_new
    return x
```

```python
import functools
import math

import numpy as np
import jax
import jax.numpy as jnp
from jax import lax
from jax.experimental import pallas as pl
from jax.experimental.pallas import tpu as pltpu

F32 = jnp.float32
BF16 = jnp.bfloat16

D_MODEL = 2048
GRID_W = 64
EPS = 1e-6

GLA_HEADS = 4
GLA_DK = 128
GLA_DV = 256
GLA_TAU = 16.0
GLA_CHUNK = 64
GLA_BLOCK = 8

S5_WIDTH = 512
S5_GROUP = 16
S5_GROUPS = 32
S5_STATE = 64
S5_T = 16
S5_COLS = 640
S5_CB = 128

HY_WIDTH = 512
HY_EMB = 33
HY_BANDS = 16
HY_FFN = 64
HY_MIN_DECAY = math.log(1e-2) / 0.3
HY_MAX_DECAY = math.log(1e-2) / 1.5
FFT_N1 = 64
FFT_N2 = 128
HY_CT = 128
HY_KB = 4

N_MAIN = 6144
N_STATE = 2560
COL_S5U, COL_S5G = 4, 7

VMEM_LIMIT_BYTES = 56 * 1024 * 1024


def _cparams(*sem):
    return pltpu.CompilerParams(dimension_semantics=sem, vmem_limit_bytes=VMEM_LIMIT_BYTES)


def _bf(x):
    return x.astype(BF16)


def _split(x):
    hi = _bf(x)
    return hi, _bf(x - hi.astype(F32))


def _split3(x):
    a = _bf(x)
    r = x - a.astype(F32)
    b = _bf(r)
    return a, b, _bf(r - b.astype(F32))


_NN = (((1,), (0,)), ((), ()))
_NT = (((1,), (1,)), ((), ()))
_TN = (((0,), (0,)), ((), ()))


def _mm(a, b, dims=_NN):
    return lax.dot_general(a, b, dims, preferred_element_type=F32)


def _mm3(a_hi, a_lo, b, dims=_NN):
    b_hi, b_lo = _split(b)
    return _mm(a_hi, b_hi, dims) + _mm(a_hi, b_lo, dims) + _mm(a_lo, b_hi, dims)


def _mod_kernel(s_ref, w_ref, b_ref, o_ref):
    s = s_ref[...]
    s = s * jax.nn.sigmoid(s)
    s_hi, s_lo = _split(s)
    o_ref[0] = _mm3(s_hi, s_lo, w_ref[0]) + b_ref[0]


def _modulation(cvec, w_mod, b_mod):
    depth, d, n = w_mod.shape
    tn = 512
    return pl.pallas_call(
        _mod_kernel,
        grid=(depth, n // tn),
        in_specs=[
            pl.BlockSpec((8, d), lambda l, j: (0, 0)),
            pl.BlockSpec((1, d, tn), lambda l, j: (l, 0, j)),
            pl.BlockSpec((1, 1, tn), lambda l, j: (l, 0, j)),
        ],
        out_specs=pl.BlockSpec((1, 8, tn), lambda l, j: (l, 0, j)),
        out_shape=jax.ShapeDtypeStruct((depth, 8, n), F32),
        compiler_params=_cparams("parallel", "parallel"),
        name="modulation",
    )(cvec, w_mod, b_mod.reshape(depth, 1, n))


def _inproj_kernel(x_ref, sc_ref, sh_ref, w_ref, wlr_ref, p_ref, lr_ref, h_scr, *, colmajor, tm):
    @pl.when(pl.program_id(2) == 0)
    def _():
        def norm(xb):
            ms = jnp.mean(xb * xb, axis=-1, keepdims=True)
            return _bf(xb * lax.rsqrt(ms + EPS) * sc_ref[0] + sh_ref[0])

        if colmajor:
            for j in range(tm // GRID_W):
                h_scr[j * GRID_W:(j + 1) * GRID_W, :] = norm(x_ref[0, :, j, :])
        else:
            h_scr[...] = norm(x_ref[0])
        lr_ref[0] = _mm(h_scr[...], wlr_ref[0])

    p_ref[0] = _mm(h_scr[...], w_ref[0])


def _inproj(x, scale, shift, w_main, w_lr, layer, *, n, colmajor, tm, tn):
    bsz, seq, d = x.shape
    if colmajor:
        x_in = x.reshape(bsz, GRID_W, seq // GRID_W, d)
        x_spec = pl.BlockSpec((1, GRID_W, tm // GRID_W, d), lambda b, m, j: (b, 0, m, 0))
    else:
        x_in = x
        x_spec = pl.BlockSpec((1, tm, d), lambda b, m, j: (b, m, 0))
    return pl.pallas_call(
        functools.partial(_inproj_kernel, colmajor=colmajor, tm=tm),
        grid=(bsz, seq // tm, n // tn),
        in_specs=[
            x_spec,
            pl.BlockSpec((1, 1, d), lambda b, m, j: (b, 0, 0)),
            pl.BlockSpec((1, 1, d), lambda b, m, j: (b, 0, 0)),
            pl.BlockSpec((1, d, tn), lambda b, m, j: (layer, 0, j)),
            pl.BlockSpec((1, d, 128), lambda b, m, j: (layer, 0, 0)),
        ],
        out_specs=[
            pl.BlockSpec((1, tm, tn), lambda b, m, j: (b, m, j)),
            pl.BlockSpec((1, tm, 128), lambda b, m, j: (b, m, 0)),
        ],
        out_shape=[jax.ShapeDtypeStruct((bsz, seq, n), F32),
                   jax.ShapeDtypeStruct((bsz, seq, 128), F32)],
        scratch_shapes=[pltpu.VMEM((tm, d), BF16)],
        compiler_params=_cparams("parallel", "parallel", "arbitrary"),
        name="inproj",
    )(x_in, scale, shift, w_main, w_lr)


def _gla_block(q, k, v, lr, st, wg_hi, wg_lo, bg, direction, mask, rowc, nb):
    c = GLA_CHUNK
    n = c * nb
    lr_hi, lr_lo = _split(lr)
    z = _mm(lr_hi, wg_hi) + _mm(lr_hi, wg_lo) + _mm(lr_lo, wg_hi) + bg
    g = (jnp.minimum(z, 0.0) - jnp.log1p(jnp.exp(-jnp.abs(z)))) * (1.0 / GLA_TAU)
    cum = g
    s = 1
    while s < c:
        if direction == 0:
            cum = cum + jnp.where(rowc >= s, pltpu.roll(cum, s, 0), 0.0)
        else:
            cum = cum + jnp.where(rowc < c - s, pltpu.roll(cum, n - s, 0), 0.0)
        s *= 2
    cum3 = cum.reshape(nb, c, GLA_DK)
    tot3 = cum3[:, c - 1:c, :] if direction == 0 else cum3[:, 0:1, :]
    q3 = q.reshape(nb, c, GLA_DK)
    k3 = k.reshape(nb, c, GLA_DK)
    qg3 = _bf(q3 * jnp.exp(cum3))
    kg3 = _bf(k3 * jnp.exp(-cum3))
    kd3 = _bf(k3 * jnp.exp(tot3 - cum3))
    a3 = jnp.exp(tot3)
    vb3 = _bf(v).reshape(nb, c, GLA_DV)
    att = [jnp.where(mask, _mm(qg3[i], kg3[i], _NT), 0.0) for i in range(nb)]
    o_in = [_mm(_bf(att[i]), vb3[i]) for i in range(nb)]
    d_st = [_mm(vb3[i], kd3[i], _TN) for i in range(nb)]
    outs = [None] * nb
    for i in (range(nb) if direction == 0 else range(nb - 1, -1, -1)):
        outs[i] = o_in[i] + _mm(qg3[i], _bf(st), _NT)
        st = st * a3[i] + d_st[i]
    return jnp.concatenate(outs, axis=0), st


def _gla_kernel(*refs, ctx_out, seq, seq_c):
    if ctx_out:
        (q_ref, k_ref, v_ref, gt_ref, lr_ref, qc_ref, kc_ref, vc_ref, gtc_ref, lrc_ref,
         wgh_ref, wgl_ref, bg_ref, gn_ref, y_ref, yc_ref) = refs
    else:
        (q_ref, k_ref, v_ref, gt_ref, lr_ref, qc_ref, kc_ref, vc_ref, lrc_ref,
         wgh_ref, wgl_ref, bg_ref, gn_ref, y_ref) = refs
        gtc_ref = yc_ref = None
    c = GLA_CHUNK
    scale = GLA_DK ** -0.5
    row = lax.broadcasted_iota(jnp.int32, (c, c), 0)
    col = lax.broadcasted_iota(jnp.int32, (c, c), 1)
    masks = (col <= row, col >= row)
    gn = gn_ref[...]

    def run(refs4, n_rows, st, direction, store):
        qr, kr, vr, lrr = refs4
        nb = min(GLA_BLOCK, n_rows // c)
        n = c * nb
        rowc = lax.broadcasted_iota(jnp.int32, (n, GLA_DK), 0) % c
        wg_hi = wgh_ref[direction, 0]
        wg_lo = wgl_ref[direction, 0]
        bg = bg_ref[direction, 0]
        n_blocks = n_rows // n

        def body(i, st):
            bi = i if direction == 0 else n_blocks - 1 - i
            rows = pl.ds(pl.multiple_of(bi * n, n), n)
            q = qr[0, rows, :] * scale
            k = kr[0, rows, :]
            v = vr[0, rows, :]
            o, st = _gla_block(q, k, v, lrr[0, rows, :], st, wg_hi, wg_lo, bg, direction,
                               masks[direction], rowc, nb)
            store(rows, q, k, v, o)
            return st

        return lax.fori_loop(0, n_blocks, body, st)

    def store_fwd(out_ref):
        def f(rows, q, k, v, o):
            if out_ref is not None:
                out_ref[0, rows, :] = o
        return f

    def store_bwd(out_ref, gate_ref):
        def f(rows, q, k, v, o):
            if out_ref is None:
                return
            y = out_ref[0, rows, :] + o - jnp.sum(q * k, axis=-1, keepdims=True) * v
            ms = jnp.mean(y * y, axis=-1, keepdims=True)
            y = y * lax.rsqrt(ms + EPS) * gn
            gt = gate_ref[0, rows, :]
            out_ref[0, rows, :] = y * (gt * jax.nn.sigmoid(gt))
        return f

    lat = (q_ref, k_ref, v_ref, lr_ref)
    ctx = (qc_ref, kc_ref, vc_ref, lrc_ref)
    zero = jnp.zeros((GLA_DV, GLA_DK), F32)
    st = run(ctx, seq_c, zero, 0, store_fwd(yc_ref))
    run(lat, seq, st, 0, store_fwd(y_ref))
    st = run(ctx, seq_c, zero, 1, store_bwd(yc_ref, gtc_ref))
    run(lat, seq, st, 1, store_bwd(y_ref, gt_ref))


def _gla(p, lr, pc, lrc, wg_hi, wg_lo, bg, gnorm, *, ctx_out):
    bsz, seq, _ = p.shape
    seq_c = pc.shape[1]
    h = GLA_HEADS

    def specs(n):
        return [
            pl.BlockSpec((1, n, 128), lambda b, i: (b, 0, i)),
            pl.BlockSpec((1, n, 128), lambda b, i: (b, 0, 4 + i)),
            pl.BlockSpec((1, n, 256), lambda b, i: (b, 0, 4 + i)),
            pl.BlockSpec((1, n, 256), lambda b, i: (b, 0, 10 + i)),
            pl.BlockSpec((1, n, 128), lambda b, i: (b, 0, 0)),
        ]

    lat_specs = specs(seq)
    ctx_specs = specs(seq_c)
    lat_args = [p, p, p, p, lr]
    ctx_args = [pc, pc, pc, pc, lrc]
    if not ctx_out:
        del ctx_specs[3], ctx_args[3]
    w_specs = [
        pl.BlockSpec((2, 1, 128, 128), lambda b, i: (0, i, 0, 0)),
        pl.BlockSpec((2, 1, 128, 128), lambda b, i: (0, i, 0, 0)),
        pl.BlockSpec((2, 1, 1, 128), lambda b, i: (0, i, 0, 0)),
        pl.BlockSpec((1, GLA_DV), lambda b, i: (0, 0)),
    ]
    out_specs = [pl.BlockSpec((1, seq, 256), lambda b, i: (b, 0, i))]
    out_shape = [jax.ShapeDtypeStruct((bsz, seq, h * GLA_DV), F32)]
    if ctx_out:
        out_specs.append(pl.BlockSpec((1, seq_c, 256), lambda b, i: (b, 0, i)))
        out_shape.append(jax.ShapeDtypeStruct((bsz, seq_c, h * GLA_DV), F32))
    outs = pl.pallas_call(
        functools.partial(_gla_kernel, ctx_out=ctx_out, seq=seq, seq_c=seq_c),
        grid=(bsz, h),
        in_specs=lat_specs + ctx_specs + w_specs,
        out_specs=out_specs,
        out_shape=out_shape,
        compiler_params=_cparams("parallel", "parallel"),
        name="gla",
    )(*lat_args, *ctx_args, wg_hi, wg_lo, bg, gnorm)
    return (outs[0], outs[1]) if ctx_out else (outs[0], None)


def _s5_operators(lam_re, lam_im, log_step, b_re, b_im, c_re, c_im):
    t_len, g_n, p_n, h_n = S5_T, S5_GROUPS, S5_STATE, S5_GROUP
    lam = lax.complex(lam_re, lam_im)
    dt = jnp.exp(log_step)[..., None]
    lam_bar = jnp.exp(lam * dt)
    b_bar = ((lam_bar - 1.0) / lam)[..., None] * lax.complex(b_re, b_im)
    cm = lax.complex(c_re, c_im)
    ks = jnp.arange(t_len + 1, dtype=F32)
    pw = jnp.exp((lam * dt)[..., None] * ks)
    taps = jnp.einsum('dghp,dgpt,dgpk->dgthk', cm, pw[..., :t_len], b_bar).real
    tt = np.arange(t_len)
    diff = tt[:, None] - tt[None, :]
    ktoep = []
    for direction in range(2):
        lag = diff if direction == 0 else -diff
        valid = jnp.asarray(lag >= 0)
        blk = taps[direction][:, np.clip(lag, 0, t_len - 1)]
        blk = jnp.where(valid[None, :, :, None, None], blk, 0.0)
        ktoep.append(blk.transpose(0, 1, 3, 2, 4).reshape(g_n, t_len * h_n, t_len * h_n))
    ktoep = jnp.stack(ktoep)
    pw_in = jnp.stack([pw[0][..., t_len - 1 - tt], pw[1][..., tt]])
    bpow = pw_in[..., None] * b_bar[..., None, :]
    bpow = bpow.reshape(2, g_n // 2, 2, p_n, t_len * h_n)
    eye2 = jnp.eye(2, dtype=F32)
    def pair_rows(m):
        return jnp.einsum('dqepc,ef->dqepfc', m, eye2).reshape(2, g_n // 2, 2 * p_n, 2 * t_len * h_n)
    b_pair_re, b_pair_im = pair_rows(bpow.real), pair_rows(bpow.imag)
    pw_out = jnp.stack([pw[0][..., tt + 1], pw[1][..., t_len - tt]])
    cpow = cm[:, :, None, :, :] * pw_out.transpose(0, 1, 3, 2)[:, :, :, None, :]
    cpow = cpow.reshape(2, g_n // 2, 2, t_len * h_n, p_n)
    def pair_cols(m):
        return jnp.einsum('dqerp,ef->dqerfp', m, eye2).reshape(2, g_n, t_len * h_n, 2 * p_n)
    cpad = jnp.concatenate([pair_cols(cpow.real), -pair_cols(cpow.imag)], axis=-1)
    lam_t = pw[..., t_len].reshape(2, 1, g_n * p_n)
    return (_bf(ktoep), _bf(b_pair_re), _bf(b_pair_im), _bf(cpad), lam_t.real, lam_t.imag)


def _s5_in_kernel(u_ref, uc_ref, o_ref):
    j = pl.program_id(0)
    last = pl.num_programs(0) - 1

    def emit(slab_of_t):
        for t in range(S5_T):
            o_ref[:, t * S5_GROUP:(t + 1) * S5_GROUP, :] = _bf(
                slab_of_t(t).T.reshape(S5_GROUPS, S5_GROUP, S5_CB))

    @pl.when(j < last)
    def _():
        emit(lambda t: u_ref[0, :, t, :])

    @pl.when(j == last)
    def _():
        pad = jnp.zeros((S5_CB - 2 * uc_ref.shape[1], S5_WIDTH), F32)
        emit(lambda t: jnp.concatenate([uc_ref[0, :, t, :], uc_ref[1, :, t, :], pad], axis=0))


def _s5_lat_index(n_half):
    def index(j):
        jj = jnp.minimum(j, 2 * n_half - 1)
        return jj // n_half, jj % n_half
    return index


def _s5_relayout_in(p, pc):
    bsz, seq, n = p.shape
    seq_c, n_c = pc.shape[1], pc.shape[2]
    n_lat, n_ctx = seq // S5_T, seq_c // S5_T
    n_half = n_lat // S5_CB
    idx = _s5_lat_index(n_half)
    return pl.pallas_call(
        _s5_in_kernel,
        grid=(bsz * n_half + 1,),
        in_specs=[
            pl.BlockSpec((1, S5_CB, S5_T, 512), lambda j: (*idx(j), 0, COL_S5U)),
            pl.BlockSpec((bsz, n_ctx, S5_T, 512), lambda j: (0, 0, 0, COL_S5U)),
        ],
        out_specs=pl.BlockSpec((S5_GROUPS, S5_T * S5_GROUP, S5_CB), lambda j: (0, 0, j)),
        out_shape=jax.ShapeDtypeStruct((S5_GROUPS, S5_T * S5_GROUP, S5_COLS), BF16),
        compiler_params=_cparams("arbitrary"),
        name="s5_relayout_in",
    )(p.reshape(bsz, n_lat, S5_T, n), pc.reshape(bsz, n_ctx, S5_T, n_c))


def _s5_state_in_kernel(ug_ref, bre_ref, bim_ref, vre_ref, vim_ref):
    u = ug_ref[...].reshape(2 * S5_T * S5_GROUP, S5_COLS)
    for direction in range(2):
        vre_ref[direction] = _mm(bre_ref[direction, 0], u)
        vim_ref[direction] = _mm(bim_ref[direction, 0], u)


def _s5_state_in(ug, b_pair_re, b_pair_im):
    pairs = S5_GROUPS // 2
    rows = S5_T * S5_GROUP
    lanes = S5_GROUPS * S5_STATE
    out = jax.ShapeDtypeStruct((2, lanes, S5_COLS), F32)
    return pl.pallas_call(
        _s5_state_in_kernel,
        grid=(pairs,),
        in_specs=[
            pl.BlockSpec((2, rows, S5_COLS), lambda q: (q, 0, 0)),
            pl.BlockSpec((2, 1, 128, 2 * rows), lambda q: (0, q, 0, 0)),
            pl.BlockSpec((2, 1, 128, 2 * rows), lambda q: (0, q, 0, 0)),
        ],
        out_specs=[pl.BlockSpec((2, 128, S5_COLS), lambda q: (0, q, 0))] * 2,
        out_shape=[out, out],
        compiler_params=_cparams("parallel"),
        name="s5_state_in",
    )(ug, b_pair_re, b_pair_im)


def _s5_scan_kernel(vre_ref, vim_ref, lre_ref, lim_ref, xre_ref, xim_ref, tre_scr, tim_scr, *, n_lat, n_ctx):
    direction = pl.program_id(0)
    a_re = lre_ref[0]
    a_im = lim_ref[0]
    lanes = a_re.shape[-1]
    tre_scr[...] = vre_ref[0].T
    tim_scr[...] = vim_ref[0].T

    def step(row, xr, xi):
        sl = pl.ds(row, 1)
        xre_ref[0, sl, :] = xr
        xim_ref[0, sl, :] = xi
        return (a_re * xr - a_im * xi + tre_scr[sl, :],
                a_re * xi + a_im * xr + tim_scr[sl, :])

    def run(base0, base1, n, carry):
        def body(i, carry):
            j = i + direction * (n - 1 - 2 * i)
            x0r, x0i, x1r, x1i = carry
            x0r, x0i = step(base0 + j, x0r, x0i)
            x1r, x1i = step(base1 + j, x1r, x1i)
            return x0r, x0i, x1r, x1i
        return lax.fori_loop(0, n, body, carry)

    zero = jnp.zeros((1, lanes), F32)
    carry = run(2 * n_lat, 2 * n_lat + n_ctx, n_ctx, (zero, zero, zero, zero))
    run(0, n_lat, n_lat, carry)
    pad0 = 2 * n_lat + 2 * n_ctx
    xre_ref[0, pad0:, :] = jnp.zeros((S5_COLS - pad0, lanes), F32)
    xim_ref[0, pad0:, :] = jnp.zeros((S5_COLS - pad0, lanes), F32)


def _s5_scan(vre, vim, lam_re, lam_im, *, n_lat, n_ctx):
    lanes = vre.shape[1]
    tl = 512
    vspec = pl.BlockSpec((1, tl, S5_COLS), lambda d, j: (d, j, 0))
    xspec = pl.BlockSpec((1, S5_COLS, tl), lambda d, j: (d, 0, j))
    lspec = pl.BlockSpec((1, 1, tl), lambda d, j: (d, 0, j))
    out = jax.ShapeDtypeStruct((2, S5_COLS, lanes), F32)
    return pl.pallas_call(
        functools.partial(_s5_scan_kernel, n_lat=n_lat, n_ctx=n_ctx),
        grid=(2, lanes // tl),
        in_specs=[vspec, vspec, lspec, lspec],
        out_specs=[xspec, xspec],
        out_shape=[out, out],
        scratch_shapes=[pltpu.VMEM((S5_COLS, tl), F32), pltpu.VMEM((S5_COLS, tl), F32)],
        compiler_params=_cparams("parallel", "parallel"),
        name="s5_scan",
    )(vre, vim, lam_re, lam_im)


def _s5_readout_kernel(ug_ref, kt_ref, cp_ref, xre_ref, xim_ref, y_ref):
    for e in range(2):
        u = ug_ref[e]
        acc = None
        for direction in range(2):
            xcat = _bf(jnp.concatenate([xre_ref[direction], xim_ref[direction]], axis=1))
            term = _mm(kt_ref[direction, e], u) + _mm(cp_ref[direction, e], xcat, _NT)
            acc = term if acc is None else acc + term
        y_ref[e] = acc


def _s5_readout(ug, ktoep, cpad, xre, xim):
    pairs = S5_GROUPS // 2
    rows = S5_T * S5_GROUP
    return pl.pallas_call(
        _s5_readout_kernel,
        grid=(pairs,),
        in_specs=[
            pl.BlockSpec((2, rows, S5_COLS), lambda q: (q, 0, 0)),
            pl.BlockSpec((2, 2, rows, rows), lambda q: (0, q, 0, 0)),
            pl.BlockSpec((2, 2, rows, rows), lambda q: (0, q, 0, 0)),
            pl.BlockSpec((2, S5_COLS, 128), lambda q: (0, 0, q)),
            pl.BlockSpec((2, S5_COLS, 128), lambda q: (0, 0, q)),
        ],
        out_specs=pl.BlockSpec((2, rows, S5_COLS), lambda q: (q, 0, 0)),
        out_shape=jax.ShapeDtypeStruct((S5_GROUPS, rows, S5_COLS), F32),
        compiler_params=_cparams("parallel"),
        name="s5_readout",
    )(ug, ktoep, cpad, xre, xim)


def _s5_out_kernel(*refs, ctx_out):
    if ctx_out:
        yg_ref, u_ref, g_ref, uc_ref, gc_ref, d_ref, w_ref, b_ref, o_ref, oc_ref = refs
    else:
        yg_ref, u_ref, g_ref, d_ref, w_ref, b_ref, o_ref = refs
    d = d_ref[...]
    w = w_ref[...]
    bias = b_ref[...]

    def finish(yy, u, gate):
        yy = yy + d * u
        yg = jax.nn.gelu(yy)
        out = yg * jax.nn.sigmoid(_mm(_bf(yg), w) + bias)
        return out * (gate * jax.nn.sigmoid(gate))

    def y_of(t):
        return yg_ref[:, t * S5_GROUP:(t + 1) * S5_GROUP, :].reshape(S5_WIDTH, S5_CB).T

    def lat():
        for t in range(S5_T):
            o_ref[0, :, t, :] = finish(y_of(t), u_ref[0, :, t, :], g_ref[0, :, t, :])

    if not ctx_out:
        lat()
        return
    j = pl.program_id(0)
    last = pl.num_programs(0) - 1
    pl.when(j < last)(lat)

    @pl.when(j == last)
    def _():
        n_ctx = uc_ref.shape[1]
        for t in range(S5_T):
            y = y_of(t)
            for b in range(2):
                oc_ref[b, :, t, :] = finish(y[b * n_ctx:(b + 1) * n_ctx], uc_ref[b, :, t, :], gc_ref[b, :, t, :])


def _s5_relayout_out(yg, p, pc, d_skip, w_glu, b_glu, *, ctx_out):
    bsz, seq, n = p.shape
    seq_c, n_c = pc.shape[1], pc.shape[2]
    n_lat, n_ctx = seq // S5_T, seq_c // S5_T
    n_half = n_lat // S5_CB
    idx = _s5_lat_index(n_half)
    pv = p.reshape(bsz, n_lat, S5_T, n)
    lat_block = (1, S5_CB, S5_T, 512)
    in_specs = [
        pl.BlockSpec((S5_GROUPS, S5_T * S5_GROUP, S5_CB), lambda j: (0, 0, j)),
        pl.BlockSpec(lat_block, lambda j: (*idx(j), 0, COL_S5U)),
        pl.BlockSpec(lat_block, lambda j: (*idx(j), 0, COL_S5G)),
    ]
    args = [yg, pv, pv]
    out_specs = [pl.BlockSpec(lat_block, lambda j: (*idx(j), 0, 0))]
    out_shape = [jax.ShapeDtypeStruct((bsz, n_lat, S5_T, S5_WIDTH), F32)]
    if ctx_out:
        pcv = pc.reshape(bsz, n_ctx, S5_T, n_c)
        ctx_block = (bsz, n_ctx, S5_T, 512)
        in_specs += [
            pl.BlockSpec(ctx_block, lambda j: (0, 0, 0, COL_S5U)),
            pl.BlockSpec(ctx_block, lambda j: (0, 0, 0, COL_S5G)),
        ]
        args += [pcv, pcv]
        out_specs.append(pl.BlockSpec(ctx_block, lambda j: (0, 0, 0, 0)))
        out_shape.append(jax.ShapeDtypeStruct((bsz, n_ctx, S5_T, S5_WIDTH), F32))
    in_specs += [
        pl.BlockSpec((1, S5_WIDTH), lambda j: (0, 0)),
        pl.BlockSpec((S5_WIDTH, S5_WIDTH), lambda j: (0, 0)),
        pl.BlockSpec((1, S5_WIDTH), lambda j: (0, 0)),
    ]
    args += [d_skip, w_glu, b_glu]
    outs = pl.pallas_call(
        functools.partial(_s5_out_kernel, ctx_out=ctx_out),
        grid=(bsz * n_half + (1 if ctx_out else 0),),
        in_specs=in_specs,
        out_specs=out_specs,
        out_shape=out_shape,
        compiler_params=_cparams("arbitrary"),
        name="s5_relayout_out",
    )(*args)
    y = outs[0].reshape(bsz, seq, S5_WIDTH)
    yc = outs[1].reshape(bsz, seq_c, S5_WIDTH) if ctx_out else None
    return y, yc


def _s5(p, pc, ops, d_skip, w_glu, b_glu, *, ctx_out):
    ktoep, b_pair_re, b_pair_im, cpad, lam_re, lam_im = ops
    ug = _s5_relayout_in(p, pc)
    vre, vim = _s5_state_in(ug, b_pair_re, b_pair_im)
    xre, xim = _s5_scan(vre, vim, lam_re, lam_im, n_lat=p.shape[1] // S5_T, n_ctx=pc.shape[1] // S5_T)
    yg = _s5_readout(ug, ktoep, cpad, xre, xim)
    return _s5_relayout_out(yg, p, pc, d_skip, w_glu, b_glu, ctx_out=ctx_out)


def _dft_tables():
    n1, n2 = FFT_N1, FFT_N2
    n = n1 * n2
    a = jnp.arange(n1, dtype=jnp.int32)
    ph1 = (a[:, None] * a[None, :]) % n1
    ang1 = ph1.astype(F32) * (2.0 * math.pi / n1)
    c1, s1 = jnp.cos(ang1), -jnp.sin(ang1)
    h = n1 // 2
    m1 = jnp.concatenate([jnp.concatenate([c1[:, :h], -s1[:, :h]], 1),
                          jnp.concatenate([s1[:, :h], c1[:, :h]], 1)], 0)
    m1_real = jnp.concatenate([c1, s1], 0)
    m3 = m1.T / n
    k1 = jnp.arange(n1, dtype=jnp.int32)[:, None, None]
    k2 = jnp.arange(n2, dtype=jnp.int32)[None, :, None]
    m2 = jnp.arange(n2, dtype=jnp.int32)[None, None, :]
    ph = (n1 * m2 * k2 + m2 * k1) % n
    ang = ph.astype(F32) * (2.0 * math.pi / n)
    gr, gi = jnp.cos(ang), -jnp.sin(ang)
    g = jnp.concatenate([jnp.concatenate([gr, -gi], 2), jnp.concatenate([gi, gr], 2)], 1)
    gt = jnp.swapaxes(g, 1, 2)
    return tuple(_split(t) for t in (m1, m1_real, m3, g, gt))


def _filter_features(length):
    t = jnp.linspace(0.0, 1.0, length, dtype=F32)[:, None]
    ang = (2.0 * math.pi / length) * jnp.arange(length, dtype=F32)[:, None]
    bands = jnp.linspace(1e-4, HY_BANDS - 1, HY_BANDS, dtype=F32)[None, :]
    feats = jnp.concatenate([t, jnp.cos(bands * ang), -jnp.sin(bands * ang)], axis=-1)
    feats = jnp.pad(feats, ((0, 0), (0, 128 - HY_EMB)))
    rev = jnp.roll(feats[::-1], 1, axis=0)
    return jnp.stack([feats, rev])


def _filter_kernel(f_ref, w1_ref, b1_ref, f1_ref, w2_ref, b2_ref, f2_ref, w3_ref, dl_ref, o_ref):
    half = pl.program_id(1)
    tile = pl.program_id(2)
    x = f_ref[0]
    x_hi, x_lo = _split(x)
    h = jnp.sin(f1_ref[0] * (_mm3(x_hi, x_lo, w1_ref[0]) + b1_ref[0]))
    h_hi, h_lo = _split(h)
    h = jnp.sin(f2_ref[0] * (_mm3(h_hi, h_lo, w2_ref[0]) + b2_ref[0]))
    h_hi, h_lo = _split(h)
    y = _mm3(h_hi, h_lo, w3_ref[0, 0]) * jnp.exp(-x[:, 0:1] * dl_ref[...])
    rows = lax.broadcasted_iota(jnp.int32, y.shape, 0)
    drop = jnp.logical_and(jnp.logical_and(half == 1, tile == 0), rows == 0)
    o_ref[0] = jnp.where(drop, 0.0, y)


def _filter_taps(feats, w1, b1, f1, w2, b2, f2, w3, deltas, tmf):
    depth = w1.shape[0]
    length = feats.shape[1]
    nt = length // tmf
    wide = w3.shape[-1]
    vec = pl.BlockSpec((1, 1, 128), lambda l, s, i: (l, 0, 0))
    sq = pl.BlockSpec((1, 128, 128), lambda l, s, i: (l, 0, 0))
    return pl.pallas_call(
        _filter_kernel,
        grid=(depth, 2, nt),
        in_specs=[
            pl.BlockSpec((1, tmf, 128), lambda l, s, i: (s, i, 0)),
            sq, vec, vec, sq, vec, vec,
            pl.BlockSpec((1, 1, 128, wide), lambda l, s, i: (l, s, 0, 0)),
            pl.BlockSpec((1, wide), lambda l, s, i: (0, 0)),
        ],
        out_specs=pl.BlockSpec((1, tmf, wide), lambda l, s, i: (l, s * nt + i, 0)),
        out_shape=jax.ShapeDtypeStruct((depth, 2 * length, wide), F32),
        compiler_params=_cparams("parallel", "parallel", "parallel"),
        name="hyena_filter",
    )(feats, w1, b1, f1, w2, b2, f2, w3, deltas)


def _dft_stage1(read_rows, m_hi, m_lo, a_scr):
    blk = 2 * FFT_N1

    def body(n2, carry):
        a_scr[pl.ds(pl.multiple_of(n2 * blk, blk), blk), :] = _mm3(m_hi, m_lo, read_rows(n2))
        return carry

    lax.fori_loop(0, FFT_N2, body, 0)


def _stage2_rows(k1):
    return (pl.ds(k1, FFT_N2, stride=2 * FFT_N1), pl.ds(k1 + FFT_N1, FFT_N2, stride=2 * FFT_N1))


def _spectrum_kernel(mh_ref, ml_ref, gh_ref, gl_ref, t_ref, o_ref, a_scr):
    kb = pl.program_id(2)
    ct = t_ref.shape[-1]

    @pl.when(kb == 0)
    def _():
        _dft_stage1(lambda n2: t_ref[0, pl.ds(n2, FFT_N1, stride=FFT_N2), :], mh_ref[...], ml_ref[...], a_scr)

    for i in range(HY_KB):
        re, im = _stage2_rows(kb * HY_KB + i)
        a = jnp.concatenate([a_scr[re, :], a_scr[im, :]], axis=0)
        o_ref[0, i] = _mm3(gh_ref[i], gl_ref[i], a).reshape(2, FFT_N2, ct)


def _filter_spectrum(tables, taps):
    _, (m_hi, m_lo), _, (g_hi, g_lo), _ = tables
    depth, n, ch = taps.shape
    ct = HY_CT
    gspec = pl.BlockSpec((HY_KB, 2 * FFT_N2, 2 * FFT_N2), lambda l, c, k: (k, 0, 0))
    mspec = pl.BlockSpec(m_hi.shape, lambda l, c, k: (0, 0))
    return pl.pallas_call(
        _spectrum_kernel,
        grid=(depth, ch // ct, FFT_N1 // HY_KB),
        in_specs=[mspec, mspec, gspec, gspec, pl.BlockSpec((1, n, ct), lambda l, c, k: (l, 0, c))],
        out_specs=pl.BlockSpec((1, HY_KB, 2, FFT_N2, ct), lambda l, c, k: (l, k, 0, 0, c)),
        out_shape=jax.ShapeDtypeStruct((depth, FFT_N1, 2, FFT_N2, ch), F32),
        scratch_shapes=[pltpu.VMEM((2 * FFT_N1 * FFT_N2, ct), F32)],
        compiler_params=_cparams("parallel", "parallel", "arbitrary"),
        name="hyena_spectrum",
    )(m_hi, m_lo, g_hi, g_lo, taps)


def _hy_pre_kernel(pv_ref, p1_ref, p2_ref, pg_ref, wv_ref, w1_ref, w2_ref, bv_ref, b1_ref, b2_ref,
                   v_ref, x1_ref, x2_ref):
    n = pv_ref.shape[1]
    rows = lax.broadcasted_iota(jnp.int32, (n, 128), 0)

    def conv(p, w_ref, b_ref):
        up = jnp.where(rows == 0, 0.0, pltpu.roll(p, 1, 0))
        dn = jnp.where(rows == n - 1, 0.0, pltpu.roll(p, n - 1, 0))
        return up * w_ref[0:1, :] + p * w_ref[1:2, :] + dn * w_ref[2:3, :] + b_ref[...]

    v_ref[0] = conv(pv_ref[0], wv_ref, bv_ref)
    x1_ref[0] = conv(p1_ref[0], w1_ref, b1_ref)
    gate = pg_ref[0]
    x2_ref[0] = conv(p2_ref[0], w2_ref, b2_ref) * (gate * jax.nn.sigmoid(gate))


def _hy_pre(p, conv_w, conv_b):
    bsz, seq, _ = p.shape
    def pspec(off):
        return pl.BlockSpec((1, seq, 128), lambda b, j: (b, 0, off + j))
    def wspec(off):
        return pl.BlockSpec((3, 128), lambda b, j: (0, off + j))
    def bspec(off):
        return pl.BlockSpec((1, 128), lambda b, j: (0, off + j))
    out = jax.ShapeDtypeStruct((bsz, seq, HY_WIDTH), F32)
    ospec = pl.BlockSpec((1, seq, 128), lambda b, j: (b, 0, j))
    return pl.pallas_call(
        _hy_pre_kernel,
        grid=(bsz, HY_WIDTH // 128),
        in_specs=[pspec(32), pspec(36), pspec(40), pspec(44),
                  wspec(0), wspec(4), wspec(8), bspec(0), bspec(4), bspec(8)],
        out_specs=[ospec, ospec, ospec],
        out_shape=[out, out, out],
        compiler_params=_cparams("parallel", "parallel"),
        name="hyena_pre",
    )(p, p, p, p, conv_w, conv_w, conv_w, conv_b, conv_b, conv_b)


def _hy_conv_kernel(m1h_ref, m1l_ref, m3h_ref, m3l_ref, gh_ref, gl_ref, th_ref, tl_ref,
                    u_ref, gate_ref, h_ref, d_ref, o_ref, a_scr):
    kb = pl.program_id(1)
    half = FFT_N1 // 2

    def rows(n2):
        return pl.ds(n2, half, stride=FFT_N2)

    @pl.when(kb == 0)
    def _():
        _dft_stage1(lambda n2: jnp.concatenate([u_ref[0, rows(n2), :], u_ref[1, rows(n2), :]], axis=0),
                    m1h_ref[...], m1l_ref[...], a_scr)

    for i in range(HY_KB):
        re, im = _stage2_rows(kb * HY_KB + i)
        a = jnp.concatenate([a_scr[re, :], a_scr[im, :]], axis=0)
        z = _mm3(gh_ref[i], gl_ref[i], a)
        zr, zi = z[:FFT_N2], z[FFT_N2:]
        hr, hi = h_ref[0, i, 0], h_ref[0, i, 1]
        z2 = jnp.concatenate([zr * hr - zi * hi, zr * hi + zi * hr], axis=0)
        a2 = _mm3(th_ref[i], tl_ref[i], z2)
        a_scr[re, :] = a2[:FFT_N2]
        a_scr[im, :] = a2[FFT_N2:]

    @pl.when(kb == pl.num_programs(1) - 1)
    def _():
        blk = 2 * FFT_N1

        def body(n2, carry):
            y = _mm3(m3h_ref[...], m3l_ref[...], a_scr[pl.ds(pl.multiple_of(n2 * blk, blk), blk), :])
            for b in range(2):
                r = rows(n2)
                o_ref[b, r, :] = gate_ref[b, r, :] * (y[b * half:(b + 1) * half] + d_ref[...] * u_ref[b, r, :])
            return carry

        lax.fori_loop(0, FFT_N2, body, 0)


def _hy_long_conv(tables, spec, layer, order, u, gate, d):
    (m1h, m1l), _, (m3h, m3l), (g_hi, g_lo), (t_hi, t_lo) = tables
    bsz, seq, ch = u.shape
    ct = HY_CT
    gspec = pl.BlockSpec((HY_KB, 2 * FFT_N2, 2 * FFT_N2), lambda c, k: (k, 0, 0))
    full = lambda t: pl.BlockSpec(t.shape, lambda c, k: (0, 0))
    uspec = pl.BlockSpec((bsz, seq, ct), lambda c, k: (0, 0, c))
    n_ct = ch // ct
    return pl.pallas_call(
        _hy_conv_kernel,
        grid=(n_ct, FFT_N1 // HY_KB),
        in_specs=[full(m1h), full(m1l), full(m3h), full(m3l), gspec, gspec, gspec, gspec, uspec, uspec,
                  pl.BlockSpec((1, HY_KB, 2, FFT_N2, ct), lambda c, k: (layer, k, 0, 0, order * n_ct + c)),
                  pl.BlockSpec((1, ct), lambda c, k: (0, c))],
        out_specs=uspec,
        out_shape=jax.ShapeDtypeStruct(u.shape, F32),
        scratch_shapes=[pltpu.VMEM((2 * FFT_N1 * FFT_N2, ct), F32)],
        compiler_params=_cparams("parallel", "arbitrary"),
        name="hyena_conv",
    )(m1h, m1l, m3h, m3l, g_hi, g_lo, t_hi, t_lo, u, gate, spec, d.reshape(1, ch))


def _ctx_dft_tables(length):
    n = 2 * length
    k = jnp.arange(n, dtype=jnp.int32)
    ang = ((k[:, None] * k[None, :]) % n).astype(F32) * (2.0 * math.pi / n)
    fr, fi = jnp.cos(ang), -jnp.sin(ang)
    fwd = jnp.concatenate([jnp.concatenate([fr[:, :length], -fi[:, :length]], 1),
                           jnp.concatenate([fi[:, :length], fr[:, :length]], 1)], 0)
    real = jnp.concatenate([fr, fi], 0)
    inv = fwd.T / n
    return tuple(_split(t) for t in (fwd, real, inv))


def _hyc_kernel(fh_ref, fl_ref, rh_ref, rl_ref, ih_ref, il_ref, taps_ref, u_ref, g_ref, d_ref, o_ref):
    n = u_ref.shape[1]
    x = jnp.concatenate([u_ref[0], u_ref[1]], axis=0)
    z = _mm3(fh_ref[...], fl_ref[...], x)
    h = _mm3(rh_ref[...], rl_ref[...], taps_ref[0])
    m = 2 * n
    zr, zi, hr, hi = z[:m], z[m:], h[:m], h[m:]
    z2 = jnp.concatenate([zr * hr - zi * hi, zr * hi + zi * hr], axis=0)
    y = _mm3(ih_ref[...], il_ref[...], z2)
    for b in range(2):
        o_ref[b] = g_ref[b] * (y[b * n:(b + 1) * n] + d_ref[...] * u_ref[b])


def _hyc_long_conv(ctabs, taps, layer, order, u, gate, d):
    (f_hi, f_lo), (r_hi, r_lo), (i_hi, i_lo) = ctabs
    bsz, n, ch = u.shape
    full = lambda t: pl.BlockSpec(t.shape, lambda i: (0,) * t.ndim)
    uspec = pl.BlockSpec((bsz, n, ch), lambda i: (0, 0, 0))
    return pl.pallas_call(
        _hyc_kernel,
        grid=(1,),
        in_specs=[full(f_hi), full(f_lo), full(r_hi), full(r_lo), full(i_hi), full(i_lo),
                  pl.BlockSpec((1, 2 * n, ch), lambda i: (layer, 0, order)),
                  uspec, uspec, pl.BlockSpec((1, ch), lambda i: (0, 0))],
        out_specs=uspec,
        out_shape=jax.ShapeDtypeStruct((bsz, n, ch), F32),
        compiler_params=_cparams("arbitrary"),
        name="hyena_ctx",
    )(f_hi, f_lo, r_hi, r_lo, i_hi, i_lo, taps, u, gate, d.reshape(1, ch))


def _outproj_kernel(x_ref, ya_ref, yb_ref, yc_ref, w_ref, gp_ref, gate_ref, o_ref, *, colmajor, tm):
    na, nb = ya_ref.shape[-1], yb_ref.shape[-1]
    acc = (_mm(_bf(ya_ref[0]), w_ref[0, 0:na, :]) + _mm(_bf(yb_ref[0]), w_ref[0, na:na + nb, :])
           + _mm(_bf(yc_ref[0]), w_ref[0, na + nb:, :]))
    ms = jnp.mean(acc * acc, axis=-1, keepdims=True)
    upd = gate_ref[0] * (acc * lax.rsqrt(ms + EPS) * gp_ref[...])
    if colmajor:
        for j in range(tm // GRID_W):
            o_ref[0, :, j, :] = x_ref[0, :, j, :] + upd[j * GRID_W:(j + 1) * GRID_W]
    else:
        o_ref[0] = x_ref[0] + upd


def _outproj(x, ya, yb, yc, w_out, layer, g_post, gate, *, colmajor, tm):
    bsz, seq, d = x.shape
    if colmajor:
        x_in = x.reshape(bsz, GRID_W, seq // GRID_W, d)
        x_spec = pl.BlockSpec((1, GRID_W, tm // GRID_W, d), lambda b, m: (b, 0, m, 0))
    else:
        x_in = x
        x_spec = pl.BlockSpec((1, tm, d), lambda b, m: (b, m, 0))
    def yspec(t):
        return pl.BlockSpec((1, tm, t.shape[-1]), lambda b, m: (b, m, 0))
    out = pl.pallas_call(
        functools.partial(_outproj_kernel, colmajor=colmajor, tm=tm),
        grid=(bsz, seq // tm),
        in_specs=[x_spec, yspec(ya), yspec(yb), yspec(yc),
                  pl.BlockSpec((1,) + w_out.shape[1:], lambda b, m: (layer, 0, 0)),
                  pl.BlockSpec((1, d), lambda b, m: (0, 0)),
                  pl.BlockSpec((1, 1, d), lambda b, m: (b, 0, 0))],
        out_specs=x_spec,
        out_shape=jax.ShapeDtypeStruct(x_in.shape, F32),
        compiler_params=_cparams("parallel", "parallel"),
        name="outproj",
    )(x_in, ya, yb, yc, w_out, g_post, gate)
    return out.reshape(bsz, seq, d)


def kernel(x, c, ctx, c_ctx, w_mod, b_mod, g_pre, g_post, w_in, w_out, gla_w_gate, gla_b_gate, gla_norm, s5_lam_re, s5_lam_im, s5_log_step, s5_b_re, s5_b_im, s5_c_re, s5_c_im, s5_d, s5_w_glu, s5_b_glu, hy_conv_w, hy_conv_b, hy_w1, hy_b1, hy_f1, hy_w2, hy_b2, hy_f2, hy_w3, hy_d):
    bsz, seq, d = x.shape
    seq_c = ctx.shape[1]
    depth = w_in.shape[0]
    assert (bsz, d, seq // GRID_W) == (2, D_MODEL, GRID_W) and seq == FFT_N1 * FFT_N2 // 2

    cvec = jnp.concatenate([c, c_ctx[None], jnp.zeros((8 - bsz - 1, d), F32)], axis=0)
    mod = _modulation(cvec, w_mod, b_mod)
    shift, scale, gate = mod[..., :d], mod[..., d:2 * d], mod[..., 2 * d:]

    w_main = _bf(jnp.concatenate([w_in[..., :2048], w_in[..., 2080:]], axis=-1))
    w_lr = _bf(jnp.pad(w_in[..., 2048:2080], ((0, 0), (0, 0), (0, 96))))
    w_out_b = _bf(w_out)
    w_glu_b = _bf(s5_w_glu)

    wg = gla_w_gate.reshape(depth, 2, 16, GLA_HEADS, GLA_DK).transpose(0, 1, 3, 2, 4)
    wg_pad = jnp.stack([jnp.pad(wg[:, 0], ((0, 0), (0, 0), (0, 112), (0, 0))),
                        jnp.pad(wg[:, 1], ((0, 0), (0, 0), (16, 96), (0, 0)))], axis=1)
    wg_hi, wg_lo = _split(wg_pad)
    bg = gla_b_gate.reshape(depth, 2, GLA_HEADS, 1, GLA_DK)

    tables = _dft_tables()
    deltas = jnp.abs(jnp.linspace(HY_MIN_DECAY, HY_MAX_DECAY, HY_WIDTH, dtype=F32))
    deltas2 = jnp.tile(deltas, 2).reshape(1, 2 * HY_WIDTH)
    pad_k = lambda w: jnp.pad(w, ((0, 0), (0, 128 - w.shape[1]), (0, 128 - w.shape[2])))
    pad_v = lambda v: jnp.pad(v, ((0, 0), (0, 128 - v.shape[1]))).reshape(depth, 1, 128)
    w3 = hy_w3.reshape(depth, HY_FFN, 2, 2, HY_WIDTH).transpose(0, 3, 1, 2, 4)
    w3 = jnp.pad(w3.reshape(depth, 2, HY_FFN, 2 * HY_WIDTH), ((0, 0), (0, 0), (0, 128 - HY_FFN), (0, 0)))
    filt_w = (pad_k(hy_w1), pad_v(hy_b1), pad_v(hy_f1), pad_k(hy_w2), pad_v(hy_b2), pad_v(hy_f2), w3, deltas2)
    taps = _filter_taps(_filter_features(seq), *filt_w, tmf=512)
    spec = _filter_spectrum(tables, taps)
    taps_c = _filter_taps(_filter_features(seq_c), *filt_w, tmf=seq_c)
    ctabs = _ctx_dft_tables(seq_c)

    xc = ctx
    for l in range(depth):
        last = l == depth - 1
        ctx_out = not last
        colmajor = l % 2 == 1
        pre = g_pre[l][None, :] * (1.0 + scale[l])
        sc_l, sh_l = pre[:bsz, None, :], shift[l, :bsz, None, :]
        sc_c = jnp.broadcast_to(pre[bsz][None, None, :], (bsz, 1, d))
        sh_c = jnp.broadcast_to(shift[l, bsz][None, None, :], (bsz, 1, d))
        p, lr = _inproj(x, sc_l, sh_l, w_main, w_lr, l, n=N_MAIN, colmajor=colmajor, tm=512, tn=1024)
        pc, lrc = _inproj(xc, sc_c, sh_c, w_main, w_lr, l, n=N_MAIN if ctx_out else N_STATE,
                          colmajor=False, tm=seq_c, tn=512)

        y_gla, yc_gla = _gla(p, lr, pc, lrc, wg_hi[l], wg_lo[l], bg[l], gla_norm[l][None, :], ctx_out=ctx_out)

        ops = _s5_operators(s5_lam_re[l], s5_lam_im[l], s5_log_step[l], s5_b_re[l], s5_b_im[l],
                            s5_c_re[l], s5_c_im[l])
        y_s5, yc_s5 = _s5(p, pc, ops, s5_d[l][None, :], w_glu_b[l], s5_b_glu[l][None, :], ctx_out=ctx_out)

        v, x1, x2 = _hy_pre(p, hy_conv_w[l], hy_conv_b[l][None, :])
        z = _hy_long_conv(tables, spec, l, 0, v, x1, hy_d[l, 0])
        y_hy = _hy_long_conv(tables, spec, l, 1, z, x2, hy_d[l, 1])

        x_new = _outproj(x, y_gla, y_s5, y_hy, w_out_b, l, g_post[l][None, :], gate[l, :bsz, None, :],
                         colmajor=colmajor, tm=512)
        if ctx_out:
            vc, x1c, x2c = _hy_pre(pc, hy_conv_w[l], hy_conv_b[l][None, :])
            zc = _hyc_long_conv(ctabs, taps_c, l, 0, vc, x1c, hy_d[l, 0])
            yc_hy = _hyc_long_conv(ctabs, taps_c, l, 1, zc, x2c, hy_d[l, 1])
            gate_c = jnp.broadcast_to(gate[l, bsz][None, None, :], (bsz, 1, d))
            xc = _outproj(xc, yc_gla, yc_s5, yc_hy, w_out_b, l, g_post[l][None, :], gate_c,
                          colmajor=False, tm=seq_c)
        x = x_new
    return x
```

```python
import functools
import math

import numpy as np
import jax
import jax.numpy as jnp
from jax import lax
from jax.experimental import pallas as pl
from jax.experimental.pallas import tpu as pltpu

F32 = jnp.float32
BF16 = jnp.bfloat16

D_MODEL = 2048
GRID_W = 64
EPS = 1e-6

GLA_HEADS = 4
GLA_DK = 128
GLA_DV = 256
GLA_TAU = 16.0
GLA_CHUNK = 64
GLA_BLOCK = 8

S5_WIDTH = 512
S5_GROUP = 16
S5_GROUPS = 32
S5_STATE = 64
S5_T = 16
S5_COLS = 640
S5_CB = 128

HY_WIDTH = 512
HY_EMB = 33
HY_BANDS = 16
HY_FFN = 64
HY_MIN_DECAY = math.log(1e-2) / 0.3
HY_MAX_DECAY = math.log(1e-2) / 1.5
FFT_N1 = 64
FFT_N2 = 128
HY_CT = 128
HY_KB = 4
HY_NB = 4
HY_PASSES = 1

N_MAIN = 6144
N_STATE = 2560
N_QKV = 2048
COL_S5U, COL_S5G = 0, 3
COL_GLA_GATE = 2
COL_HY = 16

VMEM_LIMIT_BYTES = 56 * 1024 * 1024


def _cparams(*sem):
    return pltpu.CompilerParams(dimension_semantics=sem, vmem_limit_bytes=VMEM_LIMIT_BYTES)


def _bf(x):
    return x.astype(BF16)


def _split(x):
    hi = _bf(x)
    return hi, _bf(x - hi.astype(F32))


def _split3(x):
    a = _bf(x)
    r = x - a.astype(F32)
    b = _bf(r)
    return a, b, _bf(r - b.astype(F32))


_NN = (((1,), (0,)), ((), ()))
_NT = (((1,), (1,)), ((), ()))
_TN = (((0,), (0,)), ((), ()))


def _mm(a, b, dims=_NN):
    return lax.dot_general(a, b, dims, preferred_element_type=F32)


def _mm3(a_hi, a_lo, b, dims=_NN):
    b_hi, b_lo = _split(b)
    return _mm(a_hi, b_hi, dims) + _mm(a_hi, b_lo, dims) + _mm(a_lo, b_hi, dims)


def _mod_kernel(s_ref, w_ref, b_ref, o_ref):
    s = s_ref[...]
    s = s * jax.nn.sigmoid(s)
    s_hi, s_lo = _split(s)
    o_ref[0] = _mm3(s_hi, s_lo, w_ref[0]) + b_ref[0]


def _modulation(cvec, w_mod, b_mod):
    depth, d, n = w_mod.shape
    tn = 512
    return pl.pallas_call(
        _mod_kernel,
        grid=(depth, n // tn),
        in_specs=[
            pl.BlockSpec((8, d), lambda l, j: (0, 0)),
            pl.BlockSpec((1, d, tn), lambda l, j: (l, 0, j)),
            pl.BlockSpec((1, 1, tn), lambda l, j: (l, 0, j)),
        ],
        out_specs=pl.BlockSpec((1, 8, tn), lambda l, j: (l, 0, j)),
        out_shape=jax.ShapeDtypeStruct((depth, 8, n), F32),
        compiler_params=_cparams("parallel", "parallel"),
        name="modulation",
    )(cvec, w_mod, b_mod.reshape(depth, 1, n))


def _inproj_kernel(x_ref, sc_ref, sh_ref, w_ref, wlr_ref, qkv_ref, p_ref, lr_ref, h_scr, *, colmajor, tm, tn):
    j = pl.program_id(2)

    @pl.when(j == 0)
    def _():
        def norm(xb):
            ms = jnp.mean(xb * xb, axis=-1, keepdims=True)
            return _bf(xb * lax.rsqrt(ms + EPS) * sc_ref[0] + sh_ref[0])

        if colmajor:
            for j in range(tm // GRID_W):
                h_scr[j * GRID_W:(j + 1) * GRID_W, :] = norm(x_ref[0, :, j, :])
        else:
            h_scr[...] = norm(x_ref[0])
        lr_ref[0] = _mm(h_scr[...], wlr_ref[0])

    res = _mm(h_scr[...], w_ref[0])
    n_qkv = N_QKV // tn

    @pl.when(j < n_qkv)
    def _():
        qkv_ref[0] = _bf(res)

    @pl.when(j >= n_qkv)
    def _():
        p_ref[0] = res


def _inproj(x, scale, shift, w_main, w_lr, layer, *, n, colmajor, tm, tn):
    bsz, seq, d = x.shape
    n_qkv = N_QKV // tn
    if colmajor:
        x_in = x.reshape(bsz, GRID_W, seq // GRID_W, d)
        x_spec = pl.BlockSpec((1, GRID_W, tm // GRID_W, d), lambda b, m, j: (b, 0, m, 0))
    else:
        x_in = x
        x_spec = pl.BlockSpec((1, tm, d), lambda b, m, j: (b, m, 0))
    return pl.pallas_call(
        functools.partial(_inproj_kernel, colmajor=colmajor, tm=tm, tn=tn),
        grid=(bsz, seq // tm, n // tn),
        in_specs=[
            x_spec,
            pl.BlockSpec((1, 1, d), lambda b, m, j: (b, 0, 0)),
            pl.BlockSpec((1, 1, d), lambda b, m, j: (b, 0, 0)),
            pl.BlockSpec((1, d, tn), lambda b, m, j: (layer, 0, j)),
            pl.BlockSpec((1, d, 128), lambda b, m, j: (layer, 0, 0)),
        ],
        out_specs=[
            pl.BlockSpec((1, tm, tn), lambda b, m, j: (b, m, jnp.minimum(j, n_qkv - 1))),
            pl.BlockSpec((1, tm, tn), lambda b, m, j: (b, m, jnp.maximum(j - n_qkv, 0))),
            pl.BlockSpec((1, tm, 128), lambda b, m, j: (b, m, 0)),
        ],
        out_shape=[jax.ShapeDtypeStruct((bsz, seq, N_QKV), BF16),
                   jax.ShapeDtypeStruct((bsz, seq, n - N_QKV), F32),
                   jax.ShapeDtypeStruct((bsz, seq, 128), F32)],
        scratch_shapes=[pltpu.VMEM((tm, d), BF16)],
        compiler_params=_cparams("parallel", "parallel", "arbitrary"),
        name="inproj",
    )(x_in, scale, shift, w_main, w_lr)


def _gla_block(q, k, v, lr, st, wg_hi, wg_lo, bg, direction, mask, rowc, nb):
    c = GLA_CHUNK
    n = c * nb
    lr_hi, lr_lo = _split(lr)
    z = _mm(lr_hi, wg_hi) + _mm(lr_hi, wg_lo) + _mm(lr_lo, wg_hi) + bg
    g = (jnp.minimum(z, 0.0) - jnp.log1p(jnp.exp(-jnp.abs(z)))) * (1.0 / GLA_TAU)
    cum = g
    s = 1
    while s < c:
        if direction == 0:
            cum = cum + jnp.where(rowc >= s, pltpu.roll(cum, s, 0), 0.0)
        else:
            cum = cum + jnp.where(rowc < c - s, pltpu.roll(cum, n - s, 0), 0.0)
        s *= 2
    cum3 = cum.reshape(nb, c, GLA_DK)
    tot3 = cum3[:, c - 1:c, :] if direction == 0 else cum3[:, 0:1, :]
    q3 = q.reshape(nb, c, GLA_DK)
    k3 = k.reshape(nb, c, GLA_DK)
    qg3 = _bf(q3 * jnp.exp(cum3))
    kg3 = _bf(k3 * jnp.exp(-cum3))
    kd3 = _bf(k3 * jnp.exp(tot3 - cum3))
    a3 = jnp.exp(tot3)
    vb3 = _bf(v).reshape(nb, c, GLA_DV)
    att = [jnp.where(mask, _mm(qg3[i], kg3[i], _NT), 0.0) for i in range(nb)]
    o_in = [_mm(_bf(att[i]), vb3[i]) for i in range(nb)]
    d_st = [_mm(vb3[i], kd3[i], _TN) for i in range(nb)]
    outs = [None] * nb
    for i in (range(nb) if direction == 0 else range(nb - 1, -1, -1)):
        outs[i] = o_in[i] + _mm(qg3[i], _bf(st), _NT)
        st = st * a3[i] + d_st[i]
    return jnp.concatenate(outs, axis=0), st


def _gla_kernel(*refs, ctx_out, seq, seq_c):
    if ctx_out:
        (q_ref, k_ref, v_ref, gt_ref, lr_ref, qc_ref, kc_ref, vc_ref, gtc_ref, lrc_ref,
         wgh_ref, wgl_ref, bg_ref, gn_ref, y_ref, yc_ref) = refs
    else:
        (q_ref, k_ref, v_ref, gt_ref, lr_ref, qc_ref, kc_ref, vc_ref, lrc_ref,
         wgh_ref, wgl_ref, bg_ref, gn_ref, y_ref) = refs
        gtc_ref = yc_ref = None
    c = GLA_CHUNK
    scale = GLA_DK ** -0.5
    row = lax.broadcasted_iota(jnp.int32, (c, c), 0)
    col = lax.broadcasted_iota(jnp.int32, (c, c), 1)
    masks = (col <= row, col >= row)
    gn = gn_ref[...]

    def run(refs4, n_rows, st, direction, store):
        qr, kr, vr, lrr = refs4
        nb = min(GLA_BLOCK, n_rows // c)
        n = c * nb
        rowc = lax.broadcasted_iota(jnp.int32, (n, GLA_DK), 0) % c
        wg_hi = wgh_ref[direction, 0]
        wg_lo = wgl_ref[direction, 0]
        bg = bg_ref[direction, 0]
        n_blocks = n_rows // n

        def body(i, st):
            bi = i if direction == 0 else n_blocks - 1 - i
            rows = pl.ds(pl.multiple_of(bi * n, n), n)
            q = qr[0, rows, :].astype(F32) * scale
            k = kr[0, rows, :].astype(F32)
            v = vr[0, rows, :].astype(F32)
            o, st = _gla_block(q, k, v, lrr[0, rows, :], st, wg_hi, wg_lo, bg, direction,
                               masks[direction], rowc, nb)
            store(rows, q, k, v, o)
            return st

        return lax.fori_loop(0, n_blocks, body, st)

    def store_fwd(out_ref):
        def f(rows, q, k, v, o):
            if out_ref is not None:
                out_ref[0, rows, :] = o
        return f

    def store_bwd(out_ref, gate_ref):
        def f(rows, q, k, v, o):
            if out_ref is None:
                return
            y = out_ref[0, rows, :] + o - jnp.sum(q * k, axis=-1, keepdims=True) * v
            ms = jnp.mean(y * y, axis=-1, keepdims=True)
            y = y * lax.rsqrt(ms + EPS) * gn
            gt = gate_ref[0, rows, :]
            out_ref[0, rows, :] = y * (gt * jax.nn.sigmoid(gt))
        return f

    lat = (q_ref, k_ref, v_ref, lr_ref)
    ctx = (qc_ref, kc_ref, vc_ref, lrc_ref)
    zero = jnp.zeros((GLA_DV, GLA_DK), F32)
    st = run(ctx, seq_c, zero, 0, store_fwd(yc_ref))
    run(lat, seq, st, 0, store_fwd(y_ref))
    st = run(ctx, seq_c, zero, 1, store_bwd(yc_ref, gtc_ref))
    run(lat, seq, st, 1, store_bwd(y_ref, gt_ref))


def _gla(qkv, p, lr, qkvc, pc, lrc, wg_hi, wg_lo, bg, gnorm, *, ctx_out):
    bsz, seq, _ = qkv.shape
    seq_c = qkvc.shape[1]
    h = GLA_HEADS

    def specs(n):
        return [
            pl.BlockSpec((1, n, 128), lambda b, i: (b, 0, i)),
            pl.BlockSpec((1, n, 128), lambda b, i: (b, 0, 4 + i)),
            pl.BlockSpec((1, n, 256), lambda b, i: (b, 0, 4 + i)),
            pl.BlockSpec((1, n, 256), lambda b, i: (b, 0, COL_GLA_GATE + i)),
            pl.BlockSpec((1, n, 128), lambda b, i: (b, 0, 0)),
        ]

    lat_specs = specs(seq)
    ctx_specs = specs(seq_c)
    lat_args = [qkv, qkv, qkv, p, lr]
    ctx_args = [qkvc, qkvc, qkvc, pc, lrc]
    if not ctx_out:
        del ctx_specs[3], ctx_args[3]
    w_specs = [
        pl.BlockSpec((2, 1, 128, 128), lambda b, i: (0, i, 0, 0)),
        pl.BlockSpec((2, 1, 128, 128), lambda b, i: (0, i, 0, 0)),
        pl.BlockSpec((2, 1, 1, 128), lambda b, i: (0, i, 0, 0)),
        pl.BlockSpec((1, GLA_DV), lambda b, i: (0, 0)),
    ]
    out_specs = [pl.BlockSpec((1, seq, 256), lambda b, i: (b, 0, i))]
    out_shape = [jax.ShapeDtypeStruct((bsz, seq, h * GLA_DV), F32)]
    if ctx_out:
        out_specs.append(pl.BlockSpec((1, seq_c, 256), lambda b, i: (b, 0, i)))
        out_shape.append(jax.ShapeDtypeStruct((bsz, seq_c, h * GLA_DV), F32))
    outs = pl.pallas_call(
        functools.partial(_gla_kernel, ctx_out=ctx_out, seq=seq, seq_c=seq_c),
        grid=(bsz, h),
        in_specs=lat_specs + ctx_specs + w_specs,
        out_specs=out_specs,
        out_shape=out_shape,
        compiler_params=_cparams("parallel", "parallel"),
        name="gla",
    )(*lat_args, *ctx_args, wg_hi, wg_lo, bg, gnorm)
    return (outs[0], outs[1]) if ctx_out else (outs[0], None)


def _mm3f(a, b, dims=_NN):
    a_hi, a_lo = _split(a)
    return _mm3(a_hi, a_lo, b, dims)


def _s5_ops_kernel(a_ref, b_ref, lr_ref, li_ref, btr_ref, bti_ref, cr_ref, ci_ref,
                   kt_ref, bpr_ref, bpi_ref, cp_ref, ltr_ref, lti_ref):
    t_len, h_n, p_n = S5_T, S5_GROUP, S5_STATE
    rows = t_len * h_n
    lane = lax.broadcasted_iota(jnp.int32, (rows, 2 * p_n), 1)
    own = (lane < p_n, lane >= p_n)
    sel_r = lax.broadcasted_iota(jnp.int32, (h_n, rows), 0)
    sel_c = lax.broadcasted_iota(jnp.int32, (h_n, rows), 1)
    kk = lax.broadcasted_iota(jnp.int32, (24, 2 * p_n), 0).astype(F32)
    for d in range(2):
        a, b = a_ref[0, d, 0], b_ref[0, d, 0]
        lr, li = lr_ref[0, d, 0], li_ref[0, d, 0]
        mag = jnp.exp(kk * a)
        pwr, pwi = mag * jnp.cos(kk * b), mag * jnp.sin(kk * b)

        def times_pw(k, mr, mi):
            pr, pi = pwr[k:k + 1], pwi[k:k + 1]
            return pr * mr - pi * mi, pr * mi + pi * mr

        x, y = pwr[1:2] - 1.0, pwi[1:2]
        den = lr * lr + li * li
        cfr, cfi = (x * lr + y * li) / den, (y * lr - x * li) / den
        btr, bti = btr_ref[0, d, 0], bti_ref[0, d, 0]
        bbr, bbi = cfr * btr - cfi * bti, cfr * bti + cfi * btr
        cr, ci = cr_ref[0, d, 0], ci_ref[0, d, 0]

        def stack(power_of_t, mr, mi):
            parts = [times_pw(power_of_t(t), mr, mi) for t in range(t_len)]
            return (jnp.concatenate([p[0] for p in parts], axis=0),
                    jnp.concatenate([p[1] for p in parts], axis=0))

        bpr, bpi = stack((lambda t: t_len - 1 - t) if d == 0 else (lambda t: t), bbr, bbi)
        for src, dst in ((bpr, bpr_ref), (bpi, bpi_ref)):
            z = jnp.concatenate([jnp.where(own[0], src, 0.0), jnp.where(own[1], src, 0.0)], axis=0)
            dst[0, d, 0] = _bf(z.T)
        cpr, cpi = stack((lambda t: t + 1) if d == 0 else (lambda t: t_len - t), cr, ci)
        clr, cli = stack((lambda j: j) if d == 0 else (lambda j: t_len - 1 - j), cr, ci)
        for e in range(2):
            cp_ref[0, d, e] = _bf(jnp.concatenate(
                [jnp.where(own[e], cpr, 0.0), jnp.where(own[e], -cpi, 0.0)], axis=1))
            kc = _bf(_mm3f(jnp.where(own[e], clr, 0.0), bbr, _NT)
                     - _mm3f(jnp.where(own[e], cli, 0.0), bbi, _NT))
            acc = None
            for t in range(t_len):
                if d == 0:
                    pieces = [jnp.zeros((h_n * t, h_n), BF16), kc[:rows - h_n * t]]
                else:
                    s = h_n * (t_len - 1 - t)
                    pieces = [kc[s:], jnp.zeros((s, h_n), BF16)]
                shifted = jnp.concatenate([p for p in pieces if p.shape[0]], axis=0)
                place = _bf((sel_c == sel_r + h_n * t).astype(F32))
                term = _mm(shifted, place)
                acc = term if acc is None else acc + term
            kt_ref[0, d, e] = _bf(acc)
        ltr_ref[0, d, 0] = pwr[t_len:t_len + 1]
        lti_ref[0, d, 0] = pwi[t_len:t_len + 1]


def _s5_operators(lam_re, lam_im, log_step, b_re, b_im, c_re, c_im):
    depth = lam_re.shape[0]
    g_n, p_n, h_n = S5_GROUPS, S5_STATE, S5_GROUP
    pairs, rows = g_n // 2, S5_T * h_n
    dt = jnp.exp(log_step)[..., None]
    vec = lambda v: v.reshape(depth, 2, pairs, 1, 2 * p_n)
    bt = lambda m: m.reshape(depth, 2, pairs, 2, p_n, h_n).transpose(0, 1, 2, 5, 3, 4).reshape(depth, 2, pairs, h_n, 2 * p_n)
    ct = lambda m: m.reshape(depth, 2, pairs, 2, h_n, p_n).transpose(0, 1, 2, 4, 3, 5).reshape(depth, 2, pairs, h_n, 2 * p_n)
    vspec = pl.BlockSpec((1, 2, 1, 1, 2 * p_n), lambda l, q: (l, 0, q, 0, 0))
    mspec = pl.BlockSpec((1, 2, 1, h_n, 2 * p_n), lambda l, q: (l, 0, q, 0, 0))
    sq_spec = pl.BlockSpec((1, 2, 2, rows, rows), lambda l, q: (l, 0, q, 0, 0))
    bp_spec = pl.BlockSpec((1, 2, 1, 2 * p_n, 2 * rows), lambda l, q: (l, 0, q, 0, 0))
    sq = jax.ShapeDtypeStruct((depth, 2, g_n, rows, rows), BF16)
    bp = jax.ShapeDtypeStruct((depth, 2, pairs, 2 * p_n, 2 * rows), BF16)
    lt = jax.ShapeDtypeStruct((depth, 2, pairs, 1, 2 * p_n), F32)
    ktoep, bpr, bpi, cpad, ltr, lti = pl.pallas_call(
        _s5_ops_kernel,
        grid=(depth, pairs),
        in_specs=[vspec] * 4 + [mspec] * 4,
        out_specs=[sq_spec, bp_spec, bp_spec, sq_spec, vspec, vspec],
        out_shape=[sq, bp, bp, sq, lt, lt],
        compiler_params=_cparams("parallel", "parallel"),
        name="s5_operators",
    )(vec(lam_re * dt), vec(lam_im * dt), vec(lam_re), vec(lam_im), bt(b_re), bt(b_im), ct(c_re), ct(c_im))
    flat = lambda v: v.reshape(depth, 2, 1, g_n * p_n)
    return ktoep, bpr, bpi, cpad, flat(ltr), flat(lti)


def _s5_in_kernel(u_ref, uc_ref, o_ref):
    j = pl.program_id(0)
    last = pl.num_programs(0) - 1

    def emit(slab_of_t):
        for t in range(S5_T):
            o_ref[:, t * S5_GROUP:(t + 1) * S5_GROUP, :] = _bf(
                slab_of_t(t).T.reshape(S5_GROUPS, S5_GROUP, S5_CB))

    @pl.when(j < last)
    def _():
        emit(lambda t: u_ref[0, :, t, :])

    @pl.when(j == last)
    def _():
        pad = jnp.zeros((S5_CB - 2 * uc_ref.shape[1], S5_WIDTH), F32)
        emit(lambda t: jnp.concatenate([uc_ref[0, :, t, :], uc_ref[1, :, t, :], pad], axis=0))


def _s5_lat_index(n_half):
    def index(j):
        jj = jnp.minimum(j, 2 * n_half - 1)
        return jj // n_half, jj % n_half
    return index


def _s5_relayout_in(p, pc):
    bsz, seq, n = p.shape
    seq_c, n_c = pc.shape[1], pc.shape[2]
    n_lat, n_ctx = seq // S5_T, seq_c // S5_T
    n_half = n_lat // S5_CB
    idx = _s5_lat_index(n_half)
    return pl.pallas_call(
        _s5_in_kernel,
        grid=(bsz * n_half + 1,),
        in_specs=[
            pl.BlockSpec((1, S5_CB, S5_T, 512), lambda j: (*idx(j), 0, COL_S5U)),
            pl.BlockSpec((bsz, n_ctx, S5_T, 512), lambda j: (0, 0, 0, COL_S5U)),
        ],
        out_specs=pl.BlockSpec((S5_GROUPS, S5_T * S5_GROUP, S5_CB), lambda j: (0, 0, j)),
        out_shape=jax.ShapeDtypeStruct((S5_GROUPS, S5_T * S5_GROUP, S5_COLS), BF16),
        compiler_params=_cparams("arbitrary"),
        name="s5_relayout_in",
    )(p.reshape(bsz, n_lat, S5_T, n), pc.reshape(bsz, n_ctx, S5_T, n_c))


def _s5_state_in_kernel(ug_ref, bre_ref, bim_ref, vre_ref, vim_ref):
    u = ug_ref[...].reshape(2 * S5_T * S5_GROUP, S5_COLS)
    for direction in range(2):
        vre_ref[direction] = _mm(bre_ref[0, direction, 0], u)
        vim_ref[direction] = _mm(bim_ref[0, direction, 0], u)


def _s5_state_in(ug, b_pair_re, b_pair_im, layer):
    pairs = S5_GROUPS // 2
    rows = S5_T * S5_GROUP
    lanes = S5_GROUPS * S5_STATE
    out = jax.ShapeDtypeStruct((2, lanes, S5_COLS), F32)
    return pl.pallas_call(
        _s5_state_in_kernel,
        grid=(pairs,),
        in_specs=[
            pl.BlockSpec((2, rows, S5_COLS), lambda q: (q, 0, 0)),
            pl.BlockSpec((1, 2, 1, 128, 2 * rows), lambda q: (layer, 0, q, 0, 0)),
            pl.BlockSpec((1, 2, 1, 128, 2 * rows), lambda q: (layer, 0, q, 0, 0)),
        ],
        out_specs=[pl.BlockSpec((2, 128, S5_COLS), lambda q: (0, q, 0))] * 2,
        out_shape=[out, out],
        compiler_params=_cparams("parallel"),
        name="s5_state_in",
    )(ug, b_pair_re, b_pair_im)


def _s5_scan_kernel(vre_ref, vim_ref, lre_ref, lim_ref, xre_ref, xim_ref, tre_scr, tim_scr, *, n_lat, n_ctx):
    direction = pl.program_id(0)
    a_re = lre_ref[0, 0]
    a_im = lim_ref[0, 0]
    lanes = a_re.shape[-1]
    tre_scr[...] = vre_ref[0].T
    tim_scr[...] = vim_ref[0].T

    def step(row, xr, xi):
        sl = pl.ds(row, 1)
        xre_ref[0, sl, :] = xr
        xim_ref[0, sl, :] = xi
        return (a_re * xr - a_im * xi + tre_scr[sl, :],
                a_re * xi + a_im * xr + tim_scr[sl, :])

    def run(base0, base1, n, carry):
        def body(i, carry):
            j = i + direction * (n - 1 - 2 * i)
            x0r, x0i, x1r, x1i = carry
            x0r, x0i = step(base0 + j, x0r, x0i)
            x1r, x1i = step(base1 + j, x1r, x1i)
            return x0r, x0i, x1r, x1i
        return lax.fori_loop(0, n, body, carry)

    zero = jnp.zeros((1, lanes), F32)
    carry = run(2 * n_lat, 2 * n_lat + n_ctx, n_ctx, (zero, zero, zero, zero))
    run(0, n_lat, n_lat, carry)
    pad0 = 2 * n_lat + 2 * n_ctx
    xre_ref[0, pad0:, :] = jnp.zeros((S5_COLS - pad0, lanes), F32)
    xim_ref[0, pad0:, :] = jnp.zeros((S5_COLS - pad0, lanes), F32)


def _s5_scan(vre, vim, lam_re, lam_im, layer, *, n_lat, n_ctx):
    lanes = vre.shape[1]
    tl = 512
    vspec = pl.BlockSpec((1, tl, S5_COLS), lambda d, j: (d, j, 0))
    xspec = pl.BlockSpec((1, S5_COLS, tl), lambda d, j: (d, 0, j))
    lspec = pl.BlockSpec((1, 1, 1, tl), lambda d, j: (layer, d, 0, j))
    out = jax.ShapeDtypeStruct((2, S5_COLS, lanes), F32)
    return pl.pallas_call(
        functools.partial(_s5_scan_kernel, n_lat=n_lat, n_ctx=n_ctx),
        grid=(2, lanes // tl),
        in_specs=[vspec, vspec, lspec, lspec],
        out_specs=[xspec, xspec],
        out_shape=[out, out],
        scratch_shapes=[pltpu.VMEM((S5_COLS, tl), F32), pltpu.VMEM((S5_COLS, tl), F32)],
        compiler_params=_cparams("parallel", "parallel"),
        name="s5_scan",
    )(vre, vim, lam_re, lam_im)


def _s5_readout_kernel(ug_ref, kt_ref, cp_ref, xre_ref, xim_ref, y_ref):
    for e in range(2):
        u = ug_ref[e]
        acc = None
        for direction in range(2):
            xcat = _bf(jnp.concatenate([xre_ref[direction], xim_ref[direction]], axis=1))
            term = _mm(kt_ref[0, direction, e], u) + _mm(cp_ref[0, direction, e], xcat, _NT)
            acc = term if acc is None else acc + term
        y_ref[e] = acc


def _s5_readout(ug, ktoep, cpad, xre, xim, layer):
    pairs = S5_GROUPS // 2
    rows = S5_T * S5_GROUP
    return pl.pallas_call(
        _s5_readout_kernel,
        grid=(pairs,),
        in_specs=[
            pl.BlockSpec((2, rows, S5_COLS), lambda q: (q, 0, 0)),
            pl.BlockSpec((1, 2, 2, rows, rows), lambda q: (layer, 0, q, 0, 0)),
            pl.BlockSpec((1, 2, 2, rows, rows), lambda q: (layer, 0, q, 0, 0)),
            pl.BlockSpec((2, S5_COLS, 128), lambda q: (0, 0, q)),
            pl.BlockSpec((2, S5_COLS, 128), lambda q: (0, 0, q)),
        ],
        out_specs=pl.BlockSpec((2, rows, S5_COLS), lambda q: (q, 0, 0)),
        out_shape=jax.ShapeDtypeStruct((S5_GROUPS, rows, S5_COLS), F32),
        compiler_params=_cparams("parallel"),
        name="s5_readout",
    )(ug, ktoep, cpad, xre, xim)


def _s5_out_kernel(*refs, ctx_out):
    if ctx_out:
        yg_ref, u_ref, g_ref, uc_ref, gc_ref, d_ref, w_ref, b_ref, o_ref, oc_ref = refs
    else:
        yg_ref, u_ref, g_ref, d_ref, w_ref, b_ref, o_ref = refs
    d = d_ref[...]
    w = w_ref[...]
    bias = b_ref[...]

    def finish(yy, u, gate):
        yy = yy + d * u
        yg = jax.nn.gelu(yy)
        out = yg * jax.nn.sigmoid(_mm(_bf(yg), w) + bias)
        return out * (gate * jax.nn.sigmoid(gate))

    def y_of(t):
        return yg_ref[:, t * S5_GROUP:(t + 1) * S5_GROUP, :].reshape(S5_WIDTH, S5_CB).T

    def lat():
        for t in range(S5_T):
            o_ref[0, :, t, :] = finish(y_of(t), u_ref[0, :, t, :], g_ref[0, :, t, :])

    if not ctx_out:
        lat()
        return
    j = pl.program_id(0)
    last = pl.num_programs(0) - 1
    pl.when(j < last)(lat)

    @pl.when(j == last)
    def _():
        n_ctx = uc_ref.shape[1]
        for t in range(S5_T):
            y = y_of(t)
            for b in range(2):
                oc_ref[b, :, t, :] = finish(y[b * n_ctx:(b + 1) * n_ctx], uc_ref[b, :, t, :], gc_ref[b, :, t, :])


def _s5_relayout_out(yg, p, pc, d_skip, w_glu, b_glu, *, ctx_out):
    bsz, seq, n = p.shape
    seq_c, n_c = pc.shape[1], pc.shape[2]
    n_lat, n_ctx = seq // S5_T, seq_c // S5_T
    n_half = n_lat // S5_CB
    idx = _s5_lat_index(n_half)
    pv = p.reshape(bsz, n_lat, S5_T, n)
    lat_block = (1, S5_CB, S5_T, 512)
    in_specs = [
        pl.BlockSpec((S5_GROUPS, S5_T * S5_GROUP, S5_CB), lambda j: (0, 0, j)),
        pl.BlockSpec(lat_block, lambda j: (*idx(j), 0, COL_S5U)),
        pl.BlockSpec(lat_block, lambda j: (*idx(j), 0, COL_S5G)),
    ]
    args = [yg, pv, pv]
    out_specs = [pl.BlockSpec(lat_block, lambda j: (*idx(j), 0, 0))]
    out_shape = [jax.ShapeDtypeStruct((bsz, n_lat, S5_T, S5_WIDTH), F32)]
    if ctx_out:
        pcv = pc.reshape(bsz, n_ctx, S5_T, n_c)
        ctx_block = (bsz, n_ctx, S5_T, 512)
        in_specs += [
            pl.BlockSpec(ctx_block, lambda j: (0, 0, 0, COL_S5U)),
            pl.BlockSpec(ctx_block, lambda j: (0, 0, 0, COL_S5G)),
        ]
        args += [pcv, pcv]
        out_specs.append(pl.BlockSpec(ctx_block, lambda j: (0, 0, 0, 0)))
        out_shape.append(jax.ShapeDtypeStruct((bsz, n_ctx, S5_T, S5_WIDTH), F32))
    in_specs += [
        pl.BlockSpec((1, S5_WIDTH), lambda j: (0, 0)),
        pl.BlockSpec((S5_WIDTH, S5_WIDTH), lambda j: (0, 0)),
        pl.BlockSpec((1, S5_WIDTH), lambda j: (0, 0)),
    ]
    args += [d_skip, w_glu, b_glu]
    outs = pl.pallas_call(
        functools.partial(_s5_out_kernel, ctx_out=ctx_out),
        grid=(bsz * n_half + (1 if ctx_out else 0),),
        in_specs=in_specs,
        out_specs=out_specs,
        out_shape=out_shape,
        compiler_params=_cparams("arbitrary"),
        name="s5_relayout_out",
    )(*args)
    y = outs[0].reshape(bsz, seq, S5_WIDTH)
    yc = outs[1].reshape(bsz, seq_c, S5_WIDTH) if ctx_out else None
    return y, yc


def _s5(p, pc, ops, layer, d_skip, w_glu, b_glu, *, ctx_out):
    ktoep, b_pair_re, b_pair_im, cpad, lam_re, lam_im = ops
    ug = _s5_relayout_in(p, pc)
    vre, vim = _s5_state_in(ug, b_pair_re, b_pair_im, layer)
    xre, xim = _s5_scan(vre, vim, lam_re, lam_im, layer, n_lat=p.shape[1] // S5_T, n_ctx=pc.shape[1] // S5_T)
    yg = _s5_readout(ug, ktoep, cpad, xre, xim, layer)
    return _s5_relayout_out(yg, p, pc, d_skip, w_glu, b_glu, ctx_out=ctx_out)


def _dft_tables():
    n1, n2 = FFT_N1, FFT_N2
    n = n1 * n2
    a = jnp.arange(n1, dtype=jnp.int32)
    ph1 = (a[:, None] * a[None, :]) % n1
    ang1 = ph1.astype(F32) * (2.0 * math.pi / n1)
    c1, s1 = jnp.cos(ang1), -jnp.sin(ang1)
    h = n1 // 2
    m1 = jnp.concatenate([jnp.concatenate([c1[:, :h], -s1[:, :h]], 1),
                          jnp.concatenate([s1[:, :h], c1[:, :h]], 1)], 0)
    m1_real = jnp.concatenate([c1, s1], 0)
    m3 = m1.T / n
    k1 = jnp.arange(n1, dtype=jnp.int32)[:, None, None]
    k2 = jnp.arange(n2, dtype=jnp.int32)[None, :, None]
    m2 = jnp.arange(n2, dtype=jnp.int32)[None, None, :]
    ph = (n1 * m2 * k2 + m2 * k1) % n
    ang = ph.astype(F32) * (2.0 * math.pi / n)
    gr, gi = jnp.cos(ang), -jnp.sin(ang)
    g = jnp.concatenate([jnp.concatenate([gr, -gi], 2), jnp.concatenate([gi, gr], 2)], 1)
    gt = jnp.swapaxes(g, 1, 2)
    return tuple(_split(t) for t in (m1, m1_real, m3, g, gt))


def _filter_features(length):
    t = jnp.linspace(0.0, 1.0, length, dtype=F32)[:, None]
    ang = (2.0 * math.pi / length) * jnp.arange(length, dtype=F32)[:, None]
    bands = jnp.linspace(1e-4, HY_BANDS - 1, HY_BANDS, dtype=F32)[None, :]
    feats = jnp.concatenate([t, jnp.cos(bands * ang), -jnp.sin(bands * ang)], axis=-1)
    feats = jnp.pad(feats, ((0, 0), (0, 128 - HY_EMB)))
    rev = jnp.roll(feats[::-1], 1, axis=0)
    return jnp.stack([feats, rev])


def _filter_kernel(f_ref, w1_ref, b1_ref, f1_ref, w2_ref, b2_ref, f2_ref, w3_ref, dl_ref, o_ref):
    half = pl.program_id(1)
    tile = pl.program_id(2)
    x = f_ref[0]
    x_hi, x_lo = _split(x)
    h = jnp.sin(f1_ref[0] * (_mm3(x_hi, x_lo, w1_ref[0]) + b1_ref[0]))
    h_hi, h_lo = _split(h)
    h = jnp.sin(f2_ref[0] * (_mm3(h_hi, h_lo, w2_ref[0]) + b2_ref[0]))
    h_hi, h_lo = _split(h)
    y = _mm3(h_hi, h_lo, w3_ref[0, 0]) * jnp.exp(-x[:, 0:1] * dl_ref[...])
    rows = lax.broadcasted_iota(jnp.int32, y.shape, 0)
    drop = jnp.logical_and(jnp.logical_and(half == 1, tile == 0), rows == 0)
    o_ref[0] = jnp.where(drop, 0.0, y)


def _filter_taps(feats, w1, b1, f1, w2, b2, f2, w3, deltas, tmf):
    depth = w1.shape[0]
    length = feats.shape[1]
    nt = length // tmf
    wide = w3.shape[-1]
    vec = pl.BlockSpec((1, 1, 128), lambda l, s, i: (l, 0, 0))
    sq = pl.BlockSpec((1, 128, 128), lambda l, s, i: (l, 0, 0))
    return pl.pallas_call(
        _filter_kernel,
        grid=(depth, 2, nt),
        in_specs=[
            pl.BlockSpec((1, tmf, 128), lambda l, s, i: (s, i, 0)),
            sq, vec, vec, sq, vec, vec,
            pl.BlockSpec((1, 1, 128, wide), lambda l, s, i: (l, s, 0, 0)),
            pl.BlockSpec((1, wide), lambda l, s, i: (0, 0)),
        ],
        out_specs=pl.BlockSpec((1, tmf, wide), lambda l, s, i: (l, s * nt + i, 0)),
        out_shape=jax.ShapeDtypeStruct((depth, 2 * length, wide), F32),
        compiler_params=_cparams("parallel", "parallel", "parallel"),
        name="hyena_filter",
    )(feats, w1, b1, f1, w2, b2, f2, w3, deltas)


def _parts(pair):
    return tuple(pair) if HY_PASSES == 3 else tuple(pair[:1])


def _mmp(m, b):
    return _mm(m[0], _bf(b)) if len(m) == 1 else _mm3(m[0], m[1], b)


def _blk_rows(n2):
    blk = 2 * FFT_N1
    return pl.ds(pl.multiple_of(n2 * blk, blk), blk)


def _dft_stage1(read_rows, m, a_scr):
    ct = a_scr.shape[-1]

    def body(it, carry):
        n2 = it * HY_NB
        a = _mmp(m, jnp.concatenate([read_rows(n2 + j) for j in range(HY_NB)], axis=1))
        for j in range(HY_NB):
            a_scr[_blk_rows(n2 + j), :] = a[:, j * ct:(j + 1) * ct]
        return carry

    lax.fori_loop(0, FFT_N2 // HY_NB, body, 0)


def _stage2_rows(k1):
    return (pl.ds(k1, FFT_N2, stride=2 * FFT_N1), pl.ds(k1 + FFT_N1, FFT_N2, stride=2 * FFT_N1))


def _stage2_load(a_scr, kb):
    rows = [_stage2_rows(kb * HY_KB + i) for i in range(HY_KB)]
    return rows, [jnp.concatenate([a_scr[re, :], a_scr[im, :]], axis=0) for re, im in rows]


def _spectrum_kernel(*refs):
    np_ = len(_parts((0, 0)))
    m_refs, g_refs, (t_ref, o_ref, a_scr) = refs[:np_], refs[np_:2 * np_], refs[2 * np_:]
    kb = pl.program_id(2)
    ct = t_ref.shape[-1]

    @pl.when(kb == 0)
    def _():
        _dft_stage1(lambda n2: t_ref[0, pl.ds(n2, FFT_N1, stride=FFT_N2), :], [r[...] for r in m_refs], a_scr)

    _, a = _stage2_load(a_scr, kb)
    for i in range(HY_KB):
        o_ref[0, i] = _mmp([r[i] for r in g_refs], a[i]).reshape(2, FFT_N2, ct)


def _filter_spectrum(tables, taps):
    m, g = _parts(tables[1]), _parts(tables[3])
    depth, n, ch = taps.shape
    ct = HY_CT
    gspec = pl.BlockSpec((HY_KB, 2 * FFT_N2, 2 * FFT_N2), lambda l, c, k: (k, 0, 0))
    mspec = pl.BlockSpec(m[0].shape, lambda l, c, k: (0, 0))
    return pl.pallas_call(
        _spectrum_kernel,
        grid=(depth, ch // ct, FFT_N1 // HY_KB),
        in_specs=[mspec] * len(m) + [gspec] * len(g) + [pl.BlockSpec((1, n, ct), lambda l, c, k: (l, 0, c))],
        out_specs=pl.BlockSpec((1, HY_KB, 2, FFT_N2, ct), lambda l, c, k: (l, k, 0, 0, c)),
        out_shape=jax.ShapeDtypeStruct((depth, FFT_N1, 2, FFT_N2, ch), F32),
        scratch_shapes=[pltpu.VMEM((2 * FFT_N1 * FFT_N2, ct), F32)],
        compiler_params=_cparams("parallel", "parallel", "arbitrary"),
        name="hyena_spectrum",
    )(*m, *g, taps)


def _hy_pre_kernel(pv_ref, p1_ref, p2_ref, pg_ref, wv_ref, w1_ref, w2_ref, bv_ref, b1_ref, b2_ref,
                   v_ref, x1_ref, x2_ref):
    n = pv_ref.shape[1]
    rows = lax.broadcasted_iota(jnp.int32, (n, 128), 0)

    def conv(p, w_ref, b_ref):
        up = jnp.where(rows == 0, 0.0, pltpu.roll(p, 1, 0))
        dn = jnp.where(rows == n - 1, 0.0, pltpu.roll(p, n - 1, 0))
        return up * w_ref[0:1, :] + p * w_ref[1:2, :] + dn * w_ref[2:3, :] + b_ref[...]

    v_ref[0] = conv(pv_ref[0], wv_ref, bv_ref)
    x1_ref[0] = conv(p1_ref[0], w1_ref, b1_ref)
    gate = pg_ref[0]
    x2_ref[0] = conv(p2_ref[0], w2_ref, b2_ref) * (gate * jax.nn.sigmoid(gate))


def _hy_pre(p, conv_w, conv_b):
    bsz, seq, _ = p.shape
    def pspec(off):
        return pl.BlockSpec((1, seq, 128), lambda b, j: (b, 0, off + j))
    def wspec(off):
        return pl.BlockSpec((3, 128), lambda b, j: (0, off + j))
    def bspec(off):
        return pl.BlockSpec((1, 128), lambda b, j: (0, off + j))
    out = jax.ShapeDtypeStruct((bsz, seq, HY_WIDTH), F32)
    ospec = pl.BlockSpec((1, seq, 128), lambda b, j: (b, 0, j))
    return pl.pallas_call(
        _hy_pre_kernel,
        grid=(bsz, HY_WIDTH // 128),
        in_specs=[pspec(COL_HY), pspec(COL_HY + 4), pspec(COL_HY + 8), pspec(COL_HY + 12),
                  wspec(0), wspec(4), wspec(8), bspec(0), bspec(4), bspec(8)],
        out_specs=[ospec, ospec, ospec],
        out_shape=[out, out, out],
        compiler_params=_cparams("parallel", "parallel"),
        name="hyena_pre",
    )(p, p, p, p, conv_w, conv_w, conv_w, conv_b, conv_b, conv_b)


def _hy_conv_kernel(*refs):
    np_ = len(_parts((0, 0)))
    m1_refs, m3_refs, g_refs, t_refs = (refs[i * np_:(i + 1) * np_] for i in range(4))
    u_ref, gate_ref, h_ref, d_ref, o_ref, a_scr = refs[4 * np_:]
    kb = pl.program_id(1)
    ct = u_ref.shape[-1]
    half = FFT_N1 // 2

    def rows(n2):
        return pl.ds(n2, half, stride=FFT_N2)

    @pl.when(kb == 0)
    def _():
        _dft_stage1(lambda n2: jnp.concatenate([u_ref[0, rows(n2), :], u_ref[1, rows(n2), :]], axis=0),
                    [r[...] for r in m1_refs], a_scr)

    srows, a = _stage2_load(a_scr, kb)
    z = [_mmp([r[i] for r in g_refs], a[i]) for i in range(HY_KB)]
    z2 = []
    for i in range(HY_KB):
        zr, zi = z[i][:FFT_N2], z[i][FFT_N2:]
        hr, hi = h_ref[0, i, 0], h_ref[0, i, 1]
        z2.append(jnp.concatenate([zr * hr - zi * hi, zr * hi + zi * hr], axis=0))
    a2 = [_mmp([r[i] for r in t_refs], z2[i]) for i in range(HY_KB)]
    for i in range(HY_KB):
        a_scr[srows[i][0], :] = a2[i][:FFT_N2]
        a_scr[srows[i][1], :] = a2[i][FFT_N2:]

    @pl.when(kb == pl.num_programs(1) - 1)
    def _():
        m3 = [r[...] for r in m3_refs]
        d = d_ref[...]

        def body(it, carry):
            n2 = it * HY_NB
            y = _mmp(m3, jnp.concatenate([a_scr[_blk_rows(n2 + j), :] for j in range(HY_NB)], axis=1))
            for j in range(HY_NB):
                r = rows(n2 + j)
                for b in range(2):
                    yb = y[b * half:(b + 1) * half, j * ct:(j + 1) * ct]
                    o_ref[b, r, :] = gate_ref[b, r, :] * (yb + d * u_ref[b, r, :])
            return carry

        lax.fori_loop(0, FFT_N2 // HY_NB, body, 0)


def _hy_long_conv(tables, spec, layer, order, u, gate, d):
    m1, _, m3, g, gt = (_parts(t) for t in tables)
    bsz, seq, ch = u.shape
    ct = HY_CT
    gspec = pl.BlockSpec((HY_KB, 2 * FFT_N2, 2 * FFT_N2), lambda c, k: (k, 0, 0))
    full = lambda t: pl.BlockSpec(t.shape, lambda c, k: (0, 0))
    uspec = pl.BlockSpec((bsz, seq, ct), lambda c, k: (0, 0, c))
    n_ct = ch // ct
    mats = (*m1, *m3, *g, *gt)
    return pl.pallas_call(
        _hy_conv_kernel,
        grid=(n_ct, FFT_N1 // HY_KB),
        in_specs=[full(t) for t in (*m1, *m3)] + [gspec] * (len(g) + len(gt)) + [
            uspec, uspec,
            pl.BlockSpec((1, HY_KB, 2, FFT_N2, ct), lambda c, k: (layer, k, 0, 0, order * n_ct + c)),
            pl.BlockSpec((1, ct), lambda c, k: (0, c))],
        out_specs=uspec,
        out_shape=jax.ShapeDtypeStruct(u.shape, F32),
        scratch_shapes=[pltpu.VMEM((2 * FFT_N1 * FFT_N2, ct), F32)],
        compiler_params=_cparams("parallel", "arbitrary"),
        name="hyena_conv",
    )(*mats, u, gate, spec, d.reshape(1, ch))


def _ctx_dft_tables(length):
    n = 2 * length
    k = jnp.arange(n, dtype=jnp.int32)
    ang = ((k[:, None] * k[None, :]) % n).astype(F32) * (2.0 * math.pi / n)
    fr, fi = jnp.cos(ang), -jnp.sin(ang)
    fwd = jnp.concatenate([jnp.concatenate([fr[:, :length], -fi[:, :length]], 1),
                           jnp.concatenate([fi[:, :length], fr[:, :length]], 1)], 0)
    real = jnp.concatenate([fr, fi], 0)
    inv = fwd.T / n
    return tuple(_split(t) for t in (fwd, real, inv))


def _hyc_kernel(fh_ref, fl_ref, rh_ref, rl_ref, ih_ref, il_ref, taps_ref, u_ref, g_ref, d_ref, o_ref):
    n = u_ref.shape[1]
    x = jnp.concatenate([u_ref[0], u_ref[1]], axis=0)
    z = _mm3(fh_ref[...], fl_ref[...], x)
    h = _mm3(rh_ref[...], rl_ref[...], taps_ref[0])
    m = 2 * n
    zr, zi, hr, hi = z[:m], z[m:], h[:m], h[m:]
    z2 = jnp.concatenate([zr * hr - zi * hi, zr * hi + zi * hr], axis=0)
    y = _mm3(ih_ref[...], il_ref[...], z2)
    for b in range(2):
        o_ref[b] = g_ref[b] * (y[b * n:(b + 1) * n] + d_ref[...] * u_ref[b])


def _hyc_long_conv(ctabs, taps, layer, order, u, gate, d):
    (f_hi, f_lo), (r_hi, r_lo), (i_hi, i_lo) = ctabs
    bsz, n, ch = u.shape
    full = lambda t: pl.BlockSpec(t.shape, lambda i: (0,) * t.ndim)
    uspec = pl.BlockSpec((bsz, n, ch), lambda i: (0, 0, 0))
    return pl.pallas_call(
        _hyc_kernel,
        grid=(1,),
        in_specs=[full(f_hi), full(f_lo), full(r_hi), full(r_lo), full(i_hi), full(i_lo),
                  pl.BlockSpec((1, 2 * n, ch), lambda i: (layer, 0, order)),
                  uspec, uspec, pl.BlockSpec((1, ch), lambda i: (0, 0))],
        out_specs=uspec,
        out_shape=jax.ShapeDtypeStruct((bsz, n, ch), F32),
        compiler_params=_cparams("arbitrary"),
        name="hyena_ctx",
    )(f_hi, f_lo, r_hi, r_lo, i_hi, i_lo, taps, u, gate, d.reshape(1, ch))


def _outproj_kernel(x_ref, ya_ref, yb_ref, yc_ref, w_ref, gp_ref, gate_ref, o_ref, *, colmajor, tm):
    na, nb = ya_ref.shape[-1], yb_ref.shape[-1]
    acc = (_mm(_bf(ya_ref[0]), w_ref[0, 0:na, :]) + _mm(_bf(yb_ref[0]), w_ref[0, na:na + nb, :])
           + _mm(_bf(yc_ref[0]), w_ref[0, na + nb:, :]))
    ms = jnp.mean(acc * acc, axis=-1, keepdims=True)
    upd = gate_ref[0] * (acc * lax.rsqrt(ms + EPS) * gp_ref[...])
    if colmajor:
        for j in range(tm // GRID_W):
            o_ref[0, :, j, :] = x_ref[0, :, j, :] + upd[j * GRID_W:(j + 1) * GRID_W]
    else:
        o_ref[0] = x_ref[0] + upd


def _outproj(x, ya, yb, yc, w_out, layer, g_post, gate, *, colmajor, tm):
    bsz, seq, d = x.shape
    if colmajor:
        x_in = x.reshape(bsz, GRID_W, seq // GRID_W, d)
        x_spec = pl.BlockSpec((1, GRID_W, tm // GRID_W, d), lambda b, m: (b, 0, m, 0))
    else:
        x_in = x
        x_spec = pl.BlockSpec((1, tm, d), lambda b, m: (b, m, 0))
    def yspec(t):
        return pl.BlockSpec((1, tm, t.shape[-1]), lambda b, m: (b, m, 0))
    out = pl.pallas_call(
        functools.partial(_outproj_kernel, colmajor=colmajor, tm=tm),
        grid=(bsz, seq // tm),
        in_specs=[x_spec, yspec(ya), yspec(yb), yspec(yc),
                  pl.BlockSpec((1,) + w_out.shape[1:], lambda b, m: (layer, 0, 0)),
                  pl.BlockSpec((1, d), lambda b, m: (0, 0)),
                  pl.BlockSpec((1, 1, d), lambda b, m: (b, 0, 0))],
        out_specs=x_spec,
        out_shape=jax.ShapeDtypeStruct(x_in.shape, F32),
        compiler_params=_cparams("parallel", "parallel"),
        name="outproj",
    )(x_in, ya, yb, yc, w_out, g_post, gate)
    return out.reshape(bsz, seq, d)


def kernel(x, c, ctx, c_ctx, w_mod, b_mod, g_pre, g_post, w_in, w_out, gla_w_gate, gla_b_gate, gla_norm, s5_lam_re, s5_lam_im, s5_log_step, s5_b_re, s5_b_im, s5_c_re, s5_c_im, s5_d, s5_w_glu, s5_b_glu, hy_conv_w, hy_conv_b, hy_w1, hy_b1, hy_f1, hy_w2, hy_b2, hy_f2, hy_w3, hy_d):
    bsz, seq, d = x.shape
    seq_c = ctx.shape[1]
    depth = w_in.shape[0]
    assert (bsz, d, seq // GRID_W) == (2, D_MODEL, GRID_W) and seq == FFT_N1 * FFT_N2 // 2

    cvec = jnp.concatenate([c, c_ctx[None], jnp.zeros((8 - bsz - 1, d), F32)], axis=0)
    mod = _modulation(cvec, w_mod, b_mod)
    shift, scale, gate = mod[..., :d], mod[..., d:2 * d], mod[..., 2 * d:]

    w_main = _bf(jnp.concatenate([w_in[..., :2048], w_in[..., 2080:]], axis=-1))
    w_lr = _bf(jnp.pad(w_in[..., 2048:2080], ((0, 0), (0, 0), (0, 96))))
    w_out_b = _bf(w_out)
    w_glu_b = _bf(s5_w_glu)

    wg = gla_w_gate.reshape(depth, 2, 16, GLA_HEADS, GLA_DK).transpose(0, 1, 3, 2, 4)
    wg_pad = jnp.stack([jnp.pad(wg[:, 0], ((0, 0), (0, 0), (0, 112), (0, 0))),
                        jnp.pad(wg[:, 1], ((0, 0), (0, 0), (16, 96), (0, 0)))], axis=1)
    wg_hi, wg_lo = _split(wg_pad)
    bg = gla_b_gate.reshape(depth, 2, GLA_HEADS, 1, GLA_DK)

    tables = _dft_tables()
    deltas = jnp.abs(jnp.linspace(HY_MIN_DECAY, HY_MAX_DECAY, HY_WIDTH, dtype=F32))
    deltas2 = jnp.tile(deltas, 2).reshape(1, 2 * HY_WIDTH)
    pad_k = lambda w: jnp.pad(w, ((0, 0), (0, 128 - w.shape[1]), (0, 128 - w.shape[2])))
    pad_v = lambda v: jnp.pad(v, ((0, 0), (0, 128 - v.shape[1]))).reshape(depth, 1, 128)
    w3 = hy_w3.reshape(depth, HY_FFN, 2, 2, HY_WIDTH).transpose(0, 3, 1, 2, 4)
    w3 = jnp.pad(w3.reshape(depth, 2, HY_FFN, 2 * HY_WIDTH), ((0, 0), (0, 0), (0, 128 - HY_FFN), (0, 0)))
    filt_w = (pad_k(hy_w1), pad_v(hy_b1), pad_v(hy_f1), pad_k(hy_w2), pad_v(hy_b2), pad_v(hy_f2), w3, deltas2)
    taps = _filter_taps(_filter_features(seq), *filt_w, tmf=512)
    spec = _filter_spectrum(tables, taps)
    taps_c = _filter_taps(_filter_features(seq_c), *filt_w, tmf=seq_c)
    ctabs = _ctx_dft_tables(seq_c)

    s5_ops = _s5_operators(s5_lam_re, s5_lam_im, s5_log_step, s5_b_re, s5_b_im, s5_c_re, s5_c_im)

    xc = ctx
    for l in range(depth):
        last = l == depth - 1
        ctx_out = not last
        colmajor = l % 2 == 1
        pre = g_pre[l][None, :] * (1.0 + scale[l])
        sc_l, sh_l = pre[:bsz, None, :], shift[l, :bsz, None, :]
        sc_c = jnp.broadcast_to(pre[bsz][None, None, :], (bsz, 1, d))
        sh_c = jnp.broadcast_to(shift[l, bsz][None, None, :], (bsz, 1, d))
        qkv, p, lr = _inproj(x, sc_l, sh_l, w_main, w_lr, l, n=N_MAIN, colmajor=colmajor, tm=1024, tn=512)
        qkvc, pc, lrc = _inproj(xc, sc_c, sh_c, w_main, w_lr, l, n=N_MAIN if ctx_out else N_STATE,
                                colmajor=False, tm=seq_c, tn=512)

        y_gla, yc_gla = _gla(qkv, p, lr, qkvc, pc, lrc, wg_hi[l], wg_lo[l], bg[l], gla_norm[l][None, :],
                             ctx_out=ctx_out)

        y_s5, yc_s5 = _s5(p, pc, s5_ops, l, s5_d[l][None, :], w_glu_b[l], s5_b_glu[l][None, :], ctx_out=ctx_out)

        v, x1, x2 = _hy_pre(p, hy_conv_w[l], hy_conv_b[l][None, :])
        z = _hy_long_conv(tables, spec, l, 0, v, x1, hy_d[l, 0])
        y_hy = _hy_long_conv(tables, spec, l, 1, z, x2, hy_d[l, 1])

        x_new = _outproj(x, y_gla, y_s5, y_hy, w_out_b, l, g_post[l][None, :], gate[l, :bsz, None, :],
                         colmajor=colmajor, tm=512)
        if ctx_out:
            vc, x1c, x2c = _hy_pre(pc, hy_conv_w[l], hy_conv_b[l][None, :])
            zc = _hyc_long_conv(ctabs, taps_c, l, 0, vc, x1c, hy_d[l, 0])
            yc_hy = _hyc_long_conv(ctabs, taps_c, l, 1, zc, x2c, hy_d[l, 1])
            gate_c = jnp.broadcast_to(gate[l, bsz][None, None, :], (bsz, 1, d))
            xc = _outproj(xc, yc_gla, yc_s5, yc_hy, w_out_b, l, g_post[l][None, :], gate_c,
                          colmajor=False, tm=seq_c)
        x = x_new
    return x
```

```python
import functools
import math

import numpy as np
import jax
import jax.numpy as jnp
from jax import lax
from jax.experimental import pallas as pl
from jax.experimental.pallas import tpu as pltpu

F32 = jnp.float32
BF16 = jnp.bfloat16

D_MODEL = 2048
GRID_W = 64
EPS = 1e-6

GLA_HEADS = 4
GLA_DK = 128
GLA_DV = 256
GLA_TAU = 16.0
GLA_CHUNK = 64
GLA_BLOCK = 8

S5_WIDTH = 512
S5_GROUP = 16
S5_GROUPS = 32
S5_STATE = 64
S5_T = 16
S5_COLS = 640
S5_CB = 128

HY_WIDTH = 512
HY_EMB = 33
HY_BANDS = 16
HY_FFN = 64
HY_MIN_DECAY = math.log(1e-2) / 0.3
HY_MAX_DECAY = math.log(1e-2) / 1.5
FFT_N1 = 64
FFT_N2 = 128
HY_CT = 128
HY_KB = 8
HY_NB = 4
HY_PASSES = 1

N_MAIN = 6144
N_STATE = 2560
N_QKV = 2048
N_LR = 32
COL_S5U, COL_S5G = 0, 3
COL_GLA_GATE = 2
COL_HY = 16

VMEM_LIMIT_BYTES = 56 * 1024 * 1024


def _cparams(*sem):
    return pltpu.CompilerParams(dimension_semantics=sem, vmem_limit_bytes=VMEM_LIMIT_BYTES)


def _bf(x):
    return x.astype(BF16)


def _split(x):
    hi = _bf(x)
    return hi, _bf(x - hi.astype(F32))


def _split3(x):
    a = _bf(x)
    r = x - a.astype(F32)
    b = _bf(r)
    return a, b, _bf(r - b.astype(F32))


_NN = (((1,), (0,)), ((), ()))
_NT = (((1,), (1,)), ((), ()))
_TN = (((0,), (0,)), ((), ()))


def _mm(a, b, dims=_NN):
    return lax.dot_general(a, b, dims, preferred_element_type=F32)


def _mm3(a_hi, a_lo, b, dims=_NN):
    b_hi, b_lo = _split(b)
    return _mm(a_hi, b_hi, dims) + _mm(a_hi, b_lo, dims) + _mm(a_lo, b_hi, dims)


def _mod_kernel(s_ref, w_ref, b_ref, o_ref):
    s = s_ref[...]
    s = s * jax.nn.sigmoid(s)
    s_hi, s_lo = _split(s)
    o_ref[0] = _mm3(s_hi, s_lo, w_ref[0]) + b_ref[0]


def _modulation(cvec, w_mod, b_mod):
    depth, d, n = w_mod.shape
    tn = 512
    return pl.pallas_call(
        _mod_kernel,
        grid=(depth, n // tn),
        in_specs=[
            pl.BlockSpec((8, d), lambda l, j: (0, 0)),
            pl.BlockSpec((1, d, tn), lambda l, j: (l, 0, j)),
            pl.BlockSpec((1, 1, tn), lambda l, j: (l, 0, j)),
        ],
        out_specs=pl.BlockSpec((1, 8, tn), lambda l, j: (l, 0, j)),
        out_shape=jax.ShapeDtypeStruct((depth, 8, n), F32),
        compiler_params=_cparams("parallel", "parallel"),
        name="modulation",
    )(cvec, w_mod, b_mod.reshape(depth, 1, n))


def _inproj_kernel(x_ref, sc_ref, sh_ref, w_ref, wlr_ref, qkv_ref, p_ref, lr_ref, h_scr, *, colmajor, tm, tn):
    j = pl.program_id(2)

    @pl.when(j == 0)
    def _():
        def norm(xb):
            ms = jnp.mean(xb * xb, axis=-1, keepdims=True)
            return _bf(xb * lax.rsqrt(ms + EPS) * sc_ref[0] + sh_ref[0])

        if colmajor:
            for j in range(tm // GRID_W):
                h_scr[j * GRID_W:(j + 1) * GRID_W, :] = norm(x_ref[0, :, j, :])
        else:
            h_scr[...] = norm(x_ref[0])
        lr_ref[0] = _mm(h_scr[...], wlr_ref[...], _NT)

    res = _mm(h_scr[...], w_ref[...], _NT)
    n_qkv = N_QKV // tn

    @pl.when(j < n_qkv)
    def _():
        qkv_ref[0] = _bf(res)

    @pl.when(j >= n_qkv)
    def _():
        p_ref[0] = res


def _inproj(x, scale, shift, w_t, layer, *, n, colmajor, tm, tn):
    bsz, seq, d = x.shape
    n_qkv = N_QKV // tn
    if colmajor:
        x_in = x.reshape(bsz, GRID_W, seq // GRID_W, d)
        x_spec = pl.BlockSpec((1, GRID_W, tm // GRID_W, d), lambda b, m, j: (b, 0, m, 0))
    else:
        x_in = x
        x_spec = pl.BlockSpec((1, tm, d), lambda b, m, j: (b, m, 0))
    return pl.pallas_call(
        functools.partial(_inproj_kernel, colmajor=colmajor, tm=tm, tn=tn),
        grid=(bsz, seq // tm, n // tn),
        in_specs=[
            x_spec,
            pl.BlockSpec((1, 1, d), lambda b, m, j: (b, 0, 0)),
            pl.BlockSpec((1, 1, d), lambda b, m, j: (b, 0, 0)),
            pl.BlockSpec((pl.Squeezed(), pl.Element(tn), pl.Element(d)),
                         lambda b, m, j: (layer, pl.multiple_of(j * tn + jnp.where(j >= n_qkv, N_LR, 0), N_LR), 0)),
            pl.BlockSpec((pl.Squeezed(), pl.Element(128), pl.Element(d)), lambda b, m, j: (layer, N_QKV, 0)),
        ],
        out_specs=[
            pl.BlockSpec((1, tm, tn), lambda b, m, j: (b, m, jnp.minimum(j, n_qkv - 1))),
            pl.BlockSpec((1, tm, tn), lambda b, m, j: (b, m, jnp.maximum(j - n_qkv, 0))),
            pl.BlockSpec((1, tm, 128), lambda b, m, j: (b, m, 0)),
        ],
        out_shape=[jax.ShapeDtypeStruct((bsz, seq, N_QKV), BF16),
                   jax.ShapeDtypeStruct((bsz, seq, n - N_QKV), F32),
                   jax.ShapeDtypeStruct((bsz, seq, 128), F32)],
        scratch_shapes=[pltpu.VMEM((tm, d), BF16)],
        compiler_params=_cparams("parallel", "parallel", "arbitrary"),
        name="inproj",
    )(x_in, scale, shift, w_t, w_t)


def _gla_block(q, k, v, lr, st, wg_hi, wg_lo, bg, direction, mask, rowc, nb):
    c = GLA_CHUNK
    n = c * nb
    lr_hi, lr_lo = _split(lr)
    z = _mm(lr_hi, wg_hi) + _mm(lr_hi, wg_lo) + _mm(lr_lo, wg_hi) + bg
    g = (jnp.minimum(z, 0.0) - jnp.log1p(jnp.exp(-jnp.abs(z)))) * (1.0 / GLA_TAU)
    cum = g
    s = 1
    while s < c:
        if direction == 0:
            cum = cum + jnp.where(rowc >= s, pltpu.roll(cum, s, 0), 0.0)
        else:
            cum = cum + jnp.where(rowc < c - s, pltpu.roll(cum, n - s, 0), 0.0)
        s *= 2
    cum3 = cum.reshape(nb, c, GLA_DK)
    tot3 = cum3[:, c - 1:c, :] if direction == 0 else cum3[:, 0:1, :]
    q3 = q.reshape(nb, c, GLA_DK)
    k3 = k.reshape(nb, c, GLA_DK)
    qg3 = _bf(q3 * jnp.exp(cum3))
    kg3 = _bf(k3 * jnp.exp(-cum3))
    kd3 = _bf(k3 * jnp.exp(tot3 - cum3))
    a3 = jnp.exp(tot3)
    vb3 = _bf(v).reshape(nb, c, GLA_DV)
    att = [jnp.where(mask, _mm(qg3[i], kg3[i], _NT), 0.0) for i in range(nb)]
    o_in = [_mm(_bf(att[i]), vb3[i]) for i in range(nb)]
    d_st = [_mm(vb3[i], kd3[i], _TN) for i in range(nb)]
    outs = [None] * nb
    for i in (range(nb) if direction == 0 else range(nb - 1, -1, -1)):
        outs[i] = o_in[i] + _mm(qg3[i], _bf(st), _NT)
        st = st * a3[i] + d_st[i]
    return jnp.concatenate(outs, axis=0), st


def _gla_kernel(*refs, ctx_out, seq, seq_c):
    if ctx_out:
        (q_ref, k_ref, v_ref, gt_ref, lr_ref, qc_ref, kc_ref, vc_ref, gtc_ref, lrc_ref,
         wgh_ref, wgl_ref, bg_ref, gn_ref, y_ref, yc_ref) = refs
    else:
        (q_ref, k_ref, v_ref, gt_ref, lr_ref, qc_ref, kc_ref, vc_ref, lrc_ref,
         wgh_ref, wgl_ref, bg_ref, gn_ref, y_ref) = refs
        gtc_ref = yc_ref = None
    c = GLA_CHUNK
    scale = GLA_DK ** -0.5
    row = lax.broadcasted_iota(jnp.int32, (c, c), 0)
    col = lax.broadcasted_iota(jnp.int32, (c, c), 1)
    masks = (col <= row, col >= row)
    gn = gn_ref[...]

    def run(refs4, n_rows, st, direction, store):
        qr, kr, vr, lrr = refs4
        nb = min(GLA_BLOCK, n_rows // c)
        n = c * nb
        rowc = lax.broadcasted_iota(jnp.int32, (n, GLA_DK), 0) % c
        wg_hi = wgh_ref[direction, 0]
        wg_lo = wgl_ref[direction, 0]
        bg = bg_ref[direction, 0]
        n_blocks = n_rows // n

        def body(i, st):
            bi = i if direction == 0 else n_blocks - 1 - i
            rows = pl.ds(pl.multiple_of(bi * n, n), n)
            q = qr[0, rows, :].astype(F32) * scale
            k = kr[0, rows, :].astype(F32)
            v = vr[0, rows, :].astype(F32)
            o, st = _gla_block(q, k, v, lrr[0, rows, :], st, wg_hi, wg_lo, bg, direction,
                               masks[direction], rowc, nb)
            store(rows, q, k, v, o)
            return st

        return lax.fori_loop(0, n_blocks, body, st)

    def store_fwd(out_ref):
        def f(rows, q, k, v, o):
            if out_ref is not None:
                out_ref[0, rows, :] = o
        return f

    def store_bwd(out_ref, gate_ref):
        def f(rows, q, k, v, o):
            if out_ref is None:
                return
            y = out_ref[0, rows, :] + o - jnp.sum(q * k, axis=-1, keepdims=True) * v
            ms = jnp.mean(y * y, axis=-1, keepdims=True)
            y = y * lax.rsqrt(ms + EPS) * gn
            gt = gate_ref[0, rows, :]
            out_ref[0, rows, :] = y * (gt * jax.nn.sigmoid(gt))
        return f

    lat = (q_ref, k_ref, v_ref, lr_ref)
    ctx = (qc_ref, kc_ref, vc_ref, lrc_ref)
    zero = jnp.zeros((GLA_DV, GLA_DK), F32)
    st = run(ctx, seq_c, zero, 0, store_fwd(yc_ref))
    run(lat, seq, st, 0, store_fwd(y_ref))
    st = run(ctx, seq_c, zero, 1, store_bwd(yc_ref, gtc_ref))
    run(lat, seq, st, 1, store_bwd(y_ref, gt_ref))


def _gla(qkv, p, lr, qkvc, pc, lrc, wg_hi, wg_lo, bg, gnorm, *, ctx_out):
    bsz, seq, _ = qkv.shape
    seq_c = qkvc.shape[1]
    h = GLA_HEADS

    def specs(n):
        return [
            pl.BlockSpec((1, n, 128), lambda b, i: (b, 0, i)),
            pl.BlockSpec((1, n, 128), lambda b, i: (b, 0, 4 + i)),
            pl.BlockSpec((1, n, 256), lambda b, i: (b, 0, 4 + i)),
            pl.BlockSpec((1, n, 256), lambda b, i: (b, 0, COL_GLA_GATE + i)),
            pl.BlockSpec((1, n, 128), lambda b, i: (b, 0, 0)),
        ]

    lat_specs = specs(seq)
    ctx_specs = specs(seq_c)
    lat_args = [qkv, qkv, qkv, p, lr]
    ctx_args = [qkvc, qkvc, qkvc, pc, lrc]
    if not ctx_out:
        del ctx_specs[3], ctx_args[3]
    w_specs = [
        pl.BlockSpec((2, 1, 128, 128), lambda b, i: (0, i, 0, 0)),
        pl.BlockSpec((2, 1, 128, 128), lambda b, i: (0, i, 0, 0)),
        pl.BlockSpec((2, 1, 1, 128), lambda b, i: (0, i, 0, 0)),
        pl.BlockSpec((1, GLA_DV), lambda b, i: (0, 0)),
    ]
    out_specs = [pl.BlockSpec((1, seq, 256), lambda b, i: (b, 0, i))]
    out_shape = [jax.ShapeDtypeStruct((bsz, seq, h * GLA_DV), F32)]
    if ctx_out:
        out_specs.append(pl.BlockSpec((1, seq_c, 256), lambda b, i: (b, 0, i)))
        out_shape.append(jax.ShapeDtypeStruct((bsz, seq_c, h * GLA_DV), F32))
    outs = pl.pallas_call(
        functools.partial(_gla_kernel, ctx_out=ctx_out, seq=seq, seq_c=seq_c),
        grid=(bsz, h),
        in_specs=lat_specs + ctx_specs + w_specs,
        out_specs=out_specs,
        out_shape=out_shape,
        compiler_params=_cparams("parallel", "parallel"),
        name="gla",
    )(*lat_args, *ctx_args, wg_hi, wg_lo, bg, gnorm)
    return (outs[0], outs[1]) if ctx_out else (outs[0], None)


def _mm3f(a, b, dims=_NN):
    a_hi, a_lo = _split(a)
    return _mm3(a_hi, a_lo, b, dims)


def _s5_ops_kernel(a_ref, b_ref, lr_ref, li_ref, btr_ref, bti_ref, cr_ref, ci_ref,
                   kt_ref, bpr_ref, bpi_ref, cp_ref, ltr_ref, lti_ref):
    t_len, h_n, p_n = S5_T, S5_GROUP, S5_STATE
    rows = t_len * h_n
    lane = lax.broadcasted_iota(jnp.int32, (rows, 2 * p_n), 1)
    own = (lane < p_n, lane >= p_n)
    lane_blk = lax.broadcasted_iota(jnp.int32, (rows, rows), 1) // h_n
    kk = lax.broadcasted_iota(jnp.int32, (24, 2 * p_n), 0).astype(F32)
    for d in range(2):
        a, b = a_ref[0, d, 0], b_ref[0, d, 0]
        lr, li = lr_ref[0, d, 0], li_ref[0, d, 0]
        mag = jnp.exp(kk * a)
        pwr, pwi = mag * jnp.cos(kk * b), mag * jnp.sin(kk * b)

        def times_pw(k, mr, mi):
            pr, pi = pwr[k:k + 1], pwi[k:k + 1]
            return pr * mr - pi * mi, pr * mi + pi * mr

        x, y = pwr[1:2] - 1.0, pwi[1:2]
        den = lr * lr + li * li
        cfr, cfi = (x * lr + y * li) / den, (y * lr - x * li) / den
        btr, bti = btr_ref[0, d, 0], bti_ref[0, d, 0]
        bbr, bbi = cfr * btr - cfi * bti, cfr * bti + cfi * btr
        cr, ci = cr_ref[0, d, 0], ci_ref[0, d, 0]

        def stack(power_of_t, mr, mi):
            parts = [times_pw(power_of_t(t), mr, mi) for t in range(t_len)]
            return (jnp.concatenate([p[0] for p in parts], axis=0),
                    jnp.concatenate([p[1] for p in parts], axis=0))

        bpr, bpi = stack((lambda t: t_len - 1 - t) if d == 0 else (lambda t: t), bbr, bbi)
        for src, dst in ((bpr, bpr_ref), (bpi, bpi_ref)):
            z = jnp.concatenate([jnp.where(own[0], src, 0.0), jnp.where(own[1], src, 0.0)], axis=0)
            dst[0, d, 0] = _bf(z.T)
        cpr, cpi = stack((lambda t: t + 1) if d == 0 else (lambda t: t_len - t), cr, ci)
        clr, cli = stack((lambda j: j) if d == 0 else (lambda j: t_len - 1 - j), cr, ci)
        bbr_t = jnp.concatenate([bbr] * t_len, axis=0)
        bbi_t = jnp.concatenate([bbi] * t_len, axis=0)
        for e in range(2):
            cp_ref[0, d, e] = _bf(jnp.concatenate(
                [jnp.where(own[e], cpr, 0.0), jnp.where(own[e], -cpi, 0.0)], axis=1))
            kc = (_mm3f(jnp.where(own[e], clr, 0.0), bbr_t, _NT)
                  - _mm3f(jnp.where(own[e], cli, 0.0), bbi_t, _NT))
            acc = jnp.zeros((rows, rows), F32)
            for t in range(t_len):
                if d == 0:
                    pieces = [jnp.zeros((h_n * t, rows), F32), kc[:rows - h_n * t]]
                else:
                    s = h_n * (t_len - 1 - t)
                    pieces = [kc[s:], jnp.zeros((s, rows), F32)]
                shifted = jnp.concatenate([p for p in pieces if p.shape[0]], axis=0)
                acc = jnp.where(lane_blk == t, shifted, acc)
            kt_ref[0, d, e] = _bf(acc)
        ltr_ref[0, d, 0] = pwr[t_len:t_len + 1]
        lti_ref[0, d, 0] = pwi[t_len:t_len + 1]


def _s5_operators(lam_re, lam_im, log_step, b_re, b_im, c_re, c_im):
    depth = lam_re.shape[0]
    g_n, p_n, h_n = S5_GROUPS, S5_STATE, S5_GROUP
    pairs, rows = g_n // 2, S5_T * h_n
    dt = jnp.exp(log_step)[..., None]
    vec = lambda v: v.reshape(depth, 2, pairs, 1, 2 * p_n)
    bt = lambda m: m.reshape(depth, 2, pairs, 2, p_n, h_n).transpose(0, 1, 2, 5, 3, 4).reshape(depth, 2, pairs, h_n, 2 * p_n)
    ct = lambda m: m.reshape(depth, 2, pairs, 2, h_n, p_n).transpose(0, 1, 2, 4, 3, 5).reshape(depth, 2, pairs, h_n, 2 * p_n)
    vspec = pl.BlockSpec((1, 2, 1, 1, 2 * p_n), lambda l, q: (l, 0, q, 0, 0))
    mspec = pl.BlockSpec((1, 2, 1, h_n, 2 * p_n), lambda l, q: (l, 0, q, 0, 0))
    sq_spec = pl.BlockSpec((1, 2, 2, rows, rows), lambda l, q: (l, 0, q, 0, 0))
    bp_spec = pl.BlockSpec((1, 2, 1, 2 * p_n, 2 * rows), lambda l, q: (l, 0, q, 0, 0))
    sq = jax.ShapeDtypeStruct((depth, 2, g_n, rows, rows), BF16)
    bp = jax.ShapeDtypeStruct((depth, 2, pairs, 2 * p_n, 2 * rows), BF16)
    lt = jax.ShapeDtypeStruct((depth, 2, pairs, 1, 2 * p_n), F32)
    ktoep, bpr, bpi, cpad, ltr, lti = pl.pallas_call(
        _s5_ops_kernel,
        grid=(depth, pairs),
        in_specs=[vspec] * 4 + [mspec] * 4,
        out_specs=[sq_spec, bp_spec, bp_spec, sq_spec, vspec, vspec],
        out_shape=[sq, bp, bp, sq, lt, lt],
        compiler_params=_cparams("parallel", "parallel"),
        name="s5_operators",
    )(vec(lam_re * dt), vec(lam_im * dt), vec(lam_re), vec(lam_im), bt(b_re), bt(b_im), ct(c_re), ct(c_im))
    flat = lambda v: v.reshape(depth, 2, 1, g_n * p_n)
    return ktoep, bpr, bpi, cpad, flat(ltr), flat(lti)


def _s5_in_kernel(u_ref, uc_ref, o_ref):
    j = pl.program_id(0)
    last = pl.num_programs(0) - 1

    def emit(slab_of_t):
        for t in range(S5_T):
            o_ref[:, t * S5_GROUP:(t + 1) * S5_GROUP, :] = _bf(
                slab_of_t(t).T.reshape(S5_GROUPS, S5_GROUP, S5_CB))

    @pl.when(j < last)
    def _():
        emit(lambda t: u_ref[0, :, t, :])

    @pl.when(j == last)
    def _():
        pad = jnp.zeros((S5_CB - 2 * uc_ref.shape[1], S5_WIDTH), F32)
        emit(lambda t: jnp.concatenate([uc_ref[0, :, t, :], uc_ref[1, :, t, :], pad], axis=0))


def _s5_lat_index(n_half):
    def index(j):
        jj = jnp.minimum(j, 2 * n_half - 1)
        return jj // n_half, jj % n_half
    return index


def _s5_relayout_in(p, pc):
    bsz, seq, n = p.shape
    seq_c, n_c = pc.shape[1], pc.shape[2]
    n_lat, n_ctx = seq // S5_T, seq_c // S5_T
    n_half = n_lat // S5_CB
    idx = _s5_lat_index(n_half)
    return pl.pallas_call(
        _s5_in_kernel,
        grid=(bsz * n_half + 1,),
        in_specs=[
            pl.BlockSpec((1, S5_CB, S5_T, 512), lambda j: (*idx(j), 0, COL_S5U)),
            pl.BlockSpec((bsz, n_ctx, S5_T, 512), lambda j: (0, 0, 0, COL_S5U)),
        ],
        out_specs=pl.BlockSpec((S5_GROUPS, S5_T * S5_GROUP, S5_CB), lambda j: (0, 0, j)),
        out_shape=jax.ShapeDtypeStruct((S5_GROUPS, S5_T * S5_GROUP, S5_COLS), BF16),
        compiler_params=_cparams("arbitrary"),
        name="s5_relayout_in",
    )(p.reshape(bsz, n_lat, S5_T, n), pc.reshape(bsz, n_ctx, S5_T, n_c))


def _s5_state_in_kernel(ug_ref, bre_ref, bim_ref, vre_ref, vim_ref):
    u = ug_ref[...].reshape(2 * S5_T * S5_GROUP, S5_COLS)
    for direction in range(2):
        vre_ref[direction] = _mm(bre_ref[0, direction, 0], u)
        vim_ref[direction] = _mm(bim_ref[0, direction, 0], u)


def _s5_state_in(ug, b_pair_re, b_pair_im, layer):
    pairs = S5_GROUPS // 2
    rows = S5_T * S5_GROUP
    lanes = S5_GROUPS * S5_STATE
    out = jax.ShapeDtypeStruct((2, lanes, S5_COLS), F32)
    return pl.pallas_call(
        _s5_state_in_kernel,
        grid=(pairs,),
        in_specs=[
            pl.BlockSpec((2, rows, S5_COLS), lambda q: (q, 0, 0)),
            pl.BlockSpec((1, 2, 1, 128, 2 * rows), lambda q: (layer, 0, q, 0, 0)),
            pl.BlockSpec((1, 2, 1, 128, 2 * rows), lambda q: (layer, 0, q, 0, 0)),
        ],
        out_specs=[pl.BlockSpec((2, 128, S5_COLS), lambda q: (0, q, 0))] * 2,
        out_shape=[out, out],
        compiler_params=_cparams("parallel"),
        name="s5_state_in",
    )(ug, b_pair_re, b_pair_im)


def _s5_scan_kernel(vre_ref, vim_ref, lre_ref, lim_ref, xre_ref, xim_ref, tre_scr, tim_scr, *, n_lat, n_ctx):
    lanes = lre_ref.shape[-1]
    for d in range(2):
        tre_scr[d] = vre_ref[d].T
        tim_scr[d] = vim_ref[d].T
    a_re = [lre_ref[0, d] for d in range(2)]
    a_im = [lim_ref[0, d] for d in range(2)]

    def step(d, row, xr, xi):
        sl = pl.ds(row, 1)
        xre_ref[d, sl, :] = xr
        xim_ref[d, sl, :] = xi
        return (a_re[d] * xr - a_im[d] * xi + tre_scr[d, sl, :],
                a_re[d] * xi + a_im[d] * xr + tim_scr[d, sl, :])

    def run(bases, n, carry):
        def body(i, carry):
            out = []
            for d in range(2):
                j = i if d == 0 else n - 1 - i
                for b in range(2):
                    xr, xi = carry[2 * d + b]
                    out.append(step(d, bases[b] + j, xr, xi))
            return tuple(out)
        return lax.fori_loop(0, n, body, carry)

    zero = jnp.zeros((1, lanes), F32)
    carry = run((2 * n_lat, 2 * n_lat + n_ctx), n_ctx, ((zero, zero),) * 4)
    run((0, n_lat), n_lat, carry)
    pad0 = 2 * n_lat + 2 * n_ctx
    for d in range(2):
        xre_ref[d, pad0:, :] = jnp.zeros((S5_COLS - pad0, lanes), F32)
        xim_ref[d, pad0:, :] = jnp.zeros((S5_COLS - pad0, lanes), F32)


def _s5_scan(vre, vim, lam_re, lam_im, layer, *, n_lat, n_ctx):
    lanes = vre.shape[1]
    tl = 512
    vspec = pl.BlockSpec((2, tl, S5_COLS), lambda j: (0, j, 0))
    xspec = pl.BlockSpec((2, S5_COLS, tl), lambda j: (0, 0, j))
    lspec = pl.BlockSpec((1, 2, 1, tl), lambda j: (layer, 0, 0, j))
    out = jax.ShapeDtypeStruct((2, S5_COLS, lanes), F32)
    return pl.pallas_call(
        functools.partial(_s5_scan_kernel, n_lat=n_lat, n_ctx=n_ctx),
        grid=(lanes // tl,),
        in_specs=[vspec, vspec, lspec, lspec],
        out_specs=[xspec, xspec],
        out_shape=[out, out],
        scratch_shapes=[pltpu.VMEM((2, S5_COLS, tl), F32), pltpu.VMEM((2, S5_COLS, tl), F32)],
        compiler_params=_cparams("parallel"),
        name="s5_scan",
    )(vre, vim, lam_re, lam_im)


def _s5_readout_kernel(ug_ref, kt_ref, cp_ref, xre_ref, xim_ref, y_ref):
    for e in range(2):
        u = ug_ref[e]
        acc = None
        for direction in range(2):
            xcat = _bf(jnp.concatenate([xre_ref[direction], xim_ref[direction]], axis=1))
            term = _mm(kt_ref[0, direction, e], u) + _mm(cp_ref[0, direction, e], xcat, _NT)
            acc = term if acc is None else acc + term
        y_ref[e] = acc


def _s5_readout(ug, ktoep, cpad, xre, xim, layer):
    pairs = S5_GROUPS // 2
    rows = S5_T * S5_GROUP
    return pl.pallas_call(
        _s5_readout_kernel,
        grid=(pairs,),
        in_specs=[
            pl.BlockSpec((2, rows, S5_COLS), lambda q: (q, 0, 0)),
            pl.BlockSpec((1, 2, 2, rows, rows), lambda q: (layer, 0, q, 0, 0)),
            pl.BlockSpec((1, 2, 2, rows, rows), lambda q: (layer, 0, q, 0, 0)),
            pl.BlockSpec((2, S5_COLS, 128), lambda q: (0, 0, q)),
            pl.BlockSpec((2, S5_COLS, 128), lambda q: (0, 0, q)),
        ],
        out_specs=pl.BlockSpec((2, rows, S5_COLS), lambda q: (q, 0, 0)),
        out_shape=jax.ShapeDtypeStruct((S5_GROUPS, rows, S5_COLS), F32),
        compiler_params=_cparams("parallel"),
        name="s5_readout",
    )(ug, ktoep, cpad, xre, xim)


def _s5_out_kernel(*refs, ctx_out):
    if ctx_out:
        yg_ref, u_ref, g_ref, uc_ref, gc_ref, d_ref, w_ref, b_ref, o_ref, oc_ref = refs
    else:
        yg_ref, u_ref, g_ref, d_ref, w_ref, b_ref, o_ref = refs
    d = d_ref[...]
    w = w_ref[...]
    bias = b_ref[...]

    def finish(yy, u, gate):
        yy = yy + d * u
        yg = jax.nn.gelu(yy)
        out = yg * jax.nn.sigmoid(_mm(_bf(yg), w) + bias)
        return out * (gate * jax.nn.sigmoid(gate))

    def y_of(t):
        return yg_ref[:, t * S5_GROUP:(t + 1) * S5_GROUP, :].reshape(S5_WIDTH, S5_CB).T

    def lat():
        for t in range(S5_T):
            o_ref[0, :, t, :] = finish(y_of(t), u_ref[0, :, t, :], g_ref[0, :, t, :])

    if not ctx_out:
        lat()
        return
    j = pl.program_id(0)
    last = pl.num_programs(0) - 1
    pl.when(j < last)(lat)

    @pl.when(j == last)
    def _():
        n_ctx = uc_ref.shape[1]
        for t in range(S5_T):
            y = y_of(t)
            for b in range(2):
                oc_ref[b, :, t, :] = finish(y[b * n_ctx:(b + 1) * n_ctx], uc_ref[b, :, t, :], gc_ref[b, :, t, :])


def _s5_relayout_out(yg, p, pc, d_skip, w_glu, b_glu, *, ctx_out):
    bsz, seq, n = p.shape
    seq_c, n_c = pc.shape[1], pc.shape[2]
    n_lat, n_ctx = seq // S5_T, seq_c // S5_T
    n_half = n_lat // S5_CB
    idx = _s5_lat_index(n_half)
    pv = p.reshape(bsz, n_lat, S5_T, n)
    lat_block = (1, S5_CB, S5_T, 512)
    in_specs = [
        pl.BlockSpec((S5_GROUPS, S5_T * S5_GROUP, S5_CB), lambda j: (0, 0, j)),
        pl.BlockSpec(lat_block, lambda j: (*idx(j), 0, COL_S5U)),
        pl.BlockSpec(lat_block, lambda j: (*idx(j), 0, COL_S5G)),
    ]
    args = [yg, pv, pv]
    out_specs = [pl.BlockSpec(lat_block, lambda j: (*idx(j), 0, 0))]
    out_shape = [jax.ShapeDtypeStruct((bsz, n_lat, S5_T, S5_WIDTH), F32)]
    if ctx_out:
        pcv = pc.reshape(bsz, n_ctx, S5_T, n_c)
        ctx_block = (bsz, n_ctx, S5_T, 512)
        in_specs += [
            pl.BlockSpec(ctx_block, lambda j: (0, 0, 0, COL_S5U)),
            pl.BlockSpec(ctx_block, lambda j: (0, 0, 0, COL_S5G)),
        ]
        args += [pcv, pcv]
        out_specs.append(pl.BlockSpec(ctx_block, lambda j: (0, 0, 0, 0)))
        out_shape.append(jax.ShapeDtypeStruct((bsz, n_ctx, S5_T, S5_WIDTH), F32))
    in_specs += [
        pl.BlockSpec((1, S5_WIDTH), lambda j: (0, 0)),
        pl.BlockSpec((S5_WIDTH, S5_WIDTH), lambda j: (0, 0)),
        pl.BlockSpec((1, S5_WIDTH), lambda j: (0, 0)),
    ]
    args += [d_skip, w_glu, b_glu]
    outs = pl.pallas_call(
        functools.partial(_s5_out_kernel, ctx_out=ctx_out),
        grid=(bsz * n_half + (1 if ctx_out else 0),),
        in_specs=in_specs,
        out_specs=out_specs,
        out_shape=out_shape,
        compiler_params=_cparams("arbitrary"),
        name="s5_relayout_out",
    )(*args)
    y = outs[0].reshape(bsz, seq, S5_WIDTH)
    yc = outs[1].reshape(bsz, seq_c, S5_WIDTH) if ctx_out else None
    return y, yc


def _s5(p, pc, ops, layer, d_skip, w_glu, b_glu, *, ctx_out):
    ktoep, b_pair_re, b_pair_im, cpad, lam_re, lam_im = ops
    ug = _s5_relayout_in(p, pc)
    vre, vim = _s5_state_in(ug, b_pair_re, b_pair_im, layer)
    xre, xim = _s5_scan(vre, vim, lam_re, lam_im, layer, n_lat=p.shape[1] // S5_T, n_ctx=pc.shape[1] // S5_T)
    yg = _s5_readout(ug, ktoep, cpad, xre, xim, layer)
    return _s5_relayout_out(yg, p, pc, d_skip, w_glu, b_glu, ctx_out=ctx_out)


def _dft_tables():
    n1, n2 = FFT_N1, FFT_N2
    n = n1 * n2
    a = jnp.arange(n1, dtype=jnp.int32)
    ph1 = (a[:, None] * a[None, :]) % n1
    ang1 = ph1.astype(F32) * (2.0 * math.pi / n1)
    c1, s1 = jnp.cos(ang1), -jnp.sin(ang1)
    h = n1 // 2
    m1 = jnp.concatenate([jnp.concatenate([c1[:, :h], -s1[:, :h]], 1),
                          jnp.concatenate([s1[:, :h], c1[:, :h]], 1)], 0)
    m1_real = jnp.concatenate([c1, s1], 0)
    m3 = m1.T / n
    k1 = jnp.arange(n1, dtype=jnp.int32)[:, None, None]
    k2 = jnp.arange(n2, dtype=jnp.int32)[None, :, None]
    m2 = jnp.arange(n2, dtype=jnp.int32)[None, None, :]
    ph = (n1 * m2 * k2 + m2 * k1) % n
    ang = ph.astype(F32) * (2.0 * math.pi / n)
    gr, gi = jnp.cos(ang), -jnp.sin(ang)
    g = jnp.concatenate([jnp.concatenate([gr, -gi], 2), jnp.concatenate([gi, gr], 2)], 1)
    gt = jnp.swapaxes(g, 1, 2)
    return tuple(_split(t) for t in (m1, m1_real, m3, g, gt))


def _filter_features(length):
    t = jnp.linspace(0.0, 1.0, length, dtype=F32)[:, None]
    ang = (2.0 * math.pi / length) * jnp.arange(length, dtype=F32)[:, None]
    bands = jnp.linspace(1e-4, HY_BANDS - 1, HY_BANDS, dtype=F32)[None, :]
    feats = jnp.concatenate([t, jnp.cos(bands * ang), -jnp.sin(bands * ang)], axis=-1)
    feats = jnp.pad(feats, ((0, 0), (0, 128 - HY_EMB)))
    rev = jnp.roll(feats[::-1], 1, axis=0)
    return jnp.stack([feats, rev])


def _filter_kernel(f_ref, w1_ref, b1_ref, f1_ref, w2_ref, b2_ref, f2_ref, w3_ref, dl_ref, o_ref):
    half = pl.program_id(1)
    tile = pl.program_id(2)
    x = f_ref[0]
    x_hi, x_lo = _split(x)
    h = jnp.sin(f1_ref[0] * (_mm3(x_hi, x_lo, w1_ref[0]) + b1_ref[0]))
    h_hi, h_lo = _split(h)
    h = jnp.sin(f2_ref[0] * (_mm3(h_hi, h_lo, w2_ref[0]) + b2_ref[0]))
    y = _mm(_bf(h), _bf(w3_ref[0, 0])) * jnp.exp(-x[:, 0:1] * dl_ref[...])
    rows = lax.broadcasted_iota(jnp.int32, y.shape, 0)
    drop = jnp.logical_and(jnp.logical_and(half == 1, tile == 0), rows == 0)
    o_ref[0] = jnp.where(drop, 0.0, y)


def _filter_taps(feats, w1, b1, f1, w2, b2, f2, w3, deltas, tmf):
    depth = w1.shape[0]
    length = feats.shape[1]
    nt = length // tmf
    wide = w3.shape[-1]
    vec = pl.BlockSpec((1, 1, 128), lambda l, s, i: (l, 0, 0))
    sq = pl.BlockSpec((1, 128, 128), lambda l, s, i: (l, 0, 0))
    return pl.pallas_call(
        _filter_kernel,
        grid=(depth, 2, nt),
        in_specs=[
            pl.BlockSpec((1, tmf, 128), lambda l, s, i: (s, i, 0)),
            sq, vec, vec, sq, vec, vec,
            pl.BlockSpec((1, 1, 128, wide), lambda l, s, i: (l, s, 0, 0)),
            pl.BlockSpec((1, wide), lambda l, s, i: (0, 0)),
        ],
        out_specs=pl.BlockSpec((1, tmf, wide), lambda l, s, i: (l, s * nt + i, 0)),
        out_shape=jax.ShapeDtypeStruct((depth, 2 * length, wide), F32),
        compiler_params=_cparams("parallel", "parallel", "parallel"),
        name="hyena_filter",
    )(feats, w1, b1, f1, w2, b2, f2, w3, deltas)


def _parts(pair):
    return tuple(pair) if HY_PASSES == 3 else tuple(pair[:1])


def _mmp(m, b):
    return _mm(m[0], _bf(b)) if len(m) == 1 else _mm3(m[0], m[1], b)


def _blk_rows(n2):
    blk = 2 * FFT_N1
    return pl.ds(pl.multiple_of(n2 * blk, blk), blk)


def _dft_stage1(read_rows, m, a_scr):
    ct = a_scr.shape[-1]

    def body(it, carry):
        n2 = it * HY_NB
        a = _mmp(m, jnp.concatenate([read_rows(n2 + j) for j in range(HY_NB)], axis=1))
        for j in range(HY_NB):
            a_scr[_blk_rows(n2 + j), :] = a[:, j * ct:(j + 1) * ct]
        return carry

    lax.fori_loop(0, FFT_N2 // HY_NB, body, 0)


def _stage2_rows(k1):
    return (pl.ds(k1, FFT_N2, stride=2 * FFT_N1), pl.ds(k1 + FFT_N1, FFT_N2, stride=2 * FFT_N1))


def _stage2_load(a_scr, kb):
    rows = [_stage2_rows(kb * HY_KB + i) for i in range(HY_KB)]
    return rows, [jnp.concatenate([a_scr[re, :], a_scr[im, :]], axis=0) for re, im in rows]


def _spectrum_kernel(*refs):
    np_ = len(_parts((0, 0)))
    m_refs, g_refs, (t_ref, o_ref, a_scr) = refs[:np_], refs[np_:2 * np_], refs[2 * np_:]
    kb = pl.program_id(2)
    ct = t_ref.shape[-1]

    @pl.when(kb == 0)
    def _():
        _dft_stage1(lambda n2: t_ref[0, pl.ds(n2, FFT_N1, stride=FFT_N2), :], [r[...] for r in m_refs], a_scr)

    _, a = _stage2_load(a_scr, kb)
    for i in range(HY_KB):
        o_ref[0, i] = _mmp([r[i] for r in g_refs], a[i]).reshape(2, FFT_N2, ct)


def _filter_spectrum(tables, taps):
    m, g = _parts(tables[1]), _parts(tables[3])
    depth, n, ch = taps.shape
    ct = HY_CT
    gspec = pl.BlockSpec((HY_KB, 2 * FFT_N2, 2 * FFT_N2), lambda l, c, k: (k, 0, 0))
    mspec = pl.BlockSpec(m[0].shape, lambda l, c, k: (0, 0))
    return pl.pallas_call(
        _spectrum_kernel,
        grid=(depth, ch // ct, FFT_N1 // HY_KB),
        in_specs=[mspec] * len(m) + [gspec] * len(g) + [pl.BlockSpec((1, n, ct), lambda l, c, k: (l, 0, c))],
        out_specs=pl.BlockSpec((1, HY_KB, 2, FFT_N2, ct), lambda l, c, k: (l, k, 0, 0, c)),
        out_shape=jax.ShapeDtypeStruct((depth, FFT_N1, 2, FFT_N2, ch), F32),
        scratch_shapes=[pltpu.VMEM((2 * FFT_N1 * FFT_N2, ct), F32)],
        compiler_params=_cparams("parallel", "parallel", "arbitrary"),
        name="hyena_spectrum",
    )(*m, *g, taps)


def _hy_pre_kernel(pv_ref, p1_ref, p2_ref, pg_ref, wv_ref, w1_ref, w2_ref, bv_ref, b1_ref, b2_ref,
                   v_ref, x1_ref, x2_ref):
    n = pv_ref.shape[1]
    rows = lax.broadcasted_iota(jnp.int32, (n, 128), 0)

    def conv(p, w_ref, b_ref):
        up = jnp.where(rows == 0, 0.0, pltpu.roll(p, 1, 0))
        dn = jnp.where(rows == n - 1, 0.0, pltpu.roll(p, n - 1, 0))
        return up * w_ref[0:1, :] + p * w_ref[1:2, :] + dn * w_ref[2:3, :] + b_ref[...]

    v_ref[0] = conv(pv_ref[0], wv_ref, bv_ref)
    x1_ref[0] = conv(p1_ref[0], w1_ref, b1_ref)
    gate = pg_ref[0]
    x2_ref[0] = conv(p2_ref[0], w2_ref, b2_ref) * (gate * jax.nn.sigmoid(gate))


def _hy_pre(p, conv_w, conv_b):
    bsz, seq, _ = p.shape
    def pspec(off):
        return pl.BlockSpec((1, seq, 128), lambda b, j: (b, 0, off + j))
    def wspec(off):
        return pl.BlockSpec((3, 128), lambda b, j: (0, off + j))
    def bspec(off):
        return pl.BlockSpec((1, 128), lambda b, j: (0, off + j))
    out = jax.ShapeDtypeStruct((bsz, seq, HY_WIDTH), F32)
    ospec = pl.BlockSpec((1, seq, 128), lambda b, j: (b, 0, j))
    return pl.pallas_call(
        _hy_pre_kernel,
        grid=(bsz, HY_WIDTH // 128),
        in_specs=[pspec(COL_HY), pspec(COL_HY + 4), pspec(COL_HY + 8), pspec(COL_HY + 12),
                  wspec(0), wspec(4), wspec(8), bspec(0), bspec(4), bspec(8)],
        out_specs=[ospec, ospec, ospec],
        out_shape=[out, out, out],
        compiler_params=_cparams("parallel", "parallel"),
        name="hyena_pre",
    )(p, p, p, p, conv_w, conv_w, conv_w, conv_b, conv_b, conv_b)


def _hy_conv_kernel(*refs):
    np_ = len(_parts((0, 0)))
    m1_refs, m3_refs, g_refs, t_refs = (refs[i * np_:(i + 1) * np_] for i in range(4))
    u_ref, gate_ref, h_ref, d_ref, o_ref, a_scr = refs[4 * np_:]
    kb = pl.program_id(1)
    ct = u_ref.shape[-1]
    half = FFT_N1 // 2

    def rows(n2):
        return pl.ds(n2, half, stride=FFT_N2)

    @pl.when(kb == 0)
    def _():
        _dft_stage1(lambda n2: jnp.concatenate([u_ref[0, rows(n2), :], u_ref[1, rows(n2), :]], axis=0),
                    [r[...] for r in m1_refs], a_scr)

    srows, a = _stage2_load(a_scr, kb)
    z = [_mmp([r[i] for r in g_refs], a[i]) for i in range(HY_KB)]
    z2 = []
    for i in range(HY_KB):
        zr, zi = z[i][:FFT_N2], z[i][FFT_N2:]
        hr, hi = h_ref[0, i, 0], h_ref[0, i, 1]
        z2.append(jnp.concatenate([zr * hr - zi * hi, zr * hi + zi * hr], axis=0))
    a2 = [_mmp([r[i] for r in t_refs], z2[i]) for i in range(HY_KB)]
    for i in range(HY_KB):
        a_scr[srows[i][0], :] = a2[i][:FFT_N2]
        a_scr[srows[i][1], :] = a2[i][FFT_N2:]

    @pl.when(kb == pl.num_programs(1) - 1)
    def _():
        m3 = [r[...] for r in m3_refs]
        d = d_ref[...]

        def body(it, carry):
            n2 = it * HY_NB
            y = _mmp(m3, jnp.concatenate([a_scr[_blk_rows(n2 + j), :] for j in range(HY_NB)], axis=1))
            for j in range(HY_NB):
                r = rows(n2 + j)
                for b in range(2):
                    yb = y[b * half:(b + 1) * half, j * ct:(j + 1) * ct]
                    o_ref[b, r, :] = gate_ref[b, r, :] * (yb + d * u_ref[b, r, :])
            return carry

        lax.fori_loop(0, FFT_N2 // HY_NB, body, 0)


def _hy_long_conv(tables, spec, layer, order, u, gate, d):
    m1, _, m3, g, gt = (_parts(t) for t in tables)
    bsz, seq, ch = u.shape
    ct = HY_CT
    gspec = pl.BlockSpec((HY_KB, 2 * FFT_N2, 2 * FFT_N2), lambda c, k: (k, 0, 0))
    full = lambda t: pl.BlockSpec(t.shape, lambda c, k: (0, 0))
    uspec = pl.BlockSpec((bsz, seq, ct), lambda c, k: (0, 0, c))
    n_ct = ch // ct
    mats = (*m1, *m3, *g, *gt)
    return pl.pallas_call(
        _hy_conv_kernel,
        grid=(n_ct, FFT_N1 // HY_KB),
        in_specs=[full(t) for t in (*m1, *m3)] + [gspec] * (len(g) + len(gt)) + [
            uspec, uspec,
            pl.BlockSpec((1, HY_KB, 2, FFT_N2, ct), lambda c, k: (layer, k, 0, 0, order * n_ct + c)),
            pl.BlockSpec((1, ct), lambda c, k: (0, c))],
        out_specs=uspec,
        out_shape=jax.ShapeDtypeStruct(u.shape, F32),
        scratch_shapes=[pltpu.VMEM((2 * FFT_N1 * FFT_N2, ct), F32)],
        compiler_params=_cparams("parallel", "arbitrary"),
        name="hyena_conv",
    )(*mats, u, gate, spec, d.reshape(1, ch))


def _ctx_dft_tables(length):
    n = 2 * length
    k = jnp.arange(n, dtype=jnp.int32)
    ang = ((k[:, None] * k[None, :]) % n).astype(F32) * (2.0 * math.pi / n)
    fr, fi = jnp.cos(ang), -jnp.sin(ang)
    fwd = jnp.concatenate([jnp.concatenate([fr[:, :length], -fi[:, :length]], 1),
                           jnp.concatenate([fi[:, :length], fr[:, :length]], 1)], 0)
    real = jnp.concatenate([fr, fi], 0)
    inv = fwd.T / n
    return tuple(_split(t) for t in (fwd, real, inv))


def _hyc_kernel(fh_ref, fl_ref, rh_ref, rl_ref, ih_ref, il_ref, taps_ref, u_ref, g_ref, d_ref, o_ref):
    n = u_ref.shape[1]
    x = jnp.concatenate([u_ref[0], u_ref[1]], axis=0)
    z = _mm3(fh_ref[...], fl_ref[...], x)
    h = _mm3(rh_ref[...], rl_ref[...], taps_ref[0])
    m = 2 * n
    zr, zi, hr, hi = z[:m], z[m:], h[:m], h[m:]
    z2 = jnp.concatenate([zr * hr - zi * hi, zr * hi + zi * hr], axis=0)
    y = _mm3(ih_ref[...], il_ref[...], z2)
    for b in range(2):
        o_ref[b] = g_ref[b] * (y[b * n:(b + 1) * n] + d_ref[...] * u_ref[b])


def _hyc_long_conv(ctabs, taps, layer, order, u, gate, d):
    (f_hi, f_lo), (r_hi, r_lo), (i_hi, i_lo) = ctabs
    bsz, n, ch = u.shape
    full = lambda t: pl.BlockSpec(t.shape, lambda i: (0,) * t.ndim)
    uspec = pl.BlockSpec((bsz, n, ch), lambda i: (0, 0, 0))
    return pl.pallas_call(
        _hyc_kernel,
        grid=(1,),
        in_specs=[full(f_hi), full(f_lo), full(r_hi), full(r_lo), full(i_hi), full(i_lo),
                  pl.BlockSpec((1, 2 * n, ch), lambda i: (layer, 0, order)),
                  uspec, uspec, pl.BlockSpec((1, ch), lambda i: (0, 0))],
        out_specs=uspec,
        out_shape=jax.ShapeDtypeStruct((bsz, n, ch), F32),
        compiler_params=_cparams("arbitrary"),
        name="hyena_ctx",
    )(f_hi, f_lo, r_hi, r_lo, i_hi, i_lo, taps, u, gate, d.reshape(1, ch))


def _outproj_kernel(x_ref, ya_ref, yb_ref, yc_ref, w_ref, gp_ref, gate_ref, o_ref, *, colmajor, tm):
    na, nb = ya_ref.shape[-1], yb_ref.shape[-1]
    acc = (_mm(_bf(ya_ref[0]), w_ref[0, 0:na, :]) + _mm(_bf(yb_ref[0]), w_ref[0, na:na + nb, :])
           + _mm(_bf(yc_ref[0]), w_ref[0, na + nb:, :]))
    ms = jnp.mean(acc * acc, axis=-1, keepdims=True)
    upd = gate_ref[0] * (acc * lax.rsqrt(ms + EPS) * gp_ref[...])
    if colmajor:
        for j in range(tm // GRID_W):
            o_ref[0, :, j, :] = x_ref[0, :, j, :] + upd[j * GRID_W:(j + 1) * GRID_W]
    else:
        o_ref[0] = x_ref[0] + upd


def _outproj(x, ya, yb, yc, w_out, layer, g_post, gate, *, colmajor, tm):
    bsz, seq, d = x.shape
    if colmajor:
        x_in = x.reshape(bsz, GRID_W, seq // GRID_W, d)
        x_spec = pl.BlockSpec((1, GRID_W, tm // GRID_W, d), lambda b, m: (b, 0, m, 0))
    else:
        x_in = x
        x_spec = pl.BlockSpec((1, tm, d), lambda b, m: (b, m, 0))
    def yspec(t):
        return pl.BlockSpec((1, tm, t.shape[-1]), lambda b, m: (b, m, 0))
    out = pl.pallas_call(
        functools.partial(_outproj_kernel, colmajor=colmajor, tm=tm),
        grid=(bsz, seq // tm),
        in_specs=[x_spec, yspec(ya), yspec(yb), yspec(yc),
                  pl.BlockSpec((1,) + w_out.shape[1:], lambda b, m: (layer, 0, 0)),
                  pl.BlockSpec((1, d), lambda b, m: (0, 0)),
                  pl.BlockSpec((1, 1, d), lambda b, m: (b, 0, 0))],
        out_specs=x_spec,
        out_shape=jax.ShapeDtypeStruct(x_in.shape, F32),
        compiler_params=_cparams("parallel", "parallel"),
        name="outproj",
    )(x_in, ya, yb, yc, w_out, g_post, gate)
    return out.reshape(bsz, seq, d)


def kernel(x, c, ctx, c_ctx, w_mod, b_mod, g_pre, g_post, w_in, w_out, gla_w_gate, gla_b_gate, gla_norm, s5_lam_re, s5_lam_im, s5_log_step, s5_b_re, s5_b_im, s5_c_re, s5_c_im, s5_d, s5_w_glu, s5_b_glu, hy_conv_w, hy_conv_b, hy_w1, hy_b1, hy_f1, hy_w2, hy_b2, hy_f2, hy_w3, hy_d):
    bsz, seq, d = x.shape
    seq_c = ctx.shape[1]
    depth = w_in.shape[0]
    assert (bsz, d, seq // GRID_W) == (2, D_MODEL, GRID_W) and seq == FFT_N1 * FFT_N2 // 2

    cvec = jnp.concatenate([c, c_ctx[None], jnp.zeros((8 - bsz - 1, d), F32)], axis=0)
    mod = _modulation(cvec, w_mod, b_mod)
    shift, scale, gate = mod[..., :d], mod[..., d:2 * d], mod[..., 2 * d:]

    w_t = _bf(jnp.swapaxes(w_in, 1, 2))
    w_out_b = _bf(w_out)
    w_glu_b = _bf(s5_w_glu)

    wg = gla_w_gate.reshape(depth, 2, 16, GLA_HEADS, GLA_DK).transpose(0, 1, 3, 2, 4)
    wg_pad = jnp.stack([jnp.pad(wg[:, 0], ((0, 0), (0, 0), (0, 112), (0, 0))),
                        jnp.pad(wg[:, 1], ((0, 0), (0, 0), (16, 96), (0, 0)))], axis=1)
    wg_hi, wg_lo = _split(wg_pad)
    bg = gla_b_gate.reshape(depth, 2, GLA_HEADS, 1, GLA_DK)

    tables = _dft_tables()
    deltas = jnp.abs(jnp.linspace(HY_MIN_DECAY, HY_MAX_DECAY, HY_WIDTH, dtype=F32))
    deltas2 = jnp.tile(deltas, 2).reshape(1, 2 * HY_WIDTH)
    pad_k = lambda w: jnp.pad(w, ((0, 0), (0, 128 - w.shape[1]), (0, 128 - w.shape[2])))
    pad_v = lambda v: jnp.pad(v, ((0, 0), (0, 128 - v.shape[1]))).reshape(depth, 1, 128)
    w3 = hy_w3.reshape(depth, HY_FFN, 2, 2, HY_WIDTH).transpose(0, 3, 1, 2, 4)
    w3 = jnp.pad(w3.reshape(depth, 2, HY_FFN, 2 * HY_WIDTH), ((0, 0), (0, 0), (0, 128 - HY_FFN), (0, 0)))
    filt_w = (pad_k(hy_w1), pad_v(hy_b1), pad_v(hy_f1), pad_k(hy_w2), pad_v(hy_b2), pad_v(hy_f2), w3, deltas2)
    taps = _filter_taps(_filter_features(seq), *filt_w, tmf=512)
    spec = _filter_spectrum(tables, taps)
    taps_c = _filter_taps(_filter_features(seq_c), *filt_w, tmf=seq_c)
    ctabs = _ctx_dft_tables(seq_c)

    s5_ops = _s5_operators(s5_lam_re, s5_lam_im, s5_log_step, s5_b_re, s5_b_im, s5_c_re, s5_c_im)

    xc = ctx
    for l in range(depth):
        last = l == depth - 1
        ctx_out = not last
        colmajor = l % 2 == 1
        pre = g_pre[l][None, :] * (1.0 + scale[l])
        sc_l, sh_l = pre[:bsz, None, :], shift[l, :bsz, None, :]
        sc_c = jnp.broadcast_to(pre[bsz][None, None, :], (bsz, 1, d))
        sh_c = jnp.broadcast_to(shift[l, bsz][None, None, :], (bsz, 1, d))
        qkv, p, lr = _inproj(x, sc_l, sh_l, w_t, l, n=N_MAIN, colmajor=colmajor, tm=1024, tn=512)
        qkvc, pc, lrc = _inproj(xc, sc_c, sh_c, w_t, l, n=N_MAIN if ctx_out else N_STATE,
                                colmajor=False, tm=seq_c, tn=512)

        y_gla, yc_gla = _gla(qkv, p, lr, qkvc, pc, lrc, wg_hi[l], wg_lo[l], bg[l], gla_norm[l][None, :],
                             ctx_out=ctx_out)

        y_s5, yc_s5 = _s5(p, pc, s5_ops, l, s5_d[l][None, :], w_glu_b[l], s5_b_glu[l][None, :], ctx_out=ctx_out)

        v, x1, x2 = _hy_pre(p, hy_conv_w[l], hy_conv_b[l][None, :])
        z = _hy_long_conv(tables, spec, l, 0, v, x1, hy_d[l, 0])
        y_hy = _hy_long_conv(tables, spec, l, 1, z, x2, hy_d[l, 1])

        x_new = _outproj(x, y_gla, y_s5, y_hy, w_out_b, l, g_post[l][None, :], gate[l, :bsz, None, :],
                         colmajor=colmajor, tm=512)
        if ctx_out:
            vc, x1c, x2c = _hy_pre(pc, hy_conv_w[l], hy_conv_b[l][None, :])
            zc = _hyc_long_conv(ctabs, taps_c, l, 0, vc, x1c, hy_d[l, 0])
            yc_hy = _hyc_long_conv(ctabs, taps_c, l, 1, zc, x2c, hy_d[l, 1])
            gate_c = jnp.broadcast_to(gate[l, bsz][None, None, :], (bsz, 1, d))
            xc = _outproj(xc, yc_gla, yc_s5, yc_hy, w_out_b, l, g_post[l][None, :], gate_c,
                          colmajor=False, tm=seq_c)
        x = x_new
    return x
```

```python
import functools
import math

import numpy as np
import jax
import jax.numpy as jnp
from jax import lax
from jax.experimental import pallas as pl
from jax.experimental.pallas import tpu as pltpu

F32 = jnp.float32
BF16 = jnp.bfloat16

D_MODEL = 2048
GRID_W = 64
EPS = 1e-6

GLA_HEADS = 4
GLA_DK = 128
GLA_DV = 256
GLA_TAU = 16.0
GLA_CHUNK = 64
GLA_BLOCK = 8

S5_WIDTH = 512
S5_GROUP = 16
S5_GROUPS = 32
S5_STATE = 64
S5_T = 16
S5_COLS = 640
S5_CB = 128

HY_WIDTH = 512
HY_EMB = 33
HY_BANDS = 16
HY_FFN = 64
HY_MIN_DECAY = math.log(1e-2) / 0.3
HY_MAX_DECAY = math.log(1e-2) / 1.5
FFT_N1 = 64
FFT_N2 = 128
HY_CT = 128
HY_KB = 8
HY_NB = 4
HY_PITCH = 136
HY_PASSES = 1

N_MAIN = 6144
N_STATE = 2560
N_QKV = 2048
N_LR = 32
COL_S5U, COL_S5G = 0, 3
COL_GLA_GATE = 2
COL_HY = 16

VMEM_LIMIT_BYTES = 56 * 1024 * 1024


def _cparams(*sem):
    return pltpu.CompilerParams(dimension_semantics=sem, vmem_limit_bytes=VMEM_LIMIT_BYTES)


def _bf(x):
    return x.astype(BF16)


def _split(x):
    hi = _bf(x)
    return hi, _bf(x - hi.astype(F32))


def _split3(x):
    a = _bf(x)
    r = x - a.astype(F32)
    b = _bf(r)
    return a, b, _bf(r - b.astype(F32))


_NN = (((1,), (0,)), ((), ()))
_NT = (((1,), (1,)), ((), ()))
_TN = (((0,), (0,)), ((), ()))


def _mm(a, b, dims=_NN):
    return lax.dot_general(a, b, dims, preferred_element_type=F32)


def _mm3(a_hi, a_lo, b, dims=_NN):
    b_hi, b_lo = _split(b)
    return _mm(a_hi, b_hi, dims) + _mm(a_hi, b_lo, dims) + _mm(a_lo, b_hi, dims)


def _mod_kernel(s_ref, w_ref, b_ref, o_ref):
    s = s_ref[...]
    s = s * jax.nn.sigmoid(s)
    s_hi, s_lo = _split(s)
    o_ref[0] = _mm3(s_hi, s_lo, w_ref[0]) + b_ref[0]


def _modulation(cvec, w_mod, b_mod):
    depth, d, n = w_mod.shape
    tn = 512
    return pl.pallas_call(
        _mod_kernel,
        grid=(depth, n // tn),
        in_specs=[
            pl.BlockSpec((8, d), lambda l, j: (0, 0)),
            pl.BlockSpec((1, d, tn), lambda l, j: (l, 0, j)),
            pl.BlockSpec((1, 1, tn), lambda l, j: (l, 0, j)),
        ],
        out_specs=pl.BlockSpec((1, 8, tn), lambda l, j: (l, 0, j)),
        out_shape=jax.ShapeDtypeStruct((depth, 8, n), F32),
        compiler_params=_cparams("parallel", "parallel"),
        name="modulation",
    )(cvec, w_mod, b_mod.reshape(depth, 1, n))


def _inproj_kernel(x_ref, sc_ref, sh_ref, w_ref, wlr_ref, qkv_ref, p_ref, lr_ref, h_scr, *, colmajor, tm, tn):
    j = pl.program_id(2)

    @pl.when(j == 0)
    def _():
        def norm(xb):
            ms = jnp.mean(xb * xb, axis=-1, keepdims=True)
            return _bf(xb * lax.rsqrt(ms + EPS) * sc_ref[0] + sh_ref[0])

        if colmajor:
            for j in range(tm // GRID_W):
                h_scr[j * GRID_W:(j + 1) * GRID_W, :] = norm(x_ref[0, :, j, :])
        else:
            h_scr[...] = norm(x_ref[0])
        lr_ref[0] = _mm(h_scr[...], wlr_ref[...], _NT)

    res = _mm(h_scr[...], w_ref[...], _NT)
    n_qkv = N_QKV // tn

    @pl.when(j < n_qkv)
    def _():
        qkv_ref[0] = _bf(res)

    @pl.when(j >= n_qkv)
    def _():
        p_ref[0] = res


def _inproj(x, scale, shift, w_t, layer, *, n, colmajor, tm, tn):
    bsz, seq, d = x.shape
    n_qkv = N_QKV // tn
    if colmajor:
        x_in = x.reshape(bsz, GRID_W, seq // GRID_W, d)
        x_spec = pl.BlockSpec((1, GRID_W, tm // GRID_W, d), lambda b, m, j: (b, 0, m, 0))
    else:
        x_in = x
        x_spec = pl.BlockSpec((1, tm, d), lambda b, m, j: (b, m, 0))
    return pl.pallas_call(
        functools.partial(_inproj_kernel, colmajor=colmajor, tm=tm, tn=tn),
        grid=(bsz, seq // tm, n // tn),
        in_specs=[
            x_spec,
            pl.BlockSpec((1, 1, d), lambda b, m, j: (b, 0, 0)),
            pl.BlockSpec((1, 1, d), lambda b, m, j: (b, 0, 0)),
            pl.BlockSpec((pl.Squeezed(), pl.Element(tn), pl.Element(d)),
                         lambda b, m, j: (layer, pl.multiple_of(j * tn + jnp.where(j >= n_qkv, N_LR, 0), N_LR), 0)),
            pl.BlockSpec((pl.Squeezed(), pl.Element(128), pl.Element(d)), lambda b, m, j: (layer, N_QKV, 0)),
        ],
        out_specs=[
            pl.BlockSpec((1, tm, tn), lambda b, m, j: (b, m, jnp.minimum(j, n_qkv - 1))),
            pl.BlockSpec((1, tm, tn), lambda b, m, j: (b, m, jnp.maximum(j - n_qkv, 0))),
            pl.BlockSpec((1, tm, 128), lambda b, m, j: (b, m, 0)),
        ],
        out_shape=[jax.ShapeDtypeStruct((bsz, seq, N_QKV), BF16),
                   jax.ShapeDtypeStruct((bsz, seq, n - N_QKV), F32),
                   jax.ShapeDtypeStruct((bsz, seq, 128), F32)],
        scratch_shapes=[pltpu.VMEM((tm, d), BF16)],
        compiler_params=_cparams("parallel", "parallel", "arbitrary"),
        name="inproj",
    )(x_in, scale, shift, w_t, w_t)


def _gla_block(q, k, v, lr, st, wg_hi, wg_lo, bg, direction, mask, rowc, nb):
    c = GLA_CHUNK
    n = c * nb
    lr_hi, lr_lo = _split(lr)
    z = _mm(lr_hi, wg_hi) + _mm(lr_hi, wg_lo) + _mm(lr_lo, wg_hi) + bg
    g = (jnp.minimum(z, 0.0) - jnp.log1p(jnp.exp(-jnp.abs(z)))) * (1.0 / GLA_TAU)
    cum = g
    s = 1
    while s < c:
        if direction == 0:
            cum = cum + jnp.where(rowc >= s, pltpu.roll(cum, s, 0), 0.0)
        else:
            cum = cum + jnp.where(rowc < c - s, pltpu.roll(cum, n - s, 0), 0.0)
        s *= 2
    cum3 = cum.reshape(nb, c, GLA_DK)
    tot3 = cum3[:, c - 1:c, :] if direction == 0 else cum3[:, 0:1, :]
    q3 = q.reshape(nb, c, GLA_DK)
    k3 = k.reshape(nb, c, GLA_DK)
    qg3 = _bf(q3 * jnp.exp(cum3))
    kg3 = _bf(k3 * jnp.exp(-cum3))
    kd3 = _bf(k3 * jnp.exp(tot3 - cum3))
    a3 = jnp.exp(tot3)
    vb3 = _bf(v).reshape(nb, c, GLA_DV)
    att = [jnp.where(mask, _mm(qg3[i], kg3[i], _NT), 0.0) for i in range(nb)]
    o_in = [_mm(_bf(att[i]), vb3[i]) for i in range(nb)]
    d_st = [_mm(vb3[i], kd3[i], _TN) for i in range(nb)]
    outs = [None] * nb
    for i in (range(nb) if direction == 0 else range(nb - 1, -1, -1)):
        outs[i] = o_in[i] + _mm(qg3[i], _bf(st), _NT)
        st = st * a3[i] + d_st[i]
    return jnp.concatenate(outs, axis=0), st


def _gla_kernel(*refs, ctx_out, seq, seq_c):
    if ctx_out:
        (q_ref, k_ref, v_ref, gt_ref, lr_ref, qc_ref, kc_ref, vc_ref, gtc_ref, lrc_ref,
         wgh_ref, wgl_ref, bg_ref, gn_ref, y_ref, yc_ref) = refs
    else:
        (q_ref, k_ref, v_ref, gt_ref, lr_ref, qc_ref, kc_ref, vc_ref, lrc_ref,
         wgh_ref, wgl_ref, bg_ref, gn_ref, y_ref) = refs
        gtc_ref = yc_ref = None
    c = GLA_CHUNK
    scale = GLA_DK ** -0.5
    row = lax.broadcasted_iota(jnp.int32, (c, c), 0)
    col = lax.broadcasted_iota(jnp.int32, (c, c), 1)
    masks = (col <= row, col >= row)
    gn = gn_ref[...]

    def run(refs4, n_rows, st, direction, store):
        qr, kr, vr, lrr = refs4
        nb = min(GLA_BLOCK, n_rows // c)
        n = c * nb
        rowc = lax.broadcasted_iota(jnp.int32, (n, GLA_DK), 0) % c
        wg_hi = wgh_ref[direction, 0]
        wg_lo = wgl_ref[direction, 0]
        bg = bg_ref[direction, 0]
        n_blocks = n_rows // n

        def body(i, st):
            bi = i if direction == 0 else n_blocks - 1 - i
            rows = pl.ds(pl.multiple_of(bi * n, n), n)
            q = qr[0, rows, :].astype(F32) * scale
            k = kr[0, rows, :].astype(F32)
            v = vr[0, rows, :].astype(F32)
            o, st = _gla_block(q, k, v, lrr[0, rows, :], st, wg_hi, wg_lo, bg, direction,
                               masks[direction], rowc, nb)
            store(rows, q, k, v, o)
            return st

        return lax.fori_loop(0, n_blocks, body, st)

    def store_fwd(out_ref):
        def f(rows, q, k, v, o):
            if out_ref is not None:
                out_ref[0, rows, :] = o
        return f

    def store_bwd(out_ref, gate_ref):
        def f(rows, q, k, v, o):
            if out_ref is None:
                return
            y = out_ref[0, rows, :] + o - jnp.sum(q * k, axis=-1, keepdims=True) * v
            ms = jnp.mean(y * y, axis=-1, keepdims=True)
            y = y * lax.rsqrt(ms + EPS) * gn
            gt = gate_ref[0, rows, :]
            out_ref[0, rows, :] = y * (gt * jax.nn.sigmoid(gt))
        return f

    lat = (q_ref, k_ref, v_ref, lr_ref)
    ctx = (qc_ref, kc_ref, vc_ref, lrc_ref)
    zero = jnp.zeros((GLA_DV, GLA_DK), F32)
    st = run(ctx, seq_c, zero, 0, store_fwd(yc_ref))
    run(lat, seq, st, 0, store_fwd(y_ref))
    st = run(ctx, seq_c, zero, 1, store_bwd(yc_ref, gtc_ref))
    run(lat, seq, st, 1, store_bwd(y_ref, gt_ref))


def _gla(qkv, p, lr, qkvc, pc, lrc, wg_hi, wg_lo, bg, gnorm, *, ctx_out):
    bsz, seq, _ = qkv.shape
    seq_c = qkvc.shape[1]
    h = GLA_HEADS

    def specs(n):
        return [
            pl.BlockSpec((1, n, 128), lambda b, i: (b, 0, i)),
            pl.BlockSpec((1, n, 128), lambda b, i: (b, 0, 4 + i)),
            pl.BlockSpec((1, n, 256), lambda b, i: (b, 0, 4 + i)),
            pl.BlockSpec((1, n, 256), lambda b, i: (b, 0, COL_GLA_GATE + i)),
            pl.BlockSpec((1, n, 128), lambda b, i: (b, 0, 0)),
        ]

    lat_specs = specs(seq)
    ctx_specs = specs(seq_c)
    lat_args = [qkv, qkv, qkv, p, lr]
    ctx_args = [qkvc, qkvc, qkvc, pc, lrc]
    if not ctx_out:
        del ctx_specs[3], ctx_args[3]
    w_specs = [
        pl.BlockSpec((2, 1, 128, 128), lambda b, i: (0, i, 0, 0)),
        pl.BlockSpec((2, 1, 128, 128), lambda b, i: (0, i, 0, 0)),
        pl.BlockSpec((2, 1, 1, 128), lambda b, i: (0, i, 0, 0)),
        pl.BlockSpec((1, GLA_DV), lambda b, i: (0, 0)),
    ]
    out_specs = [pl.BlockSpec((1, seq, 256), lambda b, i: (b, 0, i))]
    out_shape = [jax.ShapeDtypeStruct((bsz, seq, h * GLA_DV), F32)]
    if ctx_out:
        out_specs.append(pl.BlockSpec((1, seq_c, 256), lambda b, i: (b, 0, i)))
        out_shape.append(jax.ShapeDtypeStruct((bsz, seq_c, h * GLA_DV), F32))
    outs = pl.pallas_call(
        functools.partial(_gla_kernel, ctx_out=ctx_out, seq=seq, seq_c=seq_c),
        grid=(bsz, h),
        in_specs=lat_specs + ctx_specs + w_specs,
        out_specs=out_specs,
        out_shape=out_shape,
        compiler_params=_cparams("parallel", "parallel"),
        name="gla",
    )(*lat_args, *ctx_args, wg_hi, wg_lo, bg, gnorm)
    return (outs[0], outs[1]) if ctx_out else (outs[0], None)


def _mm3f(a, b, dims=_NN):
    a_hi, a_lo = _split(a)
    return _mm3(a_hi, a_lo, b, dims)


def _s5_ops_kernel(a_ref, b_ref, lr_ref, li_ref, btr_ref, bti_ref, cr_ref, ci_ref,
                   kt_ref, bpr_ref, bpi_ref, cp_ref, ltr_ref, lti_ref):
    t_len, h_n, p_n = S5_T, S5_GROUP, S5_STATE
    rows = t_len * h_n
    lane = lax.broadcasted_iota(jnp.int32, (rows, 2 * p_n), 1)
    own = (lane < p_n, lane >= p_n)
    lane_blk = lax.broadcasted_iota(jnp.int32, (rows, rows), 1) // h_n
    kk = lax.broadcasted_iota(jnp.int32, (24, 2 * p_n), 0).astype(F32)
    for d in range(2):
        a, b = a_ref[0, d, 0], b_ref[0, d, 0]
        lr, li = lr_ref[0, d, 0], li_ref[0, d, 0]
        mag = jnp.exp(kk * a)
        pwr, pwi = mag * jnp.cos(kk * b), mag * jnp.sin(kk * b)

        def times_pw(k, mr, mi):
            pr, pi = pwr[k:k + 1], pwi[k:k + 1]
            return pr * mr - pi * mi, pr * mi + pi * mr

        x, y = pwr[1:2] - 1.0, pwi[1:2]
        den = lr * lr + li * li
        cfr, cfi = (x * lr + y * li) / den, (y * lr - x * li) / den
        btr, bti = btr_ref[0, d, 0], bti_ref[0, d, 0]
        bbr, bbi = cfr * btr - cfi * bti, cfr * bti + cfi * btr
        cr, ci = cr_ref[0, d, 0], ci_ref[0, d, 0]

        def stack(power_of_t, mr, mi):
            parts = [times_pw(power_of_t(t), mr, mi) for t in range(t_len)]
            return (jnp.concatenate([p[0] for p in parts], axis=0),
                    jnp.concatenate([p[1] for p in parts], axis=0))

        bpr, bpi = stack((lambda t: t_len - 1 - t) if d == 0 else (lambda t: t), bbr, bbi)
        for src, dst in ((bpr, bpr_ref), (bpi, bpi_ref)):
            z = jnp.concatenate([jnp.where(own[0], src, 0.0), jnp.where(own[1], src, 0.0)], axis=0)
            dst[0, d, 0] = _bf(z.T)
        cpr, cpi = stack((lambda t: t + 1) if d == 0 else (lambda t: t_len - t), cr, ci)
        clr, cli = stack((lambda j: j) if d == 0 else (lambda j: t_len - 1 - j), cr, ci)
        bbr_t = jnp.concatenate([bbr] * t_len, axis=0)
        bbi_t = jnp.concatenate([bbi] * t_len, axis=0)
        for e in range(2):
            cp_ref[0, d, e] = _bf(jnp.concatenate(
                [jnp.where(own[e], cpr, 0.0), jnp.where(own[e], -cpi, 0.0)], axis=1))
            kc = (_mm3f(jnp.where(own[e], clr, 0.0), bbr_t, _NT)
                  - _mm3f(jnp.where(own[e], cli, 0.0), bbi_t, _NT))
            acc = jnp.zeros((rows, rows), F32)
            for t in range(t_len):
                if d == 0:
                    pieces = [jnp.zeros((h_n * t, rows), F32), kc[:rows - h_n * t]]
                else:
                    s = h_n * (t_len - 1 - t)
                    pieces = [kc[s:], jnp.zeros((s, rows), F32)]
                shifted = jnp.concatenate([p for p in pieces if p.shape[0]], axis=0)
                acc = jnp.where(lane_blk == t, shifted, acc)
            kt_ref[0, d, e] = _bf(acc)
        ltr_ref[0, d, 0] = pwr[t_len:t_len + 1]
        lti_ref[0, d, 0] = pwi[t_len:t_len + 1]


def _s5_operators(lam_re, lam_im, log_step, b_re, b_im, c_re, c_im):
    depth = lam_re.shape[0]
    g_n, p_n, h_n = S5_GROUPS, S5_STATE, S5_GROUP
    pairs, rows = g_n // 2, S5_T * h_n
    dt = jnp.exp(log_step)[..., None]
    vec = lambda v: v.reshape(depth, 2, pairs, 1, 2 * p_n)
    bt = lambda m: m.reshape(depth, 2, pairs, 2, p_n, h_n).transpose(0, 1, 2, 5, 3, 4).reshape(depth, 2, pairs, h_n, 2 * p_n)
    ct = lambda m: m.reshape(depth, 2, pairs, 2, h_n, p_n).transpose(0, 1, 2, 4, 3, 5).reshape(depth, 2, pairs, h_n, 2 * p_n)
    vspec = pl.BlockSpec((1, 2, 1, 1, 2 * p_n), lambda l, q: (l, 0, q, 0, 0))
    mspec = pl.BlockSpec((1, 2, 1, h_n, 2 * p_n), lambda l, q: (l, 0, q, 0, 0))
    sq_spec = pl.BlockSpec((1, 2, 2, rows, rows), lambda l, q: (l, 0, q, 0, 0))
    bp_spec = pl.BlockSpec((1, 2, 1, 2 * p_n, 2 * rows), lambda l, q: (l, 0, q, 0, 0))
    sq = jax.ShapeDtypeStruct((depth, 2, g_n, rows, rows), BF16)
    bp = jax.ShapeDtypeStruct((depth, 2, pairs, 2 * p_n, 2 * rows), BF16)
    lt = jax.ShapeDtypeStruct((depth, 2, pairs, 1, 2 * p_n), F32)
    ktoep, bpr, bpi, cpad, ltr, lti = pl.pallas_call(
        _s5_ops_kernel,
        grid=(depth, pairs),
        in_specs=[vspec] * 4 + [mspec] * 4,
        out_specs=[sq_spec, bp_spec, bp_spec, sq_spec, vspec, vspec],
        out_shape=[sq, bp, bp, sq, lt, lt],
        compiler_params=_cparams("parallel", "parallel"),
        name="s5_operators",
    )(vec(lam_re * dt), vec(lam_im * dt), vec(lam_re), vec(lam_im), bt(b_re), bt(b_im), ct(c_re), ct(c_im))
    flat = lambda v: v.reshape(depth, 2, 1, g_n * p_n)
    return ktoep, bpr, bpi, cpad, flat(ltr), flat(lti)


def _s5_in_kernel(u_ref, uc_ref, o_ref):
    j = pl.program_id(0)
    last = pl.num_programs(0) - 1

    def emit(slab_of_t):
        for t in range(S5_T):
            o_ref[:, t * S5_GROUP:(t + 1) * S5_GROUP, :] = _bf(
                slab_of_t(t).T.reshape(S5_GROUPS, S5_GROUP, S5_CB))

    @pl.when(j < last)
    def _():
        emit(lambda t: u_ref[0, :, t, :])

    @pl.when(j == last)
    def _():
        pad = jnp.zeros((S5_CB - 2 * uc_ref.shape[1], S5_WIDTH), F32)
        emit(lambda t: jnp.concatenate([uc_ref[0, :, t, :], uc_ref[1, :, t, :], pad], axis=0))


def _s5_lat_index(n_half):
    def index(j):
        jj = jnp.minimum(j, 2 * n_half - 1)
        return jj // n_half, jj % n_half
    return index


def _s5_relayout_in(p, pc):
    bsz, seq, n = p.shape
    seq_c, n_c = pc.shape[1], pc.shape[2]
    n_lat, n_ctx = seq // S5_T, seq_c // S5_T
    n_half = n_lat // S5_CB
    idx = _s5_lat_index(n_half)
    return pl.pallas_call(
        _s5_in_kernel,
        grid=(bsz * n_half + 1,),
        in_specs=[
            pl.BlockSpec((1, S5_CB, S5_T, 512), lambda j: (*idx(j), 0, COL_S5U)),
            pl.BlockSpec((bsz, n_ctx, S5_T, 512), lambda j: (0, 0, 0, COL_S5U)),
        ],
        out_specs=pl.BlockSpec((S5_GROUPS, S5_T * S5_GROUP, S5_CB), lambda j: (0, 0, j)),
        out_shape=jax.ShapeDtypeStruct((S5_GROUPS, S5_T * S5_GROUP, S5_COLS), BF16),
        compiler_params=_cparams("arbitrary"),
        name="s5_relayout_in",
    )(p.reshape(bsz, n_lat, S5_T, n), pc.reshape(bsz, n_ctx, S5_T, n_c))


def _s5_state_in_kernel(ug_ref, bre_ref, bim_ref, vre_ref, vim_ref):
    u = ug_ref[...].reshape(2 * S5_T * S5_GROUP, S5_COLS)
    for direction in range(2):
        vre_ref[direction] = _mm(bre_ref[0, direction, 0], u)
        vim_ref[direction] = _mm(bim_ref[0, direction, 0], u)


def _s5_state_in(ug, b_pair_re, b_pair_im, layer):
    pairs = S5_GROUPS // 2
    rows = S5_T * S5_GROUP
    lanes = S5_GROUPS * S5_STATE
    out = jax.ShapeDtypeStruct((2, lanes, S5_COLS), F32)
    return pl.pallas_call(
        _s5_state_in_kernel,
        grid=(pairs,),
        in_specs=[
            pl.BlockSpec((2, rows, S5_COLS), lambda q: (q, 0, 0)),
            pl.BlockSpec((1, 2, 1, 128, 2 * rows), lambda q: (layer, 0, q, 0, 0)),
            pl.BlockSpec((1, 2, 1, 128, 2 * rows), lambda q: (layer, 0, q, 0, 0)),
        ],
        out_specs=[pl.BlockSpec((2, 128, S5_COLS), lambda q: (0, q, 0))] * 2,
        out_shape=[out, out],
        compiler_params=_cparams("parallel"),
        name="s5_state_in",
    )(ug, b_pair_re, b_pair_im)


def _s5_scan_kernel(vre_ref, vim_ref, lre_ref, lim_ref, xre_ref, xim_ref, tre_scr, tim_scr, *, n_lat, n_ctx):
    lanes = lre_ref.shape[-1]
    for d in range(2):
        tre_scr[d] = vre_ref[d].T
        tim_scr[d] = vim_ref[d].T
    a_re = [lre_ref[0, d] for d in range(2)]
    a_im = [lim_ref[0, d] for d in range(2)]

    def step(d, row, xr, xi):
        sl = pl.ds(row, 1)
        xre_ref[d, sl, :] = xr
        xim_ref[d, sl, :] = xi
        return (a_re[d] * xr - a_im[d] * xi + tre_scr[d, sl, :],
                a_re[d] * xi + a_im[d] * xr + tim_scr[d, sl, :])

    def run(bases, n, carry):
        def body(i, carry):
            out = []
            for d in range(2):
                j = i if d == 0 else n - 1 - i
                for b in range(2):
                    xr, xi = carry[2 * d + b]
                    out.append(step(d, bases[b] + j, xr, xi))
            return tuple(out)
        return lax.fori_loop(0, n, body, carry)

    zero = jnp.zeros((1, lanes), F32)
    carry = run((2 * n_lat, 2 * n_lat + n_ctx), n_ctx, ((zero, zero),) * 4)
    run((0, n_lat), n_lat, carry)
    pad0 = 2 * n_lat + 2 * n_ctx
    for d in range(2):
        xre_ref[d, pad0:, :] = jnp.zeros((S5_COLS - pad0, lanes), F32)
        xim_ref[d, pad0:, :] = jnp.zeros((S5_COLS - pad0, lanes), F32)


def _s5_scan(vre, vim, lam_re, lam_im, layer, *, n_lat, n_ctx):
    lanes = vre.shape[1]
    tl = 512
    vspec = pl.BlockSpec((2, tl, S5_COLS), lambda j: (0, j, 0))
    xspec = pl.BlockSpec((2, S5_COLS, tl), lambda j: (0, 0, j))
    lspec = pl.BlockSpec((1, 2, 1, tl), lambda j: (layer, 0, 0, j))
    out = jax.ShapeDtypeStruct((2, S5_COLS, lanes), F32)
    return pl.pallas_call(
        functools.partial(_s5_scan_kernel, n_lat=n_lat, n_ctx=n_ctx),
        grid=(lanes // tl,),
        in_specs=[vspec, vspec, lspec, lspec],
        out_specs=[xspec, xspec],
        out_shape=[out, out],
        scratch_shapes=[pltpu.VMEM((2, S5_COLS, tl), F32), pltpu.VMEM((2, S5_COLS, tl), F32)],
        compiler_params=_cparams("parallel"),
        name="s5_scan",
    )(vre, vim, lam_re, lam_im)


def _s5_readout_kernel(ug_ref, kt_ref, cp_ref, xre_ref, xim_ref, y_ref):
    for e in range(2):
        u = ug_ref[e]
        acc = None
        for direction in range(2):
            xcat = _bf(jnp.concatenate([xre_ref[direction], xim_ref[direction]], axis=1))
            term = _mm(kt_ref[0, direction, e], u) + _mm(cp_ref[0, direction, e], xcat, _NT)
            acc = term if acc is None else acc + term
        y_ref[e] = acc


def _s5_readout(ug, ktoep, cpad, xre, xim, layer):
    pairs = S5_GROUPS // 2
    rows = S5_T * S5_GROUP
    return pl.pallas_call(
        _s5_readout_kernel,
        grid=(pairs,),
        in_specs=[
            pl.BlockSpec((2, rows, S5_COLS), lambda q: (q, 0, 0)),
            pl.BlockSpec((1, 2, 2, rows, rows), lambda q: (layer, 0, q, 0, 0)),
            pl.BlockSpec((1, 2, 2, rows, rows), lambda q: (layer, 0, q, 0, 0)),
            pl.BlockSpec((2, S5_COLS, 128), lambda q: (0, 0, q)),
            pl.BlockSpec((2, S5_COLS, 128), lambda q: (0, 0, q)),
        ],
        out_specs=pl.BlockSpec((2, rows, S5_COLS), lambda q: (q, 0, 0)),
        out_shape=jax.ShapeDtypeStruct((S5_GROUPS, rows, S5_COLS), F32),
        compiler_params=_cparams("parallel"),
        name="s5_readout",
    )(ug, ktoep, cpad, xre, xim)


def _s5_out_kernel(*refs, ctx_out):
    if ctx_out:
        yg_ref, u_ref, g_ref, uc_ref, gc_ref, d_ref, w_ref, b_ref, o_ref, oc_ref = refs
    else:
        yg_ref, u_ref, g_ref, d_ref, w_ref, b_ref, o_ref = refs
    d = d_ref[...]
    w = w_ref[...]
    bias = b_ref[...]

    def finish(yy, u, gate):
        yy = yy + d * u
        yg = jax.nn.gelu(yy)
        out = yg * jax.nn.sigmoid(_mm(_bf(yg), w) + bias)
        return out * (gate * jax.nn.sigmoid(gate))

    def y_of(t):
        return yg_ref[:, t * S5_GROUP:(t + 1) * S5_GROUP, :].reshape(S5_WIDTH, S5_CB).T

    def lat():
        for t in range(S5_T):
            o_ref[0, :, t, :] = finish(y_of(t), u_ref[0, :, t, :], g_ref[0, :, t, :])

    if not ctx_out:
        lat()
        return
    j = pl.program_id(0)
    last = pl.num_programs(0) - 1
    pl.when(j < last)(lat)

    @pl.when(j == last)
    def _():
        n_ctx = uc_ref.shape[1]
        for t in range(S5_T):
            y = y_of(t)
            for b in range(2):
                oc_ref[b, :, t, :] = finish(y[b * n_ctx:(b + 1) * n_ctx], uc_ref[b, :, t, :], gc_ref[b, :, t, :])


def _s5_relayout_out(yg, p, pc, d_skip, w_glu, b_glu, *, ctx_out):
    bsz, seq, n = p.shape
    seq_c, n_c = pc.shape[1], pc.shape[2]
    n_lat, n_ctx = seq // S5_T, seq_c // S5_T
    n_half = n_lat // S5_CB
    idx = _s5_lat_index(n_half)
    pv = p.reshape(bsz, n_lat, S5_T, n)
    lat_block = (1, S5_CB, S5_T, 512)
    in_specs = [
        pl.BlockSpec((S5_GROUPS, S5_T * S5_GROUP, S5_CB), lambda j: (0, 0, j)),
        pl.BlockSpec(lat_block, lambda j: (*idx(j), 0, COL_S5U)),
        pl.BlockSpec(lat_block, lambda j: (*idx(j), 0, COL_S5G)),
    ]
    args = [yg, pv, pv]
    out_specs = [pl.BlockSpec(lat_block, lambda j: (*idx(j), 0, 0))]
    out_shape = [jax.ShapeDtypeStruct((bsz, n_lat, S5_T, S5_WIDTH), F32)]
    if ctx_out:
        pcv = pc.reshape(bsz, n_ctx, S5_T, n_c)
        ctx_block = (bsz, n_ctx, S5_T, 512)
        in_specs += [
            pl.BlockSpec(ctx_block, lambda j: (0, 0, 0, COL_S5U)),
            pl.BlockSpec(ctx_block, lambda j: (0, 0, 0, COL_S5G)),
        ]
        args += [pcv, pcv]
        out_specs.append(pl.BlockSpec(ctx_block, lambda j: (0, 0, 0, 0)))
        out_shape.append(jax.ShapeDtypeStruct((bsz, n_ctx, S5_T, S5_WIDTH), F32))
    in_specs += [
        pl.BlockSpec((1, S5_WIDTH), lambda j: (0, 0)),
        pl.BlockSpec((S5_WIDTH, S5_WIDTH), lambda j: (0, 0)),
        pl.BlockSpec((1, S5_WIDTH), lambda j: (0, 0)),
    ]
    args += [d_skip, w_glu, b_glu]
    outs = pl.pallas_call(
        functools.partial(_s5_out_kernel, ctx_out=ctx_out),
        grid=(bsz * n_half + (1 if ctx_out else 0),),
        in_specs=in_specs,
        out_specs=out_specs,
        out_shape=out_shape,
        compiler_params=_cparams("arbitrary"),
        name="s5_relayout_out",
    )(*args)
    y = outs[0].reshape(bsz, seq, S5_WIDTH)
    yc = outs[1].reshape(bsz, seq_c, S5_WIDTH) if ctx_out else None
    return y, yc


def _s5(p, pc, ops, layer, d_skip, w_glu, b_glu, *, ctx_out):
    ktoep, b_pair_re, b_pair_im, cpad, lam_re, lam_im = ops
    ug = _s5_relayout_in(p, pc)
    vre, vim = _s5_state_in(ug, b_pair_re, b_pair_im, layer)
    xre, xim = _s5_scan(vre, vim, lam_re, lam_im, layer, n_lat=p.shape[1] // S5_T, n_ctx=pc.shape[1] // S5_T)
    yg = _s5_readout(ug, ktoep, cpad, xre, xim, layer)
    return _s5_relayout_out(yg, p, pc, d_skip, w_glu, b_glu, ctx_out=ctx_out)


def _dft_tables():
    n1, n2 = FFT_N1, FFT_N2
    n = n1 * n2
    a = jnp.arange(n1, dtype=jnp.int32)
    ph1 = (a[:, None] * a[None, :]) % n1
    ang1 = ph1.astype(F32) * (2.0 * math.pi / n1)
    c1, s1 = jnp.cos(ang1), -jnp.sin(ang1)
    h = n1 // 2
    m1 = jnp.concatenate([jnp.concatenate([c1[:, :h], -s1[:, :h]], 1),
                          jnp.concatenate([s1[:, :h], c1[:, :h]], 1)], 0)
    m1_real = jnp.concatenate([c1, s1], 0)
    m3 = m1.T / n
    k1 = jnp.arange(n1, dtype=jnp.int32)[:, None, None]
    k2 = jnp.arange(n2, dtype=jnp.int32)[None, :, None]
    m2 = jnp.arange(n2, dtype=jnp.int32)[None, None, :]
    ph = (n1 * m2 * k2 + m2 * k1) % n
    ang = ph.astype(F32) * (2.0 * math.pi / n)
    gr, gi = jnp.cos(ang), -jnp.sin(ang)
    g = jnp.concatenate([jnp.concatenate([gr, -gi], 2), jnp.concatenate([gi, gr], 2)], 1)
    gt = jnp.swapaxes(g, 1, 2)
    return tuple(_split(t) for t in (m1, m1_real, m3, g, gt))


def _filter_features(length):
    t = jnp.linspace(0.0, 1.0, length, dtype=F32)[:, None]
    ang = (2.0 * math.pi / length) * jnp.arange(length, dtype=F32)[:, None]
    bands = jnp.linspace(1e-4, HY_BANDS - 1, HY_BANDS, dtype=F32)[None, :]
    feats = jnp.concatenate([t, jnp.cos(bands * ang), -jnp.sin(bands * ang)], axis=-1)
    feats = jnp.pad(feats, ((0, 0), (0, 128 - HY_EMB)))
    rev = jnp.roll(feats[::-1], 1, axis=0)
    return jnp.stack([feats, rev])


def _filter_kernel(f_ref, w1_ref, b1_ref, f1_ref, w2_ref, b2_ref, f2_ref, w3_ref, dl_ref, o_ref):
    half = pl.program_id(1)
    tile = pl.program_id(2)
    x = f_ref[0]
    x_hi, x_lo = _split(x)
    h = jnp.sin(f1_ref[0] * (_mm3(x_hi, x_lo, w1_ref[0]) + b1_ref[0]))
    h_hi, h_lo = _split(h)
    h = jnp.sin(f2_ref[0] * (_mm3(h_hi, h_lo, w2_ref[0]) + b2_ref[0]))
    y = _mm(_bf(h), _bf(w3_ref[0, 0])) * jnp.exp(-x[:, 0:1] * dl_ref[...])
    rows = lax.broadcasted_iota(jnp.int32, y.shape, 0)
    drop = jnp.logical_and(jnp.logical_and(half == 1, tile == 0), rows == 0)
    o_ref[0] = jnp.where(drop, 0.0, y)


def _filter_taps(feats, w1, b1, f1, w2, b2, f2, w3, deltas, tmf):
    depth = w1.shape[0]
    length = feats.shape[1]
    nt = length // tmf
    wide = w3.shape[-1]
    vec = pl.BlockSpec((1, 1, 128), lambda l, s, i: (l, 0, 0))
    sq = pl.BlockSpec((1, 128, 128), lambda l, s, i: (l, 0, 0))
    return pl.pallas_call(
        _filter_kernel,
        grid=(depth, 2, nt),
        in_specs=[
            pl.BlockSpec((1, tmf, 128), lambda l, s, i: (s, i, 0)),
            sq, vec, vec, sq, vec, vec,
            pl.BlockSpec((1, 1, 128, wide), lambda l, s, i: (l, s, 0, 0)),
            pl.BlockSpec((1, wide), lambda l, s, i: (0, 0)),
        ],
        out_specs=pl.BlockSpec((1, tmf, wide), lambda l, s, i: (l, s * nt + i, 0)),
        out_shape=jax.ShapeDtypeStruct((depth, 2 * length, wide), F32),
        compiler_params=_cparams("parallel", "parallel", "parallel"),
        name="hyena_filter",
    )(feats, w1, b1, f1, w2, b2, f2, w3, deltas)


def _parts(pair):
    return tuple(pair) if HY_PASSES == 3 else tuple(pair[:1])


def _mmp(m, b):
    return _mm(m[0], _bf(b)) if len(m) == 1 else _mm3(m[0], m[1], b)


def _blk_rows(n2):
    return pl.ds(pl.multiple_of(n2 * HY_PITCH, 8), 2 * FFT_N1)


def _pad_rows(n2, n):
    return pl.ds(n2, n, stride=HY_PITCH)


def _pad_copy(src, dst, n):
    for n1 in range(n):
        dst[n1 * HY_PITCH:n1 * HY_PITCH + FFT_N2, :] = src[n1 * FFT_N2:(n1 + 1) * FFT_N2, :]


def _dft_stage1(read_rows, m, a_scr):
    ct = a_scr.shape[-1]

    def body(it, carry):
        n2 = it * HY_NB
        a = _mmp(m, jnp.concatenate([read_rows(n2 + j) for j in range(HY_NB)], axis=1))
        for j in range(HY_NB):
            a_scr[_blk_rows(n2 + j), :] = a[:, j * ct:(j + 1) * ct]
        return carry

    lax.fori_loop(0, FFT_N2 // HY_NB, body, 0)


def _stage2_rows(k1):
    return (pl.ds(k1, FFT_N2, stride=HY_PITCH), pl.ds(k1 + FFT_N1, FFT_N2, stride=HY_PITCH))


def _stage2_load(a_scr, kb):
    rows = [_stage2_rows(kb * HY_KB + i) for i in range(HY_KB)]
    return rows, [jnp.concatenate([a_scr[re, :], a_scr[im, :]], axis=0) for re, im in rows]


def _spectrum_kernel(*refs):
    np_ = len(_parts((0, 0)))
    m_refs, g_refs, (t_ref, o_ref, a_scr, pad_scr) = refs[:np_], refs[np_:2 * np_], refs[2 * np_:]
    kb = pl.program_id(2)
    ct = t_ref.shape[-1]

    @pl.when(kb == 0)
    def _():
        _pad_copy(t_ref.at[0], pad_scr, FFT_N1)
        _dft_stage1(lambda n2: pad_scr[_pad_rows(n2, FFT_N1), :], [r[...] for r in m_refs], a_scr)

    _, a = _stage2_load(a_scr, kb)
    for i in range(HY_KB):
        o_ref[0, i] = _mmp([r[i] for r in g_refs], a[i]).reshape(2, FFT_N2, ct)


def _filter_spectrum(tables, taps):
    m, g = _parts(tables[1]), _parts(tables[3])
    depth, n, ch = taps.shape
    ct = HY_CT
    gspec = pl.BlockSpec((HY_KB, 2 * FFT_N2, 2 * FFT_N2), lambda l, c, k: (k, 0, 0))
    mspec = pl.BlockSpec(m[0].shape, lambda l, c, k: (0, 0))
    return pl.pallas_call(
        _spectrum_kernel,
        grid=(depth, ch // ct, FFT_N1 // HY_KB),
        in_specs=[mspec] * len(m) + [gspec] * len(g) + [pl.BlockSpec((1, n, ct), lambda l, c, k: (l, 0, c))],
        out_specs=pl.BlockSpec((1, HY_KB, 2, FFT_N2, ct), lambda l, c, k: (l, k, 0, 0, c)),
        out_shape=jax.ShapeDtypeStruct((depth, FFT_N1, 2, FFT_N2, ch), F32),
        scratch_shapes=[pltpu.VMEM((HY_PITCH * FFT_N2, ct), F32), pltpu.VMEM((HY_PITCH * FFT_N1, ct), F32)],
        compiler_params=_cparams("parallel", "parallel", "arbitrary"),
        name="hyena_spectrum",
    )(*m, *g, taps)


def _hy_pre_kernel(pv_ref, p1_ref, p2_ref, pg_ref, wv_ref, w1_ref, w2_ref, bv_ref, b1_ref, b2_ref,
                   v_ref, x1_ref, x2_ref):
    n = pv_ref.shape[1]
    rows = lax.broadcasted_iota(jnp.int32, (n, 128), 0)

    def conv(p, w_ref, b_ref):
        up = jnp.where(rows == 0, 0.0, pltpu.roll(p, 1, 0))
        dn = jnp.where(rows == n - 1, 0.0, pltpu.roll(p, n - 1, 0))
        return up * w_ref[0:1, :] + p * w_ref[1:2, :] + dn * w_ref[2:3, :] + b_ref[...]

    v_ref[0] = conv(pv_ref[0], wv_ref, bv_ref)
    x1_ref[0] = conv(p1_ref[0], w1_ref, b1_ref)
    gate = pg_ref[0]
    x2_ref[0] = conv(p2_ref[0], w2_ref, b2_ref) * (gate * jax.nn.sigmoid(gate))


def _hy_pre(p, conv_w, conv_b):
    bsz, seq, _ = p.shape
    def pspec(off):
        return pl.BlockSpec((1, seq, 128), lambda b, j: (b, 0, off + j))
    def wspec(off):
        return pl.BlockSpec((3, 128), lambda b, j: (0, off + j))
    def bspec(off):
        return pl.BlockSpec((1, 128), lambda b, j: (0, off + j))
    out = jax.ShapeDtypeStruct((bsz, seq, HY_WIDTH), F32)
    ospec = pl.BlockSpec((1, seq, 128), lambda b, j: (b, 0, j))
    return pl.pallas_call(
        _hy_pre_kernel,
        grid=(bsz, HY_WIDTH // 128),
        in_specs=[pspec(COL_HY), pspec(COL_HY + 4), pspec(COL_HY + 8), pspec(COL_HY + 12),
                  wspec(0), wspec(4), wspec(8), bspec(0), bspec(4), bspec(8)],
        out_specs=[ospec, ospec, ospec],
        out_shape=[out, out, out],
        compiler_params=_cparams("parallel", "parallel"),
        name="hyena_pre",
    )(p, p, p, p, conv_w, conv_w, conv_w, conv_b, conv_b, conv_b)


def _hy_conv_kernel(*refs):
    np_ = len(_parts((0, 0)))
    m1_refs, m3_refs, g_refs, t_refs = (refs[i * np_:(i + 1) * np_] for i in range(4))
    u_ref, gate_ref, h_ref, d_ref, o_ref, a_scr, pad_scr = refs[4 * np_:]
    kb = pl.program_id(1)
    ct = u_ref.shape[-1]
    half = FFT_N1 // 2

    @pl.when(kb == 0)
    def _():
        for b in range(2):
            _pad_copy(u_ref.at[b], pad_scr.at[b], half)
        _dft_stage1(lambda n2: jnp.concatenate([pad_scr[0, _pad_rows(n2, half), :],
                                                pad_scr[1, _pad_rows(n2, half), :]], axis=0),
                    [r[...] for r in m1_refs], a_scr)

    srows, a = _stage2_load(a_scr, kb)
    z = [_mmp([r[i] for r in g_refs], a[i]) for i in range(HY_KB)]
    z2 = []
    for i in range(HY_KB):
        zr, zi = z[i][:FFT_N2], z[i][FFT_N2:]
        hr, hi = h_ref[0, i, 0], h_ref[0, i, 1]
        z2.append(jnp.concatenate([zr * hr - zi * hi, zr * hi + zi * hr], axis=0))
    a2 = [_mmp([r[i] for r in t_refs], z2[i]) for i in range(HY_KB)]
    for i in range(HY_KB):
        a_scr[srows[i][0], :] = a2[i][:FFT_N2]
        a_scr[srows[i][1], :] = a2[i][FFT_N2:]

    @pl.when(kb == pl.num_programs(1) - 1)
    def _():
        m3 = [r[...] for r in m3_refs]
        d = d_ref[...]

        def body(it, carry):
            n2 = it * HY_NB
            y = _mmp(m3, jnp.concatenate([a_scr[_blk_rows(n2 + j), :] for j in range(HY_NB)], axis=1))
            for j in range(HY_NB):
                for b in range(2):
                    pad_scr[b, _pad_rows(n2 + j, half), :] = y[b * half:(b + 1) * half, j * ct:(j + 1) * ct]
            return carry

        lax.fori_loop(0, FFT_N2 // HY_NB, body, 0)
        for b in range(2):
            for n1 in range(half):
                r = slice(n1 * FFT_N2, (n1 + 1) * FFT_N2)
                conv = pad_scr[b, n1 * HY_PITCH:n1 * HY_PITCH + FFT_N2, :]
                o_ref[b, r, :] = gate_ref[b, r, :] * (conv + d * u_ref[b, r, :])


def _hy_long_conv(tables, spec, layer, order, u, gate, d):
    m1, _, m3, g, gt = (_parts(t) for t in tables)
    bsz, seq, ch = u.shape
    ct = HY_CT
    gspec = pl.BlockSpec((HY_KB, 2 * FFT_N2, 2 * FFT_N2), lambda c, k: (k, 0, 0))
    full = lambda t: pl.BlockSpec(t.shape, lambda c, k: (0, 0))
    uspec = pl.BlockSpec((bsz, seq, ct), lambda c, k: (0, 0, c))
    n_ct = ch // ct
    mats = (*m1, *m3, *g, *gt)
    return pl.pallas_call(
        _hy_conv_kernel,
        grid=(n_ct, FFT_N1 // HY_KB),
        in_specs=[full(t) for t in (*m1, *m3)] + [gspec] * (len(g) + len(gt)) + [
            uspec, uspec,
            pl.BlockSpec((1, HY_KB, 2, FFT_N2, ct), lambda c, k: (layer, k, 0, 0, order * n_ct + c)),
            pl.BlockSpec((1, ct), lambda c, k: (0, c))],
        out_specs=uspec,
        out_shape=jax.ShapeDtypeStruct(u.shape, F32),
        scratch_shapes=[pltpu.VMEM((HY_PITCH * FFT_N2, ct), F32),
                        pltpu.VMEM((bsz, HY_PITCH * FFT_N1 // 2, ct), F32)],
        compiler_params=_cparams("parallel", "arbitrary"),
        name="hyena_conv",
    )(*mats, u, gate, spec, d.reshape(1, ch))


def _ctx_dft_tables(length):
    n = 2 * length
    k = jnp.arange(n, dtype=jnp.int32)
    ang = ((k[:, None] * k[None, :]) % n).astype(F32) * (2.0 * math.pi / n)
    fr, fi = jnp.cos(ang), -jnp.sin(ang)
    fwd = jnp.concatenate([jnp.concatenate([fr[:, :length], -fi[:, :length]], 1),
                           jnp.concatenate([fi[:, :length], fr[:, :length]], 1)], 0)
    real = jnp.concatenate([fr, fi], 0)
    inv = fwd.T / n
    return tuple(_split(t) for t in (fwd, real, inv))


def _hyc_kernel(fh_ref, fl_ref, rh_ref, rl_ref, ih_ref, il_ref, taps_ref, u_ref, g_ref, d_ref, o_ref):
    n = u_ref.shape[1]
    x = jnp.concatenate([u_ref[0], u_ref[1]], axis=0)
    z = _mm3(fh_ref[...], fl_ref[...], x)
    h = _mm3(rh_ref[...], rl_ref[...], taps_ref[0])
    m = 2 * n
    zr, zi, hr, hi = z[:m], z[m:], h[:m], h[m:]
    z2 = jnp.concatenate([zr * hr - zi * hi, zr * hi + zi * hr], axis=0)
    y = _mm3(ih_ref[...], il_ref[...], z2)
    for b in range(2):
        o_ref[b] = g_ref[b] * (y[b * n:(b + 1) * n] + d_ref[...] * u_ref[b])


def _hyc_long_conv(ctabs, taps, layer, order, u, gate, d):
    (f_hi, f_lo), (r_hi, r_lo), (i_hi, i_lo) = ctabs
    bsz, n, ch = u.shape
    full = lambda t: pl.BlockSpec(t.shape, lambda i: (0,) * t.ndim)
    uspec = pl.BlockSpec((bsz, n, ch), lambda i: (0, 0, 0))
    return pl.pallas_call(
        _hyc_kernel,
        grid=(1,),
        in_specs=[full(f_hi), full(f_lo), full(r_hi), full(r_lo), full(i_hi), full(i_lo),
                  pl.BlockSpec((1, 2 * n, ch), lambda i: (layer, 0, order)),
                  uspec, uspec, pl.BlockSpec((1, ch), lambda i: (0, 0))],
        out_specs=uspec,
        out_shape=jax.ShapeDtypeStruct((bsz, n, ch), F32),
        compiler_params=_cparams("arbitrary"),
        name="hyena_ctx",
    )(f_hi, f_lo, r_hi, r_lo, i_hi, i_lo, taps, u, gate, d.reshape(1, ch))


def _outproj_kernel(x_ref, ya_ref, yb_ref, yc_ref, w_ref, gp_ref, gate_ref, o_ref, *, colmajor, tm):
    na, nb = ya_ref.shape[-1], yb_ref.shape[-1]
    acc = (_mm(_bf(ya_ref[0]), w_ref[0, 0:na, :]) + _mm(_bf(yb_ref[0]), w_ref[0, na:na + nb, :])
           + _mm(_bf(yc_ref[0]), w_ref[0, na + nb:, :]))
    ms = jnp.mean(acc * acc, axis=-1, keepdims=True)
    upd = gate_ref[0] * (acc * lax.rsqrt(ms + EPS) * gp_ref[...])
    if colmajor:
        for j in range(tm // GRID_W):
            o_ref[0, :, j, :] = x_ref[0, :, j, :] + upd[j * GRID_W:(j + 1) * GRID_W]
    else:
        o_ref[0] = x_ref[0] + upd


def _outproj(x, ya, yb, yc, w_out, layer, g_post, gate, *, colmajor, tm):
    bsz, seq, d = x.shape
    if colmajor:
        x_in = x.reshape(bsz, GRID_W, seq // GRID_W, d)
        x_spec = pl.BlockSpec((1, GRID_W, tm // GRID_W, d), lambda b, m: (b, 0, m, 0))
    else:
        x_in = x
        x_spec = pl.BlockSpec((1, tm, d), lambda b, m: (b, m, 0))
    def yspec(t):
        return pl.BlockSpec((1, tm, t.shape[-1]), lambda b, m: (b, m, 0))
    out = pl.pallas_call(
        functools.partial(_outproj_kernel, colmajor=colmajor, tm=tm),
        grid=(bsz, seq // tm),
        in_specs=[x_spec, yspec(ya), yspec(yb), yspec(yc),
                  pl.BlockSpec((1,) + w_out.shape[1:], lambda b, m: (layer, 0, 0)),
                  pl.BlockSpec((1, d), lambda b, m: (0, 0)),
                  pl.BlockSpec((1, 1, d), lambda b, m: (b, 0, 0))],
        out_specs=x_spec,
        out_shape=jax.ShapeDtypeStruct(x_in.shape, F32),
        compiler_params=_cparams("parallel", "parallel"),
        name="outproj",
    )(x_in, ya, yb, yc, w_out, g_post, gate)
    return out.reshape(bsz, seq, d)


def kernel(x, c, ctx, c_ctx, w_mod, b_mod, g_pre, g_post, w_in, w_out, gla_w_gate, gla_b_gate, gla_norm, s5_lam_re, s5_lam_im, s5_log_step, s5_b_re, s5_b_im, s5_c_re, s5_c_im, s5_d, s5_w_glu, s5_b_glu, hy_conv_w, hy_conv_b, hy_w1, hy_b1, hy_f1, hy_w2, hy_b2, hy_f2, hy_w3, hy_d):
    bsz, seq, d = x.shape
    seq_c = ctx.shape[1]
    depth = w_in.shape[0]
    assert (bsz, d, seq // GRID_W) == (2, D_MODEL, GRID_W) and seq == FFT_N1 * FFT_N2 // 2

    cvec = jnp.concatenate([c, c_ctx[None], jnp.zeros((8 - bsz - 1, d), F32)], axis=0)
    mod = _modulation(cvec, w_mod, b_mod)
    shift, scale, gate = mod[..., :d], mod[..., d:2 * d], mod[..., 2 * d:]

    w_t = _bf(jnp.swapaxes(w_in, 1, 2))
    w_out_b = _bf(w_out)
    w_glu_b = _bf(s5_w_glu)

    wg = gla_w_gate.reshape(depth, 2, 16, GLA_HEADS, GLA_DK).transpose(0, 1, 3, 2, 4)
    wg_pad = jnp.stack([jnp.pad(wg[:, 0], ((0, 0), (0, 0), (0, 112), (0, 0))),
                        jnp.pad(wg[:, 1], ((0, 0), (0, 0), (16, 96), (0, 0)))], axis=1)
    wg_hi, wg_lo = _split(wg_pad)
    bg = gla_b_gate.reshape(depth, 2, GLA_HEADS, 1, GLA_DK)

    tables = _dft_tables()
    deltas = jnp.abs(jnp.linspace(HY_MIN_DECAY, HY_MAX_DECAY, HY_WIDTH, dtype=F32))
    deltas2 = jnp.tile(deltas, 2).reshape(1, 2 * HY_WIDTH)
    pad_k = lambda w: jnp.pad(w, ((0, 0), (0, 128 - w.shape[1]), (0, 128 - w.shape[2])))
    pad_v = lambda v: jnp.pad(v, ((0, 0), (0, 128 - v.shape[1]))).reshape(depth, 1, 128)
    w3 = hy_w3.reshape(depth, HY_FFN, 2, 2, HY_WIDTH).transpose(0, 3, 1, 2, 4)
    w3 = jnp.pad(w3.reshape(depth, 2, HY_FFN, 2 * HY_WIDTH), ((0, 0), (0, 0), (0, 128 - HY_FFN), (0, 0)))
    filt_w = (pad_k(hy_w1), pad_v(hy_b1), pad_v(hy_f1), pad_k(hy_w2), pad_v(hy_b2), pad_v(hy_f2), w3, deltas2)
    taps = _filter_taps(_filter_features(seq), *filt_w, tmf=512)
    spec = _filter_spectrum(tables, taps)
    taps_c = _filter_taps(_filter_features(seq_c), *filt_w, tmf=seq_c)
    ctabs = _ctx_dft_tables(seq_c)

    s5_ops = _s5_operators(s5_lam_re, s5_lam_im, s5_log_step, s5_b_re, s5_b_im, s5_c_re, s5_c_im)

    xc = ctx
    for l in range(depth):
        last = l == depth - 1
        ctx_out = not last
        colmajor = l % 2 == 1
        pre = g_pre[l][None, :] * (1.0 + scale[l])
        sc_l, sh_l = pre[:bsz, None, :], shift[l, :bsz, None, :]
        sc_c = jnp.broadcast_to(pre[bsz][None, None, :], (bsz, 1, d))
        sh_c = jnp.broadcast_to(shift[l, bsz][None, None, :], (bsz, 1, d))
        qkv, p, lr = _inproj(x, sc_l, sh_l, w_t, l, n=N_MAIN, colmajor=colmajor, tm=512 if colmajor else 1024,
                             tn=512)
        qkvc, pc, lrc = _inproj(xc, sc_c, sh_c, w_t, l, n=N_MAIN if ctx_out else N_STATE,
                                colmajor=False, tm=seq_c, tn=512)

        y_gla, yc_gla = _gla(qkv, p, lr, qkvc, pc, lrc, wg_hi[l], wg_lo[l], bg[l], gla_norm[l][None, :],
                             ctx_out=ctx_out)

        y_s5, yc_s5 = _s5(p, pc, s5_ops, l, s5_d[l][None, :], w_glu_b[l], s5_b_glu[l][None, :], ctx_out=ctx_out)

        v, x1, x2 = _hy_pre(p, hy_conv_w[l], hy_conv_b[l][None, :])
        z = _hy_long_conv(tables, spec, l, 0, v, x1, hy_d[l, 0])
        y_hy = _hy_long_conv(tables, spec, l, 1, z, x2, hy_d[l, 1])

        x_new = _outproj(x, y_gla, y_s5, y_hy, w_out_b, l, g_post[l][None, :], gate[l, :bsz, None, :],
                         colmajor=colmajor, tm=512)
        if ctx_out:
            vc, x1c, x2c = _hy_pre(pc, hy_conv_w[l], hy_conv_b[l][None, :])
            zc = _hyc_long_conv(ctabs, taps_c, l, 0, vc, x1c, hy_d[l, 0])
            yc_hy = _hyc_long_conv(ctabs, taps_c, l, 1, zc, x2c, hy_d[l, 1])
            gate_c = jnp.broadcast_to(gate[l, bsz][None, None, :], (bsz, 1, d))
            xc = _outproj(xc, yc_gla, yc_s5, yc_hy, w_out_b, l, g_post[l][None, :], gate_c,
                          colmajor=False, tm=seq_c)
        x = x_new
    return x
```

```python
import functools
import math

import numpy as np
import jax
import jax.numpy as jnp
from jax import lax
from jax.experimental import pallas as pl
from jax.experimental.pallas import tpu as pltpu

F32 = jnp.float32
BF16 = jnp.bfloat16

D_MODEL = 2048
GRID_W = 64
EPS = 1e-6

GLA_HEADS = 4
GLA_DK = 128
GLA_DV = 256
GLA_TAU = 16.0
GLA_CHUNK = 64
GLA_BLOCK = 8

S5_WIDTH = 512
S5_GROUP = 16
S5_GROUPS = 32
S5_STATE = 64
S5_T = 16
S5_COLS = 640
S5_CB = 128

HY_WIDTH = 512
HY_EMB = 33
HY_BANDS = 16
HY_FFN = 64
HY_MIN_DECAY = math.log(1e-2) / 0.3
HY_MAX_DECAY = math.log(1e-2) / 1.5
FFT_N1 = 64
FFT_N2 = 128
HY_CT = 128
HY_KB = 8
HY_NB = 4
HY_PITCH = 136
HY_PASSES = 1

N_MAIN = 6144
N_STATE = 2560
N_QKV = 2048
N_LR = 32
COL_S5U, COL_S5G = 0, 3
COL_GLA_GATE = 2
COL_HY = 16

VMEM_LIMIT_BYTES = 56 * 1024 * 1024


def _cparams(*sem):
    return pltpu.CompilerParams(dimension_semantics=sem, vmem_limit_bytes=VMEM_LIMIT_BYTES)


def _bf(x):
    return x.astype(BF16)


def _split(x):
    hi = _bf(x)
    return hi, _bf(x - hi.astype(F32))


def _split3(x):
    a = _bf(x)
    r = x - a.astype(F32)
    b = _bf(r)
    return a, b, _bf(r - b.astype(F32))


_NN = (((1,), (0,)), ((), ()))
_NT = (((1,), (1,)), ((), ()))
_TN = (((0,), (0,)), ((), ()))


def _mm(a, b, dims=_NN):
    return lax.dot_general(a, b, dims, preferred_element_type=F32)


def _mm3(a_hi, a_lo, b, dims=_NN):
    b_hi, b_lo = _split(b)
    return _mm(a_hi, b_hi, dims) + _mm(a_hi, b_lo, dims) + _mm(a_lo, b_hi, dims)


def _mod_kernel(s_ref, w_ref, b_ref, o_ref):
    s = s_ref[...]
    s = s * jax.nn.sigmoid(s)
    s_hi, s_lo = _split(s)
    o_ref[0] = _mm3(s_hi, s_lo, w_ref[0]) + b_ref[0]


def _modulation(cvec, w_mod, b_mod):
    depth, d, n = w_mod.shape
    tn = 512
    return pl.pallas_call(
        _mod_kernel,
        grid=(depth, n // tn),
        in_specs=[
            pl.BlockSpec((8, d), lambda l, j: (0, 0)),
            pl.BlockSpec((1, d, tn), lambda l, j: (l, 0, j)),
            pl.BlockSpec((1, 1, tn), lambda l, j: (l, 0, j)),
        ],
        out_specs=pl.BlockSpec((1, 8, tn), lambda l, j: (l, 0, j)),
        out_shape=jax.ShapeDtypeStruct((depth, 8, n), F32),
        compiler_params=_cparams("parallel", "parallel"),
        name="modulation",
    )(cvec, w_mod, b_mod.reshape(depth, 1, n))


def _inproj_kernel(*refs, colmajor, tm, tn):
    if colmajor:
        x_ref, sc_ref, sh_ref, w_ref, wlr_ref, perm_ref, qkv_ref, p_ref, lr_ref, h_scr = refs
    else:
        x_ref, sc_ref, sh_ref, w_ref, wlr_ref, qkv_ref, p_ref, lr_ref, h_scr = refs
    j = pl.program_id(2)

    @pl.when(j == 0)
    def _():
        def norm(xb):
            ms = jnp.mean(xb * xb, axis=-1, keepdims=True)
            return _bf(xb * lax.rsqrt(ms + EPS) * sc_ref[0] + sh_ref[0])

        if colmajor:
            h = norm(x_ref[0].reshape(tm, D_MODEL))
            for c in range(0, D_MODEL, 512):
                h_scr[:, c:c + 512] = _bf(_mm(perm_ref[...], h[:, c:c + 512]))
        else:
            h_scr[...] = norm(x_ref[0])
        lr_ref[0] = _mm(h_scr[...], wlr_ref[...], _NT)

    res = _mm(h_scr[...], w_ref[...], _NT)
    n_qkv = N_QKV // tn

    @pl.when(j < n_qkv)
    def _():
        qkv_ref[0] = _bf(res)

    @pl.when(j >= n_qkv)
    def _():
        p_ref[0] = res


def _inproj(x, scale, shift, w_t, layer, *, n, colmajor, tm, tn):
    bsz, seq, d = x.shape
    n_qkv = N_QKV // tn
    extra_specs, extra_args = [], []
    if colmajor:
        cols = tm // GRID_W
        x_in = x.reshape(bsz, GRID_W, seq // GRID_W, d)
        x_spec = pl.BlockSpec((1, GRID_W, cols, d), lambda b, m, j: (b, 0, m, 0))
        r_new = jnp.arange(tm, dtype=jnp.int32)[:, None]
        r_old = jnp.arange(tm, dtype=jnp.int32)[None, :]
        perm = _bf((r_old == (r_new % GRID_W) * cols + r_new // GRID_W).astype(F32))
        extra_specs, extra_args = [pl.BlockSpec((tm, tm), lambda b, m, j: (0, 0))], [perm]
    else:
        x_in = x
        x_spec = pl.BlockSpec((1, tm, d), lambda b, m, j: (b, m, 0))
    return pl.pallas_call(
        functools.partial(_inproj_kernel, colmajor=colmajor, tm=tm, tn=tn),
        grid=(bsz, seq // tm, n // tn),
        in_specs=[
            x_spec,
            pl.BlockSpec((1, 1, d), lambda b, m, j: (b, 0, 0)),
            pl.BlockSpec((1, 1, d), lambda b, m, j: (b, 0, 0)),
            pl.BlockSpec((pl.Squeezed(), pl.Element(tn), pl.Element(d)),
                         lambda b, m, j: (layer, pl.multiple_of(j * tn + jnp.where(j >= n_qkv, N_LR, 0), N_LR), 0)),
            pl.BlockSpec((pl.Squeezed(), pl.Element(128), pl.Element(d)), lambda b, m, j: (layer, N_QKV, 0)),
        ] + extra_specs,
        out_specs=[
            pl.BlockSpec((1, tm, tn), lambda b, m, j: (b, m, jnp.minimum(j, n_qkv - 1))),
            pl.BlockSpec((1, tm, tn), lambda b, m, j: (b, m, jnp.maximum(j - n_qkv, 0))),
            pl.BlockSpec((1, tm, 128), lambda b, m, j: (b, m, 0)),
        ],
        out_shape=[jax.ShapeDtypeStruct((bsz, seq, N_QKV), BF16),
                   jax.ShapeDtypeStruct((bsz, seq, n - N_QKV), F32),
                   jax.ShapeDtypeStruct((bsz, seq, 128), F32)],
        scratch_shapes=[pltpu.VMEM((tm, d), BF16)],
        compiler_params=_cparams("parallel", "parallel", "arbitrary"),
        name="inproj",
    )(x_in, scale, shift, w_t, w_t, *extra_args)


def _gla_block(q, k, v, lr, st, wg_hi, wg_lo, bg, direction, mask, rowc, nb):
    c = GLA_CHUNK
    n = c * nb
    lr_hi, lr_lo = _split(lr)
    z = _mm(lr_hi, wg_hi) + _mm(lr_hi, wg_lo) + _mm(lr_lo, wg_hi) + bg
    g = (jnp.minimum(z, 0.0) - jnp.log(1.0 + jnp.exp(-jnp.abs(z)))) * (1.0 / GLA_TAU)
    cum = g
    s = 1
    while s < c:
        if direction == 0:
            cum = cum + jnp.where(rowc >= s, pltpu.roll(cum, s, 0), 0.0)
        else:
            cum = cum + jnp.where(rowc < c - s, pltpu.roll(cum, n - s, 0), 0.0)
        s *= 2
    cum3 = cum.reshape(nb, c, GLA_DK)
    tot3 = cum3[:, c - 1:c, :] if direction == 0 else cum3[:, 0:1, :]
    q3 = q.reshape(nb, c, GLA_DK)
    k3 = k.reshape(nb, c, GLA_DK)
    qg3 = _bf(q3 * jnp.exp(cum3))
    kg3 = _bf(k3 * jnp.exp(-cum3))
    kd3 = _bf(k3 * jnp.exp(tot3 - cum3))
    a3 = jnp.exp(tot3)
    vb3 = _bf(v).reshape(nb, c, GLA_DV)
    att = [jnp.where(mask, _mm(qg3[i], kg3[i], _NT), 0.0) for i in range(nb)]
    o_in = [_mm(_bf(att[i]), vb3[i]) for i in range(nb)]
    d_st = [_mm(vb3[i], kd3[i], _TN) for i in range(nb)]
    outs = [None] * nb
    for i in (range(nb) if direction == 0 else range(nb - 1, -1, -1)):
        outs[i] = o_in[i] + _mm(qg3[i], _bf(st), _NT)
        st = st * a3[i] + d_st[i]
    return jnp.concatenate(outs, axis=0), st


def _gla_kernel(*refs, ctx_out, seq, seq_c):
    if ctx_out:
        (q_ref, k_ref, v_ref, gt_ref, lr_ref, qc_ref, kc_ref, vc_ref, gtc_ref, lrc_ref,
         wgh_ref, wgl_ref, bg_ref, gn_ref, y_ref, yc_ref) = refs
    else:
        (q_ref, k_ref, v_ref, gt_ref, lr_ref, qc_ref, kc_ref, vc_ref, lrc_ref,
         wgh_ref, wgl_ref, bg_ref, gn_ref, y_ref) = refs
        gtc_ref = yc_ref = None
    c = GLA_CHUNK
    scale = GLA_DK ** -0.5
    row = lax.broadcasted_iota(jnp.int32, (c, c), 0)
    col = lax.broadcasted_iota(jnp.int32, (c, c), 1)
    masks = (col <= row, col >= row)
    gn = gn_ref[...]

    def run(refs4, n_rows, st, direction, store):
        qr, kr, vr, lrr = refs4
        nb = min(GLA_BLOCK, n_rows // c)
        n = c * nb
        rowc = lax.broadcasted_iota(jnp.int32, (n, GLA_DK), 0) % c
        wg_hi = wgh_ref[direction, 0]
        wg_lo = wgl_ref[direction, 0]
        bg = bg_ref[direction, 0]
        n_blocks = n_rows // n

        def body(i, st):
            bi = i if direction == 0 else n_blocks - 1 - i
            rows = pl.ds(pl.multiple_of(bi * n, n), n)
            q = qr[0, rows, :].astype(F32) * scale
            k = kr[0, rows, :].astype(F32)
            v = vr[0, rows, :].astype(F32)
            o, st = _gla_block(q, k, v, lrr[0, rows, :], st, wg_hi, wg_lo, bg, direction,
                               masks[direction], rowc, nb)
            store(rows, q, k, v, o)
            return st

        return lax.fori_loop(0, n_blocks, body, st)

    def store_fwd(out_ref):
        def f(rows, q, k, v, o):
            if out_ref is not None:
                out_ref[0, rows, :] = o
        return f

    def store_bwd(out_ref, gate_ref):
        def f(rows, q, k, v, o):
            if out_ref is None:
                return
            y = out_ref[0, rows, :] + o - jnp.sum(q * k, axis=-1, keepdims=True) * v
            ms = jnp.mean(y * y, axis=-1, keepdims=True)
            y = y * lax.rsqrt(ms + EPS) * gn
            gt = gate_ref[0, rows, :]
            out_ref[0, rows, :] = y * (gt * jax.nn.sigmoid(gt))
        return f

    lat = (q_ref, k_ref, v_ref, lr_ref)
    ctx = (qc_ref, kc_ref, vc_ref, lrc_ref)
    zero = jnp.zeros((GLA_DV, GLA_DK), F32)
    st = run(ctx, seq_c, zero, 0, store_fwd(yc_ref))
    run(lat, seq, st, 0, store_fwd(y_ref))
    st = run(ctx, seq_c, zero, 1, store_bwd(yc_ref, gtc_ref))
    run(lat, seq, st, 1, store_bwd(y_ref, gt_ref))


def _gla(qkv, p, lr, qkvc, pc, lrc, wg_hi, wg_lo, bg, gnorm, *, ctx_out):
    bsz, seq, _ = qkv.shape
    seq_c = qkvc.shape[1]
    h = GLA_HEADS

    def specs(n):
        return [
            pl.BlockSpec((1, n, 128), lambda b, i: (b, 0, i)),
            pl.BlockSpec((1, n, 128), lambda b, i: (b, 0, 4 + i)),
            pl.BlockSpec((1, n, 256), lambda b, i: (b, 0, 4 + i)),
            pl.BlockSpec((1, n, 256), lambda b, i: (b, 0, COL_GLA_GATE + i)),
            pl.BlockSpec((1, n, 128), lambda b, i: (b, 0, 0)),
        ]

    lat_specs = specs(seq)
    ctx_specs = specs(seq_c)
    lat_args = [qkv, qkv, qkv, p, lr]
    ctx_args = [qkvc, qkvc, qkvc, pc, lrc]
    if not ctx_out:
        del ctx_specs[3], ctx_args[3]
    w_specs = [
        pl.BlockSpec((2, 1, 128, 128), lambda b, i: (0, i, 0, 0)),
        pl.BlockSpec((2, 1, 128, 128), lambda b, i: (0, i, 0, 0)),
        pl.BlockSpec((2, 1, 1, 128), lambda b, i: (0, i, 0, 0)),
        pl.BlockSpec((1, GLA_DV), lambda b, i: (0, 0)),
    ]
    out_specs = [pl.BlockSpec((1, seq, 256), lambda b, i: (b, 0, i))]
    out_shape = [jax.ShapeDtypeStruct((bsz, seq, h * GLA_DV), F32)]
    if ctx_out:
        out_specs.append(pl.BlockSpec((1, seq_c, 256), lambda b, i: (b, 0, i)))
        out_shape.append(jax.ShapeDtypeStruct((bsz, seq_c, h * GLA_DV), F32))
    outs = pl.pallas_call(
        functools.partial(_gla_kernel, ctx_out=ctx_out, seq=seq, seq_c=seq_c),
        grid=(bsz, h),
        in_specs=lat_specs + ctx_specs + w_specs,
        out_specs=out_specs,
        out_shape=out_shape,
        compiler_params=_cparams("parallel", "parallel"),
        name="gla",
    )(*lat_args, *ctx_args, wg_hi, wg_lo, bg, gnorm)
    return (outs[0], outs[1]) if ctx_out else (outs[0], None)


def _mm3f(a, b, dims=_NN):
    a_hi, a_lo = _split(a)
    return _mm3(a_hi, a_lo, b, dims)


def _s5_ops_kernel(a_ref, b_ref, lr_ref, li_ref, btr_ref, bti_ref, cr_ref, ci_ref,
                   kt_ref, bpr_ref, bpi_ref, cp_ref, ltr_ref, lti_ref):
    t_len, h_n, p_n = S5_T, S5_GROUP, S5_STATE
    rows = t_len * h_n
    lane = lax.broadcasted_iota(jnp.int32, (rows, 2 * p_n), 1)
    own = (lane < p_n, lane >= p_n)
    lane_blk = lax.broadcasted_iota(jnp.int32, (rows, rows), 1) // h_n
    kk = lax.broadcasted_iota(jnp.int32, (24, 2 * p_n), 0).astype(F32)
    for d in range(2):
        a, b = a_ref[0, d, 0], b_ref[0, d, 0]
        lr, li = lr_ref[0, d, 0], li_ref[0, d, 0]
        mag = jnp.exp(kk * a)
        pwr, pwi = mag * jnp.cos(kk * b), mag * jnp.sin(kk * b)

        def times_pw(k, mr, mi):
            pr, pi = pwr[k:k + 1], pwi[k:k + 1]
            return pr * mr - pi * mi, pr * mi + pi * mr

        x, y = pwr[1:2] - 1.0, pwi[1:2]
        den = lr * lr + li * li
        cfr, cfi = (x * lr + y * li) / den, (y * lr - x * li) / den
        btr, bti = btr_ref[0, d, 0], bti_ref[0, d, 0]
        bbr, bbi = cfr * btr - cfi * bti, cfr * bti + cfi * btr
        cr, ci = cr_ref[0, d, 0], ci_ref[0, d, 0]

        def stack(power_of_t, mr, mi):
            parts = [times_pw(power_of_t(t), mr, mi) for t in range(t_len)]
            return (jnp.concatenate([p[0] for p in parts], axis=0),
                    jnp.concatenate([p[1] for p in parts], axis=0))

        bpr, bpi = stack((lambda t: t_len - 1 - t) if d == 0 else (lambda t: t), bbr, bbi)
        for src, dst in ((bpr, bpr_ref), (bpi, bpi_ref)):
            z = jnp.concatenate([jnp.where(own[0], src, 0.0), jnp.where(own[1], src, 0.0)], axis=0)
            dst[0, d, 0] = _bf(z.T)
        cpr, cpi = stack((lambda t: t + 1) if d == 0 else (lambda t: t_len - t), cr, ci)
        clr, cli = stack((lambda j: j) if d == 0 else (lambda j: t_len - 1 - j), cr, ci)
        bbr_t = jnp.concatenate([bbr] * t_len, axis=0)
        bbi_t = jnp.concatenate([bbi] * t_len, axis=0)
        for e in range(2):
            cp_ref[0, d, e] = _bf(jnp.concatenate(
                [jnp.where(own[e], cpr, 0.0), jnp.where(own[e], -cpi, 0.0)], axis=1))
            kc = (_mm3f(jnp.where(own[e], clr, 0.0), bbr_t, _NT)
                  - _mm3f(jnp.where(own[e], cli, 0.0), bbi_t, _NT))
            acc = jnp.zeros((rows, rows), F32)
            for t in range(t_len):
                if d == 0:
                    pieces = [jnp.zeros((h_n * t, rows), F32), kc[:rows - h_n * t]]
                else:
                    s = h_n * (t_len - 1 - t)
                    pieces = [kc[s:], jnp.zeros((s, rows), F32)]
                shifted = jnp.concatenate([p for p in pieces if p.shape[0]], axis=0)
                acc = jnp.where(lane_blk == t, shifted, acc)
            kt_ref[0, d, e] = _bf(acc)
        ltr_ref[0, d, 0] = pwr[t_len:t_len + 1]
        lti_ref[0, d, 0] = pwi[t_len:t_len + 1]


def _s5_operators(lam_re, lam_im, log_step, b_re, b_im, c_re, c_im):
    depth = lam_re.shape[0]
    g_n, p_n, h_n = S5_GROUPS, S5_STATE, S5_GROUP
    pairs, rows = g_n // 2, S5_T * h_n
    dt = jnp.exp(log_step)[..., None]
    vec = lambda v: v.reshape(depth, 2, pairs, 1, 2 * p_n)
    bt = lambda m: m.reshape(depth, 2, pairs, 2, p_n, h_n).transpose(0, 1, 2, 5, 3, 4).reshape(depth, 2, pairs, h_n, 2 * p_n)
    ct = lambda m: m.reshape(depth, 2, pairs, 2, h_n, p_n).transpose(0, 1, 2, 4, 3, 5).reshape(depth, 2, pairs, h_n, 2 * p_n)
    vspec = pl.BlockSpec((1, 2, 1, 1, 2 * p_n), lambda l, q: (l, 0, q, 0, 0))
    mspec = pl.BlockSpec((1, 2, 1, h_n, 2 * p_n), lambda l, q: (l, 0, q, 0, 0))
    sq_spec = pl.BlockSpec((1, 2, 2, rows, rows), lambda l, q: (l, 0, q, 0, 0))
    bp_spec = pl.BlockSpec((1, 2, 1, 2 * p_n, 2 * rows), lambda l, q: (l, 0, q, 0, 0))
    sq = jax.ShapeDtypeStruct((depth, 2, g_n, rows, rows), BF16)
    bp = jax.ShapeDtypeStruct((depth, 2, pairs, 2 * p_n, 2 * rows), BF16)
    lt = jax.ShapeDtypeStruct((depth, 2, pairs, 1, 2 * p_n), F32)
    ktoep, bpr, bpi, cpad, ltr, lti = pl.pallas_call(
        _s5_ops_kernel,
        grid=(depth, pairs),
        in_specs=[vspec] * 4 + [mspec] * 4,
        out_specs=[sq_spec, bp_spec, bp_spec, sq_spec, vspec, vspec],
        out_shape=[sq, bp, bp, sq, lt, lt],
        compiler_params=_cparams("parallel", "parallel"),
        name="s5_operators",
    )(vec(lam_re * dt), vec(lam_im * dt), vec(lam_re), vec(lam_im), bt(b_re), bt(b_im), ct(c_re), ct(c_im))
    flat = lambda v: v.reshape(depth, 2, 1, g_n * p_n)
    return ktoep, bpr, bpi, cpad, flat(ltr), flat(lti)


def _s5_in_kernel(u_ref, uc_ref, o_ref):
    j = pl.program_id(0)
    last = pl.num_programs(0) - 1

    def emit(slab_of_t):
        for t in range(S5_T):
            o_ref[:, t * S5_GROUP:(t + 1) * S5_GROUP, :] = _bf(
                slab_of_t(t).T.reshape(S5_GROUPS, S5_GROUP, S5_CB))

    @pl.when(j < last)
    def _():
        emit(lambda t: u_ref[0, :, t, :])

    @pl.when(j == last)
    def _():
        pad = jnp.zeros((S5_CB - 2 * uc_ref.shape[1], S5_WIDTH), F32)
        emit(lambda t: jnp.concatenate([uc_ref[0, :, t, :], uc_ref[1, :, t, :], pad], axis=0))


def _s5_lat_index(n_half):
    def index(j):
        jj = jnp.minimum(j, 2 * n_half - 1)
        return jj // n_half, jj % n_half
    return index


def _s5_relayout_in(p, pc):
    bsz, seq, n = p.shape
    seq_c, n_c = pc.shape[1], pc.shape[2]
    n_lat, n_ctx = seq // S5_T, seq_c // S5_T
    n_half = n_lat // S5_CB
    idx = _s5_lat_index(n_half)
    return pl.pallas_call(
        _s5_in_kernel,
        grid=(bsz * n_half + 1,),
        in_specs=[
            pl.BlockSpec((1, S5_CB, S5_T, 512), lambda j: (*idx(j), 0, COL_S5U)),
            pl.BlockSpec((bsz, n_ctx, S5_T, 512), lambda j: (0, 0, 0, COL_S5U)),
        ],
        out_specs=pl.BlockSpec((S5_GROUPS, S5_T * S5_GROUP, S5_CB), lambda j: (0, 0, j)),
        out_shape=jax.ShapeDtypeStruct((S5_GROUPS, S5_T * S5_GROUP, S5_COLS), BF16),
        compiler_params=_cparams("arbitrary"),
        name="s5_relayout_in",
    )(p.reshape(bsz, n_lat, S5_T, n), pc.reshape(bsz, n_ctx, S5_T, n_c))


def _s5_state_in_kernel(ug_ref, bre_ref, bim_ref, vre_ref, vim_ref):
    u = ug_ref[...].reshape(2 * S5_T * S5_GROUP, S5_COLS)
    for direction in range(2):
        vre_ref[direction] = _mm(bre_ref[0, direction, 0], u)
        vim_ref[direction] = _mm(bim_ref[0, direction, 0], u)


def _s5_state_in(ug, b_pair_re, b_pair_im, layer):
    pairs = S5_GROUPS // 2
    rows = S5_T * S5_GROUP
    lanes = S5_GROUPS * S5_STATE
    out = jax.ShapeDtypeStruct((2, lanes, S5_COLS), F32)
    return pl.pallas_call(
        _s5_state_in_kernel,
        grid=(pairs,),
        in_specs=[
            pl.BlockSpec((2, rows, S5_COLS), lambda q: (q, 0, 0)),
            pl.BlockSpec((1, 2, 1, 128, 2 * rows), lambda q: (layer, 0, q, 0, 0)),
            pl.BlockSpec((1, 2, 1, 128, 2 * rows), lambda q: (layer, 0, q, 0, 0)),
        ],
        out_specs=[pl.BlockSpec((2, 128, S5_COLS), lambda q: (0, q, 0))] * 2,
        out_shape=[out, out],
        compiler_params=_cparams("parallel"),
        name="s5_state_in",
    )(ug, b_pair_re, b_pair_im)


def _s5_scan_kernel(vre_ref, vim_ref, lre_ref, lim_ref, xre_ref, xim_ref, tre_scr, tim_scr, *, n_lat, n_ctx):
    lanes = lre_ref.shape[-1]
    for d in range(2):
        tre_scr[d] = vre_ref[d].T
        tim_scr[d] = vim_ref[d].T
    a_re = [lre_ref[0, d] for d in range(2)]
    a_im = [lim_ref[0, d] for d in range(2)]

    def step(d, row, xr, xi):
        sl = pl.ds(row, 1)
        xre_ref[d, sl, :] = xr
        xim_ref[d, sl, :] = xi
        return (a_re[d] * xr - a_im[d] * xi + tre_scr[d, sl, :],
                a_re[d] * xi + a_im[d] * xr + tim_scr[d, sl, :])

    def run(bases, n, carry):
        def body(i, carry):
            out = []
            for d in range(2):
                j = i if d == 0 else n - 1 - i
                for b in range(2):
                    xr, xi = carry[2 * d + b]
                    out.append(step(d, bases[b] + j, xr, xi))
            return tuple(out)
        return lax.fori_loop(0, n, body, carry)

    zero = jnp.zeros((1, lanes), F32)
    carry = run((2 * n_lat, 2 * n_lat + n_ctx), n_ctx, ((zero, zero),) * 4)
    run((0, n_lat), n_lat, carry)
    pad0 = 2 * n_lat + 2 * n_ctx
    for d in range(2):
        xre_ref[d, pad0:, :] = jnp.zeros((S5_COLS - pad0, lanes), F32)
        xim_ref[d, pad0:, :] = jnp.zeros((S5_COLS - pad0, lanes), F32)


def _s5_scan(vre, vim, lam_re, lam_im, layer, *, n_lat, n_ctx):
    lanes = vre.shape[1]
    tl = 512
    vspec = pl.BlockSpec((2, tl, S5_COLS), lambda j: (0, j, 0))
    xspec = pl.BlockSpec((2, S5_COLS, tl), lambda j: (0, 0, j))
    lspec = pl.BlockSpec((1, 2, 1, tl), lambda j: (layer, 0, 0, j))
    out = jax.ShapeDtypeStruct((2, S5_COLS, lanes), F32)
    return pl.pallas_call(
        functools.partial(_s5_scan_kernel, n_lat=n_lat, n_ctx=n_ctx),
        grid=(lanes // tl,),
        in_specs=[vspec, vspec, lspec, lspec],
        out_specs=[xspec, xspec],
        out_shape=[out, out],
        scratch_shapes=[pltpu.VMEM((2, S5_COLS, tl), F32), pltpu.VMEM((2, S5_COLS, tl), F32)],
        compiler_params=_cparams("parallel"),
        name="s5_scan",
    )(vre, vim, lam_re, lam_im)


def _s5_readout_kernel(ug_ref, kt_ref, cp_ref, xre_ref, xim_ref, y_ref):
    for e in range(2):
        u = ug_ref[e]
        acc = None
        for direction in range(2):
            xcat = _bf(jnp.concatenate([xre_ref[direction], xim_ref[direction]], axis=1))
            term = _mm(kt_ref[0, direction, e], u) + _mm(cp_ref[0, direction, e], xcat, _NT)
            acc = term if acc is None else acc + term
        y_ref[e] = acc


def _s5_readout(ug, ktoep, cpad, xre, xim, layer):
    pairs = S5_GROUPS // 2
    rows = S5_T * S5_GROUP
    return pl.pallas_call(
        _s5_readout_kernel,
        grid=(pairs,),
        in_specs=[
            pl.BlockSpec((2, rows, S5_COLS), lambda q: (q, 0, 0)),
            pl.BlockSpec((1, 2, 2, rows, rows), lambda q: (layer, 0, q, 0, 0)),
            pl.BlockSpec((1, 2, 2, rows, rows), lambda q: (layer, 0, q, 0, 0)),
            pl.BlockSpec((2, S5_COLS, 128), lambda q: (0, 0, q)),
            pl.BlockSpec((2, S5_COLS, 128), lambda q: (0, 0, q)),
        ],
        out_specs=pl.BlockSpec((2, rows, S5_COLS), lambda q: (q, 0, 0)),
        out_shape=jax.ShapeDtypeStruct((S5_GROUPS, rows, S5_COLS), F32),
        compiler_params=_cparams("parallel"),
        name="s5_readout",
    )(ug, ktoep, cpad, xre, xim)


def _s5_out_kernel(*refs, ctx_out):
    if ctx_out:
        yg_ref, u_ref, g_ref, uc_ref, gc_ref, d_ref, w_ref, b_ref, o_ref, oc_ref = refs
    else:
        yg_ref, u_ref, g_ref, d_ref, w_ref, b_ref, o_ref = refs
    d = d_ref[...]
    w = w_ref[...]
    bias = b_ref[...]

    def finish(yy, u, gate):
        yy = yy + d * u
        yg = jax.nn.gelu(yy)
        out = yg * jax.nn.sigmoid(_mm(_bf(yg), w) + bias)
        return out * (gate * jax.nn.sigmoid(gate))

    def y_of(t):
        return yg_ref[:, t * S5_GROUP:(t + 1) * S5_GROUP, :].reshape(S5_WIDTH, S5_CB).T

    def lat():
        for t in range(S5_T):
            o_ref[0, :, t, :] = finish(y_of(t), u_ref[0, :, t, :], g_ref[0, :, t, :])

    if not ctx_out:
        lat()
        return
    j = pl.program_id(0)
    last = pl.num_programs(0) - 1
    pl.when(j < last)(lat)

    @pl.when(j == last)
    def _():
        n_ctx = uc_ref.shape[1]
        for t in range(S5_T):
            y = y_of(t)
            for b in range(2):
                oc_ref[b, :, t, :] = finish(y[b * n_ctx:(b + 1) * n_ctx], uc_ref[b, :, t, :], gc_ref[b, :, t, :])


def _s5_relayout_out(yg, p, pc, d_skip, w_glu, b_glu, *, ctx_out):
    bsz, seq, n = p.shape
    seq_c, n_c = pc.shape[1], pc.shape[2]
    n_lat, n_ctx = seq // S5_T, seq_c // S5_T
    n_half = n_lat // S5_CB
    idx = _s5_lat_index(n_half)
    pv = p.reshape(bsz, n_lat, S5_T, n)
    lat_block = (1, S5_CB, S5_T, 512)
    in_specs = [
        pl.BlockSpec((S5_GROUPS, S5_T * S5_GROUP, S5_CB), lambda j: (0, 0, j)),
        pl.BlockSpec(lat_block, lambda j: (*idx(j), 0, COL_S5U)),
        pl.BlockSpec(lat_block, lambda j: (*idx(j), 0, COL_S5G)),
    ]
    args = [yg, pv, pv]
    out_specs = [pl.BlockSpec(lat_block, lambda j: (*idx(j), 0, 0))]
    out_shape = [jax.ShapeDtypeStruct((bsz, n_lat, S5_T, S5_WIDTH), F32)]
    if ctx_out:
        pcv = pc.reshape(bsz, n_ctx, S5_T, n_c)
        ctx_block = (bsz, n_ctx, S5_T, 512)
        in_specs += [
            pl.BlockSpec(ctx_block, lambda j: (0, 0, 0, COL_S5U)),
            pl.BlockSpec(ctx_block, lambda j: (0, 0, 0, COL_S5G)),
        ]
        args += [pcv, pcv]
        out_specs.append(pl.BlockSpec(ctx_block, lambda j: (0, 0, 0, 0)))
        out_shape.append(jax.ShapeDtypeStruct((bsz, n_ctx, S5_T, S5_WIDTH), F32))
    in_specs += [
        pl.BlockSpec((1, S5_WIDTH), lambda j: (0, 0)),
        pl.BlockSpec((S5_WIDTH, S5_WIDTH), lambda j: (0, 0)),
        pl.BlockSpec((1, S5_WIDTH), lambda j: (0, 0)),
    ]
    args += [d_skip, w_glu, b_glu]
    outs = pl.pallas_call(
        functools.partial(_s5_out_kernel, ctx_out=ctx_out),
        grid=(bsz * n_half + (1 if ctx_out else 0),),
        in_specs=in_specs,
        out_specs=out_specs,
        out_shape=out_shape,
        compiler_params=_cparams("arbitrary"),
        name="s5_relayout_out",
    )(*args)
    y = outs[0].reshape(bsz, seq, S5_WIDTH)
    yc = outs[1].reshape(bsz, seq_c, S5_WIDTH) if ctx_out else None
    return y, yc


def _s5(p, pc, ops, layer, d_skip, w_glu, b_glu, *, ctx_out):
    ktoep, b_pair_re, b_pair_im, cpad, lam_re, lam_im = ops
    ug = _s5_relayout_in(p, pc)
    vre, vim = _s5_state_in(ug, b_pair_re, b_pair_im, layer)
    xre, xim = _s5_scan(vre, vim, lam_re, lam_im, layer, n_lat=p.shape[1] // S5_T, n_ctx=pc.shape[1] // S5_T)
    yg = _s5_readout(ug, ktoep, cpad, xre, xim, layer)
    return _s5_relayout_out(yg, p, pc, d_skip, w_glu, b_glu, ctx_out=ctx_out)


def _dft_tables():
    n1, n2 = FFT_N1, FFT_N2
    n = n1 * n2
    a = jnp.arange(n1, dtype=jnp.int32)
    ph1 = (a[:, None] * a[None, :]) % n1
    ang1 = ph1.astype(F32) * (2.0 * math.pi / n1)
    c1, s1 = jnp.cos(ang1), -jnp.sin(ang1)
    h = n1 // 2
    m1 = jnp.concatenate([jnp.concatenate([c1[:, :h], -s1[:, :h]], 1),
                          jnp.concatenate([s1[:, :h], c1[:, :h]], 1)], 0)
    m1_real = jnp.concatenate([c1, s1], 0)
    m3 = m1.T / n
    k1 = jnp.arange(n1, dtype=jnp.int32)[:, None, None]
    k2 = jnp.arange(n2, dtype=jnp.int32)[None, :, None]
    m2 = jnp.arange(n2, dtype=jnp.int32)[None, None, :]
    ph = (n1 * m2 * k2 + m2 * k1) % n
    ang = ph.astype(F32) * (2.0 * math.pi / n)
    gr, gi = jnp.cos(ang), -jnp.sin(ang)
    g = jnp.concatenate([jnp.concatenate([gr, -gi], 2), jnp.concatenate([gi, gr], 2)], 1)
    gt = jnp.swapaxes(g, 1, 2)
    return tuple(_split(t) for t in (m1, m1_real, m3, g, gt))


def _filter_features(length):
    t = jnp.linspace(0.0, 1.0, length, dtype=F32)[:, None]
    ang = (2.0 * math.pi / length) * jnp.arange(length, dtype=F32)[:, None]
    bands = jnp.linspace(1e-4, HY_BANDS - 1, HY_BANDS, dtype=F32)[None, :]
    feats = jnp.concatenate([t, jnp.cos(bands * ang), -jnp.sin(bands * ang)], axis=-1)
    feats = jnp.pad(feats, ((0, 0), (0, 128 - HY_EMB)))
    rev = jnp.roll(feats[::-1], 1, axis=0)
    return jnp.stack([feats, rev])


def _filter_kernel(f_ref, w1_ref, b1_ref, f1_ref, w2_ref, b2_ref, f2_ref, w3_ref, dl_ref, o_ref):
    half = pl.program_id(1)
    tile = pl.program_id(2)
    x = f_ref[0]
    x_hi, x_lo = _split(x)
    h = jnp.sin(f1_ref[0] * (_mm3(x_hi, x_lo, w1_ref[0]) + b1_ref[0]))
    h_hi, h_lo = _split(h)
    h = jnp.sin(f2_ref[0] * (_mm3(h_hi, h_lo, w2_ref[0]) + b2_ref[0]))
    y = _mm(_bf(h), _bf(w3_ref[0, 0])) * jnp.exp(-x[:, 0:1] * dl_ref[...])
    rows = lax.broadcasted_iota(jnp.int32, y.shape, 0)
    drop = jnp.logical_and(jnp.logical_and(half == 1, tile == 0), rows == 0)
    o_ref[0] = jnp.where(drop, 0.0, y)


def _filter_taps(feats, w1, b1, f1, w2, b2, f2, w3, deltas, tmf):
    depth = w1.shape[0]
    length = feats.shape[1]
    nt = length // tmf
    wide = w3.shape[-1]
    vec = pl.BlockSpec((1, 1, 128), lambda l, s, i: (l, 0, 0))
    sq = pl.BlockSpec((1, 128, 128), lambda l, s, i: (l, 0, 0))
    return pl.pallas_call(
        _filter_kernel,
        grid=(depth, 2, nt),
        in_specs=[
            pl.BlockSpec((1, tmf, 128), lambda l, s, i: (s, i, 0)),
            sq, vec, vec, sq, vec, vec,
            pl.BlockSpec((1, 1, 128, wide), lambda l, s, i: (l, s, 0, 0)),
            pl.BlockSpec((1, wide), lambda l, s, i: (0, 0)),
        ],
        out_specs=pl.BlockSpec((1, tmf, wide), lambda l, s, i: (l, s * nt + i, 0)),
        out_shape=jax.ShapeDtypeStruct((depth, 2 * length, wide), F32),
        compiler_params=_cparams("parallel", "parallel", "parallel"),
        name="hyena_filter",
    )(feats, w1, b1, f1, w2, b2, f2, w3, deltas)


def _parts(pair):
    return tuple(pair) if HY_PASSES == 3 else tuple(pair[:1])


def _mmp(m, b):
    return _mm(m[0], _bf(b)) if len(m) == 1 else _mm3(m[0], m[1], b)


def _blk_rows(n2):
    return pl.ds(pl.multiple_of(n2 * HY_PITCH, 8), 2 * FFT_N1)


def _pad_rows(n2, n):
    return pl.ds(n2, n, stride=HY_PITCH)


def _pad_copy(src, dst, n):
    for n1 in range(n):
        dst[n1 * HY_PITCH:n1 * HY_PITCH + FFT_N2, :] = src[n1 * FFT_N2:(n1 + 1) * FFT_N2, :]


def _dft_stage1(read_rows, m, a_scr):
    ct = a_scr.shape[-1]

    def body(it, carry):
        n2 = it * HY_NB
        a = _mmp(m, jnp.concatenate([read_rows(n2 + j) for j in range(HY_NB)], axis=1))
        for j in range(HY_NB):
            a_scr[_blk_rows(n2 + j), :] = a[:, j * ct:(j + 1) * ct]
        return carry

    lax.fori_loop(0, FFT_N2 // HY_NB, body, 0)


def _stage2_rows(k1):
    return (pl.ds(k1, FFT_N2, stride=HY_PITCH), pl.ds(k1 + FFT_N1, FFT_N2, stride=HY_PITCH))


def _stage2_load(a_scr, kb):
    rows = [_stage2_rows(kb * HY_KB + i) for i in range(HY_KB)]
    return rows, [jnp.concatenate([a_scr[re, :], a_scr[im, :]], axis=0) for re, im in rows]


def _spectrum_kernel(*refs):
    np_ = len(_parts((0, 0)))
    m_refs, g_refs, (t_ref, o_ref, a_scr, pad_scr) = refs[:np_], refs[np_:2 * np_], refs[2 * np_:]
    kb = pl.program_id(2)
    ct = t_ref.shape[-1]

    @pl.when(kb == 0)
    def _():
        _pad_copy(t_ref.at[0], pad_scr, FFT_N1)
        _dft_stage1(lambda n2: pad_scr[_pad_rows(n2, FFT_N1), :], [r[...] for r in m_refs], a_scr)

    _, a = _stage2_load(a_scr, kb)
    for i in range(HY_KB):
        o_ref[0, i] = _mmp([r[i] for r in g_refs], a[i]).reshape(2, FFT_N2, ct)


def _filter_spectrum(tables, taps):
    m, g = _parts(tables[1]), _parts(tables[3])
    depth, n, ch = taps.shape
    ct = HY_CT
    gspec = pl.BlockSpec((HY_KB, 2 * FFT_N2, 2 * FFT_N2), lambda l, c, k: (k, 0, 0))
    mspec = pl.BlockSpec(m[0].shape, lambda l, c, k: (0, 0))
    return pl.pallas_call(
        _spectrum_kernel,
        grid=(depth, ch // ct, FFT_N1 // HY_KB),
        in_specs=[mspec] * len(m) + [gspec] * len(g) + [pl.BlockSpec((1, n, ct), lambda l, c, k: (l, 0, c))],
        out_specs=pl.BlockSpec((1, HY_KB, 2, FFT_N2, ct), lambda l, c, k: (l, k, 0, 0, c)),
        out_shape=jax.ShapeDtypeStruct((depth, FFT_N1, 2, FFT_N2, ch), F32),
        scratch_shapes=[pltpu.VMEM((HY_PITCH * FFT_N2, ct), F32), pltpu.VMEM((HY_PITCH * FFT_N1, ct), F32)],
        compiler_params=_cparams("parallel", "parallel", "arbitrary"),
        name="hyena_spectrum",
    )(*m, *g, taps)


def _hy_pre_kernel(pv_ref, p1_ref, p2_ref, pg_ref, wv_ref, w1_ref, w2_ref, bv_ref, b1_ref, b2_ref,
                   v_ref, x1_ref, x2_ref):
    n = pv_ref.shape[1]
    rows = lax.broadcasted_iota(jnp.int32, (n, 128), 0)

    def conv(p, w_ref, b_ref):
        up = jnp.where(rows == 0, 0.0, pltpu.roll(p, 1, 0))
        dn = jnp.where(rows == n - 1, 0.0, pltpu.roll(p, n - 1, 0))
        return up * w_ref[0:1, :] + p * w_ref[1:2, :] + dn * w_ref[2:3, :] + b_ref[...]

    v_ref[0] = conv(pv_ref[0], wv_ref, bv_ref)
    x1_ref[0] = conv(p1_ref[0], w1_ref, b1_ref)
    gate = pg_ref[0]
    x2_ref[0] = conv(p2_ref[0], w2_ref, b2_ref) * (gate * jax.nn.sigmoid(gate))


def _hy_pre(p, conv_w, conv_b):
    bsz, seq, _ = p.shape
    def pspec(off):
        return pl.BlockSpec((1, seq, 128), lambda b, j: (b, 0, off + j))
    def wspec(off):
        return pl.BlockSpec((3, 128), lambda b, j: (0, off + j))
    def bspec(off):
        return pl.BlockSpec((1, 128), lambda b, j: (0, off + j))
    out = jax.ShapeDtypeStruct((bsz, seq, HY_WIDTH), F32)
    ospec = pl.BlockSpec((1, seq, 128), lambda b, j: (b, 0, j))
    return pl.pallas_call(
        _hy_pre_kernel,
        grid=(bsz, HY_WIDTH // 128),
        in_specs=[pspec(COL_HY), pspec(COL_HY + 4), pspec(COL_HY + 8), pspec(COL_HY + 12),
                  wspec(0), wspec(4), wspec(8), bspec(0), bspec(4), bspec(8)],
        out_specs=[ospec, ospec, ospec],
        out_shape=[out, out, out],
        compiler_params=_cparams("parallel", "parallel"),
        name="hyena_pre",
    )(p, p, p, p, conv_w, conv_w, conv_w, conv_b, conv_b, conv_b)


def _hy_conv_kernel(*refs):
    np_ = len(_parts((0, 0)))
    m1_refs, m3_refs, g_refs, t_refs = (refs[i * np_:(i + 1) * np_] for i in range(4))
    u_ref, gate_ref, h_ref, d_ref, o_ref, a_scr, pad_scr = refs[4 * np_:]
    kb = pl.program_id(1)
    ct = u_ref.shape[-1]
    half = FFT_N1 // 2

    @pl.when(kb == 0)
    def _():
        for b in range(2):
            _pad_copy(u_ref.at[b], pad_scr.at[b], half)
        _dft_stage1(lambda n2: jnp.concatenate([pad_scr[0, _pad_rows(n2, half), :],
                                                pad_scr[1, _pad_rows(n2, half), :]], axis=0),
                    [r[...] for r in m1_refs], a_scr)

    srows, a = _stage2_load(a_scr, kb)
    z = [_mmp([r[i] for r in g_refs], a[i]) for i in range(HY_KB)]
    z2 = []
    for i in range(HY_KB):
        zr, zi = z[i][:FFT_N2], z[i][FFT_N2:]
        hr, hi = h_ref[0, i, 0], h_ref[0, i, 1]
        z2.append(jnp.concatenate([zr * hr - zi * hi, zr * hi + zi * hr], axis=0))
    a2 = [_mmp([r[i] for r in t_refs], z2[i]) for i in range(HY_KB)]
    for i in range(HY_KB):
        a_scr[srows[i][0], :] = a2[i][:FFT_N2]
        a_scr[srows[i][1], :] = a2[i][FFT_N2:]

    @pl.when(kb == pl.num_programs(1) - 1)
    def _():
        m3 = [r[...] for r in m3_refs]
        d = d_ref[...]

        def body(it, carry):
            n2 = it * HY_NB
            y = _mmp(m3, jnp.concatenate([a_scr[_blk_rows(n2 + j), :] for j in range(HY_NB)], axis=1))
            for j in range(HY_NB):
                for b in range(2):
                    pad_scr[b, _pad_rows(n2 + j, half), :] = y[b * half:(b + 1) * half, j * ct:(j + 1) * ct]
            return carry

        lax.fori_loop(0, FFT_N2 // HY_NB, body, 0)
        for b in range(2):
            for n1 in range(half):
                r = slice(n1 * FFT_N2, (n1 + 1) * FFT_N2)
                conv = pad_scr[b, n1 * HY_PITCH:n1 * HY_PITCH + FFT_N2, :]
                o_ref[b, r, :] = gate_ref[b, r, :] * (conv + d * u_ref[b, r, :])


def _hy_long_conv(tables, spec, layer, order, u, gate, d):
    m1, _, m3, g, gt = (_parts(t) for t in tables)
    bsz, seq, ch = u.shape
    ct = HY_CT
    gspec = pl.BlockSpec((HY_KB, 2 * FFT_N2, 2 * FFT_N2), lambda c, k: (k, 0, 0))
    full = lambda t: pl.BlockSpec(t.shape, lambda c, k: (0, 0))
    uspec = pl.BlockSpec((bsz, seq, ct), lambda c, k: (0, 0, c))
    n_ct = ch // ct
    mats = (*m1, *m3, *g, *gt)
    return pl.pallas_call(
        _hy_conv_kernel,
        grid=(n_ct, FFT_N1 // HY_KB),
        in_specs=[full(t) for t in (*m1, *m3)] + [gspec] * (len(g) + len(gt)) + [
            uspec, uspec,
            pl.BlockSpec((1, HY_KB, 2, FFT_N2, ct), lambda c, k: (layer, k, 0, 0, order * n_ct + c)),
            pl.BlockSpec((1, ct), lambda c, k: (0, c))],
        out_specs=uspec,
        out_shape=jax.ShapeDtypeStruct(u.shape, F32),
        scratch_shapes=[pltpu.VMEM((HY_PITCH * FFT_N2, ct), F32),
                        pltpu.VMEM((bsz, HY_PITCH * FFT_N1 // 2, ct), F32)],
        compiler_params=_cparams("parallel", "arbitrary"),
        name="hyena_conv",
    )(*mats, u, gate, spec, d.reshape(1, ch))


def _ctx_dft_tables(length):
    n = 2 * length
    k = jnp.arange(n, dtype=jnp.int32)
    ang = ((k[:, None] * k[None, :]) % n).astype(F32) * (2.0 * math.pi / n)
    fr, fi = jnp.cos(ang), -jnp.sin(ang)
    fwd = jnp.concatenate([jnp.concatenate([fr[:, :length], -fi[:, :length]], 1),
                           jnp.concatenate([fi[:, :length], fr[:, :length]], 1)], 0)
    real = jnp.concatenate([fr, fi], 0)
    inv = fwd.T / n
    return tuple(_split(t) for t in (fwd, real, inv))


def _hyc_kernel(fh_ref, fl_ref, rh_ref, rl_ref, ih_ref, il_ref, taps_ref, u_ref, g_ref, d_ref, o_ref):
    n = u_ref.shape[1]
    x = jnp.concatenate([u_ref[0], u_ref[1]], axis=0)
    z = _mm3(fh_ref[...], fl_ref[...], x)
    h = _mm3(rh_ref[...], rl_ref[...], taps_ref[0])
    m = 2 * n
    zr, zi, hr, hi = z[:m], z[m:], h[:m], h[m:]
    z2 = jnp.concatenate([zr * hr - zi * hi, zr * hi + zi * hr], axis=0)
    y = _mm3(ih_ref[...], il_ref[...], z2)
    for b in range(2):
        o_ref[b] = g_ref[b] * (y[b * n:(b + 1) * n] + d_ref[...] * u_ref[b])


def _hyc_long_conv(ctabs, taps, layer, order, u, gate, d):
    (f_hi, f_lo), (r_hi, r_lo), (i_hi, i_lo) = ctabs
    bsz, n, ch = u.shape
    full = lambda t: pl.BlockSpec(t.shape, lambda i: (0,) * t.ndim)
    uspec = pl.BlockSpec((bsz, n, ch), lambda i: (0, 0, 0))
    return pl.pallas_call(
        _hyc_kernel,
        grid=(1,),
        in_specs=[full(f_hi), full(f_lo), full(r_hi), full(r_lo), full(i_hi), full(i_lo),
                  pl.BlockSpec((1, 2 * n, ch), lambda i: (layer, 0, order)),
                  uspec, uspec, pl.BlockSpec((1, ch), lambda i: (0, 0))],
        out_specs=uspec,
        out_shape=jax.ShapeDtypeStruct((bsz, n, ch), F32),
        compiler_params=_cparams("arbitrary"),
        name="hyena_ctx",
    )(f_hi, f_lo, r_hi, r_lo, i_hi, i_lo, taps, u, gate, d.reshape(1, ch))


def _outproj_kernel(*refs, colmajor, tm):
    if colmajor:
        x_ref, ya_ref, yb_ref, yc_ref, w_ref, gp_ref, gate_ref, perm_ref, o_ref = refs
        ld = lambda r: _bf(_mm(perm_ref[...], _bf(r[0])))
    else:
        x_ref, ya_ref, yb_ref, yc_ref, w_ref, gp_ref, gate_ref, o_ref = refs
        ld = lambda r: _bf(r[0])
    na, nb = ya_ref.shape[-1], yb_ref.shape[-1]
    acc = (_mm(ld(ya_ref), w_ref[0, 0:na, :]) + _mm(ld(yb_ref), w_ref[0, na:na + nb, :])
           + _mm(ld(yc_ref), w_ref[0, na + nb:, :]))
    ms = jnp.mean(acc * acc, axis=-1, keepdims=True)
    upd = gate_ref[0] * (acc * lax.rsqrt(ms + EPS) * gp_ref[...])
    o_ref[0] = x_ref[0] + upd.reshape(x_ref.shape[1:])


def _outproj(x, ya, yb, yc, w_out, layer, g_post, gate, *, colmajor, tm):
    bsz, seq, d = x.shape
    extra_specs, extra_args = [], []
    if colmajor:
        cols = tm // GRID_W
        x_in = x.reshape(bsz, GRID_W, seq // GRID_W, d)
        x_spec = pl.BlockSpec((1, GRID_W, cols, d), lambda b, m: (b, 0, m, 0))
        r_old = jnp.arange(tm, dtype=jnp.int32)[:, None]
        r_new = jnp.arange(tm, dtype=jnp.int32)[None, :]
        perm = _bf((r_new == (r_old % cols) * GRID_W + r_old // cols).astype(F32))
        extra_specs, extra_args = [pl.BlockSpec((tm, tm), lambda b, m: (0, 0))], [perm]
    else:
        x_in = x
        x_spec = pl.BlockSpec((1, tm, d), lambda b, m: (b, m, 0))
    def yspec(t):
        return pl.BlockSpec((1, tm, t.shape[-1]), lambda b, m: (b, m, 0))
    out = pl.pallas_call(
        functools.partial(_outproj_kernel, colmajor=colmajor, tm=tm),
        grid=(bsz, seq // tm),
        in_specs=[x_spec, yspec(ya), yspec(yb), yspec(yc),
                  pl.BlockSpec((1,) + w_out.shape[1:], lambda b, m: (layer, 0, 0)),
                  pl.BlockSpec((1, d), lambda b, m: (0, 0)),
                  pl.BlockSpec((1, 1, d), lambda b, m: (b, 0, 0))] + extra_specs,
        out_specs=x_spec,
        out_shape=jax.ShapeDtypeStruct(x_in.shape, F32),
        compiler_params=_cparams("parallel", "parallel"),
        name="outproj",
    )(x_in, ya, yb, yc, w_out, g_post, gate, *extra_args)
    return out.reshape(bsz, seq, d)


def kernel(x, c, ctx, c_ctx, w_mod, b_mod, g_pre, g_post, w_in, w_out, gla_w_gate, gla_b_gate, gla_norm, s5_lam_re, s5_lam_im, s5_log_step, s5_b_re, s5_b_im, s5_c_re, s5_c_im, s5_d, s5_w_glu, s5_b_glu, hy_conv_w, hy_conv_b, hy_w1, hy_b1, hy_f1, hy_w2, hy_b2, hy_f2, hy_w3, hy_d):
    bsz, seq, d = x.shape
    seq_c = ctx.shape[1]
    depth = w_in.shape[0]
    assert (bsz, d, seq // GRID_W) == (2, D_MODEL, GRID_W) and seq == FFT_N1 * FFT_N2 // 2

    cvec = jnp.concatenate([c, c_ctx[None], jnp.zeros((8 - bsz - 1, d), F32)], axis=0)
    mod = _modulation(cvec, w_mod, b_mod)
    shift, scale, gate = mod[..., :d], mod[..., d:2 * d], mod[..., 2 * d:]

    w_t = _bf(jnp.swapaxes(w_in, 1, 2))
    w_out_b = _bf(w_out)
    w_glu_b = _bf(s5_w_glu)

    wg = gla_w_gate.reshape(depth, 2, 16, GLA_HEADS, GLA_DK).transpose(0, 1, 3, 2, 4)
    wg_pad = jnp.stack([jnp.pad(wg[:, 0], ((0, 0), (0, 0), (0, 112), (0, 0))),
                        jnp.pad(wg[:, 1], ((0, 0), (0, 0), (16, 96), (0, 0)))], axis=1)
    wg_hi, wg_lo = _split(wg_pad)
    bg = gla_b_gate.reshape(depth, 2, GLA_HEADS, 1, GLA_DK)

    tables = _dft_tables()
    deltas = jnp.abs(jnp.linspace(HY_MIN_DECAY, HY_MAX_DECAY, HY_WIDTH, dtype=F32))
    deltas2 = jnp.tile(deltas, 2).reshape(1, 2 * HY_WIDTH)
    pad_k = lambda w: jnp.pad(w, ((0, 0), (0, 128 - w.shape[1]), (0, 128 - w.shape[2])))
    pad_v = lambda v: jnp.pad(v, ((0, 0), (0, 128 - v.shape[1]))).reshape(depth, 1, 128)
    w3 = hy_w3.reshape(depth, HY_FFN, 2, 2, HY_WIDTH).transpose(0, 3, 1, 2, 4)
    w3 = jnp.pad(w3.reshape(depth, 2, HY_FFN, 2 * HY_WIDTH), ((0, 0), (0, 0), (0, 128 - HY_FFN), (0, 0)))
    filt_w = (pad_k(hy_w1), pad_v(hy_b1), pad_v(hy_f1), pad_k(hy_w2), pad_v(hy_b2), pad_v(hy_f2), w3, deltas2)
    taps = _filter_taps(_filter_features(seq), *filt_w, tmf=512)
    spec = _filter_spectrum(tables, taps)
    taps_c = _filter_taps(_filter_features(seq_c), *filt_w, tmf=seq_c)
    ctabs = _ctx_dft_tables(seq_c)

    s5_ops = _s5_operators(s5_lam_re, s5_lam_im, s5_log_step, s5_b_re, s5_b_im, s5_c_re, s5_c_im)

    xc = ctx
    for l in range(depth):
        last = l == depth - 1
        ctx_out = not last
        colmajor = l % 2 == 1
        pre = g_pre[l][None, :] * (1.0 + scale[l])
        sc_l, sh_l = pre[:bsz, None, :], shift[l, :bsz, None, :]
        sc_c = jnp.broadcast_to(pre[bsz][None, None, :], (bsz, 1, d))
        sh_c = jnp.broadcast_to(shift[l, bsz][None, None, :], (bsz, 1, d))
        qkv, p, lr = _inproj(x, sc_l, sh_l, w_t, l, n=N_MAIN, colmajor=colmajor, tm=1024, tn=512)
        qkvc, pc, lrc = _inproj(xc, sc_c, sh_c, w_t, l, n=N_MAIN if ctx_out else N_STATE,
                                colmajor=False, tm=seq_c, tn=512)

        y_gla, yc_gla = _gla(qkv, p, lr, qkvc, pc, lrc, wg_hi[l], wg_lo[l], bg[l], gla_norm[l][None, :],
                             ctx_out=ctx_out)

        y_s5, yc_s5 = _s5(p, pc, s5_ops, l, s5_d[l][None, :], w_glu_b[l], s5_b_glu[l][None, :], ctx_out=ctx_out)

        v, x1, x2 = _hy_pre(p, hy_conv_w[l], hy_conv_b[l][None, :])
        z = _hy_long_conv(tables, spec, l, 0, v, x1, hy_d[l, 0])
        y_hy = _hy_long_conv(tables, spec, l, 1, z, x2, hy_d[l, 1])

        x_new = _outproj(x, y_gla, y_s5, y_hy, w_out_b, l, g_post[l][None, :], gate[l, :bsz, None, :],
                         colmajor=colmajor, tm=512)
        if ctx_out:
            vc, x1c, x2c = _hy_pre(pc, hy_conv_w[l], hy_conv_b[l][None, :])
            zc = _hyc_long_conv(ctabs, taps_c, l, 0, vc, x1c, hy_d[l, 0])
            yc_hy = _hyc_long_conv(ctabs, taps_c, l, 1, zc, x2c, hy_d[l, 1])
            gate_c = jnp.broadcast_to(gate[l, bsz][None, None, :], (bsz, 1, d))
            xc = _outproj(xc, yc_gla, yc_s5, yc_hy, w_out_b, l, g_post[l][None, :], gate_c,
                          colmajor=False, tm=seq_c)
        x = x_new
    return x
```

```python
import functools
import math

import numpy as np
import jax
import jax.numpy as jnp
from jax import lax
from jax.experimental import pallas as pl
from jax.experimental.pallas import tpu as pltpu

F32 = jnp.float32
BF16 = jnp.bfloat16

D_MODEL = 2048
GRID_W = 64
EPS = 1e-6

GLA_HEADS = 4
GLA_DK = 128
GLA_DV = 256
GLA_TAU = 16.0
GLA_CHUNK = 64
GLA_BLOCK = 8

S5_WIDTH = 512
S5_GROUP = 16
S5_GROUPS = 32
S5_STATE = 64
S5_T = 16
S5_COLS = 640
S5_CB = 128

HY_WIDTH = 512
HY_EMB = 33
HY_BANDS = 16
HY_FFN = 64
HY_MIN_DECAY = math.log(1e-2) / 0.3
HY_MAX_DECAY = math.log(1e-2) / 1.5
FFT_N1 = 64
FFT_N2 = 128
HY_CT = 128
HY_KB = 8
HY_NB = 4
HY_PITCH = 136
HY_PASSES = 1

N_MAIN = 6144
N_STATE = 2560
N_QKV = 2048
N_LR = 32
COL_S5U, COL_S5G = 0, 3
COL_GLA_GATE = 2
COL_HY = 16

VMEM_LIMIT_BYTES = 56 * 1024 * 1024


def _cparams(*sem):
    return pltpu.CompilerParams(dimension_semantics=sem, vmem_limit_bytes=VMEM_LIMIT_BYTES)


def _bf(x):
    return x.astype(BF16)


def _split(x):
    hi = _bf(x)
    return hi, _bf(x - hi.astype(F32))


def _split3(x):
    a = _bf(x)
    r = x - a.astype(F32)
    b = _bf(r)
    return a, b, _bf(r - b.astype(F32))


_NN = (((1,), (0,)), ((), ()))
_NT = (((1,), (1,)), ((), ()))
_TN = (((0,), (0,)), ((), ()))


def _mm(a, b, dims=_NN):
    return lax.dot_general(a, b, dims, preferred_element_type=F32)


def _mm3(a_hi, a_lo, b, dims=_NN):
    b_hi, b_lo = _split(b)
    return _mm(a_hi, b_hi, dims) + _mm(a_hi, b_lo, dims) + _mm(a_lo, b_hi, dims)


def _mod_kernel(s_ref, w_ref, b_ref, o_ref):
    s = s_ref[...]
    s = s * jax.nn.sigmoid(s)
    s_hi, s_lo = _split(s)
    o_ref[0] = _mm3(s_hi, s_lo, w_ref[0]) + b_ref[0]


def _modulation(cvec, w_mod, b_mod):
    depth, d, n = w_mod.shape
    tn = 512
    return pl.pallas_call(
        _mod_kernel,
        grid=(depth, n // tn),
        in_specs=[
            pl.BlockSpec((8, d), lambda l, j: (0, 0)),
            pl.BlockSpec((1, d, tn), lambda l, j: (l, 0, j)),
            pl.BlockSpec((1, 1, tn), lambda l, j: (l, 0, j)),
        ],
        out_specs=pl.BlockSpec((1, 8, tn), lambda l, j: (l, 0, j)),
        out_shape=jax.ShapeDtypeStruct((depth, 8, n), F32),
        compiler_params=_cparams("parallel", "parallel"),
        name="modulation",
    )(cvec, w_mod, b_mod.reshape(depth, 1, n))


def _inproj_kernel(*refs, colmajor, tm, tn):
    if colmajor:
        x_ref, sc_ref, sh_ref, w_ref, wlr_ref, perm_ref, qkv_ref, p_ref, lr_ref, h_scr = refs
    else:
        x_ref, sc_ref, sh_ref, w_ref, wlr_ref, qkv_ref, p_ref, lr_ref, h_scr = refs
    j = pl.program_id(2)

    @pl.when(j == 0)
    def _():
        def norm(xb):
            ms = jnp.mean(xb * xb, axis=-1, keepdims=True)
            return _bf(xb * lax.rsqrt(ms + EPS) * sc_ref[0] + sh_ref[0])

        if colmajor:
            h = norm(x_ref[0].reshape(tm, D_MODEL))
            for c in range(0, D_MODEL, 512):
                h_scr[:, c:c + 512] = _bf(_mm(perm_ref[...], h[:, c:c + 512]))
        else:
            h_scr[...] = norm(x_ref[0])
        lr_ref[0] = _mm(h_scr[...], wlr_ref[...], _NT)

    res = _mm(h_scr[...], w_ref[...], _NT)
    n_qkv = N_QKV // tn

    @pl.when(j < n_qkv)
    def _():
        qkv_ref[0] = _bf(res)

    @pl.when(j >= n_qkv)
    def _():
        p_ref[0] = res


def _inproj(x, scale, shift, w_t, layer, *, n, colmajor, tm, tn):
    bsz, seq, d = x.shape
    n_qkv = N_QKV // tn
    extra_specs, extra_args = [], []
    if colmajor:
        cols = tm // GRID_W
        x_in = x.reshape(bsz, GRID_W, seq // GRID_W, d)
        x_spec = pl.BlockSpec((1, GRID_W, cols, d), lambda b, m, j: (b, 0, m, 0))
        r_new = jnp.arange(tm, dtype=jnp.int32)[:, None]
        r_old = jnp.arange(tm, dtype=jnp.int32)[None, :]
        perm = _bf((r_old == (r_new % GRID_W) * cols + r_new // GRID_W).astype(F32))
        extra_specs, extra_args = [pl.BlockSpec((tm, tm), lambda b, m, j: (0, 0))], [perm]
    else:
        x_in = x
        x_spec = pl.BlockSpec((1, tm, d), lambda b, m, j: (b, m, 0))
    return pl.pallas_call(
        functools.partial(_inproj_kernel, colmajor=colmajor, tm=tm, tn=tn),
        grid=(bsz, seq // tm, n // tn),
        in_specs=[
            x_spec,
            pl.BlockSpec((1, 1, d), lambda b, m, j: (b, 0, 0)),
            pl.BlockSpec((1, 1, d), lambda b, m, j: (b, 0, 0)),
            pl.BlockSpec((pl.Squeezed(), pl.Element(tn), pl.Element(d)),
                         lambda b, m, j: (layer, pl.multiple_of(j * tn + jnp.where(j >= n_qkv, N_LR, 0), N_LR), 0)),
            pl.BlockSpec((pl.Squeezed(), pl.Element(128), pl.Element(d)), lambda b, m, j: (layer, N_QKV, 0)),
        ] + extra_specs,
        out_specs=[
            pl.BlockSpec((1, tm, tn), lambda b, m, j: (b, m, jnp.minimum(j, n_qkv - 1))),
            pl.BlockSpec((1, tm, tn), lambda b, m, j: (b, m, jnp.maximum(j - n_qkv, 0))),
            pl.BlockSpec((1, tm, 128), lambda b, m, j: (b, m, 0)),
        ],
        out_shape=[jax.ShapeDtypeStruct((bsz, seq, N_QKV), BF16),
                   jax.ShapeDtypeStruct((bsz, seq, n - N_QKV), F32),
                   jax.ShapeDtypeStruct((bsz, seq, 128), F32)],
        scratch_shapes=[pltpu.VMEM((tm, d), BF16)],
        compiler_params=_cparams("parallel", "parallel", "arbitrary"),
        name="inproj",
    )(x_in, scale, shift, w_t, w_t, *extra_args)


def _gla_block(q, k, v, lr, st, wg_hi, wg_lo, bg, direction, mask, rowc, nb):
    c = GLA_CHUNK
    n = c * nb
    lr_hi, lr_lo = _split(lr)
    z = _mm(lr_hi, wg_hi) + _mm(lr_hi, wg_lo) + _mm(lr_lo, wg_hi) + bg
    g = (jnp.minimum(z, 0.0) - jnp.log(1.0 + jnp.exp(-jnp.abs(z)))) * (1.0 / GLA_TAU)
    cum = g
    s = 1
    while s < c:
        if direction == 0:
            cum = cum + jnp.where(rowc >= s, pltpu.roll(cum, s, 0), 0.0)
        else:
            cum = cum + jnp.where(rowc < c - s, pltpu.roll(cum, n - s, 0), 0.0)
        s *= 2
    cum3 = cum.reshape(nb, c, GLA_DK)
    tot3 = cum3[:, c - 1:c, :] if direction == 0 else cum3[:, 0:1, :]
    q3 = q.reshape(nb, c, GLA_DK)
    k3 = k.reshape(nb, c, GLA_DK)
    qg3 = _bf(q3 * jnp.exp(cum3))
    kg3 = _bf(k3 * jnp.exp(-cum3))
    kd3 = _bf(k3 * jnp.exp(tot3 - cum3))
    a3 = jnp.exp(tot3)
    vb3 = _bf(v).reshape(nb, c, GLA_DV)
    att = [jnp.where(mask, _mm(qg3[i], kg3[i], _NT), 0.0) for i in range(nb)]
    o_in = [_mm(_bf(att[i]), vb3[i]) for i in range(nb)]
    d_st = [_mm(vb3[i], kd3[i], _TN) for i in range(nb)]
    outs = [None] * nb
    for i in (range(nb) if direction == 0 else range(nb - 1, -1, -1)):
        outs[i] = o_in[i] + _mm(qg3[i], _bf(st), _NT)
        st = st * a3[i] + d_st[i]
    return jnp.concatenate(outs, axis=0), st


def _gla_kernel(*refs, ctx_out, seq, seq_c):
    if ctx_out:
        (q_ref, k_ref, v_ref, gt_ref, lr_ref, qc_ref, kc_ref, vc_ref, gtc_ref, lrc_ref,
         wgh_ref, wgl_ref, bg_ref, gn_ref, y_ref, yc_ref) = refs
    else:
        (q_ref, k_ref, v_ref, gt_ref, lr_ref, qc_ref, kc_ref, vc_ref, lrc_ref,
         wgh_ref, wgl_ref, bg_ref, gn_ref, y_ref) = refs
        gtc_ref = yc_ref = None
    c = GLA_CHUNK
    scale = GLA_DK ** -0.5
    row = lax.broadcasted_iota(jnp.int32, (c, c), 0)
    col = lax.broadcasted_iota(jnp.int32, (c, c), 1)
    masks = (col <= row, col >= row)
    gn = gn_ref[...]

    def run(refs4, n_rows, st, direction, store):
        qr, kr, vr, lrr = refs4
        nb = min(GLA_BLOCK, n_rows // c)
        n = c * nb
        rowc = lax.broadcasted_iota(jnp.int32, (n, GLA_DK), 0) % c
        wg_hi = wgh_ref[direction, 0]
        wg_lo = wgl_ref[direction, 0]
        bg = bg_ref[direction, 0]
        n_blocks = n_rows // n

        def body(i, st):
            bi = i if direction == 0 else n_blocks - 1 - i
            rows = pl.ds(pl.multiple_of(bi * n, n), n)
            q = qr[0, rows, :].astype(F32) * scale
            k = kr[0, rows, :].astype(F32)
            v = vr[0, rows, :].astype(F32)
            o, st = _gla_block(q, k, v, lrr[0, rows, :], st, wg_hi, wg_lo, bg, direction,
                               masks[direction], rowc, nb)
            store(rows, q, k, v, o)
            return st

        return lax.fori_loop(0, n_blocks, body, st)

    def store_fwd(out_ref):
        def f(rows, q, k, v, o):
            if out_ref is not None:
                out_ref[0, rows, :] = o
        return f

    def store_bwd(out_ref, gate_ref):
        def f(rows, q, k, v, o):
            if out_ref is None:
                return
            y = out_ref[0, rows, :] + o - jnp.sum(q * k, axis=-1, keepdims=True) * v
            ms = jnp.mean(y * y, axis=-1, keepdims=True)
            y = y * lax.rsqrt(ms + EPS) * gn
            gt = gate_ref[0, rows, :]
            out_ref[0, rows, :] = y * (gt * jax.nn.sigmoid(gt))
        return f

    lat = (q_ref, k_ref, v_ref, lr_ref)
    ctx = (qc_ref, kc_ref, vc_ref, lrc_ref)
    zero = jnp.zeros((GLA_DV, GLA_DK), F32)
    st = run(ctx, seq_c, zero, 0, store_fwd(yc_ref))
    run(lat, seq, st, 0, store_fwd(y_ref))
    st = run(ctx, seq_c, zero, 1, store_bwd(yc_ref, gtc_ref))
    run(lat, seq, st, 1, store_bwd(y_ref, gt_ref))


def _gla(qkv, p, lr, qkvc, pc, lrc, wg_hi, wg_lo, bg, gnorm, *, ctx_out):
    bsz, seq, _ = qkv.shape
    seq_c = qkvc.shape[1]
    h = GLA_HEADS

    def specs(n):
        return [
            pl.BlockSpec((1, n, 128), lambda b, i: (b, 0, i)),
            pl.BlockSpec((1, n, 128), lambda b, i: (b, 0, 4 + i)),
            pl.BlockSpec((1, n, 256), lambda b, i: (b, 0, 4 + i)),
            pl.BlockSpec((1, n, 256), lambda b, i: (b, 0, COL_GLA_GATE + i)),
            pl.BlockSpec((1, n, 128), lambda b, i: (b, 0, 0)),
        ]

    lat_specs = specs(seq)
    ctx_specs = specs(seq_c)
    lat_args = [qkv, qkv, qkv, p, lr]
    ctx_args = [qkvc, qkvc, qkvc, pc, lrc]
    if not ctx_out:
        del ctx_specs[3], ctx_args[3]
    w_specs = [
        pl.BlockSpec((2, 1, 128, 128), lambda b, i: (0, i, 0, 0)),
        pl.BlockSpec((2, 1, 128, 128), lambda b, i: (0, i, 0, 0)),
        pl.BlockSpec((2, 1, 1, 128), lambda b, i: (0, i, 0, 0)),
        pl.BlockSpec((1, GLA_DV), lambda b, i: (0, 0)),
    ]
    out_specs = [pl.BlockSpec((1, seq, 256), lambda b, i: (b, 0, i))]
    out_shape = [jax.ShapeDtypeStruct((bsz, seq, h * GLA_DV), F32)]
    if ctx_out:
        out_specs.append(pl.BlockSpec((1, seq_c, 256), lambda b, i: (b, 0, i)))
        out_shape.append(jax.ShapeDtypeStruct((bsz, seq_c, h * GLA_DV), F32))
    outs = pl.pallas_call(
        functools.partial(_gla_kernel, ctx_out=ctx_out, seq=seq, seq_c=seq_c),
        grid=(bsz, h),
        in_specs=lat_specs + ctx_specs + w_specs,
        out_specs=out_specs,
        out_shape=out_shape,
        compiler_params=_cparams("parallel", "parallel"),
        name="gla",
    )(*lat_args, *ctx_args, wg_hi, wg_lo, bg, gnorm)
    return (outs[0], outs[1]) if ctx_out else (outs[0], None)


def _mm3f(a, b, dims=_NN):
    a_hi, a_lo = _split(a)
    return _mm3(a_hi, a_lo, b, dims)


def _s5_ops_kernel(a_ref, b_ref, lr_ref, li_ref, btr_ref, bti_ref, cr_ref, ci_ref,
                   kt_ref, bpr_ref, bpi_ref, cp_ref, ltr_ref, lti_ref):
    t_len, h_n, p_n = S5_T, S5_GROUP, S5_STATE
    rows = t_len * h_n
    lane = lax.broadcasted_iota(jnp.int32, (rows, 2 * p_n), 1)
    own = (lane < p_n, lane >= p_n)
    lane_blk = lax.broadcasted_iota(jnp.int32, (rows, rows), 1) // h_n
    kk = lax.broadcasted_iota(jnp.int32, (24, 2 * p_n), 0).astype(F32)
    for d in range(2):
        a, b = a_ref[0, d, 0], b_ref[0, d, 0]
        lr, li = lr_ref[0, d, 0], li_ref[0, d, 0]
        mag = jnp.exp(kk * a)
        pwr, pwi = mag * jnp.cos(kk * b), mag * jnp.sin(kk * b)

        def times_pw(k, mr, mi):
            pr, pi = pwr[k:k + 1], pwi[k:k + 1]
            return pr * mr - pi * mi, pr * mi + pi * mr

        x, y = pwr[1:2] - 1.0, pwi[1:2]
        den = lr * lr + li * li
        cfr, cfi = (x * lr + y * li) / den, (y * lr - x * li) / den
        btr, bti = btr_ref[0, d, 0], bti_ref[0, d, 0]
        bbr, bbi = cfr * btr - cfi * bti, cfr * bti + cfi * btr
        cr, ci = cr_ref[0, d, 0], ci_ref[0, d, 0]

        def stack(power_of_t, mr, mi):
            parts = [times_pw(power_of_t(t), mr, mi) for t in range(t_len)]
            return (jnp.concatenate([p[0] for p in parts], axis=0),
                    jnp.concatenate([p[1] for p in parts], axis=0))

        bpr, bpi = stack((lambda t: t_len - 1 - t) if d == 0 else (lambda t: t), bbr, bbi)
        for src, dst in ((bpr, bpr_ref), (bpi, bpi_ref)):
            z = jnp.concatenate([jnp.where(own[0], src, 0.0), jnp.where(own[1], src, 0.0)], axis=0)
            dst[0, d, 0] = _bf(z.T)
        cpr, cpi = stack((lambda t: t + 1) if d == 0 else (lambda t: t_len - t), cr, ci)
        clr, cli = stack((lambda j: j) if d == 0 else (lambda j: t_len - 1 - j), cr, ci)
        bbr_t = jnp.concatenate([bbr] * t_len, axis=0)
        bbi_t = jnp.concatenate([bbi] * t_len, axis=0)
        for e in range(2):
            cp_ref[0, d, e] = _bf(jnp.concatenate(
                [jnp.where(own[e], cpr, 0.0), jnp.where(own[e], -cpi, 0.0)], axis=1))
            kc = (_mm3f(jnp.where(own[e], clr, 0.0), bbr_t, _NT)
                  - _mm3f(jnp.where(own[e], cli, 0.0), bbi_t, _NT))
            acc = jnp.zeros((rows, rows), F32)
            for t in range(t_len):
                if d == 0:
                    pieces = [jnp.zeros((h_n * t, rows), F32), kc[:rows - h_n * t]]
                else:
                    s = h_n * (t_len - 1 - t)
                    pieces = [kc[s:], jnp.zeros((s, rows), F32)]
                shifted = jnp.concatenate([p for p in pieces if p.shape[0]], axis=0)
                acc = jnp.where(lane_blk == t, shifted, acc)
            kt_ref[0, d, e] = _bf(acc)
        ltr_ref[0, d, 0] = pwr[t_len:t_len + 1]
        lti_ref[0, d, 0] = pwi[t_len:t_len + 1]


def _s5_operators(lam_re, lam_im, log_step, b_re, b_im, c_re, c_im):
    depth = lam_re.shape[0]
    g_n, p_n, h_n = S5_GROUPS, S5_STATE, S5_GROUP
    pairs, rows = g_n // 2, S5_T * h_n
    dt = jnp.exp(log_step)[..., None]
    vec = lambda v: v.reshape(depth, 2, pairs, 1, 2 * p_n)
    bt = lambda m: m.reshape(depth, 2, pairs, 2, p_n, h_n).transpose(0, 1, 2, 5, 3, 4).reshape(depth, 2, pairs, h_n, 2 * p_n)
    ct = lambda m: m.reshape(depth, 2, pairs, 2, h_n, p_n).transpose(0, 1, 2, 4, 3, 5).reshape(depth, 2, pairs, h_n, 2 * p_n)
    vspec = pl.BlockSpec((1, 2, 1, 1, 2 * p_n), lambda l, q: (l, 0, q, 0, 0))
    mspec = pl.BlockSpec((1, 2, 1, h_n, 2 * p_n), lambda l, q: (l, 0, q, 0, 0))
    sq_spec = pl.BlockSpec((1, 2, 2, rows, rows), lambda l, q: (l, 0, q, 0, 0))
    bp_spec = pl.BlockSpec((1, 2, 1, 2 * p_n, 2 * rows), lambda l, q: (l, 0, q, 0, 0))
    sq = jax.ShapeDtypeStruct((depth, 2, g_n, rows, rows), BF16)
    bp = jax.ShapeDtypeStruct((depth, 2, pairs, 2 * p_n, 2 * rows), BF16)
    lt = jax.ShapeDtypeStruct((depth, 2, pairs, 1, 2 * p_n), F32)
    ktoep, bpr, bpi, cpad, ltr, lti = pl.pallas_call(
        _s5_ops_kernel,
        grid=(depth, pairs),
        in_specs=[vspec] * 4 + [mspec] * 4,
        out_specs=[sq_spec, bp_spec, bp_spec, sq_spec, vspec, vspec],
        out_shape=[sq, bp, bp, sq, lt, lt],
        compiler_params=_cparams("parallel", "parallel"),
        name="s5_operators",
    )(vec(lam_re * dt), vec(lam_im * dt), vec(lam_re), vec(lam_im), bt(b_re), bt(b_im), ct(c_re), ct(c_im))
    flat = lambda v: v.reshape(depth, 2, 1, g_n * p_n)
    return ktoep, bpr, bpi, cpad, flat(ltr), flat(lti)


def _s5_in_kernel(u_ref, uc_ref, o_ref):
    j = pl.program_id(0)
    last = pl.num_programs(0) - 1

    def emit(slab_of_t):
        for t in range(S5_T):
            o_ref[:, t * S5_GROUP:(t + 1) * S5_GROUP, :] = _bf(
                slab_of_t(t).T.reshape(S5_GROUPS, S5_GROUP, S5_CB))

    @pl.when(j < last)
    def _():
        emit(lambda t: u_ref[0, :, t, :])

    @pl.when(j == last)
    def _():
        pad = jnp.zeros((S5_CB - 2 * uc_ref.shape[1], S5_WIDTH), F32)
        emit(lambda t: jnp.concatenate([uc_ref[0, :, t, :], uc_ref[1, :, t, :], pad], axis=0))


def _s5_lat_index(n_half):
    def index(j):
        jj = jnp.minimum(j, 2 * n_half - 1)
        return jj // n_half, jj % n_half
    return index


def _s5_relayout_in(p, pc):
    bsz, seq, n = p.shape
    seq_c, n_c = pc.shape[1], pc.shape[2]
    n_lat, n_ctx = seq // S5_T, seq_c // S5_T
    n_half = n_lat // S5_CB
    idx = _s5_lat_index(n_half)
    return pl.pallas_call(
        _s5_in_kernel,
        grid=(bsz * n_half + 1,),
        in_specs=[
            pl.BlockSpec((1, S5_CB, S5_T, 512), lambda j: (*idx(j), 0, COL_S5U)),
            pl.BlockSpec((bsz, n_ctx, S5_T, 512), lambda j: (0, 0, 0, COL_S5U)),
        ],
        out_specs=pl.BlockSpec((S5_GROUPS, S5_T * S5_GROUP, S5_CB), lambda j: (0, 0, j)),
        out_shape=jax.ShapeDtypeStruct((S5_GROUPS, S5_T * S5_GROUP, S5_COLS), BF16),
        compiler_params=_cparams("arbitrary"),
        name="s5_relayout_in",
    )(p.reshape(bsz, n_lat, S5_T, n), pc.reshape(bsz, n_ctx, S5_T, n_c))


def _s5_state_in_kernel(ug_ref, bre_ref, bim_ref, vre_ref, vim_ref):
    u = ug_ref[...].reshape(2 * S5_T * S5_GROUP, S5_COLS)
    for direction in range(2):
        vre_ref[direction] = _mm(bre_ref[0, direction, 0], u)
        vim_ref[direction] = _mm(bim_ref[0, direction, 0], u)


def _s5_state_in(ug, b_pair_re, b_pair_im, layer):
    pairs = S5_GROUPS // 2
    rows = S5_T * S5_GROUP
    lanes = S5_GROUPS * S5_STATE
    out = jax.ShapeDtypeStruct((2, lanes, S5_COLS), F32)
    return pl.pallas_call(
        _s5_state_in_kernel,
        grid=(pairs,),
        in_specs=[
            pl.BlockSpec((2, rows, S5_COLS), lambda q: (q, 0, 0)),
            pl.BlockSpec((1, 2, 1, 128, 2 * rows), lambda q: (layer, 0, q, 0, 0)),
            pl.BlockSpec((1, 2, 1, 128, 2 * rows), lambda q: (layer, 0, q, 0, 0)),
        ],
        out_specs=[pl.BlockSpec((2, 128, S5_COLS), lambda q: (0, q, 0))] * 2,
        out_shape=[out, out],
        compiler_params=_cparams("parallel"),
        name="s5_state_in",
    )(ug, b_pair_re, b_pair_im)


def _s5_scan_kernel(vre_ref, vim_ref, lre_ref, lim_ref, xre_ref, xim_ref, tre_scr, tim_scr, *, n_lat, n_ctx):
    lanes = lre_ref.shape[-1]
    for d in range(2):
        tre_scr[d] = vre_ref[d].T
        tim_scr[d] = vim_ref[d].T
    a_re = [lre_ref[0, d] for d in range(2)]
    a_im = [lim_ref[0, d] for d in range(2)]

    def step(d, row, xr, xi):
        sl = pl.ds(row, 1)
        xre_ref[d, sl, :] = xr
        xim_ref[d, sl, :] = xi
        return (a_re[d] * xr - a_im[d] * xi + tre_scr[d, sl, :],
                a_re[d] * xi + a_im[d] * xr + tim_scr[d, sl, :])

    def run(bases, n, carry):
        def body(i, carry):
            out = []
            for d in range(2):
                j = i if d == 0 else n - 1 - i
                for b in range(2):
                    xr, xi = carry[2 * d + b]
                    out.append(step(d, bases[b] + j, xr, xi))
            return tuple(out)
        return lax.fori_loop(0, n, body, carry)

    zero = jnp.zeros((1, lanes), F32)
    carry = run((2 * n_lat, 2 * n_lat + n_ctx), n_ctx, ((zero, zero),) * 4)
    run((0, n_lat), n_lat, carry)
    pad0 = 2 * n_lat + 2 * n_ctx
    for d in range(2):
        xre_ref[d, pad0:, :] = jnp.zeros((S5_COLS - pad0, lanes), F32)
        xim_ref[d, pad0:, :] = jnp.zeros((S5_COLS - pad0, lanes), F32)


def _s5_scan(vre, vim, lam_re, lam_im, layer, *, n_lat, n_ctx):
    lanes = vre.shape[1]
    tl = 512
    vspec = pl.BlockSpec((2, tl, S5_COLS), lambda j: (0, j, 0))
    xspec = pl.BlockSpec((2, S5_COLS, tl), lambda j: (0, 0, j))
    lspec = pl.BlockSpec((1, 2, 1, tl), lambda j: (layer, 0, 0, j))
    out = jax.ShapeDtypeStruct((2, S5_COLS, lanes), F32)
    return pl.pallas_call(
        functools.partial(_s5_scan_kernel, n_lat=n_lat, n_ctx=n_ctx),
        grid=(lanes // tl,),
        in_specs=[vspec, vspec, lspec, lspec],
        out_specs=[xspec, xspec],
        out_shape=[out, out],
        scratch_shapes=[pltpu.VMEM((2, S5_COLS, tl), F32), pltpu.VMEM((2, S5_COLS, tl), F32)],
        compiler_params=_cparams("parallel"),
        name="s5_scan",
    )(vre, vim, lam_re, lam_im)


def _s5_readout_kernel(ug_ref, kt_ref, cp_ref, xre_ref, xim_ref, y_ref):
    for e in range(2):
        u = ug_ref[e]
        acc = None
        for direction in range(2):
            xcat = _bf(jnp.concatenate([xre_ref[direction], xim_ref[direction]], axis=1))
            term = _mm(kt_ref[0, direction, e], u) + _mm(cp_ref[0, direction, e], xcat, _NT)
            acc = term if acc is None else acc + term
        y_ref[e] = acc


def _s5_readout(ug, ktoep, cpad, xre, xim, layer):
    pairs = S5_GROUPS // 2
    rows = S5_T * S5_GROUP
    return pl.pallas_call(
        _s5_readout_kernel,
        grid=(pairs,),
        in_specs=[
            pl.BlockSpec((2, rows, S5_COLS), lambda q: (q, 0, 0)),
            pl.BlockSpec((1, 2, 2, rows, rows), lambda q: (layer, 0, q, 0, 0)),
            pl.BlockSpec((1, 2, 2, rows, rows), lambda q: (layer, 0, q, 0, 0)),
            pl.BlockSpec((2, S5_COLS, 128), lambda q: (0, 0, q)),
            pl.BlockSpec((2, S5_COLS, 128), lambda q: (0, 0, q)),
        ],
        out_specs=pl.BlockSpec((2, rows, S5_COLS), lambda q: (q, 0, 0)),
        out_shape=jax.ShapeDtypeStruct((S5_GROUPS, rows, S5_COLS), F32),
        compiler_params=_cparams("parallel"),
        name="s5_readout",
    )(ug, ktoep, cpad, xre, xim)


def _s5_out_kernel(*refs, ctx_out):
    if ctx_out:
        yg_ref, u_ref, g_ref, uc_ref, gc_ref, d_ref, w_ref, b_ref, o_ref, oc_ref = refs
    else:
        yg_ref, u_ref, g_ref, d_ref, w_ref, b_ref, o_ref = refs
    d = d_ref[...]
    w = w_ref[...]
    bias = b_ref[...]

    def finish(yy, u, gate):
        yy = yy + d * u
        yg = jax.nn.gelu(yy)
        out = yg * jax.nn.sigmoid(_mm(_bf(yg), w) + bias)
        return out * (gate * jax.nn.sigmoid(gate))

    def y_of(t):
        return yg_ref[:, t * S5_GROUP:(t + 1) * S5_GROUP, :].reshape(S5_WIDTH, S5_CB).T

    def lat():
        for t in range(S5_T):
            o_ref[0, :, t, :] = finish(y_of(t), u_ref[0, :, t, :], g_ref[0, :, t, :])

    if not ctx_out:
        lat()
        return
    j = pl.program_id(0)
    last = pl.num_programs(0) - 1
    pl.when(j < last)(lat)

    @pl.when(j == last)
    def _():
        n_ctx = uc_ref.shape[1]
        for t in range(S5_T):
            y = y_of(t)
            for b in range(2):
                oc_ref[b, :, t, :] = finish(y[b * n_ctx:(b + 1) * n_ctx], uc_ref[b, :, t, :], gc_ref[b, :, t, :])


def _s5_relayout_out(yg, p, pc, d_skip, w_glu, b_glu, *, ctx_out):
    bsz, seq, n = p.shape
    seq_c, n_c = pc.shape[1], pc.shape[2]
    n_lat, n_ctx = seq // S5_T, seq_c // S5_T
    n_half = n_lat // S5_CB
    idx = _s5_lat_index(n_half)
    pv = p.reshape(bsz, n_lat, S5_T, n)
    lat_block = (1, S5_CB, S5_T, 512)
    in_specs = [
        pl.BlockSpec((S5_GROUPS, S5_T * S5_GROUP, S5_CB), lambda j: (0, 0, j)),
        pl.BlockSpec(lat_block, lambda j: (*idx(j), 0, COL_S5U)),
        pl.BlockSpec(lat_block, lambda j: (*idx(j), 0, COL_S5G)),
    ]
    args = [yg, pv, pv]
    out_specs = [pl.BlockSpec(lat_block, lambda j: (*idx(j), 0, 0))]
    out_shape = [jax.ShapeDtypeStruct((bsz, n_lat, S5_T, S5_WIDTH), F32)]
    if ctx_out:
        pcv = pc.reshape(bsz, n_ctx, S5_T, n_c)
        ctx_block = (bsz, n_ctx, S5_T, 512)
        in_specs += [
            pl.BlockSpec(ctx_block, lambda j: (0, 0, 0, COL_S5U)),
            pl.BlockSpec(ctx_block, lambda j: (0, 0, 0, COL_S5G)),
        ]
        args += [pcv, pcv]
        out_specs.append(pl.BlockSpec(ctx_block, lambda j: (0, 0, 0, 0)))
        out_shape.append(jax.ShapeDtypeStruct((bsz, n_ctx, S5_T, S5_WIDTH), F32))
    in_specs += [
        pl.BlockSpec((1, S5_WIDTH), lambda j: (0, 0)),
        pl.BlockSpec((S5_WIDTH, S5_WIDTH), lambda j: (0, 0)),
        pl.BlockSpec((1, S5_WIDTH), lambda j: (0, 0)),
    ]
    args += [d_skip, w_glu, b_glu]
    outs = pl.pallas_call(
        functools.partial(_s5_out_kernel, ctx_out=ctx_out),
        grid=(bsz * n_half + (1 if ctx_out else 0),),
        in_specs=in_specs,
        out_specs=out_specs,
        out_shape=out_shape,
        compiler_params=_cparams("arbitrary"),
        name="s5_relayout_out",
    )(*args)
    y = outs[0].reshape(bsz, seq, S5_WIDTH)
    yc = outs[1].reshape(bsz, seq_c, S5_WIDTH) if ctx_out else None
    return y, yc


def _s5(p, pc, ops, layer, d_skip, w_glu, b_glu, *, ctx_out):
    ktoep, b_pair_re, b_pair_im, cpad, lam_re, lam_im = ops
    ug = _s5_relayout_in(p, pc)
    vre, vim = _s5_state_in(ug, b_pair_re, b_pair_im, layer)
    xre, xim = _s5_scan(vre, vim, lam_re, lam_im, layer, n_lat=p.shape[1] // S5_T, n_ctx=pc.shape[1] // S5_T)
    yg = _s5_readout(ug, ktoep, cpad, xre, xim, layer)
    return _s5_relayout_out(yg, p, pc, d_skip, w_glu, b_glu, ctx_out=ctx_out)


def _dft_tables():
    n1, n2 = FFT_N1, FFT_N2
    n = n1 * n2
    a = jnp.arange(n1, dtype=jnp.int32)
    ph1 = (a[:, None] * a[None, :]) % n1
    ang1 = ph1.astype(F32) * (2.0 * math.pi / n1)
    c1, s1 = jnp.cos(ang1), -jnp.sin(ang1)
    h = n1 // 2
    m1 = jnp.concatenate([jnp.concatenate([c1[:, :h], -s1[:, :h]], 1),
                          jnp.concatenate([s1[:, :h], c1[:, :h]], 1)], 0)
    m1_real = jnp.concatenate([c1, s1], 0)
    m3 = m1.T / n
    b = jnp.arange(n2, dtype=jnp.int32)
    ang2 = ((b[:, None] * b[None, :]) % n2).astype(F32) * (2.0 * math.pi / n2)
    fr, fi = jnp.cos(ang2), -jnp.sin(ang2)
    f2 = jnp.concatenate([jnp.concatenate([fr, -fi], 1), jnp.concatenate([fi, fr], 1)], 0)
    angt = ((a[:, None] * b[None, :]) % n).astype(F32) * (2.0 * math.pi / n)
    tw = jnp.stack([jnp.cos(angt), -jnp.sin(angt)], axis=1)
    tw = jnp.broadcast_to(tw[..., None], (n1, 2, n2, HY_CT))
    return tuple(_split(t) for t in (m1, m1_real, m3, f2, f2.T)) + (tw,)


def _filter_features(length):
    t = jnp.linspace(0.0, 1.0, length, dtype=F32)[:, None]
    ang = (2.0 * math.pi / length) * jnp.arange(length, dtype=F32)[:, None]
    bands = jnp.linspace(1e-4, HY_BANDS - 1, HY_BANDS, dtype=F32)[None, :]
    feats = jnp.concatenate([t, jnp.cos(bands * ang), -jnp.sin(bands * ang)], axis=-1)
    feats = jnp.pad(feats, ((0, 0), (0, 128 - HY_EMB)))
    rev = jnp.roll(feats[::-1], 1, axis=0)
    return jnp.stack([feats, rev])


def _filter_kernel(f_ref, w1_ref, b1_ref, f1_ref, w2_ref, b2_ref, f2_ref, w3_ref, dl_ref, o_ref):
    half = pl.program_id(1)
    tile = pl.program_id(2)
    x = f_ref[0]
    x_hi, x_lo = _split(x)
    h = jnp.sin(f1_ref[0] * (_mm3(x_hi, x_lo, w1_ref[0]) + b1_ref[0]))
    h_hi, h_lo = _split(h)
    h = jnp.sin(f2_ref[0] * (_mm3(h_hi, h_lo, w2_ref[0]) + b2_ref[0]))
    y = _mm(_bf(h), _bf(w3_ref[0, 0])) * jnp.exp(-x[:, 0:1] * dl_ref[...])
    rows = lax.broadcasted_iota(jnp.int32, y.shape, 0)
    drop = jnp.logical_and(jnp.logical_and(half == 1, tile == 0), rows == 0)
    o_ref[0] = jnp.where(drop, 0.0, y)


def _filter_taps(feats, w1, b1, f1, w2, b2, f2, w3, deltas, tmf):
    depth = w1.shape[0]
    length = feats.shape[1]
    nt = length // tmf
    wide = w3.shape[-1]
    vec = pl.BlockSpec((1, 1, 128), lambda l, s, i: (l, 0, 0))
    sq = pl.BlockSpec((1, 128, 128), lambda l, s, i: (l, 0, 0))
    return pl.pallas_call(
        _filter_kernel,
        grid=(depth, 2, nt),
        in_specs=[
            pl.BlockSpec((1, tmf, 128), lambda l, s, i: (s, i, 0)),
            sq, vec, vec, sq, vec, vec,
            pl.BlockSpec((1, 1, 128, wide), lambda l, s, i: (l, s, 0, 0)),
            pl.BlockSpec((1, wide), lambda l, s, i: (0, 0)),
        ],
        out_specs=pl.BlockSpec((1, tmf, wide), lambda l, s, i: (l, s * nt + i, 0)),
        out_shape=jax.ShapeDtypeStruct((depth, 2 * length, wide), F32),
        compiler_params=_cparams("parallel", "parallel", "parallel"),
        name="hyena_filter",
    )(feats, w1, b1, f1, w2, b2, f2, w3, deltas)


def _parts(pair):
    return tuple(pair) if HY_PASSES == 3 else tuple(pair[:1])


def _mmp(m, b):
    return _mm(m[0], _bf(b)) if len(m) == 1 else _mm3(m[0], m[1], b)


def _blk_rows(n2):
    return pl.ds(pl.multiple_of(n2 * HY_PITCH, 8), 2 * FFT_N1)


def _pad_rows(n2, n):
    return pl.ds(n2, n, stride=HY_PITCH)


def _pad_copy(src, dst, n):
    for n1 in range(n):
        dst[n1 * HY_PITCH:n1 * HY_PITCH + FFT_N2, :] = src[n1 * FFT_N2:(n1 + 1) * FFT_N2, :]


def _dft_stage1(read_rows, m, a_scr):
    ct = a_scr.shape[-1]

    def body(it, carry):
        n2 = it * HY_NB
        a = _mmp(m, jnp.concatenate([read_rows(n2 + j) for j in range(HY_NB)], axis=1))
        for j in range(HY_NB):
            a_scr[_blk_rows(n2 + j), :] = a[:, j * ct:(j + 1) * ct]
        return carry

    lax.fori_loop(0, FFT_N2 // HY_NB, body, 0)


def _stage2_rows(k1):
    return (pl.ds(k1, FFT_N2, stride=HY_PITCH), pl.ds(k1 + FFT_N1, FFT_N2, stride=HY_PITCH))


def _stage2_load(a_scr, tw_ref, kb):
    rows = [_stage2_rows(kb * HY_KB + i) for i in range(HY_KB)]
    vals = []
    for i, (re, im) in enumerate(rows):
        ar, ai = a_scr[re, :], a_scr[im, :]
        tr, ti = tw_ref[kb * HY_KB + i, 0], tw_ref[kb * HY_KB + i, 1]
        vals.append(jnp.concatenate([ar * tr - ai * ti, ar * ti + ai * tr], axis=0))
    return rows, vals


def _spectrum_kernel(*refs):
    np_ = len(_parts((0, 0)))
    m_refs, f_refs, (tw_ref, t_ref, o_ref, a_scr, pad_scr) = refs[:np_], refs[np_:2 * np_], refs[2 * np_:]
    kb = pl.program_id(2)
    ct = t_ref.shape[-1]

    @pl.when(kb == 0)
    def _():
        _pad_copy(t_ref.at[0], pad_scr, FFT_N1)
        _dft_stage1(lambda n2: pad_scr[_pad_rows(n2, FFT_N1), :], [r[...] for r in m_refs], a_scr)

    _, a = _stage2_load(a_scr, tw_ref, kb)
    f2 = [r[...] for r in f_refs]
    for i in range(HY_KB):
        o_ref[0, i] = _bf(_mmp(f2, a[i])).reshape(2, FFT_N2, ct)


def _filter_spectrum(tables, taps):
    m, f2, tw = _parts(tables[1]), _parts(tables[3]), tables[5]
    depth, n, ch = taps.shape
    ct = HY_CT
    const = lambda t: pl.BlockSpec(t.shape, lambda l, c, k: (0,) * t.ndim, pipeline_mode=pl.Buffered(1))
    return pl.pallas_call(
        _spectrum_kernel,
        grid=(depth, ch // ct, FFT_N1 // HY_KB),
        in_specs=[const(t) for t in (*m, *f2, tw)] + [pl.BlockSpec((1, n, ct), lambda l, c, k: (l, 0, c))],
        out_specs=pl.BlockSpec((1, HY_KB, 2, FFT_N2, ct), lambda l, c, k: (l, k, 0, 0, c)),
        out_shape=jax.ShapeDtypeStruct((depth, FFT_N1, 2, FFT_N2, ch), BF16),
        scratch_shapes=[pltpu.VMEM((HY_PITCH * FFT_N2, ct), F32), pltpu.VMEM((HY_PITCH * FFT_N1, ct), F32)],
        compiler_params=_cparams("parallel", "parallel", "arbitrary"),
        name="hyena_spectrum",
    )(*m, *f2, tw, taps)


def _hy_pre_kernel(pv_ref, p1_ref, p2_ref, pg_ref, wv_ref, w1_ref, w2_ref, bv_ref, b1_ref, b2_ref,
                   v_ref, x1_ref, x2_ref):
    n = pv_ref.shape[1]
    rows = lax.broadcasted_iota(jnp.int32, (n, 128), 0)

    def conv(p, w_ref, b_ref):
        up = jnp.where(rows == 0, 0.0, pltpu.roll(p, 1, 0))
        dn = jnp.where(rows == n - 1, 0.0, pltpu.roll(p, n - 1, 0))
        return up * w_ref[0:1, :] + p * w_ref[1:2, :] + dn * w_ref[2:3, :] + b_ref[...]

    v_ref[0] = conv(pv_ref[0], wv_ref, bv_ref)
    x1_ref[0] = conv(p1_ref[0], w1_ref, b1_ref)
    gate = pg_ref[0]
    x2_ref[0] = conv(p2_ref[0], w2_ref, b2_ref) * (gate * jax.nn.sigmoid(gate))


def _hy_pre(p, conv_w, conv_b):
    bsz, seq, _ = p.shape
    def pspec(off):
        return pl.BlockSpec((1, seq, 128), lambda b, j: (b, 0, off + j))
    def wspec(off):
        return pl.BlockSpec((3, 128), lambda b, j: (0, off + j))
    def bspec(off):
        return pl.BlockSpec((1, 128), lambda b, j: (0, off + j))
    out = jax.ShapeDtypeStruct((bsz, seq, HY_WIDTH), F32)
    ospec = pl.BlockSpec((1, seq, 128), lambda b, j: (b, 0, j))
    return pl.pallas_call(
        _hy_pre_kernel,
        grid=(bsz, HY_WIDTH // 128),
        in_specs=[pspec(COL_HY), pspec(COL_HY + 4), pspec(COL_HY + 8), pspec(COL_HY + 12),
                  wspec(0), wspec(4), wspec(8), bspec(0), bspec(4), bspec(8)],
        out_specs=[ospec, ospec, ospec],
        out_shape=[out, out, out],
        compiler_params=_cparams("parallel", "parallel"),
        name="hyena_pre",
    )(p, p, p, p, conv_w, conv_w, conv_w, conv_b, conv_b, conv_b)


def _hy_conv_kernel(*refs):
    np_ = len(_parts((0, 0)))
    m1_refs, m3_refs, f_refs, ft_refs = (refs[i * np_:(i + 1) * np_] for i in range(4))
    tw_ref, u_ref, gate_ref, h_ref, d_ref, o_ref, a_scr, pad_scr = refs[4 * np_:]
    kb = pl.program_id(1)
    ct = u_ref.shape[-1]
    half = FFT_N1 // 2

    @pl.when(kb == 0)
    def _():
        for b in range(2):
            _pad_copy(u_ref.at[b], pad_scr.at[b], half)
        _dft_stage1(lambda n2: jnp.concatenate([pad_scr[0, _pad_rows(n2, half), :],
                                                pad_scr[1, _pad_rows(n2, half), :]], axis=0),
                    [r[...] for r in m1_refs], a_scr)

    srows, a = _stage2_load(a_scr, tw_ref, kb)
    f2, f2t = [r[...] for r in f_refs], [r[...] for r in ft_refs]
    z = [_mmp(f2, a[i]) for i in range(HY_KB)]
    z2 = []
    for i in range(HY_KB):
        zr, zi = z[i][:FFT_N2], z[i][FFT_N2:]
        hr, hi = h_ref[0, i, 0].astype(F32), h_ref[0, i, 1].astype(F32)
        z2.append(jnp.concatenate([zr * hr - zi * hi, zr * hi + zi * hr], axis=0))
    a2 = [_mmp(f2t, z2[i]) for i in range(HY_KB)]
    for i in range(HY_KB):
        cr, ci = a2[i][:FFT_N2], a2[i][FFT_N2:]
        tr, ti = tw_ref[kb * HY_KB + i, 0], tw_ref[kb * HY_KB + i, 1]
        a_scr[srows[i][0], :] = cr * tr + ci * ti
        a_scr[srows[i][1], :] = ci * tr - cr * ti

    @pl.when(kb == pl.num_programs(1) - 1)
    def _():
        m3 = [r[...] for r in m3_refs]
        d = d_ref[...]

        def body(it, carry):
            n2 = it * HY_NB
            y = _mmp(m3, jnp.concatenate([a_scr[_blk_rows(n2 + j), :] for j in range(HY_NB)], axis=1))
            for j in range(HY_NB):
                for b in range(2):
                    pad_scr[b, _pad_rows(n2 + j, half), :] = y[b * half:(b + 1) * half, j * ct:(j + 1) * ct]
            return carry

        lax.fori_loop(0, FFT_N2 // HY_NB, body, 0)
        for b in range(2):
            for n1 in range(half):
                r = slice(n1 * FFT_N2, (n1 + 1) * FFT_N2)
                conv = pad_scr[b, n1 * HY_PITCH:n1 * HY_PITCH + FFT_N2, :]
                o_ref[b, r, :] = gate_ref[b, r, :] * (conv + d * u_ref[b, r, :])


def _hy_long_conv(tables, spec, layer, order, u, gate, d):
    m1, _, m3, f2, f2t = (_parts(t) for t in tables[:5])
    bsz, seq, ch = u.shape
    ct = HY_CT
    const = lambda t: pl.BlockSpec(t.shape, lambda c, k: (0,) * t.ndim, pipeline_mode=pl.Buffered(1))
    uspec = pl.BlockSpec((bsz, seq, ct), lambda c, k: (0, 0, c))
    n_ct = ch // ct
    mats = (*m1, *m3, *f2, *f2t, tables[5])
    return pl.pallas_call(
        _hy_conv_kernel,
        grid=(n_ct, FFT_N1 // HY_KB),
        in_specs=[const(t) for t in mats] + [
            uspec, uspec,
            pl.BlockSpec((1, HY_KB, 2, FFT_N2, ct), lambda c, k: (layer, k, 0, 0, order * n_ct + c)),
            pl.BlockSpec((1, ct), lambda c, k: (0, c))],
        out_specs=uspec,
        out_shape=jax.ShapeDtypeStruct(u.shape, F32),
        scratch_shapes=[pltpu.VMEM((HY_PITCH * FFT_N2, ct), F32),
                        pltpu.VMEM((bsz, HY_PITCH * FFT_N1 // 2, ct), F32)],
        compiler_params=_cparams("parallel", "arbitrary"),
        name="hyena_conv",
    )(*mats, u, gate, spec, d.reshape(1, ch))


def _ctx_dft_tables(length):
    n = 2 * length
    k = jnp.arange(n, dtype=jnp.int32)
    ang = ((k[:, None] * k[None, :]) % n).astype(F32) * (2.0 * math.pi / n)
    fr, fi = jnp.cos(ang), -jnp.sin(ang)
    fwd = jnp.concatenate([jnp.concatenate([fr[:, :length], -fi[:, :length]], 1),
                           jnp.concatenate([fi[:, :length], fr[:, :length]], 1)], 0)
    real = jnp.concatenate([fr, fi], 0)
    inv = fwd.T / n
    return tuple(_split(t) for t in (fwd, real, inv))


def _hyc_kernel(fh_ref, fl_ref, rh_ref, rl_ref, ih_ref, il_ref, taps_ref, u_ref, g_ref, d_ref, o_ref):
    n = u_ref.shape[1]
    x = jnp.concatenate([u_ref[0], u_ref[1]], axis=0)
    z = _mm3(fh_ref[...], fl_ref[...], x)
    h = _mm3(rh_ref[...], rl_ref[...], taps_ref[0])
    m = 2 * n
    zr, zi, hr, hi = z[:m], z[m:], h[:m], h[m:]
    z2 = jnp.concatenate([zr * hr - zi * hi, zr * hi + zi * hr], axis=0)
    y = _mm3(ih_ref[...], il_ref[...], z2)
    for b in range(2):
        o_ref[b] = g_ref[b] * (y[b * n:(b + 1) * n] + d_ref[...] * u_ref[b])


def _hyc_long_conv(ctabs, taps, layer, order, u, gate, d):
    (f_hi, f_lo), (r_hi, r_lo), (i_hi, i_lo) = ctabs
    bsz, n, ch = u.shape
    full = lambda t: pl.BlockSpec(t.shape, lambda i: (0,) * t.ndim)
    uspec = pl.BlockSpec((bsz, n, ch), lambda i: (0, 0, 0))
    return pl.pallas_call(
        _hyc_kernel,
        grid=(1,),
        in_specs=[full(f_hi), full(f_lo), full(r_hi), full(r_lo), full(i_hi), full(i_lo),
                  pl.BlockSpec((1, 2 * n, ch), lambda i: (layer, 0, order)),
                  uspec, uspec, pl.BlockSpec((1, ch), lambda i: (0, 0))],
        out_specs=uspec,
        out_shape=jax.ShapeDtypeStruct((bsz, n, ch), F32),
        compiler_params=_cparams("arbitrary"),
        name="hyena_ctx",
    )(f_hi, f_lo, r_hi, r_lo, i_hi, i_lo, taps, u, gate, d.reshape(1, ch))


def _outproj_kernel(*refs, colmajor, tm):
    if colmajor:
        x_ref, ya_ref, yb_ref, yc_ref, w_ref, gp_ref, gate_ref, perm_ref, o_ref = refs
        ld = lambda r: _bf(_mm(perm_ref[...], _bf(r[0])))
    else:
        x_ref, ya_ref, yb_ref, yc_ref, w_ref, gp_ref, gate_ref, o_ref = refs
        ld = lambda r: _bf(r[0])
    na, nb = ya_ref.shape[-1], yb_ref.shape[-1]
    acc = (_mm(ld(ya_ref), w_ref[0, 0:na, :]) + _mm(ld(yb_ref), w_ref[0, na:na + nb, :])
           + _mm(ld(yc_ref), w_ref[0, na + nb:, :]))
    ms = jnp.mean(acc * acc, axis=-1, keepdims=True)
    upd = gate_ref[0] * (acc * lax.rsqrt(ms + EPS) * gp_ref[...])
    o_ref[0] = x_ref[0] + upd.reshape(x_ref.shape[1:])


def _outproj(x, ya, yb, yc, w_out, layer, g_post, gate, *, colmajor, tm):
    bsz, seq, d = x.shape
    extra_specs, extra_args = [], []
    if colmajor:
        cols = tm // GRID_W
        x_in = x.reshape(bsz, GRID_W, seq // GRID_W, d)
        x_spec = pl.BlockSpec((1, GRID_W, cols, d), lambda b, m: (b, 0, m, 0))
        r_old = jnp.arange(tm, dtype=jnp.int32)[:, None]
        r_new = jnp.arange(tm, dtype=jnp.int32)[None, :]
        perm = _bf((r_new == (r_old % cols) * GRID_W + r_old // cols).astype(F32))
        extra_specs, extra_args = [pl.BlockSpec((tm, tm), lambda b, m: (0, 0))], [perm]
    else:
        x_in = x
        x_spec = pl.BlockSpec((1, tm, d), lambda b, m: (b, m, 0))
    def yspec(t):
        return pl.BlockSpec((1, tm, t.shape[-1]), lambda b, m: (b, m, 0))
    out = pl.pallas_call(
        functools.partial(_outproj_kernel, colmajor=colmajor, tm=tm),
        grid=(bsz, seq // tm),
        in_specs=[x_spec, yspec(ya), yspec(yb), yspec(yc),
                  pl.BlockSpec((1,) + w_out.shape[1:], lambda b, m: (layer, 0, 0)),
                  pl.BlockSpec((1, d), lambda b, m: (0, 0)),
                  pl.BlockSpec((1, 1, d), lambda b, m: (b, 0, 0))] + extra_specs,
        out_specs=x_spec,
        out_shape=jax.ShapeDtypeStruct(x_in.shape, F32),
        compiler_params=_cparams("parallel", "parallel"),
        name="outproj",
    )(x_in, ya, yb, yc, w_out, g_post, gate, *extra_args)
    return out.reshape(bsz, seq, d)


def kernel(x, c, ctx, c_ctx, w_mod, b_mod, g_pre, g_post, w_in, w_out, gla_w_gate, gla_b_gate, gla_norm, s5_lam_re, s5_lam_im, s5_log_step, s5_b_re, s5_b_im, s5_c_re, s5_c_im, s5_d, s5_w_glu, s5_b_glu, hy_conv_w, hy_conv_b, hy_w1, hy_b1, hy_f1, hy_w2, hy_b2, hy_f2, hy_w3, hy_d):
    bsz, seq, d = x.shape
    seq_c = ctx.shape[1]
    depth = w_in.shape[0]
    assert (bsz, d, seq // GRID_W) == (2, D_MODEL, GRID_W) and seq == FFT_N1 * FFT_N2 // 2

    cvec = jnp.concatenate([c, c_ctx[None], jnp.zeros((8 - bsz - 1, d), F32)], axis=0)
    mod = _modulation(cvec, w_mod, b_mod)
    shift, scale, gate = mod[..., :d], mod[..., d:2 * d], mod[..., 2 * d:]

    w_t = _bf(jnp.swapaxes(w_in, 1, 2))
    w_out_b = _bf(w_out)
    w_glu_b = _bf(s5_w_glu)

    wg = gla_w_gate.reshape(depth, 2, 16, GLA_HEADS, GLA_DK).transpose(0, 1, 3, 2, 4)
    wg_pad = jnp.stack([jnp.pad(wg[:, 0], ((0, 0), (0, 0), (0, 112), (0, 0))),
                        jnp.pad(wg[:, 1], ((0, 0), (0, 0), (16, 96), (0, 0)))], axis=1)
    wg_hi, wg_lo = _split(wg_pad)
    bg = gla_b_gate.reshape(depth, 2, GLA_HEADS, 1, GLA_DK)

    tables = _dft_tables()
    deltas = jnp.abs(jnp.linspace(HY_MIN_DECAY, HY_MAX_DECAY, HY_WIDTH, dtype=F32))
    deltas2 = jnp.tile(deltas, 2).reshape(1, 2 * HY_WIDTH)
    pad_k = lambda w: jnp.pad(w, ((0, 0), (0, 128 - w.shape[1]), (0, 128 - w.shape[2])))
    pad_v = lambda v: jnp.pad(v, ((0, 0), (0, 128 - v.shape[1]))).reshape(depth, 1, 128)
    w3 = hy_w3.reshape(depth, HY_FFN, 2, 2, HY_WIDTH).transpose(0, 3, 1, 2, 4)
    w3 = jnp.pad(w3.reshape(depth, 2, HY_FFN, 2 * HY_WIDTH), ((0, 0), (0, 0), (0, 128 - HY_FFN), (0, 0)))
    filt_w = (pad_k(hy_w1), pad_v(hy_b1), pad_v(hy_f1), pad_k(hy_w2), pad_v(hy_b2), pad_v(hy_f2), w3, deltas2)
    taps = _filter_taps(_filter_features(seq), *filt_w, tmf=512)
    spec = _filter_spectrum(tables, taps)
    taps_c = _filter_taps(_filter_features(seq_c), *filt_w, tmf=seq_c)
    ctabs = _ctx_dft_tables(seq_c)

    s5_ops = _s5_operators(s5_lam_re, s5_lam_im, s5_log_step, s5_b_re, s5_b_im, s5_c_re, s5_c_im)

    xc = ctx
    for l in range(depth):
        last = l == depth - 1
        ctx_out = not last
        colmajor = l % 2 == 1
        pre = g_pre[l][None, :] * (1.0 + scale[l])
        sc_l, sh_l = pre[:bsz, None, :], shift[l, :bsz, None, :]
        sc_c = jnp.broadcast_to(pre[bsz][None, None, :], (bsz, 1, d))
        sh_c = jnp.broadcast_to(shift[l, bsz][None, None, :], (bsz, 1, d))
        qkv, p, lr = _inproj(x, sc_l, sh_l, w_t, l, n=N_MAIN, colmajor=colmajor, tm=1024, tn=512)
        qkvc, pc, lrc = _inproj(xc, sc_c, sh_c, w_t, l, n=N_MAIN if ctx_out else N_STATE,
                                colmajor=False, tm=seq_c, tn=512)

        y_gla, yc_gla = _gla(qkv, p, lr, qkvc, pc, lrc, wg_hi[l], wg_lo[l], bg[l], gla_norm[l][None, :],
                             ctx_out=ctx_out)

        y_s5, yc_s5 = _s5(p, pc, s5_ops, l, s5_d[l][None, :], w_glu_b[l], s5_b_glu[l][None, :], ctx_out=ctx_out)

        v, x1, x2 = _hy_pre(p, hy_conv_w[l], hy_conv_b[l][None, :])
        z = _hy_long_conv(tables, spec, l, 0, v, x1, hy_d[l, 0])
        y_hy = _hy_long_conv(tables, spec, l, 1, z, x2, hy_d[l, 1])

        x_new = _outproj(x, y_gla, y_s5, y_hy, w_out_b, l, g_post[l][None, :], gate[l, :bsz, None, :],
                         colmajor=colmajor, tm=512)
        if ctx_out:
            vc, x1c, x2c = _hy_pre(pc, hy_conv_w[l], hy_conv_b[l][None, :])
            zc = _hyc_long_conv(ctabs, taps_c, l, 0, vc, x1c, hy_d[l, 0])
            yc_hy = _hyc_long_conv(ctabs, taps_c, l, 1, zc, x2c, hy_d[l, 1])
            gate_c = jnp.broadcast_to(gate[l, bsz][None, None, :], (bsz, 1, d))
            xc = _outproj(xc, yc_gla, yc_s5, yc_hy, w_out_b, l, g_post[l][None, :], gate_c,
                          colmajor=False, tm=seq_c)
        x = x_new
    return x
```

```python
import functools
import math

import numpy as np
import jax
import jax.numpy as jnp
from jax import lax
from jax.experimental import pallas as pl
from jax.experimental.pallas import tpu as pltpu

F32 = jnp.float32
BF16 = jnp.bfloat16

D_MODEL = 2048
GRID_W = 64
EPS = 1e-6

GLA_HEADS = 4
GLA_DK = 128
GLA_DV = 256
GLA_TAU = 16.0
GLA_CHUNK = 64
GLA_BLOCK = 8

S5_WIDTH = 512
S5_GROUP = 16
S5_GROUPS = 32
S5_STATE = 64
S5_T = 16
S5_COLS = 640
S5_CB = 128

HY_WIDTH = 512
HY_EMB = 33
HY_BANDS = 16
HY_FFN = 64
HY_MIN_DECAY = math.log(1e-2) / 0.3
HY_MAX_DECAY = math.log(1e-2) / 1.5
FFT_N1 = 64
FFT_N2 = 128
HY_CT = 128
HY_KB = 8
HY_NB = 8
HY_PITCH = 136
HY_PASSES = 1

N_MAIN = 6144
N_STATE = 2560
N_QKV = 2048
N_LR = 32
COL_S5U, COL_S5G = 0, 3
COL_GLA_GATE = 2
COL_HY = 16

VMEM_LIMIT_BYTES = 56 * 1024 * 1024


def _cparams(*sem):
    return pltpu.CompilerParams(dimension_semantics=sem, vmem_limit_bytes=VMEM_LIMIT_BYTES)


def _bf(x):
    return x.astype(BF16)


def _split(x):
    hi = _bf(x)
    return hi, _bf(x - hi.astype(F32))


def _split3(x):
    a = _bf(x)
    r = x - a.astype(F32)
    b = _bf(r)
    return a, b, _bf(r - b.astype(F32))


_NN = (((1,), (0,)), ((), ()))
_NT = (((1,), (1,)), ((), ()))
_TN = (((0,), (0,)), ((), ()))


def _mm(a, b, dims=_NN):
    return lax.dot_general(a, b, dims, preferred_element_type=F32)


def _mm3(a_hi, a_lo, b, dims=_NN):
    b_hi, b_lo = _split(b)
    return _mm(a_hi, b_hi, dims) + _mm(a_hi, b_lo, dims) + _mm(a_lo, b_hi, dims)


def _mod_kernel(s_ref, w_ref, b_ref, o_ref):
    s = s_ref[...]
    s = s * jax.nn.sigmoid(s)
    s_hi, s_lo = _split(s)
    o_ref[0] = _mm3(s_hi, s_lo, w_ref[0]) + b_ref[0]


def _modulation(cvec, w_mod, b_mod):
    depth, d, n = w_mod.shape
    tn = 512
    return pl.pallas_call(
        _mod_kernel,
        grid=(depth, n // tn),
        in_specs=[
            pl.BlockSpec((8, d), lambda l, j: (0, 0)),
            pl.BlockSpec((1, d, tn), lambda l, j: (l, 0, j)),
            pl.BlockSpec((1, 1, tn), lambda l, j: (l, 0, j)),
        ],
        out_specs=pl.BlockSpec((1, 8, tn), lambda l, j: (l, 0, j)),
        out_shape=jax.ShapeDtypeStruct((depth, 8, n), F32),
        compiler_params=_cparams("parallel", "parallel"),
        name="modulation",
    )(cvec, w_mod, b_mod.reshape(depth, 1, n))


def _inproj_kernel(*refs, colmajor, tm, tn):
    if colmajor:
        x_ref, sc_ref, sh_ref, w_ref, wlr_ref, perm_ref, qkv_ref, p_ref, lr_ref, h_scr = refs
    else:
        x_ref, sc_ref, sh_ref, w_ref, wlr_ref, qkv_ref, p_ref, lr_ref, h_scr = refs
    j = pl.program_id(2)

    @pl.when(j == 0)
    def _():
        def norm(xb):
            ms = jnp.mean(xb * xb, axis=-1, keepdims=True)
            return _bf(xb * lax.rsqrt(ms + EPS) * sc_ref[0] + sh_ref[0])

        if colmajor:
            h = norm(x_ref[0].reshape(tm, D_MODEL))
            for c in range(0, D_MODEL, 512):
                h_scr[:, c:c + 512] = _bf(_mm(perm_ref[...], h[:, c:c + 512]))
        else:
            h_scr[...] = norm(x_ref[0])
        lr_ref[0] = _mm(h_scr[...], wlr_ref[0], _NT)

    res = _mm(h_scr[...], w_ref[...], _NT)
    n_qkv = N_QKV // tn

    @pl.when(j < n_qkv)
    def _():
        qkv_ref[0] = _bf(res)

    @pl.when(j >= n_qkv)
    def _():
        p_ref[0] = res


def _inproj(x, scale, shift, w_t, w_lr3, layer, *, n, colmajor, tm, tn):
    bsz, seq, d = x.shape
    n_qkv = N_QKV // tn
    extra_specs, extra_args = [], []
    if colmajor:
        cols = tm // GRID_W
        x_in = x.reshape(bsz, GRID_W, seq // GRID_W, d)
        x_spec = pl.BlockSpec((1, GRID_W, cols, d), lambda b, m, j: (b, 0, m, 0))
        r_new = jnp.arange(tm, dtype=jnp.int32)[:, None]
        r_old = jnp.arange(tm, dtype=jnp.int32)[None, :]
        perm = _bf((r_old == (r_new % GRID_W) * cols + r_new // GRID_W).astype(F32))
        extra_specs, extra_args = [pl.BlockSpec((tm, tm), lambda b, m, j: (0, 0))], [perm]
    else:
        x_in = x
        x_spec = pl.BlockSpec((1, tm, d), lambda b, m, j: (b, m, 0))
    return pl.pallas_call(
        functools.partial(_inproj_kernel, colmajor=colmajor, tm=tm, tn=tn),
        grid=(bsz, seq // tm, n // tn),
        in_specs=[
            x_spec,
            pl.BlockSpec((1, 1, d), lambda b, m, j: (b, 0, 0)),
            pl.BlockSpec((1, 1, d), lambda b, m, j: (b, 0, 0)),
            pl.BlockSpec((pl.Squeezed(), pl.Element(tn), pl.Element(d)),
                         lambda b, m, j: (layer, pl.multiple_of(j * tn + jnp.where(j >= n_qkv, N_LR, 0), N_LR), 0)),
            pl.BlockSpec((1, 128, d), lambda b, m, j: (layer, 0, 0)),
        ] + extra_specs,
        out_specs=[
            pl.BlockSpec((1, tm, tn), lambda b, m, j: (b, m, jnp.minimum(j, n_qkv - 1))),
            pl.BlockSpec((1, tm, tn), lambda b, m, j: (b, m, jnp.maximum(j - n_qkv, 0))),
            pl.BlockSpec((1, tm, 128), lambda b, m, j: (b, m, 0)),
        ],
        out_shape=[jax.ShapeDtypeStruct((bsz, seq, N_QKV), BF16),
                   jax.ShapeDtypeStruct((bsz, seq, n - N_QKV), F32),
                   jax.ShapeDtypeStruct((bsz, seq, 128), F32)],
        scratch_shapes=[pltpu.VMEM((tm, d), BF16)],
        compiler_params=_cparams("parallel", "parallel", "arbitrary"),
        name="inproj",
    )(x_in, scale, shift, w_t, w_lr3, *extra_args)


def _gla_block(q, k, v, lr, st, wg, bg, direction, mask, rowc, hi_lanes, nb):
    c = GLA_CHUNK
    n = c * nb
    lr_hi, lr_lo = _split(lr)
    z = _mm(jnp.where(hi_lanes, lr_hi, lr_lo), wg) + bg
    g = (jnp.minimum(z, 0.0) - jnp.log(1.0 + jnp.exp(-jnp.abs(z)))) * (1.0 / GLA_TAU)
    cum = g
    s = 1
    while s < c:
        if direction == 0:
            cum = cum + jnp.where(rowc >= s, pltpu.roll(cum, s, 0), 0.0)
        else:
            cum = cum + jnp.where(rowc < c - s, pltpu.roll(cum, n - s, 0), 0.0)
        s *= 2
    cum3 = cum.reshape(nb, c, GLA_DK)
    tot3 = cum3[:, c - 1:c, :] if direction == 0 else cum3[:, 0:1, :]
    q3 = q.reshape(nb, c, GLA_DK)
    k3 = k.reshape(nb, c, GLA_DK)
    qg3 = _bf(q3 * jnp.exp(cum3))
    kg3 = _bf(k3 * jnp.exp(-cum3))
    kd3 = _bf(k3 * jnp.exp(tot3 - cum3))
    a3 = jnp.exp(tot3)
    vb3 = _bf(v).reshape(nb, c, GLA_DV)
    att = [jnp.where(mask, _mm(qg3[i], kg3[i], _NT), 0.0) for i in range(nb)]
    o_in = [_mm(_bf(att[i]), vb3[i]) for i in range(nb)]
    d_st = [_mm(vb3[i], kd3[i], _TN) for i in range(nb)]
    outs = [None] * nb
    for i in (range(nb) if direction == 0 else range(nb - 1, -1, -1)):
        outs[i] = o_in[i] + _mm(qg3[i], _bf(st), _NT)
        st = st * a3[i] + d_st[i]
    return jnp.concatenate(outs, axis=0), st


def _gla_kernel(*refs, ctx_out, seq, seq_c):
    if ctx_out:
        (q_ref, k_ref, v_ref, gt_ref, lr_ref, qc_ref, kc_ref, vc_ref, gtc_ref, lrc_ref,
         wg_ref, bg_ref, gn_ref, y_ref, yc_ref) = refs
    else:
        (q_ref, k_ref, v_ref, gt_ref, lr_ref, qc_ref, kc_ref, vc_ref, lrc_ref,
         wg_ref, bg_ref, gn_ref, y_ref) = refs
        gtc_ref = yc_ref = None
    c = GLA_CHUNK
    scale = GLA_DK ** -0.5
    row = lax.broadcasted_iota(jnp.int32, (c, c), 0)
    col = lax.broadcasted_iota(jnp.int32, (c, c), 1)
    masks = (col <= row, col >= row)
    gn = gn_ref[...]

    def run(refs4, n_rows, st, direction, store):
        qr, kr, vr, lrr = refs4
        nb = min(GLA_BLOCK, n_rows // c)
        n = c * nb
        rowc = lax.broadcasted_iota(jnp.int32, (n, GLA_DK), 0) % c
        hi_lanes = (lax.broadcasted_iota(jnp.int32, (n, 128), 1) // N_LR) % 2 == 0
        wg = wg_ref[direction, 0]
        bg = bg_ref[direction, 0]
        n_blocks = n_rows // n

        def body(i, st):
            bi = i if direction == 0 else n_blocks - 1 - i
            rows = pl.ds(pl.multiple_of(bi * n, n), n)
            q = qr[0, rows, :].astype(F32) * scale
            k = kr[0, rows, :].astype(F32)
            v = vr[0, rows, :].astype(F32)
            o, st = _gla_block(q, k, v, lrr[0, rows, :], st, wg, bg, direction,
                               masks[direction], rowc, hi_lanes, nb)
            store(rows, q, k, v, o)
            return st

        return lax.fori_loop(0, n_blocks, body, st)

    def store_fwd(out_ref):
        def f(rows, q, k, v, o):
            if out_ref is not None:
                out_ref[0, rows, :] = o
        return f

    def store_bwd(out_ref, gate_ref):
        def f(rows, q, k, v, o):
            if out_ref is None:
                return
            y = out_ref[0, rows, :] + o - jnp.sum(q * k, axis=-1, keepdims=True) * v
            ms = jnp.mean(y * y, axis=-1, keepdims=True)
            y = y * lax.rsqrt(ms + EPS) * gn
            gt = gate_ref[0, rows, :]
            out_ref[0, rows, :] = y * (gt * jax.nn.sigmoid(gt))
        return f

    lat = (q_ref, k_ref, v_ref, lr_ref)
    ctx = (qc_ref, kc_ref, vc_ref, lrc_ref)
    zero = jnp.zeros((GLA_DV, GLA_DK), F32)
    st = run(ctx, seq_c, zero, 0, store_fwd(yc_ref))
    run(lat, seq, st, 0, store_fwd(y_ref))
    st = run(ctx, seq_c, zero, 1, store_bwd(yc_ref, gtc_ref))
    run(lat, seq, st, 1, store_bwd(y_ref, gt_ref))


def _gla(qkv, p, lr, qkvc, pc, lrc, wg, bg, gnorm, *, ctx_out):
    bsz, seq, _ = qkv.shape
    seq_c = qkvc.shape[1]
    h = GLA_HEADS

    def specs(n):
        return [
            pl.BlockSpec((1, n, 128), lambda b, i: (b, 0, i)),
            pl.BlockSpec((1, n, 128), lambda b, i: (b, 0, 4 + i)),
            pl.BlockSpec((1, n, 256), lambda b, i: (b, 0, 4 + i)),
            pl.BlockSpec((1, n, 256), lambda b, i: (b, 0, COL_GLA_GATE + i)),
            pl.BlockSpec((1, n, 128), lambda b, i: (b, 0, 0)),
        ]

    lat_specs = specs(seq)
    ctx_specs = specs(seq_c)
    lat_args = [qkv, qkv, qkv, p, lr]
    ctx_args = [qkvc, qkvc, qkvc, pc, lrc]
    if not ctx_out:
        del ctx_specs[3], ctx_args[3]
    w_specs = [
        pl.BlockSpec((2, 1, 128, 128), lambda b, i: (0, i, 0, 0)),
        pl.BlockSpec((2, 1, 1, 128), lambda b, i: (0, i, 0, 0)),
        pl.BlockSpec((1, GLA_DV), lambda b, i: (0, 0)),
    ]
    out_specs = [pl.BlockSpec((1, seq, 256), lambda b, i: (b, 0, i))]
    out_shape = [jax.ShapeDtypeStruct((bsz, seq, h * GLA_DV), F32)]
    if ctx_out:
        out_specs.append(pl.BlockSpec((1, seq_c, 256), lambda b, i: (b, 0, i)))
        out_shape.append(jax.ShapeDtypeStruct((bsz, seq_c, h * GLA_DV), F32))
    outs = pl.pallas_call(
        functools.partial(_gla_kernel, ctx_out=ctx_out, seq=seq, seq_c=seq_c),
        grid=(bsz, h),
        in_specs=lat_specs + ctx_specs + w_specs,
        out_specs=out_specs,
        out_shape=out_shape,
        compiler_params=_cparams("parallel", "parallel"),
        name="gla",
    )(*lat_args, *ctx_args, wg, bg, gnorm)
    return (outs[0], outs[1]) if ctx_out else (outs[0], None)


def _mm3f(a, b, dims=_NN):
    a_hi, a_lo = _split(a)
    return _mm3(a_hi, a_lo, b, dims)


def _s5_ops_kernel(a_ref, b_ref, lr_ref, li_ref, btr_ref, bti_ref, cr_ref, ci_ref,
                   kt_ref, bpr_ref, bpi_ref, cp_ref, ltr_ref, lti_ref):
    t_len, h_n, p_n = S5_T, S5_GROUP, S5_STATE
    rows = t_len * h_n
    lane = lax.broadcasted_iota(jnp.int32, (rows, 2 * p_n), 1)
    own = (lane < p_n, lane >= p_n)
    lane_blk = lax.broadcasted_iota(jnp.int32, (rows, rows), 1) // h_n
    kk = lax.broadcasted_iota(jnp.int32, (24, 2 * p_n), 0).astype(F32)
    for d in range(2):
        a, b = a_ref[0, d, 0], b_ref[0, d, 0]
        lr, li = lr_ref[0, d, 0], li_ref[0, d, 0]
        mag = jnp.exp(kk * a)
        pwr, pwi = mag * jnp.cos(kk * b), mag * jnp.sin(kk * b)

        def times_pw(k, mr, mi):
            pr, pi = pwr[k:k + 1], pwi[k:k + 1]
            return pr * mr - pi * mi, pr * mi + pi * mr

        x, y = pwr[1:2] - 1.0, pwi[1:2]
        den = lr * lr + li * li
        cfr, cfi = (x * lr + y * li) / den, (y * lr - x * li) / den
        btr, bti = btr_ref[0, d, 0], bti_ref[0, d, 0]
        bbr, bbi = cfr * btr - cfi * bti, cfr * bti + cfi * btr
        cr, ci = cr_ref[0, d, 0], ci_ref[0, d, 0]

        def stack(power_of_t, mr, mi):
            parts = [times_pw(power_of_t(t), mr, mi) for t in range(t_len)]
            return (jnp.concatenate([p[0] for p in parts], axis=0),
                    jnp.concatenate([p[1] for p in parts], axis=0))

        bpr, bpi = stack((lambda t: t_len - 1 - t) if d == 0 else (lambda t: t), bbr, bbi)
        for src, dst in ((bpr, bpr_ref), (bpi, bpi_ref)):
            z = jnp.concatenate([jnp.where(own[0], src, 0.0), jnp.where(own[1], src, 0.0)], axis=0)
            dst[0, d, 0] = _bf(z.T)
        cpr, cpi = stack((lambda t: t + 1) if d == 0 else (lambda t: t_len - t), cr, ci)
        clr, cli = stack((lambda j: j) if d == 0 else (lambda j: t_len - 1 - j), cr, ci)
        bbr_t = jnp.concatenate([bbr] * t_len, axis=0)
        bbi_t = jnp.concatenate([bbi] * t_len, axis=0)
        for e in range(2):
            cp_ref[0, d, e] = _bf(jnp.concatenate(
                [jnp.where(own[e], cpr, 0.0), jnp.where(own[e], -cpi, 0.0)], axis=1))
            kc = (_mm3f(jnp.where(own[e], clr, 0.0), bbr_t, _NT)
                  - _mm3f(jnp.where(own[e], cli, 0.0), bbi_t, _NT))
            acc = jnp.zeros((rows, rows), F32)
            for t in range(t_len):
                if d == 0:
                    pieces = [jnp.zeros((h_n * t, rows), F32), kc[:rows - h_n * t]]
                else:
                    s = h_n * (t_len - 1 - t)
                    pieces = [kc[s:], jnp.zeros((s, rows), F32)]
                shifted = jnp.concatenate([p for p in pieces if p.shape[0]], axis=0)
                acc = jnp.where(lane_blk == t, shifted, acc)
            kt_ref[0, d, e] = _bf(acc)
        ltr_ref[0, d, 0] = pwr[t_len:t_len + 1]
        lti_ref[0, d, 0] = pwi[t_len:t_len + 1]


def _s5_operators(lam_re, lam_im, log_step, b_re, b_im, c_re, c_im):
    depth = lam_re.shape[0]
    g_n, p_n, h_n = S5_GROUPS, S5_STATE, S5_GROUP
    pairs, rows = g_n // 2, S5_T * h_n
    dt = jnp.exp(log_step)[..., None]
    vec = lambda v: v.reshape(depth, 2, pairs, 1, 2 * p_n)
    bt = lambda m: m.reshape(depth, 2, pairs, 2, p_n, h_n).transpose(0, 1, 2, 5, 3, 4).reshape(depth, 2, pairs, h_n, 2 * p_n)
    ct = lambda m: m.reshape(depth, 2, pairs, 2, h_n, p_n).transpose(0, 1, 2, 4, 3, 5).reshape(depth, 2, pairs, h_n, 2 * p_n)
    vspec = pl.BlockSpec((1, 2, 1, 1, 2 * p_n), lambda l, q: (l, 0, q, 0, 0))
    mspec = pl.BlockSpec((1, 2, 1, h_n, 2 * p_n), lambda l, q: (l, 0, q, 0, 0))
    sq_spec = pl.BlockSpec((1, 2, 2, rows, rows), lambda l, q: (l, 0, q, 0, 0))
    bp_spec = pl.BlockSpec((1, 2, 1, 2 * p_n, 2 * rows), lambda l, q: (l, 0, q, 0, 0))
    sq = jax.ShapeDtypeStruct((depth, 2, g_n, rows, rows), BF16)
    bp = jax.ShapeDtypeStruct((depth, 2, pairs, 2 * p_n, 2 * rows), BF16)
    lt = jax.ShapeDtypeStruct((depth, 2, pairs, 1, 2 * p_n), F32)
    ktoep, bpr, bpi, cpad, ltr, lti = pl.pallas_call(
        _s5_ops_kernel,
        grid=(depth, pairs),
        in_specs=[vspec] * 4 + [mspec] * 4,
        out_specs=[sq_spec, bp_spec, bp_spec, sq_spec, vspec, vspec],
        out_shape=[sq, bp, bp, sq, lt, lt],
        compiler_params=_cparams("parallel", "parallel"),
        name="s5_operators",
    )(vec(lam_re * dt), vec(lam_im * dt), vec(lam_re), vec(lam_im), bt(b_re), bt(b_im), ct(c_re), ct(c_im))
    flat = lambda v: v.reshape(depth, 2, 1, g_n * p_n)
    return ktoep, bpr, bpi, cpad, flat(ltr), flat(lti)


def _s5_in_kernel(u_ref, uc_ref, o_ref):
    j = pl.program_id(0)
    last = pl.num_programs(0) - 1

    def emit(slab_of_t):
        for t in range(S5_T):
            o_ref[:, t * S5_GROUP:(t + 1) * S5_GROUP, :] = _bf(
                slab_of_t(t).T.reshape(S5_GROUPS, S5_GROUP, S5_CB))

    @pl.when(j < last)
    def _():
        emit(lambda t: u_ref[0, :, t, :])

    @pl.when(j == last)
    def _():
        pad = jnp.zeros((S5_CB - 2 * uc_ref.shape[1], S5_WIDTH), F32)
        emit(lambda t: jnp.concatenate([uc_ref[0, :, t, :], uc_ref[1, :, t, :], pad], axis=0))


def _s5_lat_index(n_half):
    def index(j):
        jj = jnp.minimum(j, 2 * n_half - 1)
        return jj // n_half, jj % n_half
    return index


def _s5_relayout_in(p, pc):
    bsz, seq, n = p.shape
    seq_c, n_c = pc.shape[1], pc.shape[2]
    n_lat, n_ctx = seq // S5_T, seq_c // S5_T
    n_half = n_lat // S5_CB
    idx = _s5_lat_index(n_half)
    return pl.pallas_call(
        _s5_in_kernel,
        grid=(bsz * n_half + 1,),
        in_specs=[
            pl.BlockSpec((1, S5_CB, S5_T, 512), lambda j: (*idx(j), 0, COL_S5U)),
            pl.BlockSpec((bsz, n_ctx, S5_T, 512), lambda j: (0, 0, 0, COL_S5U)),
        ],
        out_specs=pl.BlockSpec((S5_GROUPS, S5_T * S5_GROUP, S5_CB), lambda j: (0, 0, j)),
        out_shape=jax.ShapeDtypeStruct((S5_GROUPS, S5_T * S5_GROUP, S5_COLS), BF16),
        compiler_params=_cparams("arbitrary"),
        name="s5_relayout_in",
    )(p.reshape(bsz, n_lat, S5_T, n), pc.reshape(bsz, n_ctx, S5_T, n_c))


def _s5_state_in_kernel(ug_ref, bre_ref, bim_ref, vre_ref, vim_ref):
    u = ug_ref[...].reshape(2 * S5_T * S5_GROUP, S5_COLS)
    for direction in range(2):
        vre_ref[direction] = _mm(bre_ref[0, direction, 0], u)
        vim_ref[direction] = _mm(bim_ref[0, direction, 0], u)


def _s5_state_in(ug, b_pair_re, b_pair_im, layer):
    pairs = S5_GROUPS // 2
    rows = S5_T * S5_GROUP
    lanes = S5_GROUPS * S5_STATE
    out = jax.ShapeDtypeStruct((2, lanes, S5_COLS), F32)
    return pl.pallas_call(
        _s5_state_in_kernel,
        grid=(pairs,),
        in_specs=[
            pl.BlockSpec((2, rows, S5_COLS), lambda q: (q, 0, 0)),
            pl.BlockSpec((1, 2, 1, 128, 2 * rows), lambda q: (layer, 0, q, 0, 0)),
            pl.BlockSpec((1, 2, 1, 128, 2 * rows), lambda q: (layer, 0, q, 0, 0)),
        ],
        out_specs=[pl.BlockSpec((2, 128, S5_COLS), lambda q: (0, q, 0))] * 2,
        out_shape=[out, out],
        compiler_params=_cparams("parallel"),
        name="s5_state_in",
    )(ug, b_pair_re, b_pair_im)


def _s5_scan_kernel(vre_ref, vim_ref, lre_ref, lim_ref, xre_ref, xim_ref, tre_scr, tim_scr, *, n_lat, n_ctx):
    lanes = lre_ref.shape[-1]
    for d in range(2):
        tre_scr[d] = vre_ref[d].T
        tim_scr[d] = vim_ref[d].T
    a_re = [lre_ref[0, d] for d in range(2)]
    a_im = [lim_ref[0, d] for d in range(2)]

    def step(d, row, xr, xi):
        sl = pl.ds(row, 1)
        xre_ref[d, sl, :] = xr
        xim_ref[d, sl, :] = xi
        return (a_re[d] * xr - a_im[d] * xi + tre_scr[d, sl, :],
                a_re[d] * xi + a_im[d] * xr + tim_scr[d, sl, :])

    def run(bases, n, carry):
        def body(i, carry):
            out = []
            for d in range(2):
                j = i if d == 0 else n - 1 - i
                for b in range(2):
                    xr, xi = carry[2 * d + b]
                    out.append(step(d, bases[b] + j, xr, xi))
            return tuple(out)
        return lax.fori_loop(0, n, body, carry)

    zero = jnp.zeros((1, lanes), F32)
    carry = run((2 * n_lat, 2 * n_lat + n_ctx), n_ctx, ((zero, zero),) * 4)
    run((0, n_lat), n_lat, carry)
    pad0 = 2 * n_lat + 2 * n_ctx
    for d in range(2):
        xre_ref[d, pad0:, :] = jnp.zeros((S5_COLS - pad0, lanes), F32)
        xim_ref[d, pad0:, :] = jnp.zeros((S5_COLS - pad0, lanes), F32)


def _s5_scan(vre, vim, lam_re, lam_im, layer, *, n_lat, n_ctx):
    lanes = vre.shape[1]
    tl = 512
    vspec = pl.BlockSpec((2, tl, S5_COLS), lambda j: (0, j, 0))
    xspec = pl.BlockSpec((2, S5_COLS, tl), lambda j: (0, 0, j))
    lspec = pl.BlockSpec((1, 2, 1, tl), lambda j: (layer, 0, 0, j))
    out = jax.ShapeDtypeStruct((2, S5_COLS, lanes), F32)
    return pl.pallas_call(
        functools.partial(_s5_scan_kernel, n_lat=n_lat, n_ctx=n_ctx),
        grid=(lanes // tl,),
        in_specs=[vspec, vspec, lspec, lspec],
        out_specs=[xspec, xspec],
        out_shape=[out, out],
        scratch_shapes=[pltpu.VMEM((2, S5_COLS, tl), F32), pltpu.VMEM((2, S5_COLS, tl), F32)],
        compiler_params=_cparams("parallel"),
        name="s5_scan",
    )(vre, vim, lam_re, lam_im)


def _s5_readout_kernel(ug_ref, kt_ref, cp_ref, xre_ref, xim_ref, y_ref):
    for e in range(2):
        u = ug_ref[e]
        acc = None
        for direction in range(2):
            xcat = _bf(jnp.concatenate([xre_ref[direction], xim_ref[direction]], axis=1))
            term = _mm(kt_ref[0, direction, e], u) + _mm(cp_ref[0, direction, e], xcat, _NT)
            acc = term if acc is None else acc + term
        y_ref[e] = acc


def _s5_readout(ug, ktoep, cpad, xre, xim, layer):
    pairs = S5_GROUPS // 2
    rows = S5_T * S5_GROUP
    return pl.pallas_call(
        _s5_readout_kernel,
        grid=(pairs,),
        in_specs=[
            pl.BlockSpec((2, rows, S5_COLS), lambda q: (q, 0, 0)),
            pl.BlockSpec((1, 2, 2, rows, rows), lambda q: (layer, 0, q, 0, 0)),
            pl.BlockSpec((1, 2, 2, rows, rows), lambda q: (layer, 0, q, 0, 0)),
            pl.BlockSpec((2, S5_COLS, 128), lambda q: (0, 0, q)),
            pl.BlockSpec((2, S5_COLS, 128), lambda q: (0, 0, q)),
        ],
        out_specs=pl.BlockSpec((2, rows, S5_COLS), lambda q: (q, 0, 0)),
        out_shape=jax.ShapeDtypeStruct((S5_GROUPS, rows, S5_COLS), F32),
        compiler_params=_cparams("parallel"),
        name="s5_readout",
    )(ug, ktoep, cpad, xre, xim)


def _s5_out_kernel(*refs, ctx_out):
    if ctx_out:
        yg_ref, u_ref, g_ref, uc_ref, gc_ref, d_ref, w_ref, b_ref, o_ref, oc_ref = refs
    else:
        yg_ref, u_ref, g_ref, d_ref, w_ref, b_ref, o_ref = refs
    d = d_ref[...]
    w = w_ref[...]
    bias = b_ref[...]

    def finish(yy, u, gate):
        yy = yy + d * u
        yg = jax.nn.gelu(yy)
        out = yg * jax.nn.sigmoid(_mm(_bf(yg), w) + bias)
        return out * (gate * jax.nn.sigmoid(gate))

    def y_of(t):
        return yg_ref[:, t * S5_GROUP:(t + 1) * S5_GROUP, :].reshape(S5_WIDTH, S5_CB).T

    def lat():
        for t in range(S5_T):
            o_ref[0, :, t, :] = finish(y_of(t), u_ref[0, :, t, :], g_ref[0, :, t, :])

    if not ctx_out:
        lat()
        return
    j = pl.program_id(0)
    last = pl.num_programs(0) - 1
    pl.when(j < last)(lat)

    @pl.when(j == last)
    def _():
        n_ctx = uc_ref.shape[1]
        for t in range(S5_T):
            y = y_of(t)
            for b in range(2):
                oc_ref[b, :, t, :] = finish(y[b * n_ctx:(b + 1) * n_ctx], uc_ref[b, :, t, :], gc_ref[b, :, t, :])


def _s5_relayout_out(yg, p, pc, d_skip, w_glu, b_glu, *, ctx_out):
    bsz, seq, n = p.shape
    seq_c, n_c = pc.shape[1], pc.shape[2]
    n_lat, n_ctx = seq // S5_T, seq_c // S5_T
    n_half = n_lat // S5_CB
    idx = _s5_lat_index(n_half)
    pv = p.reshape(bsz, n_lat, S5_T, n)
    lat_block = (1, S5_CB, S5_T, 512)
    in_specs = [
        pl.BlockSpec((S5_GROUPS, S5_T * S5_GROUP, S5_CB), lambda j: (0, 0, j)),
        pl.BlockSpec(lat_block, lambda j: (*idx(j), 0, COL_S5U)),
        pl.BlockSpec(lat_block, lambda j: (*idx(j), 0, COL_S5G)),
    ]
    args = [yg, pv, pv]
    out_specs = [pl.BlockSpec(lat_block, lambda j: (*idx(j), 0, 0))]
    out_shape = [jax.ShapeDtypeStruct((bsz, n_lat, S5_T, S5_WIDTH), F32)]
    if ctx_out:
        pcv = pc.reshape(bsz, n_ctx, S5_T, n_c)
        ctx_block = (bsz, n_ctx, S5_T, 512)
        in_specs += [
            pl.BlockSpec(ctx_block, lambda j: (0, 0, 0, COL_S5U)),
            pl.BlockSpec(ctx_block, lambda j: (0, 0, 0, COL_S5G)),
        ]
        args += [pcv, pcv]
        out_specs.append(pl.BlockSpec(ctx_block, lambda j: (0, 0, 0, 0)))
        out_shape.append(jax.ShapeDtypeStruct((bsz, n_ctx, S5_T, S5_WIDTH), F32))
    in_specs += [
        pl.BlockSpec((1, S5_WIDTH), lambda j: (0, 0)),
        pl.BlockSpec((S5_WIDTH, S5_WIDTH), lambda j: (0, 0)),
        pl.BlockSpec((1, S5_WIDTH), lambda j: (0, 0)),
    ]
    args += [d_skip, w_glu, b_glu]
    outs = pl.pallas_call(
        functools.partial(_s5_out_kernel, ctx_out=ctx_out),
        grid=(bsz * n_half + (1 if ctx_out else 0),),
        in_specs=in_specs,
        out_specs=out_specs,
        out_shape=out_shape,
        compiler_params=_cparams("arbitrary"),
        name="s5_relayout_out",
    )(*args)
    y = outs[0].reshape(bsz, seq, S5_WIDTH)
    yc = outs[1].reshape(bsz, seq_c, S5_WIDTH) if ctx_out else None
    return y, yc


def _s5(p, pc, ops, layer, d_skip, w_glu, b_glu, *, ctx_out):
    ktoep, b_pair_re, b_pair_im, cpad, lam_re, lam_im = ops
    ug = _s5_relayout_in(p, pc)
    vre, vim = _s5_state_in(ug, b_pair_re, b_pair_im, layer)
    xre, xim = _s5_scan(vre, vim, lam_re, lam_im, layer, n_lat=p.shape[1] // S5_T, n_ctx=pc.shape[1] // S5_T)
    yg = _s5_readout(ug, ktoep, cpad, xre, xim, layer)
    return _s5_relayout_out(yg, p, pc, d_skip, w_glu, b_glu, ctx_out=ctx_out)


def _dft_tables():
    n1, n2 = FFT_N1, FFT_N2
    n = n1 * n2
    a = jnp.arange(n1, dtype=jnp.int32)
    ph1 = (a[:, None] * a[None, :]) % n1
    ang1 = ph1.astype(F32) * (2.0 * math.pi / n1)
    c1, s1 = jnp.cos(ang1), -jnp.sin(ang1)
    h = n1 // 2
    m1 = jnp.concatenate([jnp.concatenate([c1[:, :h], -s1[:, :h]], 1),
                          jnp.concatenate([s1[:, :h], c1[:, :h]], 1)], 0)
    m1_real = jnp.concatenate([c1, s1], 0)
    m3 = m1.T / n
    b = jnp.arange(n2, dtype=jnp.int32)
    ang2 = ((b[:, None] * b[None, :]) % n2).astype(F32) * (2.0 * math.pi / n2)
    fr, fi = jnp.cos(ang2), -jnp.sin(ang2)
    f2 = jnp.concatenate([jnp.concatenate([fr, -fi], 1), jnp.concatenate([fi, fr], 1)], 0)
    angt = ((a[:, None] * b[None, :]) % n).astype(F32) * (2.0 * math.pi / n)
    tw = jnp.stack([jnp.cos(angt), -jnp.sin(angt)], axis=1)
    tw = jnp.broadcast_to(tw[..., None], (n1, 2, n2, HY_CT))
    return tuple(_split(t) for t in (m1, m1_real, m3, f2, f2.T)) + (tw,)


def _filter_features(length):
    t = jnp.linspace(0.0, 1.0, length, dtype=F32)[:, None]
    ang = (2.0 * math.pi / length) * jnp.arange(length, dtype=F32)[:, None]
    bands = jnp.linspace(1e-4, HY_BANDS - 1, HY_BANDS, dtype=F32)[None, :]
    feats = jnp.concatenate([t, jnp.cos(bands * ang), -jnp.sin(bands * ang)], axis=-1)
    feats = jnp.pad(feats, ((0, 0), (0, 128 - HY_EMB)))
    rev = jnp.roll(feats[::-1], 1, axis=0)
    return jnp.stack([feats, rev])


def _filter_kernel(f_ref, w1_ref, b1_ref, f1_ref, w2_ref, b2_ref, f2_ref, w3_ref, dl_ref, o_ref):
    half = pl.program_id(1)
    tile = pl.program_id(2)
    x = f_ref[0]
    x_hi, x_lo = _split(x)
    h = jnp.sin(f1_ref[0] * (_mm3(x_hi, x_lo, w1_ref[0]) + b1_ref[0]))
    h_hi, h_lo = _split(h)
    h = jnp.sin(f2_ref[0] * (_mm3(h_hi, h_lo, w2_ref[0]) + b2_ref[0]))
    y = _mm(_bf(h), _bf(w3_ref[0, 0])) * jnp.exp(-x[:, 0:1] * dl_ref[...])
    rows = lax.broadcasted_iota(jnp.int32, y.shape, 0)
    drop = jnp.logical_and(jnp.logical_and(half == 1, tile == 0), rows == 0)
    o_ref[0] = jnp.where(drop, 0.0, y)


def _filter_taps(feats, w1, b1, f1, w2, b2, f2, w3, deltas, tmf):
    depth = w1.shape[0]
    length = feats.shape[1]
    nt = length // tmf
    wide = w3.shape[-1]
    vec = pl.BlockSpec((1, 1, 128), lambda l, s, i: (l, 0, 0))
    sq = pl.BlockSpec((1, 128, 128), lambda l, s, i: (l, 0, 0))
    return pl.pallas_call(
        _filter_kernel,
        grid=(depth, 2, nt),
        in_specs=[
            pl.BlockSpec((1, tmf, 128), lambda l, s, i: (s, i, 0)),
            sq, vec, vec, sq, vec, vec,
            pl.BlockSpec((1, 1, 128, wide), lambda l, s, i: (l, s, 0, 0)),
            pl.BlockSpec((1, wide), lambda l, s, i: (0, 0)),
        ],
        out_specs=pl.BlockSpec((1, tmf, wide), lambda l, s, i: (l, s * nt + i, 0)),
        out_shape=jax.ShapeDtypeStruct((depth, 2 * length, wide), F32),
        compiler_params=_cparams("parallel", "parallel", "parallel"),
        name="hyena_filter",
    )(feats, w1, b1, f1, w2, b2, f2, w3, deltas)


def _parts(pair):
    return tuple(pair) if HY_PASSES == 3 else tuple(pair[:1])


def _mmp(m, b):
    return _mm(m[0], _bf(b)) if len(m) == 1 else _mm3(m[0], m[1], b)


def _blk_rows(n2):
    return pl.ds(pl.multiple_of(n2 * HY_PITCH, 8), 2 * FFT_N1)


def _pad_rows(n2, n):
    return pl.ds(n2, n, stride=HY_PITCH)


def _pad_copy(src, dst, n):
    for n1 in range(n):
        dst[n1 * HY_PITCH:n1 * HY_PITCH + FFT_N2, :] = src[n1 * FFT_N2:(n1 + 1) * FFT_N2, :]


def _dft_stage1(read_rows, m, a_scr):
    ct = a_scr.shape[-1]

    def body(it, carry):
        n2 = it * HY_NB
        a = _mmp(m, jnp.concatenate([read_rows(n2 + j) for j in range(HY_NB)], axis=1))
        for j in range(HY_NB):
            a_scr[_blk_rows(n2 + j), :] = a[:, j * ct:(j + 1) * ct]
        return carry

    lax.fori_loop(0, FFT_N2 // HY_NB, body, 0)


def _stage2_rows(k1):
    return (pl.ds(k1, FFT_N2, stride=HY_PITCH), pl.ds(k1 + FFT_N1, FFT_N2, stride=HY_PITCH))


def _stage2_load(a_scr, tw_ref, kb):
    rows = [_stage2_rows(kb * HY_KB + i) for i in range(HY_KB)]
    vals = []
    for i, (re, im) in enumerate(rows):
        ar, ai = a_scr[re, :], a_scr[im, :]
        tr, ti = tw_ref[kb * HY_KB + i, 0], tw_ref[kb * HY_KB + i, 1]
        vals.append(jnp.concatenate([ar * tr - ai * ti, ar * ti + ai * tr], axis=0))
    return rows, vals


def _spectrum_kernel(*refs):
    np_ = len(_parts((0, 0)))
    m_refs, f_refs, (tw_ref, t_ref, o_ref, a_scr, pad_scr) = refs[:np_], refs[np_:2 * np_], refs[2 * np_:]
    kb = pl.program_id(2)
    ct = t_ref.shape[-1]

    @pl.when(kb == 0)
    def _():
        _pad_copy(t_ref.at[0], pad_scr, FFT_N1)
        _dft_stage1(lambda n2: pad_scr[_pad_rows(n2, FFT_N1), :], [r[...] for r in m_refs], a_scr)

    _, a = _stage2_load(a_scr, tw_ref, kb)
    f2 = [r[...] for r in f_refs]
    for i in range(HY_KB):
        o_ref[0, i] = _bf(_mmp(f2, a[i])).reshape(2, FFT_N2, ct)


def _filter_spectrum(tables, taps):
    m, f2, tw = _parts(tables[1]), _parts(tables[3]), tables[5]
    depth, n, ch = taps.shape
    ct = HY_CT
    const = lambda t: pl.BlockSpec(t.shape, lambda l, c, k: (0,) * t.ndim, pipeline_mode=pl.Buffered(1))
    return pl.pallas_call(
        _spectrum_kernel,
        grid=(depth, ch // ct, FFT_N1 // HY_KB),
        in_specs=[const(t) for t in (*m, *f2, tw)] + [pl.BlockSpec((1, n, ct), lambda l, c, k: (l, 0, c))],
        out_specs=pl.BlockSpec((1, HY_KB, 2, FFT_N2, ct), lambda l, c, k: (l, k, 0, 0, c)),
        out_shape=jax.ShapeDtypeStruct((depth, FFT_N1, 2, FFT_N2, ch), BF16),
        scratch_shapes=[pltpu.VMEM((HY_PITCH * FFT_N2, ct), F32), pltpu.VMEM((HY_PITCH * FFT_N1, ct), F32)],
        compiler_params=_cparams("parallel", "parallel", "arbitrary"),
        name="hyena_spectrum",
    )(*m, *f2, tw, taps)


def _hy_pre_kernel(pv_ref, p1_ref, p2_ref, pg_ref, wv_ref, w1_ref, w2_ref, bv_ref, b1_ref, b2_ref,
                   v_ref, x1_ref, x2_ref):
    n = pv_ref.shape[1]
    rows = lax.broadcasted_iota(jnp.int32, (n, 128), 0)

    def conv(p, w_ref, b_ref):
        up = jnp.where(rows == 0, 0.0, pltpu.roll(p, 1, 0))
        dn = jnp.where(rows == n - 1, 0.0, pltpu.roll(p, n - 1, 0))
        return up * w_ref[0:1, :] + p * w_ref[1:2, :] + dn * w_ref[2:3, :] + b_ref[...]

    v_ref[0] = conv(pv_ref[0], wv_ref, bv_ref)
    x1_ref[0] = conv(p1_ref[0], w1_ref, b1_ref)
    gate = pg_ref[0]
    x2_ref[0] = conv(p2_ref[0], w2_ref, b2_ref) * (gate * jax.nn.sigmoid(gate))


def _hy_pre(p, conv_w, conv_b):
    bsz, seq, _ = p.shape
    def pspec(off):
        return pl.BlockSpec((1, seq, 128), lambda b, j: (b, 0, off + j))
    def wspec(off):
        return pl.BlockSpec((3, 128), lambda b, j: (0, off + j))
    def bspec(off):
        return pl.BlockSpec((1, 128), lambda b, j: (0, off + j))
    out = jax.ShapeDtypeStruct((bsz, seq, HY_WIDTH), F32)
    ospec = pl.BlockSpec((1, seq, 128), lambda b, j: (b, 0, j))
    return pl.pallas_call(
        _hy_pre_kernel,
        grid=(bsz, HY_WIDTH // 128),
        in_specs=[pspec(COL_HY), pspec(COL_HY + 4), pspec(COL_HY + 8), pspec(COL_HY + 12),
                  wspec(0), wspec(4), wspec(8), bspec(0), bspec(4), bspec(8)],
        out_specs=[ospec, ospec, ospec],
        out_shape=[out, out, out],
        compiler_params=_cparams("parallel", "parallel"),
        name="hyena_pre",
    )(p, p, p, p, conv_w, conv_w, conv_w, conv_b, conv_b, conv_b)


def _hy_conv_kernel(*refs):
    np_ = len(_parts((0, 0)))
    m1_refs, m3_refs, f_refs, ft_refs = (refs[i * np_:(i + 1) * np_] for i in range(4))
    tw_ref, u_ref, gate_ref, h_ref, d_ref, o_ref, a_scr, pad_scr = refs[4 * np_:]
    kb = pl.program_id(1)
    ct = u_ref.shape[-1]
    half = FFT_N1 // 2

    @pl.when(kb == 0)
    def _():
        for b in range(2):
            _pad_copy(u_ref.at[b], pad_scr.at[b], half)
        _dft_stage1(lambda n2: jnp.concatenate([pad_scr[0, _pad_rows(n2, half), :],
                                                pad_scr[1, _pad_rows(n2, half), :]], axis=0),
                    [r[...] for r in m1_refs], a_scr)

    srows, a = _stage2_load(a_scr, tw_ref, kb)
    f2, f2t = [r[...] for r in f_refs], [r[...] for r in ft_refs]
    z = [_mmp(f2, a[i]) for i in range(HY_KB)]
    z2 = []
    for i in range(HY_KB):
        zr, zi = z[i][:FFT_N2], z[i][FFT_N2:]
        hr, hi = h_ref[0, i, 0].astype(F32), h_ref[0, i, 1].astype(F32)
        z2.append(jnp.concatenate([zr * hr - zi * hi, zr * hi + zi * hr], axis=0))
    a2 = [_mmp(f2t, z2[i]) for i in range(HY_KB)]
    for i in range(HY_KB):
        cr, ci = a2[i][:FFT_N2], a2[i][FFT_N2:]
        tr, ti = tw_ref[kb * HY_KB + i, 0], tw_ref[kb * HY_KB + i, 1]
        a_scr[srows[i][0], :] = cr * tr + ci * ti
        a_scr[srows[i][1], :] = ci * tr - cr * ti

    @pl.when(kb == pl.num_programs(1) - 1)
    def _():
        m3 = [r[...] for r in m3_refs]
        d = d_ref[...]

        def body(it, carry):
            n2 = it * HY_NB
            y = _mmp(m3, jnp.concatenate([a_scr[_blk_rows(n2 + j), :] for j in range(HY_NB)], axis=1))
            for j in range(HY_NB):
                for b in range(2):
                    pad_scr[b, _pad_rows(n2 + j, half), :] = y[b * half:(b + 1) * half, j * ct:(j + 1) * ct]
            return carry

        lax.fori_loop(0, FFT_N2 // HY_NB, body, 0)
        for b in range(2):
            for n1 in range(half):
                r = slice(n1 * FFT_N2, (n1 + 1) * FFT_N2)
                conv = pad_scr[b, n1 * HY_PITCH:n1 * HY_PITCH + FFT_N2, :]
                o_ref[b, r, :] = gate_ref[b, r, :] * (conv + d * u_ref[b, r, :])


def _hy_long_conv(tables, spec, layer, order, u, gate, d):
    m1, _, m3, f2, f2t = (_parts(t) for t in tables[:5])
    bsz, seq, ch = u.shape
    ct = HY_CT
    const = lambda t: pl.BlockSpec(t.shape, lambda c, k: (0,) * t.ndim, pipeline_mode=pl.Buffered(1))
    uspec = pl.BlockSpec((bsz, seq, ct), lambda c, k: (0, 0, c))
    n_ct = ch // ct
    mats = (*m1, *m3, *f2, *f2t, tables[5])
    return pl.pallas_call(
        _hy_conv_kernel,
        grid=(n_ct, FFT_N1 // HY_KB),
        in_specs=[const(t) for t in mats] + [
            uspec, uspec,
            pl.BlockSpec((1, HY_KB, 2, FFT_N2, ct), lambda c, k: (layer, k, 0, 0, order * n_ct + c)),
            pl.BlockSpec((1, ct), lambda c, k: (0, c))],
        out_specs=uspec,
        out_shape=jax.ShapeDtypeStruct(u.shape, F32),
        scratch_shapes=[pltpu.VMEM((HY_PITCH * FFT_N2, ct), F32),
                        pltpu.VMEM((bsz, HY_PITCH * FFT_N1 // 2, ct), F32)],
        compiler_params=_cparams("parallel", "arbitrary"),
        name="hyena_conv",
    )(*mats, u, gate, spec, d.reshape(1, ch))


def _ctx_dft_tables(length):
    n = 2 * length
    k = jnp.arange(n, dtype=jnp.int32)
    ang = ((k[:, None] * k[None, :]) % n).astype(F32) * (2.0 * math.pi / n)
    fr, fi = jnp.cos(ang), -jnp.sin(ang)
    fwd = jnp.concatenate([jnp.concatenate([fr[:, :length], -fi[:, :length]], 1),
                           jnp.concatenate([fi[:, :length], fr[:, :length]], 1)], 0)
    real = jnp.concatenate([fr, fi], 0)
    inv = fwd.T / n
    return tuple(_split(t) for t in (fwd, real, inv))


def _hyc_kernel(fh_ref, fl_ref, rh_ref, rl_ref, ih_ref, il_ref, taps_ref, u_ref, g_ref, d_ref, o_ref):
    n = u_ref.shape[1]
    x = jnp.concatenate([u_ref[0], u_ref[1]], axis=0)
    z = _mm3(fh_ref[...], fl_ref[...], x)
    h = _mm3(rh_ref[...], rl_ref[...], taps_ref[0])
    m = 2 * n
    zr, zi, hr, hi = z[:m], z[m:], h[:m], h[m:]
    z2 = jnp.concatenate([zr * hr - zi * hi, zr * hi + zi * hr], axis=0)
    y = _mm3(ih_ref[...], il_ref[...], z2)
    for b in range(2):
        o_ref[b] = g_ref[b] * (y[b * n:(b + 1) * n] + d_ref[...] * u_ref[b])


def _hyc_long_conv(ctabs, taps, layer, order, u, gate, d):
    (f_hi, f_lo), (r_hi, r_lo), (i_hi, i_lo) = ctabs
    bsz, n, ch = u.shape
    full = lambda t: pl.BlockSpec(t.shape, lambda i: (0,) * t.ndim)
    uspec = pl.BlockSpec((bsz, n, ch), lambda i: (0, 0, 0))
    return pl.pallas_call(
        _hyc_kernel,
        grid=(1,),
        in_specs=[full(f_hi), full(f_lo), full(r_hi), full(r_lo), full(i_hi), full(i_lo),
                  pl.BlockSpec((1, 2 * n, ch), lambda i: (layer, 0, order)),
                  uspec, uspec, pl.BlockSpec((1, ch), lambda i: (0, 0))],
        out_specs=uspec,
        out_shape=jax.ShapeDtypeStruct((bsz, n, ch), F32),
        compiler_params=_cparams("arbitrary"),
        name="hyena_ctx",
    )(f_hi, f_lo, r_hi, r_lo, i_hi, i_lo, taps, u, gate, d.reshape(1, ch))


def _outproj_kernel(*refs, colmajor, tm):
    if colmajor:
        x_ref, ya_ref, yb_ref, yc_ref, w_ref, gp_ref, gate_ref, perm_ref, o_ref = refs
        ld = lambda r: _bf(_mm(perm_ref[...], _bf(r[0])))
    else:
        x_ref, ya_ref, yb_ref, yc_ref, w_ref, gp_ref, gate_ref, o_ref = refs
        ld = lambda r: _bf(r[0])
    na, nb = ya_ref.shape[-1], yb_ref.shape[-1]
    acc = (_mm(ld(ya_ref), w_ref[0, 0:na, :]) + _mm(ld(yb_ref), w_ref[0, na:na + nb, :])
           + _mm(ld(yc_ref), w_ref[0, na + nb:, :]))
    ms = jnp.mean(acc * acc, axis=-1, keepdims=True)
    upd = gate_ref[0] * (acc * lax.rsqrt(ms + EPS) * gp_ref[...])
    o_ref[0] = x_ref[0] + upd.reshape(x_ref.shape[1:])


def _outproj(x, ya, yb, yc, w_out, layer, g_post, gate, *, colmajor, tm):
    bsz, seq, d = x.shape
    extra_specs, extra_args = [], []
    if colmajor:
        cols = tm // GRID_W
        x_in = x.reshape(bsz, GRID_W, seq // GRID_W, d)
        x_spec = pl.BlockSpec((1, GRID_W, cols, d), lambda b, m: (b, 0, m, 0))
        r_old = jnp.arange(tm, dtype=jnp.int32)[:, None]
        r_new = jnp.arange(tm, dtype=jnp.int32)[None, :]
        perm = _bf((r_new == (r_old % cols) * GRID_W + r_old // cols).astype(F32))
        extra_specs, extra_args = [pl.BlockSpec((tm, tm), lambda b, m: (0, 0))], [perm]
    else:
        x_in = x
        x_spec = pl.BlockSpec((1, tm, d), lambda b, m: (b, m, 0))
    def yspec(t):
        return pl.BlockSpec((1, tm, t.shape[-1]), lambda b, m: (b, m, 0))
    out = pl.pallas_call(
        functools.partial(_outproj_kernel, colmajor=colmajor, tm=tm),
        grid=(bsz, seq // tm),
        in_specs=[x_spec, yspec(ya), yspec(yb), yspec(yc),
                  pl.BlockSpec((1,) + w_out.shape[1:], lambda b, m: (layer, 0, 0)),
                  pl.BlockSpec((1, d), lambda b, m: (0, 0)),
                  pl.BlockSpec((1, 1, d), lambda b, m: (b, 0, 0))] + extra_specs,
        out_specs=x_spec,
        out_shape=jax.ShapeDtypeStruct(x_in.shape, F32),
        compiler_params=_cparams("parallel", "parallel"),
        name="outproj",
    )(x_in, ya, yb, yc, w_out, g_post, gate, *extra_args)
    return out.reshape(bsz, seq, d)


def kernel(x, c, ctx, c_ctx, w_mod, b_mod, g_pre, g_post, w_in, w_out, gla_w_gate, gla_b_gate, gla_norm, s5_lam_re, s5_lam_im, s5_log_step, s5_b_re, s5_b_im, s5_c_re, s5_c_im, s5_d, s5_w_glu, s5_b_glu, hy_conv_w, hy_conv_b, hy_w1, hy_b1, hy_f1, hy_w2, hy_b2, hy_f2, hy_w3, hy_d):
    bsz, seq, d = x.shape
    seq_c = ctx.shape[1]
    depth = w_in.shape[0]
    assert (bsz, d, seq // GRID_W) == (2, D_MODEL, GRID_W) and seq == FFT_N1 * FFT_N2 // 2

    cvec = jnp.concatenate([c, c_ctx[None], jnp.zeros((8 - bsz - 1, d), F32)], axis=0)
    mod = _modulation(cvec, w_mod, b_mod)
    shift, scale, gate = mod[..., :d], mod[..., d:2 * d], mod[..., 2 * d:]

    w_t = _bf(jnp.swapaxes(w_in, 1, 2))
    w_lr = w_t[:, N_QKV:N_QKV + N_LR]
    w_lr3 = jnp.concatenate([w_lr, w_lr, w_lr, jnp.zeros_like(w_lr)], axis=1)
    w_out_b = _bf(w_out)
    w_glu_b = _bf(s5_w_glu)

    wg = gla_w_gate.reshape(depth, 2, 16, GLA_HEADS, GLA_DK).transpose(0, 1, 3, 2, 4)
    wg_pad = jnp.stack([jnp.pad(wg[:, 0], ((0, 0), (0, 0), (0, 112), (0, 0))),
                        jnp.pad(wg[:, 1], ((0, 0), (0, 0), (16, 96), (0, 0)))], axis=1)
    wg_hi, wg_lo = _split(wg_pad[..., :N_LR, :])
    wg3 = jnp.concatenate([wg_hi, wg_hi, wg_lo, jnp.zeros_like(wg_hi)], axis=-2)
    bg = gla_b_gate.reshape(depth, 2, GLA_HEADS, 1, GLA_DK)

    tables = _dft_tables()
    deltas = jnp.abs(jnp.linspace(HY_MIN_DECAY, HY_MAX_DECAY, HY_WIDTH, dtype=F32))
    deltas2 = jnp.tile(deltas, 2).reshape(1, 2 * HY_WIDTH)
    pad_k = lambda w: jnp.pad(w, ((0, 0), (0, 128 - w.shape[1]), (0, 128 - w.shape[2])))
    pad_v = lambda v: jnp.pad(v, ((0, 0), (0, 128 - v.shape[1]))).reshape(depth, 1, 128)
    w3 = hy_w3.reshape(depth, HY_FFN, 2, 2, HY_WIDTH).transpose(0, 3, 1, 2, 4)
    w3 = jnp.pad(w3.reshape(depth, 2, HY_FFN, 2 * HY_WIDTH), ((0, 0), (0, 0), (0, 128 - HY_FFN), (0, 0)))
    filt_w = (pad_k(hy_w1), pad_v(hy_b1), pad_v(hy_f1), pad_k(hy_w2), pad_v(hy_b2), pad_v(hy_f2), w3, deltas2)
    taps = _filter_taps(_filter_features(seq), *filt_w, tmf=512)
    spec = _filter_spectrum(tables, taps)
    taps_c = _filter_taps(_filter_features(seq_c), *filt_w, tmf=seq_c)
    ctabs = _ctx_dft_tables(seq_c)

    s5_ops = _s5_operators(s5_lam_re, s5_lam_im, s5_log_step, s5_b_re, s5_b_im, s5_c_re, s5_c_im)

    xc = ctx
    for l in range(depth):
        last = l == depth - 1
        ctx_out = not last
        colmajor = l % 2 == 1
        pre = g_pre[l][None, :] * (1.0 + scale[l])
        sc_l, sh_l = pre[:bsz, None, :], shift[l, :bsz, None, :]
        sc_c = jnp.broadcast_to(pre[bsz][None, None, :], (bsz, 1, d))
        sh_c = jnp.broadcast_to(shift[l, bsz][None, None, :], (bsz, 1, d))
        qkv, p, lr = _inproj(x, sc_l, sh_l, w_t, w_lr3, l, n=N_MAIN, colmajor=colmajor, tm=1024, tn=512)
        qkvc, pc, lrc = _inproj(xc, sc_c, sh_c, w_t, w_lr3, l, n=N_MAIN if ctx_out else N_STATE,
                                colmajor=False, tm=seq_c, tn=512)

        y_gla, yc_gla = _gla(qkv, p, lr, qkvc, pc, lrc, wg3[l], bg[l], gla_norm[l][None, :],
                             ctx_out=ctx_out)

        y_s5, yc_s5 = _s5(p, pc, s5_ops, l, s5_d[l][None, :], w_glu_b[l], s5_b_glu[l][None, :], ctx_out=ctx_out)

        v, x1, x2 = _hy_pre(p, hy_conv_w[l], hy_conv_b[l][None, :])
        z = _hy_long_conv(tables, spec, l, 0, v, x1, hy_d[l, 0])
        y_hy = _hy_long_conv(tables, spec, l, 1, z, x2, hy_d[l, 1])

        x_new = _outproj(x, y_gla, y_s5, y_hy, w_out_b, l, g_post[l][None, :], gate[l, :bsz, None, :],
                         colmajor=colmajor, tm=512)
        if ctx_out:
            vc, x1c, x2c = _hy_pre(pc, hy_conv_w[l], hy_conv_b[l][None, :])
            zc = _hyc_long_conv(ctabs, taps_c, l, 0, vc, x1c, hy_d[l, 0])
            yc_hy = _hyc_long_conv(ctabs, taps_c, l, 1, zc, x2c, hy_d[l, 1])
            gate_c = jnp.broadcast_to(gate[l, bsz][None, None, :], (bsz, 1, d))
            xc = _outproj(xc, yc_gla, yc_s5, yc_hy, w_out_b, l, g_post[l][None, :], gate_c,
                          colmajor=False, tm=seq_c)
        x = x_new
    return x
```

```python
import functools
import math

import numpy as np
import jax
import jax.numpy as jnp
from jax import lax
from jax.experimental import pallas as pl
from jax.experimental.pallas import tpu as pltpu

F32 = jnp.float32
BF16 = jnp.bfloat16

D_MODEL = 2048
GRID_W = 64
EPS = 1e-6

GLA_HEADS = 4
GLA_DK = 128
GLA_DV = 256
GLA_TAU = 16.0
GLA_CHUNK = 64
GLA_BLOCK = 8

S5_WIDTH = 512
S5_GROUP = 16
S5_GROUPS = 32
S5_STATE = 64
S5_T = 16
S5_COLS = 640
S5_CB = 128

HY_WIDTH = 512
HY_EMB = 33
HY_BANDS = 16
HY_FFN = 64
HY_MIN_DECAY = math.log(1e-2) / 0.3
HY_MAX_DECAY = math.log(1e-2) / 1.5
FFT_N1 = 64
FFT_N2 = 128
HY_CT = 128
HY_KB = 8
HY_NB = 8
HY_PITCH = 136
HY_PASSES = 1

N_MAIN = 6144
N_STATE = 2560
N_QKV = 2048
N_LR = 32
COL_S5U, COL_S5G = 0, 3
COL_GLA_GATE = 2
COL_HY = 16

VMEM_LIMIT_BYTES = 56 * 1024 * 1024


def _cparams(*sem):
    return pltpu.CompilerParams(dimension_semantics=sem, vmem_limit_bytes=VMEM_LIMIT_BYTES)


def _bf(x):
    return x.astype(BF16)


def _split(x):
    hi = _bf(x)
    return hi, _bf(x - hi.astype(F32))


def _split3(x):
    a = _bf(x)
    r = x - a.astype(F32)
    b = _bf(r)
    return a, b, _bf(r - b.astype(F32))


_NN = (((1,), (0,)), ((), ()))
_NT = (((1,), (1,)), ((), ()))
_TN = (((0,), (0,)), ((), ()))


def _mm(a, b, dims=_NN):
    return lax.dot_general(a, b, dims, preferred_element_type=F32)


def _mm3(a_hi, a_lo, b, dims=_NN):
    b_hi, b_lo = _split(b)
    return _mm(a_hi, b_hi, dims) + _mm(a_hi, b_lo, dims) + _mm(a_lo, b_hi, dims)


def _mod_kernel(s_ref, w_ref, b_ref, o_ref):
    s = s_ref[...]
    s = s * jax.nn.sigmoid(s)
    s_hi, s_lo = _split(s)
    o_ref[0] = _mm3(s_hi, s_lo, w_ref[0]) + b_ref[0]


def _modulation(cvec, w_mod, b_mod):
    depth, d, n = w_mod.shape
    tn = 512
    return pl.pallas_call(
        _mod_kernel,
        grid=(depth, n // tn),
        in_specs=[
            pl.BlockSpec((8, d), lambda l, j: (0, 0)),
            pl.BlockSpec((1, d, tn), lambda l, j: (l, 0, j)),
            pl.BlockSpec((1, 1, tn), lambda l, j: (l, 0, j)),
        ],
        out_specs=pl.BlockSpec((1, 8, tn), lambda l, j: (l, 0, j)),
        out_shape=jax.ShapeDtypeStruct((depth, 8, n), F32),
        compiler_params=_cparams("parallel", "parallel"),
        name="modulation",
    )(cvec, w_mod, b_mod.reshape(depth, 1, n))


def _inproj_kernel(*refs, colmajor, tm, tn):
    if colmajor:
        x_ref, sc_ref, sh_ref, w_ref, wlr_ref, perm_ref, qkv_ref, p_ref, lr_ref, h_scr = refs
    else:
        x_ref, sc_ref, sh_ref, w_ref, wlr_ref, qkv_ref, p_ref, lr_ref, h_scr = refs
    j = pl.program_id(2)

    @pl.when(j == 0)
    def _():
        def norm(xb):
            ms = jnp.mean(xb * xb, axis=-1, keepdims=True)
            return _bf(xb * lax.rsqrt(ms + EPS) * sc_ref[0] + sh_ref[0])

        if colmajor:
            h = norm(x_ref[0].reshape(tm, D_MODEL))
            for c in range(0, D_MODEL, 512):
                h_scr[:, c:c + 512] = _bf(_mm(perm_ref[...], h[:, c:c + 512]))
        else:
            h_scr[...] = norm(x_ref[0])
        lr_ref[0] = _mm(h_scr[...], wlr_ref[0], _NT)

    res = _mm(h_scr[...], w_ref[...], _NT)
    n_qkv = N_QKV // tn

    @pl.when(j < n_qkv)
    def _():
        qkv_ref[0] = _bf(res)

    @pl.when(j >= n_qkv)
    def _():
        p_ref[0] = res


def _inproj(x, scale, shift, w_t, w_lr3, layer, *, n, colmajor, tm, tn):
    bsz, seq, d = x.shape
    n_qkv = N_QKV // tn
    extra_specs, extra_args = [], []
    if colmajor:
        cols = tm // GRID_W
        x_in = x.reshape(bsz, GRID_W, seq // GRID_W, d)
        x_spec = pl.BlockSpec((1, GRID_W, cols, d), lambda b, m, j: (b, 0, m, 0))
        r_new = jnp.arange(tm, dtype=jnp.int32)[:, None]
        r_old = jnp.arange(tm, dtype=jnp.int32)[None, :]
        perm = _bf((r_old == (r_new % GRID_W) * cols + r_new // GRID_W).astype(F32))
        extra_specs, extra_args = [pl.BlockSpec((tm, tm), lambda b, m, j: (0, 0))], [perm]
    else:
        x_in = x
        x_spec = pl.BlockSpec((1, tm, d), lambda b, m, j: (b, m, 0))
    return pl.pallas_call(
        functools.partial(_inproj_kernel, colmajor=colmajor, tm=tm, tn=tn),
        grid=(bsz, seq // tm, n // tn),
        in_specs=[
            x_spec,
            pl.BlockSpec((1, 1, d), lambda b, m, j: (b, 0, 0)),
            pl.BlockSpec((1, 1, d), lambda b, m, j: (b, 0, 0)),
            pl.BlockSpec((pl.Squeezed(), pl.Element(tn), pl.Element(d)),
                         lambda b, m, j: (layer, pl.multiple_of(j * tn + jnp.where(j >= n_qkv, N_LR, 0), N_LR), 0)),
            pl.BlockSpec((1, 128, d), lambda b, m, j: (layer, 0, 0)),
        ] + extra_specs,
        out_specs=[
            pl.BlockSpec((1, tm, tn), lambda b, m, j: (b, m, jnp.minimum(j, n_qkv - 1))),
            pl.BlockSpec((1, tm, tn), lambda b, m, j: (b, m, jnp.maximum(j - n_qkv, 0))),
            pl.BlockSpec((1, tm, 128), lambda b, m, j: (b, m, 0)),
        ],
        out_shape=[jax.ShapeDtypeStruct((bsz, seq, N_QKV), BF16),
                   jax.ShapeDtypeStruct((bsz, seq, n - N_QKV), F32),
                   jax.ShapeDtypeStruct((bsz, seq, 128), F32)],
        scratch_shapes=[pltpu.VMEM((tm, d), BF16)],
        compiler_params=_cparams("parallel", "parallel", "arbitrary"),
        name="inproj",
    )(x_in, scale, shift, w_t, w_lr3, *extra_args)


def _gla_block(q, k, v, lr, st, wg, bg, direction, mask, rowc, hi_lanes, nb):
    c = GLA_CHUNK
    n = c * nb
    lr_hi, lr_lo = _split(lr)
    z = _mm(jnp.where(hi_lanes, lr_hi, lr_lo), wg) + bg
    g = (jnp.minimum(z, 0.0) - jnp.log(1.0 + jnp.exp(-jnp.abs(z)))) * (1.0 / GLA_TAU)
    cum = g
    s = 1
    while s < c:
        if direction == 0:
            cum = cum + jnp.where(rowc >= s, pltpu.roll(cum, s, 0), 0.0)
        else:
            cum = cum + jnp.where(rowc < c - s, pltpu.roll(cum, n - s, 0), 0.0)
        s *= 2
    cum3 = cum.reshape(nb, c, GLA_DK)
    tot3 = cum3[:, c - 1:c, :] if direction == 0 else cum3[:, 0:1, :]
    q3 = q.reshape(nb, c, GLA_DK)
    k3 = k.reshape(nb, c, GLA_DK)
    qg3 = _bf(q3 * jnp.exp(cum3))
    kg3 = _bf(k3 * jnp.exp(-cum3))
    kd3 = _bf(k3 * jnp.exp(tot3 - cum3))
    a3 = jnp.exp(tot3)
    vb3 = _bf(v).reshape(nb, c, GLA_DV)
    att = [jnp.where(mask, _mm(qg3[i], kg3[i], _NT), 0.0) for i in range(nb)]
    o_in = [_mm(_bf(att[i]), vb3[i]) for i in range(nb)]
    d_st = [_mm(vb3[i], kd3[i], _TN) for i in range(nb)]
    outs = [None] * nb
    for i in (range(nb) if direction == 0 else range(nb - 1, -1, -1)):
        outs[i] = o_in[i] + _mm(qg3[i], _bf(st), _NT)
        st = st * a3[i] + d_st[i]
    return jnp.concatenate(outs, axis=0), st


def _gla_kernel(*refs, ctx_out, seq, seq_c):
    if ctx_out:
        (q_ref, k_ref, v_ref, gt_ref, lr_ref, qc_ref, kc_ref, vc_ref, gtc_ref, lrc_ref,
         wg_ref, bg_ref, gn_ref, y_ref, yc_ref) = refs
    else:
        (q_ref, k_ref, v_ref, gt_ref, lr_ref, qc_ref, kc_ref, vc_ref, lrc_ref,
         wg_ref, bg_ref, gn_ref, y_ref) = refs
        gtc_ref = yc_ref = None
    c = GLA_CHUNK
    scale = GLA_DK ** -0.5
    row = lax.broadcasted_iota(jnp.int32, (c, c), 0)
    col = lax.broadcasted_iota(jnp.int32, (c, c), 1)
    masks = (col <= row, col >= row)
    gn = gn_ref[...]

    def run(refs4, n_rows, st, direction, store):
        qr, kr, vr, lrr = refs4
        nb = min(GLA_BLOCK, n_rows // c)
        n = c * nb
        rowc = lax.broadcasted_iota(jnp.int32, (n, GLA_DK), 0) % c
        hi_lanes = (lax.broadcasted_iota(jnp.int32, (n, 128), 1) // N_LR) % 2 == 0
        wg = wg_ref[direction, 0]
        bg = bg_ref[direction, 0]
        n_blocks = n_rows // n

        def body(i, st):
            bi = i if direction == 0 else n_blocks - 1 - i
            rows = pl.ds(pl.multiple_of(bi * n, n), n)
            q = qr[0, rows, :].astype(F32) * scale
            k = kr[0, rows, :].astype(F32)
            v = vr[0, rows, :].astype(F32)
            o, st = _gla_block(q, k, v, lrr[0, rows, :], st, wg, bg, direction,
                               masks[direction], rowc, hi_lanes, nb)
            store(rows, q, k, v, o)
            return st

        return lax.fori_loop(0, n_blocks, body, st)

    def store_fwd(out_ref):
        def f(rows, q, k, v, o):
            if out_ref is not None:
                out_ref[0, rows, :] = o
        return f

    def store_bwd(out_ref, gate_ref):
        def f(rows, q, k, v, o):
            if out_ref is None:
                return
            y = out_ref[0, rows, :] + o - jnp.sum(q * k, axis=-1, keepdims=True) * v
            ms = jnp.mean(y * y, axis=-1, keepdims=True)
            y = y * lax.rsqrt(ms + EPS) * gn
            gt = gate_ref[0, rows, :]
            out_ref[0, rows, :] = y * (gt * jax.nn.sigmoid(gt))
        return f

    lat = (q_ref, k_ref, v_ref, lr_ref)
    ctx = (qc_ref, kc_ref, vc_ref, lrc_ref)
    zero = jnp.zeros((GLA_DV, GLA_DK), F32)
    st = run(ctx, seq_c, zero, 0, store_fwd(yc_ref))
    run(lat, seq, st, 0, store_fwd(y_ref))
    st = run(ctx, seq_c, zero, 1, store_bwd(yc_ref, gtc_ref))
    run(lat, seq, st, 1, store_bwd(y_ref, gt_ref))


def _gla(qkv, p, lr, qkvc, pc, lrc, wg, bg, gnorm, *, ctx_out):
    bsz, seq, _ = qkv.shape
    seq_c = qkvc.shape[1]
    h = GLA_HEADS

    def specs(n):
        return [
            pl.BlockSpec((1, n, 128), lambda b, i: (b, 0, i)),
            pl.BlockSpec((1, n, 128), lambda b, i: (b, 0, 4 + i)),
            pl.BlockSpec((1, n, 256), lambda b, i: (b, 0, 4 + i)),
            pl.BlockSpec((1, n, 256), lambda b, i: (b, 0, COL_GLA_GATE + i)),
            pl.BlockSpec((1, n, 128), lambda b, i: (b, 0, 0)),
        ]

    lat_specs = specs(seq)
    ctx_specs = specs(seq_c)
    lat_args = [qkv, qkv, qkv, p, lr]
    ctx_args = [qkvc, qkvc, qkvc, pc, lrc]
    if not ctx_out:
        del ctx_specs[3], ctx_args[3]
    w_specs = [
        pl.BlockSpec((2, 1, 128, 128), lambda b, i: (0, i, 0, 0)),
        pl.BlockSpec((2, 1, 1, 128), lambda b, i: (0, i, 0, 0)),
        pl.BlockSpec((1, GLA_DV), lambda b, i: (0, 0)),
    ]
    out_specs = [pl.BlockSpec((1, seq, 256), lambda b, i: (b, 0, i))]
    out_shape = [jax.ShapeDtypeStruct((bsz, seq, h * GLA_DV), F32)]
    if ctx_out:
        out_specs.append(pl.BlockSpec((1, seq_c, 256), lambda b, i: (b, 0, i)))
        out_shape.append(jax.ShapeDtypeStruct((bsz, seq_c, h * GLA_DV), F32))
    outs = pl.pallas_call(
        functools.partial(_gla_kernel, ctx_out=ctx_out, seq=seq, seq_c=seq_c),
        grid=(bsz, h),
        in_specs=lat_specs + ctx_specs + w_specs,
        out_specs=out_specs,
        out_shape=out_shape,
        compiler_params=_cparams("parallel", "parallel"),
        name="gla",
    )(*lat_args, *ctx_args, wg, bg, gnorm)
    return (outs[0], outs[1]) if ctx_out else (outs[0], None)


def _mm3f(a, b, dims=_NN):
    a_hi, a_lo = _split(a)
    return _mm3(a_hi, a_lo, b, dims)


def _s5_ops_kernel(a_ref, b_ref, lr_ref, li_ref, btr_ref, bti_ref, cr_ref, ci_ref,
                   kt_ref, bpr_ref, bpi_ref, cp_ref, ltr_ref, lti_ref):
    t_len, h_n, p_n = S5_T, S5_GROUP, S5_STATE
    rows = t_len * h_n
    lane = lax.broadcasted_iota(jnp.int32, (rows, 2 * p_n), 1)
    own = (lane < p_n, lane >= p_n)
    lane_blk = lax.broadcasted_iota(jnp.int32, (rows, rows), 1) // h_n
    kk = lax.broadcasted_iota(jnp.int32, (24, 2 * p_n), 0).astype(F32)
    for d in range(2):
        a, b = a_ref[0, d, 0], b_ref[0, d, 0]
        lr, li = lr_ref[0, d, 0], li_ref[0, d, 0]
        mag = jnp.exp(kk * a)
        pwr, pwi = mag * jnp.cos(kk * b), mag * jnp.sin(kk * b)

        def times_pw(k, mr, mi):
            pr, pi = pwr[k:k + 1], pwi[k:k + 1]
            return pr * mr - pi * mi, pr * mi + pi * mr

        x, y = pwr[1:2] - 1.0, pwi[1:2]
        den = lr * lr + li * li
        cfr, cfi = (x * lr + y * li) / den, (y * lr - x * li) / den
        btr, bti = btr_ref[0, d, 0], bti_ref[0, d, 0]
        bbr, bbi = cfr * btr - cfi * bti, cfr * bti + cfi * btr
        cr, ci = cr_ref[0, d, 0], ci_ref[0, d, 0]

        def stack(power_of_t, mr, mi):
            parts = [times_pw(power_of_t(t), mr, mi) for t in range(t_len)]
            return (jnp.concatenate([p[0] for p in parts], axis=0),
                    jnp.concatenate([p[1] for p in parts], axis=0))

        bpr, bpi = stack((lambda t: t_len - 1 - t) if d == 0 else (lambda t: t), bbr, bbi)
        for src, dst in ((bpr, bpr_ref), (bpi, bpi_ref)):
            z = jnp.concatenate([jnp.where(own[0], src, 0.0), jnp.where(own[1], src, 0.0)], axis=0)
            dst[0, d, 0] = _bf(z.T)
        cpr, cpi = stack((lambda t: t + 1) if d == 0 else (lambda t: t_len - t), cr, ci)
        clr, cli = stack((lambda j: j) if d == 0 else (lambda j: t_len - 1 - j), cr, ci)
        bbr_t = jnp.concatenate([bbr] * t_len, axis=0)
        bbi_t = jnp.concatenate([bbi] * t_len, axis=0)
        for e in range(2):
            cp_ref[0, d, e] = _bf(jnp.concatenate(
                [jnp.where(own[e], cpr, 0.0), jnp.where(own[e], -cpi, 0.0)], axis=1))
            kc = (_mm3f(jnp.where(own[e], clr, 0.0), bbr_t, _NT)
                  - _mm3f(jnp.where(own[e], cli, 0.0), bbi_t, _NT))
            acc = jnp.zeros((rows, rows), F32)
            for t in range(t_len):
                if d == 0:
                    pieces = [jnp.zeros((h_n * t, rows), F32), kc[:rows - h_n * t]]
                else:
                    s = h_n * (t_len - 1 - t)
                    pieces = [kc[s:], jnp.zeros((s, rows), F32)]
                shifted = jnp.concatenate([p for p in pieces if p.shape[0]], axis=0)
                acc = jnp.where(lane_blk == t, shifted, acc)
            kt_ref[0, d, e] = _bf(acc)
        ltr_ref[0, d, 0] = pwr[t_len:t_len + 1]
        lti_ref[0, d, 0] = pwi[t_len:t_len + 1]


def _s5_operators(lam_re, lam_im, log_step, b_re, b_im, c_re, c_im):
    depth = lam_re.shape[0]
    g_n, p_n, h_n = S5_GROUPS, S5_STATE, S5_GROUP
    pairs, rows = g_n // 2, S5_T * h_n
    dt = jnp.exp(log_step)[..., None]
    vec = lambda v: v.reshape(depth, 2, pairs, 1, 2 * p_n)
    bt = lambda m: m.reshape(depth, 2, pairs, 2, p_n, h_n).transpose(0, 1, 2, 5, 3, 4).reshape(depth, 2, pairs, h_n, 2 * p_n)
    ct = lambda m: m.reshape(depth, 2, pairs, 2, h_n, p_n).transpose(0, 1, 2, 4, 3, 5).reshape(depth, 2, pairs, h_n, 2 * p_n)
    vspec = pl.BlockSpec((1, 2, 1, 1, 2 * p_n), lambda l, q: (l, 0, q, 0, 0))
    mspec = pl.BlockSpec((1, 2, 1, h_n, 2 * p_n), lambda l, q: (l, 0, q, 0, 0))
    sq_spec = pl.BlockSpec((1, 2, 2, rows, rows), lambda l, q: (l, 0, q, 0, 0))
    bp_spec = pl.BlockSpec((1, 2, 1, 2 * p_n, 2 * rows), lambda l, q: (l, 0, q, 0, 0))
    sq = jax.ShapeDtypeStruct((depth, 2, g_n, rows, rows), BF16)
    bp = jax.ShapeDtypeStruct((depth, 2, pairs, 2 * p_n, 2 * rows), BF16)
    lt = jax.ShapeDtypeStruct((depth, 2, pairs, 1, 2 * p_n), F32)
    ktoep, bpr, bpi, cpad, ltr, lti = pl.pallas_call(
        _s5_ops_kernel,
        grid=(depth, pairs),
        in_specs=[vspec] * 4 + [mspec] * 4,
        out_specs=[sq_spec, bp_spec, bp_spec, sq_spec, vspec, vspec],
        out_shape=[sq, bp, bp, sq, lt, lt],
        compiler_params=_cparams("parallel", "parallel"),
        name="s5_operators",
    )(vec(lam_re * dt), vec(lam_im * dt), vec(lam_re), vec(lam_im), bt(b_re), bt(b_im), ct(c_re), ct(c_im))
    flat = lambda v: v.reshape(depth, 2, 1, g_n * p_n)
    return ktoep, bpr, bpi, cpad, flat(ltr), flat(lti)


def _s5_in_kernel(u_ref, uc_ref, o_ref):
    j = pl.program_id(0)
    last = pl.num_programs(0) - 1

    def emit(slab_of_t):
        for t in range(S5_T):
            o_ref[:, t * S5_GROUP:(t + 1) * S5_GROUP, :] = _bf(
                slab_of_t(t).T.reshape(S5_GROUPS, S5_GROUP, S5_CB))

    @pl.when(j < last)
    def _():
        emit(lambda t: u_ref[0, :, t, :])

    @pl.when(j == last)
    def _():
        pad = jnp.zeros((S5_CB - 2 * uc_ref.shape[1], S5_WIDTH), F32)
        emit(lambda t: jnp.concatenate([uc_ref[0, :, t, :], uc_ref[1, :, t, :], pad], axis=0))


def _s5_lat_index(n_half):
    def index(j):
        jj = jnp.minimum(j, 2 * n_half - 1)
        return jj // n_half, jj % n_half
    return index


def _s5_relayout_in(p, pc):
    bsz, seq, n = p.shape
    seq_c, n_c = pc.shape[1], pc.shape[2]
    n_lat, n_ctx = seq // S5_T, seq_c // S5_T
    n_half = n_lat // S5_CB
    idx = _s5_lat_index(n_half)
    return pl.pallas_call(
        _s5_in_kernel,
        grid=(bsz * n_half + 1,),
        in_specs=[
            pl.BlockSpec((1, S5_CB, S5_T, 512), lambda j: (*idx(j), 0, COL_S5U)),
            pl.BlockSpec((bsz, n_ctx, S5_T, 512), lambda j: (0, 0, 0, COL_S5U)),
        ],
        out_specs=pl.BlockSpec((S5_GROUPS, S5_T * S5_GROUP, S5_CB), lambda j: (0, 0, j)),
        out_shape=jax.ShapeDtypeStruct((S5_GROUPS, S5_T * S5_GROUP, S5_COLS), BF16),
        compiler_params=_cparams("arbitrary"),
        name="s5_relayout_in",
    )(p.reshape(bsz, n_lat, S5_T, n), pc.reshape(bsz, n_ctx, S5_T, n_c))


def _s5_state_in_kernel(ug_ref, bre_ref, bim_ref, vre_ref, vim_ref):
    u = ug_ref[...].reshape(2 * S5_T * S5_GROUP, S5_COLS)
    for direction in range(2):
        vre_ref[direction] = _mm(bre_ref[0, direction, 0], u)
        vim_ref[direction] = _mm(bim_ref[0, direction, 0], u)


def _s5_state_in(ug, b_pair_re, b_pair_im, layer):
    pairs = S5_GROUPS // 2
    rows = S5_T * S5_GROUP
    lanes = S5_GROUPS * S5_STATE
    out = jax.ShapeDtypeStruct((2, lanes, S5_COLS), F32)
    return pl.pallas_call(
        _s5_state_in_kernel,
        grid=(pairs,),
        in_specs=[
            pl.BlockSpec((2, rows, S5_COLS), lambda q: (q, 0, 0)),
            pl.BlockSpec((1, 2, 1, 128, 2 * rows), lambda q: (layer, 0, q, 0, 0)),
            pl.BlockSpec((1, 2, 1, 128, 2 * rows), lambda q: (layer, 0, q, 0, 0)),
        ],
        out_specs=[pl.BlockSpec((2, 128, S5_COLS), lambda q: (0, q, 0))] * 2,
        out_shape=[out, out],
        compiler_params=_cparams("parallel"),
        name="s5_state_in",
    )(ug, b_pair_re, b_pair_im)


def _s5_scan_kernel(vre_ref, vim_ref, lre_ref, lim_ref, xre_ref, xim_ref, tre_scr, tim_scr, *, n_lat, n_ctx):
    lanes = lre_ref.shape[-1]
    for d in range(2):
        tre_scr[d] = vre_ref[d].T
        tim_scr[d] = vim_ref[d].T
    a_re = [lre_ref[0, d] for d in range(2)]
    a_im = [lim_ref[0, d] for d in range(2)]

    def step(d, row, xr, xi):
        sl = pl.ds(row, 1)
        xre_ref[d, sl, :] = xr
        xim_ref[d, sl, :] = xi
        return (a_re[d] * xr - a_im[d] * xi + tre_scr[d, sl, :],
                a_re[d] * xi + a_im[d] * xr + tim_scr[d, sl, :])

    def run(bases, n, carry):
        def body(i, carry):
            out = []
            for d in range(2):
                j = i if d == 0 else n - 1 - i
                for b in range(2):
                    xr, xi = carry[2 * d + b]
                    out.append(step(d, bases[b] + j, xr, xi))
            return tuple(out)
        return lax.fori_loop(0, n, body, carry)

    zero = jnp.zeros((1, lanes), F32)
    carry = run((2 * n_lat, 2 * n_lat + n_ctx), n_ctx, ((zero, zero),) * 4)
    run((0, n_lat), n_lat, carry)
    pad0 = 2 * n_lat + 2 * n_ctx
    for d in range(2):
        xre_ref[d, pad0:, :] = jnp.zeros((S5_COLS - pad0, lanes), F32)
        xim_ref[d, pad0:, :] = jnp.zeros((S5_COLS - pad0, lanes), F32)


def _s5_scan(vre, vim, lam_re, lam_im, layer, *, n_lat, n_ctx):
    lanes = vre.shape[1]
    tl = 512
    vspec = pl.BlockSpec((2, tl, S5_COLS), lambda j: (0, j, 0))
    xspec = pl.BlockSpec((2, S5_COLS, tl), lambda j: (0, 0, j))
    lspec = pl.BlockSpec((1, 2, 1, tl), lambda j: (layer, 0, 0, j))
    out = jax.ShapeDtypeStruct((2, S5_COLS, lanes), F32)
    return pl.pallas_call(
        functools.partial(_s5_scan_kernel, n_lat=n_lat, n_ctx=n_ctx),
        grid=(lanes // tl,),
        in_specs=[vspec, vspec, lspec, lspec],
        out_specs=[xspec, xspec],
        out_shape=[out, out],
        scratch_shapes=[pltpu.VMEM((2, S5_COLS, tl), F32), pltpu.VMEM((2, S5_COLS, tl), F32)],
        compiler_params=_cparams("parallel"),
        name="s5_scan",
    )(vre, vim, lam_re, lam_im)


def _s5_readout_kernel(ug_ref, kt_ref, cp_ref, xre_ref, xim_ref, y_ref):
    for e in range(2):
        u = ug_ref[e]
        acc = None
        for direction in range(2):
            xcat = _bf(jnp.concatenate([xre_ref[direction], xim_ref[direction]], axis=1))
            term = _mm(kt_ref[0, direction, e], u) + _mm(cp_ref[0, direction, e], xcat, _NT)
            acc = term if acc is None else acc + term
        y_ref[e] = acc


def _s5_readout(ug, ktoep, cpad, xre, xim, layer):
    pairs = S5_GROUPS // 2
    rows = S5_T * S5_GROUP
    return pl.pallas_call(
        _s5_readout_kernel,
        grid=(pairs,),
        in_specs=[
            pl.BlockSpec((2, rows, S5_COLS), lambda q: (q, 0, 0)),
            pl.BlockSpec((1, 2, 2, rows, rows), lambda q: (layer, 0, q, 0, 0)),
            pl.BlockSpec((1, 2, 2, rows, rows), lambda q: (layer, 0, q, 0, 0)),
            pl.BlockSpec((2, S5_COLS, 128), lambda q: (0, 0, q)),
            pl.BlockSpec((2, S5_COLS, 128), lambda q: (0, 0, q)),
        ],
        out_specs=pl.BlockSpec((2, rows, S5_COLS), lambda q: (q, 0, 0)),
        out_shape=jax.ShapeDtypeStruct((S5_GROUPS, rows, S5_COLS), F32),
        compiler_params=_cparams("parallel"),
        name="s5_readout",
    )(ug, ktoep, cpad, xre, xim)


def _s5_out_kernel(*refs, ctx_out):
    if ctx_out:
        yg_ref, u_ref, g_ref, uc_ref, gc_ref, d_ref, w_ref, b_ref, o_ref, oc_ref = refs
    else:
        yg_ref, u_ref, g_ref, d_ref, w_ref, b_ref, o_ref = refs
    d = d_ref[...]
    w = w_ref[...]
    bias = b_ref[...]

    def finish(yy, u, gate):
        yy = yy + d * u
        yg = jax.nn.gelu(yy)
        out = yg * jax.nn.sigmoid(_mm(_bf(yg), w) + bias)
        return out * (gate * jax.nn.sigmoid(gate))

    def y_of(t):
        return yg_ref[:, t * S5_GROUP:(t + 1) * S5_GROUP, :].reshape(S5_WIDTH, S5_CB).T

    def lat():
        for t in range(S5_T):
            o_ref[0, :, t, :] = finish(y_of(t), u_ref[0, :, t, :], g_ref[0, :, t, :])

    if not ctx_out:
        lat()
        return
    j = pl.program_id(0)
    last = pl.num_programs(0) - 1
    pl.when(j < last)(lat)

    @pl.when(j == last)
    def _():
        n_ctx = uc_ref.shape[1]
        for t in range(S5_T):
            y = y_of(t)
            for b in range(2):
                oc_ref[b, :, t, :] = finish(y[b * n_ctx:(b + 1) * n_ctx], uc_ref[b, :, t, :], gc_ref[b, :, t, :])


def _s5_relayout_out(yg, p, pc, d_skip, w_glu, b_glu, *, ctx_out):
    bsz, seq, n = p.shape
    seq_c, n_c = pc.shape[1], pc.shape[2]
    n_lat, n_ctx = seq // S5_T, seq_c // S5_T
    n_half = n_lat // S5_CB
    idx = _s5_lat_index(n_half)
    pv = p.reshape(bsz, n_lat, S5_T, n)
    lat_block = (1, S5_CB, S5_T, 512)
    in_specs = [
        pl.BlockSpec((S5_GROUPS, S5_T * S5_GROUP, S5_CB), lambda j: (0, 0, j)),
        pl.BlockSpec(lat_block, lambda j: (*idx(j), 0, COL_S5U)),
        pl.BlockSpec(lat_block, lambda j: (*idx(j), 0, COL_S5G)),
    ]
    args = [yg, pv, pv]
    out_specs = [pl.BlockSpec(lat_block, lambda j: (*idx(j), 0, 0))]
    out_shape = [jax.ShapeDtypeStruct((bsz, n_lat, S5_T, S5_WIDTH), F32)]
    if ctx_out:
        pcv = pc.reshape(bsz, n_ctx, S5_T, n_c)
        ctx_block = (bsz, n_ctx, S5_T, 512)
        in_specs += [
            pl.BlockSpec(ctx_block, lambda j: (0, 0, 0, COL_S5U)),
            pl.BlockSpec(ctx_block, lambda j: (0, 0, 0, COL_S5G)),
        ]
        args += [pcv, pcv]
        out_specs.append(pl.BlockSpec(ctx_block, lambda j: (0, 0, 0, 0)))
        out_shape.append(jax.ShapeDtypeStruct((bsz, n_ctx, S5_T, S5_WIDTH), F32))
    in_specs += [
        pl.BlockSpec((1, S5_WIDTH), lambda j: (0, 0)),
        pl.BlockSpec((S5_WIDTH, S5_WIDTH), lambda j: (0, 0)),
        pl.BlockSpec((1, S5_WIDTH), lambda j: (0, 0)),
    ]
    args += [d_skip, w_glu, b_glu]
    outs = pl.pallas_call(
        functools.partial(_s5_out_kernel, ctx_out=ctx_out),
        grid=(bsz * n_half + (1 if ctx_out else 0),),
        in_specs=in_specs,
        out_specs=out_specs,
        out_shape=out_shape,
        compiler_params=_cparams("arbitrary"),
        name="s5_relayout_out",
    )(*args)
    y = outs[0].reshape(bsz, seq, S5_WIDTH)
    yc = outs[1].reshape(bsz, seq_c, S5_WIDTH) if ctx_out else None
    return y, yc


def _s5(p, pc, ops, layer, d_skip, w_glu, b_glu, *, ctx_out):
    ktoep, b_pair_re, b_pair_im, cpad, lam_re, lam_im = ops
    ug = _s5_relayout_in(p, pc)
    vre, vim = _s5_state_in(ug, b_pair_re, b_pair_im, layer)
    xre, xim = _s5_scan(vre, vim, lam_re, lam_im, layer, n_lat=p.shape[1] // S5_T, n_ctx=pc.shape[1] // S5_T)
    yg = _s5_readout(ug, ktoep, cpad, xre, xim, layer)
    return _s5_relayout_out(yg, p, pc, d_skip, w_glu, b_glu, ctx_out=ctx_out)


def _dft_tables():
    n1, n2 = FFT_N1, FFT_N2
    n = n1 * n2
    a = jnp.arange(n1, dtype=jnp.int32)
    ph1 = (a[:, None] * a[None, :]) % n1
    ang1 = ph1.astype(F32) * (2.0 * math.pi / n1)
    c1, s1 = jnp.cos(ang1), -jnp.sin(ang1)
    h = n1 // 2
    m1 = jnp.concatenate([jnp.concatenate([c1[:, :h], -s1[:, :h]], 1),
                          jnp.concatenate([s1[:, :h], c1[:, :h]], 1)], 0)
    m1_real = jnp.concatenate([c1, s1], 0)
    m3 = m1.T / n
    b = jnp.arange(n2, dtype=jnp.int32)
    ang2 = ((b[:, None] * b[None, :]) % n2).astype(F32) * (2.0 * math.pi / n2)
    fr, fi = jnp.cos(ang2), -jnp.sin(ang2)
    f2 = jnp.concatenate([jnp.concatenate([fr, -fi], 1), jnp.concatenate([fi, fr], 1)], 0)
    angt = ((a[:, None] * b[None, :]) % n).astype(F32) * (2.0 * math.pi / n)
    tw = jnp.stack([jnp.cos(angt), -jnp.sin(angt)], axis=1)
    tw = jnp.broadcast_to(tw[..., None], (n1, 2, n2, HY_CT))
    return tuple(_split(t) for t in (m1, m1_real, m3, f2, f2.T)) + (tw,)


def _filter_features(length):
    t = jnp.linspace(0.0, 1.0, length, dtype=F32)[:, None]
    ang = (2.0 * math.pi / length) * jnp.arange(length, dtype=F32)[:, None]
    bands = jnp.linspace(1e-4, HY_BANDS - 1, HY_BANDS, dtype=F32)[None, :]
    feats = jnp.concatenate([t, jnp.cos(bands * ang), -jnp.sin(bands * ang)], axis=-1)
    feats = jnp.pad(feats, ((0, 0), (0, 128 - HY_EMB)))
    rev = jnp.roll(feats[::-1], 1, axis=0)
    return jnp.stack([feats, rev])


def _filter_kernel(f_ref, w1_ref, b1_ref, f1_ref, w2_ref, b2_ref, f2_ref, w3_ref, dl_ref, o_ref):
    half = pl.program_id(1)
    tile = pl.program_id(2)
    x = f_ref[0]
    x_hi, x_lo = _split(x)
    h = jnp.sin(f1_ref[0] * (_mm3(x_hi, x_lo, w1_ref[0]) + b1_ref[0]))
    h_hi, h_lo = _split(h)
    h = jnp.sin(f2_ref[0] * (_mm3(h_hi, h_lo, w2_ref[0]) + b2_ref[0]))
    y = _mm(_bf(h), _bf(w3_ref[0, 0])) * jnp.exp(-x[:, 0:1] * dl_ref[...])
    rows = lax.broadcasted_iota(jnp.int32, y.shape, 0)
    drop = jnp.logical_and(jnp.logical_and(half == 1, tile == 0), rows == 0)
    o_ref[0] = jnp.where(drop, 0.0, y)


def _filter_taps(feats, w1, b1, f1, w2, b2, f2, w3, deltas, tmf):
    depth = w1.shape[0]
    length = feats.shape[1]
    nt = length // tmf
    wide = w3.shape[-1]
    vec = pl.BlockSpec((1, 1, 128), lambda l, s, i: (l, 0, 0))
    sq = pl.BlockSpec((1, 128, 128), lambda l, s, i: (l, 0, 0))
    return pl.pallas_call(
        _filter_kernel,
        grid=(depth, 2, nt),
        in_specs=[
            pl.BlockSpec((1, tmf, 128), lambda l, s, i: (s, i, 0)),
            sq, vec, vec, sq, vec, vec,
            pl.BlockSpec((1, 1, 128, wide), lambda l, s, i: (l, s, 0, 0)),
            pl.BlockSpec((1, wide), lambda l, s, i: (0, 0)),
        ],
        out_specs=pl.BlockSpec((1, tmf, wide), lambda l, s, i: (l, s * nt + i, 0)),
        out_shape=jax.ShapeDtypeStruct((depth, 2 * length, wide), F32),
        compiler_params=_cparams("parallel", "parallel", "parallel"),
        name="hyena_filter",
    )(feats, w1, b1, f1, w2, b2, f2, w3, deltas)


def _parts(pair):
    return tuple(pair) if HY_PASSES == 3 else tuple(pair[:1])


def _mmp(m, b):
    return _mm(m[0], _bf(b)) if len(m) == 1 else _mm3(m[0], m[1], b)


def _blk_rows(n2):
    return pl.ds(pl.multiple_of(n2 * HY_PITCH, 8), 2 * FFT_N1)


def _pad_rows(n2, n):
    return pl.ds(n2, n, stride=HY_PITCH)


def _pad_copy(src, dst, n):
    for n1 in range(n):
        dst[n1 * HY_PITCH:n1 * HY_PITCH + FFT_N2, :] = src[n1 * FFT_N2:(n1 + 1) * FFT_N2, :]


def _dft_stage1(read_rows, m, a_scr):
    ct = a_scr.shape[-1]

    def body(it, carry):
        n2 = it * HY_NB
        a = _mmp(m, jnp.concatenate([read_rows(n2 + j) for j in range(HY_NB)], axis=1))
        for j in range(HY_NB):
            a_scr[_blk_rows(n2 + j), :] = a[:, j * ct:(j + 1) * ct]
        return carry

    lax.fori_loop(0, FFT_N2 // HY_NB, body, 0)


def _stage2_rows(k1):
    return (pl.ds(k1, FFT_N2, stride=HY_PITCH), pl.ds(k1 + FFT_N1, FFT_N2, stride=HY_PITCH))


def _stage2_load(a_scr, tw_ref, kb):
    rows = [_stage2_rows(kb * HY_KB + i) for i in range(HY_KB)]
    vals = []
    for i, (re, im) in enumerate(rows):
        ar, ai = a_scr[re, :], a_scr[im, :]
        tr, ti = tw_ref[kb * HY_KB + i, 0], tw_ref[kb * HY_KB + i, 1]
        vals.append(jnp.concatenate([ar * tr - ai * ti, ar * ti + ai * tr], axis=0))
    return rows, vals


def _spectrum_kernel(*refs):
    np_ = len(_parts((0, 0)))
    m_refs, f_refs, (tw_ref, t_ref, o_ref, a_scr, pad_scr) = refs[:np_], refs[np_:2 * np_], refs[2 * np_:]
    kb = pl.program_id(2)
    ct = t_ref.shape[-1]

    @pl.when(kb == 0)
    def _():
        _pad_copy(t_ref.at[0], pad_scr, FFT_N1)
        _dft_stage1(lambda n2: pad_scr[_pad_rows(n2, FFT_N1), :], [r[...] for r in m_refs], a_scr)

    _, a = _stage2_load(a_scr, tw_ref, kb)
    f2 = [r[...] for r in f_refs]
    for i in range(HY_KB):
        o_ref[0, i] = _bf(_mmp(f2, a[i])).reshape(2, FFT_N2, ct)


def _filter_spectrum(tables, taps):
    m, f2, tw = _parts(tables[1]), _parts(tables[3]), tables[5]
    depth, n, ch = taps.shape
    ct = HY_CT
    const = lambda t: pl.BlockSpec(t.shape, lambda l, c, k: (0,) * t.ndim, pipeline_mode=pl.Buffered(1))
    return pl.pallas_call(
        _spectrum_kernel,
        grid=(depth, ch // ct, FFT_N1 // HY_KB),
        in_specs=[const(t) for t in (*m, *f2, tw)] + [pl.BlockSpec((1, n, ct), lambda l, c, k: (l, 0, c))],
        out_specs=pl.BlockSpec((1, HY_KB, 2, FFT_N2, ct), lambda l, c, k: (l, k, 0, 0, c)),
        out_shape=jax.ShapeDtypeStruct((depth, FFT_N1, 2, FFT_N2, ch), BF16),
        scratch_shapes=[pltpu.VMEM((HY_PITCH * FFT_N2, ct), F32), pltpu.VMEM((HY_PITCH * FFT_N1, ct), F32)],
        compiler_params=_cparams("parallel", "parallel", "arbitrary"),
        name="hyena_spectrum",
    )(*m, *f2, tw, taps)


def _short_conv(p, w_ref, b_ref):
    n = p.shape[0]
    rows = lax.broadcasted_iota(jnp.int32, p.shape, 0)
    up = jnp.where(rows == 0, 0.0, pltpu.roll(p, 1, 0))
    dn = jnp.where(rows == n - 1, 0.0, pltpu.roll(p, n - 1, 0))
    return up * w_ref[0:1, :] + p * w_ref[1:2, :] + dn * w_ref[2:3, :] + b_ref[...]


def _hy_pre_kernel(pv_ref, p1_ref, p2_ref, pg_ref, wv_ref, w1_ref, w2_ref, bv_ref, b1_ref, b2_ref,
                   v_ref, x1_ref, x2_ref):
    v_ref[0] = _short_conv(pv_ref[0], wv_ref, bv_ref)
    x1_ref[0] = _short_conv(p1_ref[0], w1_ref, b1_ref)
    gate = pg_ref[0]
    x2_ref[0] = _short_conv(p2_ref[0], w2_ref, b2_ref) * (gate * jax.nn.sigmoid(gate))


def _hy_pre(p, conv_w, conv_b):
    bsz, seq, _ = p.shape
    def pspec(off):
        return pl.BlockSpec((1, seq, 128), lambda b, j: (b, 0, off + j))
    def wspec(off):
        return pl.BlockSpec((3, 128), lambda b, j: (0, off + j))
    def bspec(off):
        return pl.BlockSpec((1, 128), lambda b, j: (0, off + j))
    out = jax.ShapeDtypeStruct((bsz, seq, HY_WIDTH), F32)
    ospec = pl.BlockSpec((1, seq, 128), lambda b, j: (b, 0, j))
    return pl.pallas_call(
        _hy_pre_kernel,
        grid=(bsz, HY_WIDTH // 128),
        in_specs=[pspec(COL_HY), pspec(COL_HY + 4), pspec(COL_HY + 8), pspec(COL_HY + 12),
                  wspec(0), wspec(4), wspec(8), bspec(0), bspec(4), bspec(8)],
        out_specs=[ospec, ospec, ospec],
        out_shape=[out, out, out],
        compiler_params=_cparams("parallel", "parallel"),
        name="hyena_pre",
    )(p, p, p, p, conv_w, conv_w, conv_w, conv_b, conv_b, conv_b)


def _hy_conv_kernel(*refs, order):
    np_ = len(_parts((0, 0)))
    m1_refs, m3_refs, f_refs, ft_refs = (refs[i * np_:(i + 1) * np_] for i in range(4))
    tw_ref = refs[4 * np_]
    if order == 0:
        pv_ref, p1_ref, wv_ref, w1_ref, bv_ref, b1_ref, h_ref, d_ref, o_ref, a_scr, pad_scr = refs[4 * np_ + 1:]
        conv_in = lambda b: _short_conv(pv_ref[b], wv_ref, bv_ref)
        gate_of = lambda b: _short_conv(p1_ref[b], w1_ref, b1_ref)
    else:
        u_ref, p2_ref, pg_ref, w2_ref, b2_ref, h_ref, d_ref, o_ref, a_scr, pad_scr = refs[4 * np_ + 1:]
        conv_in = lambda b: u_ref[b]

        def gate_of(b):
            g = pg_ref[b]
            return _short_conv(p2_ref[b], w2_ref, b2_ref) * (g * jax.nn.sigmoid(g))
    kb = pl.program_id(1)
    ct = o_ref.shape[-1]
    half = FFT_N1 // 2

    @pl.when(kb == 0)
    def _():
        for b in range(2):
            ub = conv_in(b)
            for n1 in range(half):
                pad_scr[b, n1 * HY_PITCH:n1 * HY_PITCH + FFT_N2, :] = ub[n1 * FFT_N2:(n1 + 1) * FFT_N2]
        _dft_stage1(lambda n2: jnp.concatenate([pad_scr[0, _pad_rows(n2, half), :],
                                                pad_scr[1, _pad_rows(n2, half), :]], axis=0),
                    [r[...] for r in m1_refs], a_scr)

    srows, a = _stage2_load(a_scr, tw_ref, kb)
    f2, f2t = [r[...] for r in f_refs], [r[...] for r in ft_refs]
    z = [_mmp(f2, a[i]) for i in range(HY_KB)]
    z2 = []
    for i in range(HY_KB):
        zr, zi = z[i][:FFT_N2], z[i][FFT_N2:]
        hr, hi = h_ref[0, i, 0].astype(F32), h_ref[0, i, 1].astype(F32)
        z2.append(jnp.concatenate([zr * hr - zi * hi, zr * hi + zi * hr], axis=0))
    a2 = [_mmp(f2t, z2[i]) for i in range(HY_KB)]
    for i in range(HY_KB):
        cr, ci = a2[i][:FFT_N2], a2[i][FFT_N2:]
        tr, ti = tw_ref[kb * HY_KB + i, 0], tw_ref[kb * HY_KB + i, 1]
        a_scr[srows[i][0], :] = cr * tr + ci * ti
        a_scr[srows[i][1], :] = ci * tr - cr * ti

    @pl.when(kb == pl.num_programs(1) - 1)
    def _():
        m3 = [r[...] for r in m3_refs]
        d = d_ref[...]

        def body(it, carry):
            n2 = it * HY_NB
            y = _mmp(m3, jnp.concatenate([a_scr[_blk_rows(n2 + j), :] for j in range(HY_NB)], axis=1))
            for j in range(HY_NB):
                for b in range(2):
                    pad_scr[b, _pad_rows(n2 + j, half), :] = y[b * half:(b + 1) * half, j * ct:(j + 1) * ct]
            return carry

        lax.fori_loop(0, FFT_N2 // HY_NB, body, 0)
        for b in range(2):
            ub, gb = conv_in(b), gate_of(b)
            for n1 in range(half):
                r = slice(n1 * FFT_N2, (n1 + 1) * FFT_N2)
                conv = pad_scr[b, n1 * HY_PITCH:n1 * HY_PITCH + FFT_N2, :]
                o_ref[b, r, :] = gb[r] * (conv + d * ub[r])


def _hy_long_conv(tables, spec, layer, order, u, p, conv_w, conv_b, d):
    m1, _, m3, f2, f2t = (_parts(t) for t in tables[:5])
    bsz, seq = p.shape[:2]
    ch, ct = HY_WIDTH, HY_CT
    const = lambda t: pl.BlockSpec(t.shape, lambda c, k: (0,) * t.ndim, pipeline_mode=pl.Buffered(1))
    uspec = pl.BlockSpec((bsz, seq, ct), lambda c, k: (0, 0, c))
    pspec = lambda off: pl.BlockSpec((bsz, seq, ct), lambda c, k: (0, 0, off + c), pipeline_mode=pl.Buffered(1))
    wspec = lambda off: pl.BlockSpec((3, ct), lambda c, k: (0, off + c))
    bspec = lambda off: pl.BlockSpec((1, ct), lambda c, k: (0, off + c))
    n_ct = ch // ct
    if order == 0:
        data = [p, p, conv_w, conv_w, conv_b, conv_b]
        data_specs = [pl.BlockSpec((bsz, seq, ct), lambda c, k: (0, 0, COL_HY + c)), pspec(COL_HY + n_ct),
                      wspec(0), wspec(n_ct), bspec(0), bspec(n_ct)]
    else:
        data = [u, p, p, conv_w, conv_b]
        data_specs = [uspec, pspec(COL_HY + 2 * n_ct), pspec(COL_HY + 3 * n_ct), wspec(2 * n_ct), bspec(2 * n_ct)]
    mats = (*m1, *m3, *f2, *f2t, tables[5])
    return pl.pallas_call(
        functools.partial(_hy_conv_kernel, order=order),
        grid=(n_ct, FFT_N1 // HY_KB),
        in_specs=[const(t) for t in mats] + data_specs + [
            pl.BlockSpec((1, HY_KB, 2, FFT_N2, ct), lambda c, k: (layer, k, 0, 0, order * n_ct + c)),
            pl.BlockSpec((1, ct), lambda c, k: (0, c))],
        out_specs=uspec,
        out_shape=jax.ShapeDtypeStruct((bsz, seq, ch), F32),
        scratch_shapes=[pltpu.VMEM((HY_PITCH * FFT_N2, ct), F32),
                        pltpu.VMEM((bsz, HY_PITCH * FFT_N1 // 2, ct), F32)],
        compiler_params=_cparams("parallel", "arbitrary"),
        name="hyena_conv",
    )(*mats, *data, spec, d.reshape(1, ch))


def _ctx_dft_tables(length):
    n = 2 * length
    k = jnp.arange(n, dtype=jnp.int32)
    ang = ((k[:, None] * k[None, :]) % n).astype(F32) * (2.0 * math.pi / n)
    fr, fi = jnp.cos(ang), -jnp.sin(ang)
    fwd = jnp.concatenate([jnp.concatenate([fr[:, :length], -fi[:, :length]], 1),
                           jnp.concatenate([fi[:, :length], fr[:, :length]], 1)], 0)
    real = jnp.concatenate([fr, fi], 0)
    inv = fwd.T / n
    return tuple(_split(t) for t in (fwd, real, inv))


def _hyc_kernel(fh_ref, fl_ref, rh_ref, rl_ref, ih_ref, il_ref, taps_ref, u_ref, g_ref, d_ref, o_ref):
    n = u_ref.shape[1]
    x = jnp.concatenate([u_ref[0], u_ref[1]], axis=0)
    z = _mm3(fh_ref[...], fl_ref[...], x)
    h = _mm3(rh_ref[...], rl_ref[...], taps_ref[0])
    m = 2 * n
    zr, zi, hr, hi = z[:m], z[m:], h[:m], h[m:]
    z2 = jnp.concatenate([zr * hr - zi * hi, zr * hi + zi * hr], axis=0)
    y = _mm3(ih_ref[...], il_ref[...], z2)
    for b in range(2):
        o_ref[b] = g_ref[b] * (y[b * n:(b + 1) * n] + d_ref[...] * u_ref[b])


def _hyc_long_conv(ctabs, taps, layer, order, u, gate, d):
    (f_hi, f_lo), (r_hi, r_lo), (i_hi, i_lo) = ctabs
    bsz, n, ch = u.shape
    full = lambda t: pl.BlockSpec(t.shape, lambda i: (0,) * t.ndim)
    uspec = pl.BlockSpec((bsz, n, ch), lambda i: (0, 0, 0))
    return pl.pallas_call(
        _hyc_kernel,
        grid=(1,),
        in_specs=[full(f_hi), full(f_lo), full(r_hi), full(r_lo), full(i_hi), full(i_lo),
                  pl.BlockSpec((1, 2 * n, ch), lambda i: (layer, 0, order)),
                  uspec, uspec, pl.BlockSpec((1, ch), lambda i: (0, 0))],
        out_specs=uspec,
        out_shape=jax.ShapeDtypeStruct((bsz, n, ch), F32),
        compiler_params=_cparams("arbitrary"),
        name="hyena_ctx",
    )(f_hi, f_lo, r_hi, r_lo, i_hi, i_lo, taps, u, gate, d.reshape(1, ch))


def _outproj_kernel(*refs, colmajor, tm):
    if colmajor:
        x_ref, ya_ref, yb_ref, yc_ref, w_ref, gp_ref, gate_ref, perm_ref, o_ref = refs
        ld = lambda r: _bf(_mm(perm_ref[...], _bf(r[0])))
    else:
        x_ref, ya_ref, yb_ref, yc_ref, w_ref, gp_ref, gate_ref, o_ref = refs
        ld = lambda r: _bf(r[0])
    na, nb = ya_ref.shape[-1], yb_ref.shape[-1]
    acc = (_mm(ld(ya_ref), w_ref[0, 0:na, :]) + _mm(ld(yb_ref), w_ref[0, na:na + nb, :])
           + _mm(ld(yc_ref), w_ref[0, na + nb:, :]))
    ms = jnp.mean(acc * acc, axis=-1, keepdims=True)
    upd = gate_ref[0] * (acc * lax.rsqrt(ms + EPS) * gp_ref[...])
    o_ref[0] = x_ref[0] + upd.reshape(x_ref.shape[1:])


def _outproj(x, ya, yb, yc, w_out, layer, g_post, gate, *, colmajor, tm):
    bsz, seq, d = x.shape
    extra_specs, extra_args = [], []
    if colmajor:
        cols = tm // GRID_W
        x_in = x.reshape(bsz, GRID_W, seq // GRID_W, d)
        x_spec = pl.BlockSpec((1, GRID_W, cols, d), lambda b, m: (b, 0, m, 0))
        r_old = jnp.arange(tm, dtype=jnp.int32)[:, None]
        r_new = jnp.arange(tm, dtype=jnp.int32)[None, :]
        perm = _bf((r_new == (r_old % cols) * GRID_W + r_old // cols).astype(F32))
        extra_specs, extra_args = [pl.BlockSpec((tm, tm), lambda b, m: (0, 0))], [perm]
    else:
        x_in = x
        x_spec = pl.BlockSpec((1, tm, d), lambda b, m: (b, m, 0))
    def yspec(t):
        return pl.BlockSpec((1, tm, t.shape[-1]), lambda b, m: (b, m, 0))
    out = pl.pallas_call(
        functools.partial(_outproj_kernel, colmajor=colmajor, tm=tm),
        grid=(bsz, seq // tm),
        in_specs=[x_spec, yspec(ya), yspec(yb), yspec(yc),
                  pl.BlockSpec((1,) + w_out.shape[1:], lambda b, m: (layer, 0, 0)),
                  pl.BlockSpec((1, d), lambda b, m: (0, 0)),
                  pl.BlockSpec((1, 1, d), lambda b, m: (b, 0, 0))] + extra_specs,
        out_specs=x_spec,
        out_shape=jax.ShapeDtypeStruct(x_in.shape, F32),
        compiler_params=_cparams("parallel", "parallel"),
        name="outproj",
    )(x_in, ya, yb, yc, w_out, g_post, gate, *extra_args)
    return out.reshape(bsz, seq, d)


def kernel(x, c, ctx, c_ctx, w_mod, b_mod, g_pre, g_post, w_in, w_out, gla_w_gate, gla_b_gate, gla_norm, s5_lam_re, s5_lam_im, s5_log_step, s5_b_re, s5_b_im, s5_c_re, s5_c_im, s5_d, s5_w_glu, s5_b_glu, hy_conv_w, hy_conv_b, hy_w1, hy_b1, hy_f1, hy_w2, hy_b2, hy_f2, hy_w3, hy_d):
    bsz, seq, d = x.shape
    seq_c = ctx.shape[1]
    depth = w_in.shape[0]
    assert (bsz, d, seq // GRID_W) == (2, D_MODEL, GRID_W) and seq == FFT_N1 * FFT_N2 // 2

    cvec = jnp.concatenate([c, c_ctx[None], jnp.zeros((8 - bsz - 1, d), F32)], axis=0)
    mod = _modulation(cvec, w_mod, b_mod)
    shift, scale, gate = mod[..., :d], mod[..., d:2 * d], mod[..., 2 * d:]

    w_t = _bf(jnp.swapaxes(w_in, 1, 2))
    w_lr = w_t[:, N_QKV:N_QKV + N_LR]
    w_lr3 = jnp.concatenate([w_lr, w_lr, w_lr, jnp.zeros_like(w_lr)], axis=1)
    w_out_b = _bf(w_out)
    w_glu_b = _bf(s5_w_glu)

    wg = gla_w_gate.reshape(depth, 2, 16, GLA_HEADS, GLA_DK).transpose(0, 1, 3, 2, 4)
    wg_pad = jnp.stack([jnp.pad(wg[:, 0], ((0, 0), (0, 0), (0, 112), (0, 0))),
                        jnp.pad(wg[:, 1], ((0, 0), (0, 0), (16, 96), (0, 0)))], axis=1)
    wg_hi, wg_lo = _split(wg_pad[..., :N_LR, :])
    wg3 = jnp.concatenate([wg_hi, wg_hi, wg_lo, jnp.zeros_like(wg_hi)], axis=-2)
    bg = gla_b_gate.reshape(depth, 2, GLA_HEADS, 1, GLA_DK)

    tables = _dft_tables()
    deltas = jnp.abs(jnp.linspace(HY_MIN_DECAY, HY_MAX_DECAY, HY_WIDTH, dtype=F32))
    deltas2 = jnp.tile(deltas, 2).reshape(1, 2 * HY_WIDTH)
    pad_k = lambda w: jnp.pad(w, ((0, 0), (0, 128 - w.shape[1]), (0, 128 - w.shape[2])))
    pad_v = lambda v: jnp.pad(v, ((0, 0), (0, 128 - v.shape[1]))).reshape(depth, 1, 128)
    w3 = hy_w3.reshape(depth, HY_FFN, 2, 2, HY_WIDTH).transpose(0, 3, 1, 2, 4)
    w3 = jnp.pad(w3.reshape(depth, 2, HY_FFN, 2 * HY_WIDTH), ((0, 0), (0, 0), (0, 128 - HY_FFN), (0, 0)))
    filt_w = (pad_k(hy_w1), pad_v(hy_b1), pad_v(hy_f1), pad_k(hy_w2), pad_v(hy_b2), pad_v(hy_f2), w3, deltas2)
    taps = _filter_taps(_filter_features(seq), *filt_w, tmf=512)
    spec = _filter_spectrum(tables, taps)
    taps_c = _filter_taps(_filter_features(seq_c), *filt_w, tmf=seq_c)
    ctabs = _ctx_dft_tables(seq_c)

    s5_ops = _s5_operators(s5_lam_re, s5_lam_im, s5_log_step, s5_b_re, s5_b_im, s5_c_re, s5_c_im)

    xc = ctx
    for l in range(depth):
        last = l == depth - 1
        ctx_out = not last
        colmajor = l % 2 == 1
        pre = g_pre[l][None, :] * (1.0 + scale[l])
        sc_l, sh_l = pre[:bsz, None, :], shift[l, :bsz, None, :]
        sc_c = jnp.broadcast_to(pre[bsz][None, None, :], (bsz, 1, d))
        sh_c = jnp.broadcast_to(shift[l, bsz][None, None, :], (bsz, 1, d))
        qkv, p, lr = _inproj(x, sc_l, sh_l, w_t, w_lr3, l, n=N_MAIN, colmajor=colmajor, tm=1024, tn=512)
        qkvc, pc, lrc = _inproj(xc, sc_c, sh_c, w_t, w_lr3, l, n=N_MAIN if ctx_out else N_STATE,
                                colmajor=False, tm=seq_c, tn=512)

        y_gla, yc_gla = _gla(qkv, p, lr, qkvc, pc, lrc, wg3[l], bg[l], gla_norm[l][None, :],
                             ctx_out=ctx_out)

        y_s5, yc_s5 = _s5(p, pc, s5_ops, l, s5_d[l][None, :], w_glu_b[l], s5_b_glu[l][None, :], ctx_out=ctx_out)

        z = _hy_long_conv(tables, spec, l, 0, None, p, hy_conv_w[l], hy_conv_b[l][None, :], hy_d[l, 0])
        y_hy = _hy_long_conv(tables, spec, l, 1, z, p, hy_conv_w[l], hy_conv_b[l][None, :], hy_d[l, 1])

        x_new = _outproj(x, y_gla, y_s5, y_hy, w_out_b, l, g_post[l][None, :], gate[l, :bsz, None, :],
                         colmajor=colmajor, tm=512)
        if ctx_out:
            vc, x1c, x2c = _hy_pre(pc, hy_conv_w[l], hy_conv_b[l][None, :])
            zc = _hyc_long_conv(ctabs, taps_c, l, 0, vc, x1c, hy_d[l, 0])
            yc_hy = _hyc_long_conv(ctabs, taps_c, l, 1, zc, x2c, hy_d[l, 1])
            gate_c = jnp.broadcast_to(gate[l, bsz][None, None, :], (bsz, 1, d))
            xc = _outproj(xc, yc_gla, yc_s5, yc_hy, w_out_b, l, g_post[l][None, :], gate_c,
                          colmajor=False, tm=seq_c)
        x = x_new
    return x
```

```python
import functools
import math

import numpy as np
import jax
import jax.numpy as jnp
from jax import lax
from jax.experimental import pallas as pl
from jax.experimental.pallas import tpu as pltpu

F32 = jnp.float32
BF16 = jnp.bfloat16

D_MODEL = 2048
GRID_W = 64
EPS = 1e-6

GLA_HEADS = 4
GLA_DK = 128
GLA_DV = 256
GLA_TAU = 16.0
GLA_CHUNK = 64
GLA_BLOCK = 16

S5_WIDTH = 512
S5_GROUP = 16
S5_GROUPS = 32
S5_STATE = 64
S5_T = 16
S5_COLS = 640
S5_CB = 128

HY_WIDTH = 512
HY_EMB = 33
HY_BANDS = 16
HY_FFN = 64
HY_MIN_DECAY = math.log(1e-2) / 0.3
HY_MAX_DECAY = math.log(1e-2) / 1.5
FFT_N1 = 64
FFT_N2 = 128
HY_CT = 128
HY_KB = 8
HY_NB = 16
HY_PITCH = 136
HY_PASSES = 1

N_MAIN = 6144
N_STATE = 2560
N_QKV = 2048
N_LR = 32
COL_S5U, COL_S5G = 0, 3
COL_GLA_GATE = 2
COL_HY = 16

VMEM_LIMIT_BYTES = 56 * 1024 * 1024


def _cparams(*sem):
    return pltpu.CompilerParams(dimension_semantics=sem, vmem_limit_bytes=VMEM_LIMIT_BYTES)


def _bf(x):
    return x.astype(BF16)


def _split(x):
    hi = _bf(x)
    return hi, _bf(x - hi.astype(F32))


def _split3(x):
    a = _bf(x)
    r = x - a.astype(F32)
    b = _bf(r)
    return a, b, _bf(r - b.astype(F32))


_NN = (((1,), (0,)), ((), ()))
_NT = (((1,), (1,)), ((), ()))
_TN = (((0,), (0,)), ((), ()))


def _mm(a, b, dims=_NN):
    return lax.dot_general(a, b, dims, preferred_element_type=F32)


def _mm3(a_hi, a_lo, b, dims=_NN):
    b_hi, b_lo = _split(b)
    return _mm(a_hi, b_hi, dims) + _mm(a_hi, b_lo, dims) + _mm(a_lo, b_hi, dims)


def _mod_kernel(s_ref, w_ref, b_ref, o_ref):
    s = s_ref[...]
    s = s * jax.nn.sigmoid(s)
    s_hi, s_lo = _split(s)
    o_ref[0] = _mm3(s_hi, s_lo, w_ref[0]) + b_ref[0]


def _modulation(cvec, w_mod, b_mod):
    depth, d, n = w_mod.shape
    tn = 512
    return pl.pallas_call(
        _mod_kernel,
        grid=(depth, n // tn),
        in_specs=[
            pl.BlockSpec((8, d), lambda l, j: (0, 0)),
            pl.BlockSpec((1, d, tn), lambda l, j: (l, 0, j)),
            pl.BlockSpec((1, 1, tn), lambda l, j: (l, 0, j)),
        ],
        out_specs=pl.BlockSpec((1, 8, tn), lambda l, j: (l, 0, j)),
        out_shape=jax.ShapeDtypeStruct((depth, 8, n), F32),
        compiler_params=_cparams("parallel", "parallel"),
        name="modulation",
    )(cvec, w_mod, b_mod.reshape(depth, 1, n))


def _inproj_kernel(*refs, colmajor, tm, tn):
    if colmajor:
        x_ref, sc_ref, sh_ref, w_ref, wlr_ref, perm_ref, qkv_ref, p_ref, lr_ref, h_scr = refs
    else:
        x_ref, sc_ref, sh_ref, w_ref, wlr_ref, qkv_ref, p_ref, lr_ref, h_scr = refs
    j = pl.program_id(2)

    @pl.when(j == 0)
    def _():
        def norm(xb):
            ms = jnp.mean(xb * xb, axis=-1, keepdims=True)
            return _bf(xb * lax.rsqrt(ms + EPS) * sc_ref[0] + sh_ref[0])

        if colmajor:
            h = norm(x_ref[0].reshape(tm, D_MODEL))
            for c in range(0, D_MODEL, 512):
                h_scr[:, c:c + 512] = _bf(_mm(perm_ref[...], h[:, c:c + 512]))
        else:
            h_scr[...] = norm(x_ref[0])
        lr_ref[0] = _mm(h_scr[...], wlr_ref[0], _NT)

    res = _mm(h_scr[...], w_ref[...], _NT)
    n_qkv = N_QKV // tn

    @pl.when(j < n_qkv)
    def _():
        qkv_ref[0] = _bf(res)

    @pl.when(j >= n_qkv)
    def _():
        p_ref[0] = res


def _inproj(x, scale, shift, w_t, w_lr3, layer, *, n, colmajor, tm, tn):
    bsz, seq, d = x.shape
    n_qkv = N_QKV // tn
    extra_specs, extra_args = [], []
    if colmajor:
        cols = tm // GRID_W
        x_in = x.reshape(bsz, GRID_W, seq // GRID_W, d)
        x_spec = pl.BlockSpec((1, GRID_W, cols, d), lambda b, m, j: (b, 0, m, 0))
        r_new = jnp.arange(tm, dtype=jnp.int32)[:, None]
        r_old = jnp.arange(tm, dtype=jnp.int32)[None, :]
        perm = _bf((r_old == (r_new % GRID_W) * cols + r_new // GRID_W).astype(F32))
        extra_specs, extra_args = [pl.BlockSpec((tm, tm), lambda b, m, j: (0, 0))], [perm]
    else:
        x_in = x
        x_spec = pl.BlockSpec((1, tm, d), lambda b, m, j: (b, m, 0))
    return pl.pallas_call(
        functools.partial(_inproj_kernel, colmajor=colmajor, tm=tm, tn=tn),
        grid=(bsz, seq // tm, n // tn),
        in_specs=[
            x_spec,
            pl.BlockSpec((1, 1, d), lambda b, m, j: (b, 0, 0)),
            pl.BlockSpec((1, 1, d), lambda b, m, j: (b, 0, 0)),
            pl.BlockSpec((pl.Squeezed(), pl.Element(tn), pl.Element(d)),
                         lambda b, m, j: (layer, pl.multiple_of(j * tn + jnp.where(j >= n_qkv, N_LR, 0), N_LR), 0)),
            pl.BlockSpec((1, 128, d), lambda b, m, j: (layer, 0, 0)),
        ] + extra_specs,
        out_specs=[
            pl.BlockSpec((1, tm, tn), lambda b, m, j: (b, m, jnp.minimum(j, n_qkv - 1))),
            pl.BlockSpec((1, tm, tn), lambda b, m, j: (b, m, jnp.maximum(j - n_qkv, 0))),
            pl.BlockSpec((1, tm, 128), lambda b, m, j: (b, m, 0)),
        ],
        out_shape=[jax.ShapeDtypeStruct((bsz, seq, N_QKV), BF16),
                   jax.ShapeDtypeStruct((bsz, seq, n - N_QKV), F32),
                   jax.ShapeDtypeStruct((bsz, seq, 128), F32)],
        scratch_shapes=[pltpu.VMEM((tm, d), BF16)],
        compiler_params=_cparams("parallel", "parallel", "arbitrary"),
        name="inproj",
    )(x_in, scale, shift, w_t, w_lr3, *extra_args)


def _gla_block(q, k, v, lr, st, wg, bg, direction, mask, rowc, hi_lanes, nb):
    c = GLA_CHUNK
    n = c * nb
    lr_hi, lr_lo = _split(lr)
    z = _mm(jnp.where(hi_lanes, lr_hi, lr_lo), wg) + bg
    g = (jnp.minimum(z, 0.0) - jnp.log(1.0 + jnp.exp(-jnp.abs(z)))) * (1.0 / GLA_TAU)
    cum = g
    s = 1
    while s < c:
        if direction == 0:
            cum = cum + jnp.where(rowc >= s, pltpu.roll(cum, s, 0), 0.0)
        else:
            cum = cum + jnp.where(rowc < c - s, pltpu.roll(cum, n - s, 0), 0.0)
        s *= 2
    cum3 = cum.reshape(nb, c, GLA_DK)
    tot3 = cum3[:, c - 1:c, :] if direction == 0 else cum3[:, 0:1, :]
    q3 = q.reshape(nb, c, GLA_DK)
    k3 = k.reshape(nb, c, GLA_DK)
    qg3 = _bf(q3 * jnp.exp(cum3))
    kg3 = _bf(k3 * jnp.exp(-cum3))
    kd3 = _bf(k3 * jnp.exp(tot3 - cum3))
    a3 = jnp.exp(tot3)
    vb3 = _bf(v).reshape(nb, c, GLA_DV)
    att = [jnp.where(mask, _mm(qg3[i], kg3[i], _NT), 0.0) for i in range(nb)]
    o_in = [_mm(_bf(att[i]), vb3[i]) for i in range(nb)]
    d_st = [_mm(vb3[i], kd3[i], _TN) for i in range(nb)]
    outs = [None] * nb
    for i in (range(nb) if direction == 0 else range(nb - 1, -1, -1)):
        outs[i] = o_in[i] + _mm(qg3[i], _bf(st), _NT)
        st = st * a3[i] + d_st[i]
    return jnp.concatenate(outs, axis=0), st


def _gla_kernel(*refs, ctx_out, seq, seq_c):
    if ctx_out:
        (q_ref, k_ref, v_ref, gt_ref, lr_ref, qc_ref, kc_ref, vc_ref, gtc_ref, lrc_ref,
         wg_ref, bg_ref, gn_ref, y_ref, yc_ref) = refs
    else:
        (q_ref, k_ref, v_ref, gt_ref, lr_ref, qc_ref, kc_ref, vc_ref, lrc_ref,
         wg_ref, bg_ref, gn_ref, y_ref) = refs
        gtc_ref = yc_ref = None
    c = GLA_CHUNK
    scale = GLA_DK ** -0.5
    row = lax.broadcasted_iota(jnp.int32, (c, c), 0)
    col = lax.broadcasted_iota(jnp.int32, (c, c), 1)
    masks = (col <= row, col >= row)
    gn = gn_ref[...]

    def run(refs4, n_rows, st, direction, store):
        qr, kr, vr, lrr = refs4
        nb = min(GLA_BLOCK, n_rows // c)
        n = c * nb
        rowc = lax.broadcasted_iota(jnp.int32, (n, GLA_DK), 0) % c
        hi_lanes = (lax.broadcasted_iota(jnp.int32, (n, 128), 1) // N_LR) % 2 == 0
        wg = wg_ref[direction, 0]
        bg = bg_ref[direction, 0]
        n_blocks = n_rows // n

        def body(i, st):
            bi = i if direction == 0 else n_blocks - 1 - i
            rows = pl.ds(pl.multiple_of(bi * n, n), n)
            q = qr[0, rows, :].astype(F32) * scale
            k = kr[0, rows, :].astype(F32)
            v = vr[0, rows, :].astype(F32)
            o, st = _gla_block(q, k, v, lrr[0, rows, :], st, wg, bg, direction,
                               masks[direction], rowc, hi_lanes, nb)
            store(rows, q, k, v, o)
            return st

        return lax.fori_loop(0, n_blocks, body, st)

    def store_fwd(out_ref):
        def f(rows, q, k, v, o):
            if out_ref is not None:
                out_ref[0, rows, :] = o
        return f

    def store_bwd(out_ref, gate_ref):
        def f(rows, q, k, v, o):
            if out_ref is None:
                return
            y = out_ref[0, rows, :] + o - jnp.sum(q * k, axis=-1, keepdims=True) * v
            ms = jnp.mean(y * y, axis=-1, keepdims=True)
            y = y * lax.rsqrt(ms + EPS) * gn
            gt = gate_ref[0, rows, :]
            out_ref[0, rows, :] = y * (gt * jax.nn.sigmoid(gt))
        return f

    lat = (q_ref, k_ref, v_ref, lr_ref)
    ctx = (qc_ref, kc_ref, vc_ref, lrc_ref)
    zero = jnp.zeros((GLA_DV, GLA_DK), F32)
    st = run(ctx, seq_c, zero, 0, store_fwd(yc_ref))
    run(lat, seq, st, 0, store_fwd(y_ref))
    st = run(ctx, seq_c, zero, 1, store_bwd(yc_ref, gtc_ref))
    run(lat, seq, st, 1, store_bwd(y_ref, gt_ref))


def _gla(qkv, p, lr, qkvc, pc, lrc, wg, bg, gnorm, *, ctx_out):
    bsz, seq, _ = qkv.shape
    seq_c = qkvc.shape[1]
    h = GLA_HEADS

    def specs(n):
        return [
            pl.BlockSpec((1, n, 128), lambda b, i: (b, 0, i)),
            pl.BlockSpec((1, n, 128), lambda b, i: (b, 0, 4 + i)),
            pl.BlockSpec((1, n, 256), lambda b, i: (b, 0, 4 + i)),
            pl.BlockSpec((1, n, 256), lambda b, i: (b, 0, COL_GLA_GATE + i)),
            pl.BlockSpec((1, n, 128), lambda b, i: (b, 0, 0)),
        ]

    lat_specs = specs(seq)
    ctx_specs = specs(seq_c)
    lat_args = [qkv, qkv, qkv, p, lr]
    ctx_args = [qkvc, qkvc, qkvc, pc, lrc]
    if not ctx_out:
        del ctx_specs[3], ctx_args[3]
    w_specs = [
        pl.BlockSpec((2, 1, 128, 128), lambda b, i: (0, i, 0, 0)),
        pl.BlockSpec((2, 1, 1, 128), lambda b, i: (0, i, 0, 0)),
        pl.BlockSpec((1, GLA_DV), lambda b, i: (0, 0)),
    ]
    out_specs = [pl.BlockSpec((1, seq, 256), lambda b, i: (b, 0, i))]
    out_shape = [jax.ShapeDtypeStruct((bsz, seq, h * GLA_DV), F32)]
    if ctx_out:
        out_specs.append(pl.BlockSpec((1, seq_c, 256), lambda b, i: (b, 0, i)))
        out_shape.append(jax.ShapeDtypeStruct((bsz, seq_c, h * GLA_DV), F32))
    outs = pl.pallas_call(
        functools.partial(_gla_kernel, ctx_out=ctx_out, seq=seq, seq_c=seq_c),
        grid=(bsz, h),
        in_specs=lat_specs + ctx_specs + w_specs,
        out_specs=out_specs,
        out_shape=out_shape,
        compiler_params=_cparams("parallel", "parallel"),
        name="gla",
    )(*lat_args, *ctx_args, wg, bg, gnorm)
    return (outs[0], outs[1]) if ctx_out else (outs[0], None)


def _mm3f(a, b, dims=_NN):
    a_hi, a_lo = _split(a)
    return _mm3(a_hi, a_lo, b, dims)


def _s5_ops_kernel(a_ref, b_ref, lr_ref, li_ref, btr_ref, bti_ref, cr_ref, ci_ref,
                   kt_ref, bpr_ref, bpi_ref, cp_ref, ltr_ref, lti_ref):
    t_len, h_n, p_n = S5_T, S5_GROUP, S5_STATE
    rows = t_len * h_n
    lane = lax.broadcasted_iota(jnp.int32, (rows, 2 * p_n), 1)
    own = (lane < p_n, lane >= p_n)
    lane_blk = lax.broadcasted_iota(jnp.int32, (rows, rows), 1) // h_n
    kk = lax.broadcasted_iota(jnp.int32, (24, 2 * p_n), 0).astype(F32)
    for d in range(2):
        a, b = a_ref[0, d, 0], b_ref[0, d, 0]
        lr, li = lr_ref[0, d, 0], li_ref[0, d, 0]
        mag = jnp.exp(kk * a)
        pwr, pwi = mag * jnp.cos(kk * b), mag * jnp.sin(kk * b)

        def times_pw(k, mr, mi):
            pr, pi = pwr[k:k + 1], pwi[k:k + 1]
            return pr * mr - pi * mi, pr * mi + pi * mr

        x, y = pwr[1:2] - 1.0, pwi[1:2]
        den = lr * lr + li * li
        cfr, cfi = (x * lr + y * li) / den, (y * lr - x * li) / den
        btr, bti = btr_ref[0, d, 0], bti_ref[0, d, 0]
        bbr, bbi = cfr * btr - cfi * bti, cfr * bti + cfi * btr
        cr, ci = cr_ref[0, d, 0], ci_ref[0, d, 0]

        def stack(power_of_t, mr, mi):
            parts = [times_pw(power_of_t(t), mr, mi) for t in range(t_len)]
            return (jnp.concatenate([p[0] for p in parts], axis=0),
                    jnp.concatenate([p[1] for p in parts], axis=0))

        bpr, bpi = stack((lambda t: t_len - 1 - t) if d == 0 else (lambda t: t), bbr, bbi)
        for src, dst in ((bpr, bpr_ref), (bpi, bpi_ref)):
            z = jnp.concatenate([jnp.where(own[0], src, 0.0), jnp.where(own[1], src, 0.0)], axis=0)
            dst[0, d, 0] = _bf(z.T)
        cpr, cpi = stack((lambda t: t + 1) if d == 0 else (lambda t: t_len - t), cr, ci)
        clr, cli = stack((lambda j: j) if d == 0 else (lambda j: t_len - 1 - j), cr, ci)
        bbr_t = jnp.concatenate([bbr] * t_len, axis=0)
        bbi_t = jnp.concatenate([bbi] * t_len, axis=0)
        for e in range(2):
            cp_ref[0, d, e] = _bf(jnp.concatenate(
                [jnp.where(own[e], cpr, 0.0), jnp.where(own[e], -cpi, 0.0)], axis=1))
            kc = (_mm3f(jnp.where(own[e], clr, 0.0), bbr_t, _NT)
                  - _mm3f(jnp.where(own[e], cli, 0.0), bbi_t, _NT))
            acc = jnp.zeros((rows, rows), F32)
            for t in range(t_len):
                if d == 0:
                    pieces = [jnp.zeros((h_n * t, rows), F32), kc[:rows - h_n * t]]
                else:
                    s = h_n * (t_len - 1 - t)
                    pieces = [kc[s:], jnp.zeros((s, rows), F32)]
                shifted = jnp.concatenate([p for p in pieces if p.shape[0]], axis=0)
                acc = jnp.where(lane_blk == t, shifted, acc)
            kt_ref[0, d, e] = _bf(acc)
        ltr_ref[0, d, 0] = pwr[t_len:t_len + 1]
        lti_ref[0, d, 0] = pwi[t_len:t_len + 1]


def _s5_operators(lam_re, lam_im, log_step, b_re, b_im, c_re, c_im):
    depth = lam_re.shape[0]
    g_n, p_n, h_n = S5_GROUPS, S5_STATE, S5_GROUP
    pairs, rows = g_n // 2, S5_T * h_n
    dt = jnp.exp(log_step)[..., None]
    vec = lambda v: v.reshape(depth, 2, pairs, 1, 2 * p_n)
    bt = lambda m: m.reshape(depth, 2, pairs, 2, p_n, h_n).transpose(0, 1, 2, 5, 3, 4).reshape(depth, 2, pairs, h_n, 2 * p_n)
    ct = lambda m: m.reshape(depth, 2, pairs, 2, h_n, p_n).transpose(0, 1, 2, 4, 3, 5).reshape(depth, 2, pairs, h_n, 2 * p_n)
    vspec = pl.BlockSpec((1, 2, 1, 1, 2 * p_n), lambda l, q: (l, 0, q, 0, 0))
    mspec = pl.BlockSpec((1, 2, 1, h_n, 2 * p_n), lambda l, q: (l, 0, q, 0, 0))
    sq_spec = pl.BlockSpec((1, 2, 2, rows, rows), lambda l, q: (l, 0, q, 0, 0))
    bp_spec = pl.BlockSpec((1, 2, 1, 2 * p_n, 2 * rows), lambda l, q: (l, 0, q, 0, 0))
    sq = jax.ShapeDtypeStruct((depth, 2, g_n, rows, rows), BF16)
    bp = jax.ShapeDtypeStruct((depth, 2, pairs, 2 * p_n, 2 * rows), BF16)
    lt = jax.ShapeDtypeStruct((depth, 2, pairs, 1, 2 * p_n), F32)
    ktoep, bpr, bpi, cpad, ltr, lti = pl.pallas_call(
        _s5_ops_kernel,
        grid=(depth, pairs),
        in_specs=[vspec] * 4 + [mspec] * 4,
        out_specs=[sq_spec, bp_spec, bp_spec, sq_spec, vspec, vspec],
        out_shape=[sq, bp, bp, sq, lt, lt],
        compiler_params=_cparams("parallel", "parallel"),
        name="s5_operators",
    )(vec(lam_re * dt), vec(lam_im * dt), vec(lam_re), vec(lam_im), bt(b_re), bt(b_im), ct(c_re), ct(c_im))
    flat = lambda v: v.reshape(depth, 2, 1, g_n * p_n)
    return ktoep, bpr, bpi, cpad, flat(ltr), flat(lti)


def _s5_in_kernel(u_ref, uc_ref, o_ref):
    j = pl.program_id(0)
    last = pl.num_programs(0) - 1

    def emit(slab_of_t):
        for t in range(S5_T):
            o_ref[:, t * S5_GROUP:(t + 1) * S5_GROUP, :] = _bf(
                slab_of_t(t).T.reshape(S5_GROUPS, S5_GROUP, S5_CB))

    @pl.when(j < last)
    def _():
        emit(lambda t: u_ref[0, :, t, :])

    @pl.when(j == last)
    def _():
        pad = jnp.zeros((S5_CB - 2 * uc_ref.shape[1], S5_WIDTH), F32)
        emit(lambda t: jnp.concatenate([uc_ref[0, :, t, :], uc_ref[1, :, t, :], pad], axis=0))


def _s5_lat_index(n_half):
    def index(j):
        jj = jnp.minimum(j, 2 * n_half - 1)
        return jj // n_half, jj % n_half
    return index


def _s5_relayout_in(p, pc):
    bsz, seq, n = p.shape
    seq_c, n_c = pc.shape[1], pc.shape[2]
    n_lat, n_ctx = seq // S5_T, seq_c // S5_T
    n_half = n_lat // S5_CB
    idx = _s5_lat_index(n_half)
    return pl.pallas_call(
        _s5_in_kernel,
        grid=(bsz * n_half + 1,),
        in_specs=[
            pl.BlockSpec((1, S5_CB, S5_T, 512), lambda j: (*idx(j), 0, COL_S5U)),
            pl.BlockSpec((bsz, n_ctx, S5_T, 512), lambda j: (0, 0, 0, COL_S5U)),
        ],
        out_specs=pl.BlockSpec((S5_GROUPS, S5_T * S5_GROUP, S5_CB), lambda j: (0, 0, j)),
        out_shape=jax.ShapeDtypeStruct((S5_GROUPS, S5_T * S5_GROUP, S5_COLS), BF16),
        compiler_params=_cparams("arbitrary"),
        name="s5_relayout_in",
    )(p.reshape(bsz, n_lat, S5_T, n), pc.reshape(bsz, n_ctx, S5_T, n_c))


def _s5_state_in_kernel(ug_ref, bre_ref, bim_ref, vre_ref, vim_ref):
    u = ug_ref[...].reshape(2 * S5_T * S5_GROUP, S5_COLS)
    for direction in range(2):
        vre_ref[direction] = _mm(bre_ref[0, direction, 0], u)
        vim_ref[direction] = _mm(bim_ref[0, direction, 0], u)


def _s5_state_in(ug, b_pair_re, b_pair_im, layer):
    pairs = S5_GROUPS // 2
    rows = S5_T * S5_GROUP
    lanes = S5_GROUPS * S5_STATE
    out = jax.ShapeDtypeStruct((2, lanes, S5_COLS), F32)
    return pl.pallas_call(
        _s5_state_in_kernel,
        grid=(pairs,),
        in_specs=[
            pl.BlockSpec((2, rows, S5_COLS), lambda q: (q, 0, 0)),
            pl.BlockSpec((1, 2, 1, 128, 2 * rows), lambda q: (layer, 0, q, 0, 0)),
            pl.BlockSpec((1, 2, 1, 128, 2 * rows), lambda q: (layer, 0, q, 0, 0)),
        ],
        out_specs=[pl.BlockSpec((2, 128, S5_COLS), lambda q: (0, q, 0))] * 2,
        out_shape=[out, out],
        compiler_params=_cparams("parallel"),
        name="s5_state_in",
    )(ug, b_pair_re, b_pair_im)


def _s5_scan_kernel(vre_ref, vim_ref, lre_ref, lim_ref, xre_ref, xim_ref, tre_scr, tim_scr, *, n_lat, n_ctx):
    lanes = lre_ref.shape[-1]
    for d in range(2):
        tre_scr[d] = vre_ref[d].T
        tim_scr[d] = vim_ref[d].T
    a_re = [lre_ref[0, d] for d in range(2)]
    a_im = [lim_ref[0, d] for d in range(2)]

    def step(d, row, xr, xi):
        sl = pl.ds(row, 1)
        xre_ref[d, sl, :] = xr
        xim_ref[d, sl, :] = xi
        return (a_re[d] * xr - a_im[d] * xi + tre_scr[d, sl, :],
                a_re[d] * xi + a_im[d] * xr + tim_scr[d, sl, :])

    def run(bases, n, carry):
        def body(i, carry):
            out = []
            for d in range(2):
                j = i if d == 0 else n - 1 - i
                for b in range(2):
                    xr, xi = carry[2 * d + b]
                    out.append(step(d, bases[b] + j, xr, xi))
            return tuple(out)
        return lax.fori_loop(0, n, body, carry)

    zero = jnp.zeros((1, lanes), F32)
    carry = run((2 * n_lat, 2 * n_lat + n_ctx), n_ctx, ((zero, zero),) * 4)
    run((0, n_lat), n_lat, carry)
    pad0 = 2 * n_lat + 2 * n_ctx
    for d in range(2):
        xre_ref[d, pad0:, :] = jnp.zeros((S5_COLS - pad0, lanes), F32)
        xim_ref[d, pad0:, :] = jnp.zeros((S5_COLS - pad0, lanes), F32)


def _s5_scan(vre, vim, lam_re, lam_im, layer, *, n_lat, n_ctx):
    lanes = vre.shape[1]
    tl = 512
    vspec = pl.BlockSpec((2, tl, S5_COLS), lambda j: (0, j, 0))
    xspec = pl.BlockSpec((2, S5_COLS, tl), lambda j: (0, 0, j))
    lspec = pl.BlockSpec((1, 2, 1, tl), lambda j: (layer, 0, 0, j))
    out = jax.ShapeDtypeStruct((2, S5_COLS, lanes), F32)
    return pl.pallas_call(
        functools.partial(_s5_scan_kernel, n_lat=n_lat, n_ctx=n_ctx),
        grid=(lanes // tl,),
        in_specs=[vspec, vspec, lspec, lspec],
        out_specs=[xspec, xspec],
        out_shape=[out, out],
        scratch_shapes=[pltpu.VMEM((2, S5_COLS, tl), F32), pltpu.VMEM((2, S5_COLS, tl), F32)],
        compiler_params=_cparams("parallel"),
        name="s5_scan",
    )(vre, vim, lam_re, lam_im)


def _s5_readout_kernel(ug_ref, kt_ref, cp_ref, xre_ref, xim_ref, y_ref):
    for e in range(2):
        u = ug_ref[e]
        acc = None
        for direction in range(2):
            xcat = _bf(jnp.concatenate([xre_ref[direction], xim_ref[direction]], axis=1))
            term = _mm(kt_ref[0, direction, e], u) + _mm(cp_ref[0, direction, e], xcat, _NT)
            acc = term if acc is None else acc + term
        y_ref[e] = acc


def _s5_readout(ug, ktoep, cpad, xre, xim, layer):
    pairs = S5_GROUPS // 2
    rows = S5_T * S5_GROUP
    return pl.pallas_call(
        _s5_readout_kernel,
        grid=(pairs,),
        in_specs=[
            pl.BlockSpec((2, rows, S5_COLS), lambda q: (q, 0, 0)),
            pl.BlockSpec((1, 2, 2, rows, rows), lambda q: (layer, 0, q, 0, 0)),
            pl.BlockSpec((1, 2, 2, rows, rows), lambda q: (layer, 0, q, 0, 0)),
            pl.BlockSpec((2, S5_COLS, 128), lambda q: (0, 0, q)),
            pl.BlockSpec((2, S5_COLS, 128), lambda q: (0, 0, q)),
        ],
        out_specs=pl.BlockSpec((2, rows, S5_COLS), lambda q: (q, 0, 0)),
        out_shape=jax.ShapeDtypeStruct((S5_GROUPS, rows, S5_COLS), F32),
        compiler_params=_cparams("parallel"),
        name="s5_readout",
    )(ug, ktoep, cpad, xre, xim)


def _s5_out_kernel(*refs, ctx_out):
    if ctx_out:
        yg_ref, u_ref, g_ref, uc_ref, gc_ref, d_ref, w_ref, b_ref, o_ref, oc_ref = refs
    else:
        yg_ref, u_ref, g_ref, d_ref, w_ref, b_ref, o_ref = refs
    d = d_ref[...]
    w = w_ref[...]
    bias = b_ref[...]

    def finish(yy, u, gate):
        yy = yy + d * u
        yg = jax.nn.gelu(yy)
        out = yg * jax.nn.sigmoid(_mm(_bf(yg), w) + bias)
        return out * (gate * jax.nn.sigmoid(gate))

    def y_of(t):
        return yg_ref[:, t * S5_GROUP:(t + 1) * S5_GROUP, :].reshape(S5_WIDTH, S5_CB).T

    def lat():
        for t in range(S5_T):
            o_ref[0, :, t, :] = finish(y_of(t), u_ref[0, :, t, :], g_ref[0, :, t, :])

    if not ctx_out:
        lat()
        return
    j = pl.program_id(0)
    last = pl.num_programs(0) - 1
    pl.when(j < last)(lat)

    @pl.when(j == last)
    def _():
        n_ctx = uc_ref.shape[1]
        for t in range(S5_T):
            y = y_of(t)
            for b in range(2):
                oc_ref[b, :, t, :] = finish(y[b * n_ctx:(b + 1) * n_ctx], uc_ref[b, :, t, :], gc_ref[b, :, t, :])


def _s5_relayout_out(yg, p, pc, d_skip, w_glu, b_glu, *, ctx_out):
    bsz, seq, n = p.shape
    seq_c, n_c = pc.shape[1], pc.shape[2]
    n_lat, n_ctx = seq // S5_T, seq_c // S5_T
    n_half = n_lat // S5_CB
    idx = _s5_lat_index(n_half)
    pv = p.reshape(bsz, n_lat, S5_T, n)
    lat_block = (1, S5_CB, S5_T, 512)
    in_specs = [
        pl.BlockSpec((S5_GROUPS, S5_T * S5_GROUP, S5_CB), lambda j: (0, 0, j)),
        pl.BlockSpec(lat_block, lambda j: (*idx(j), 0, COL_S5U)),
        pl.BlockSpec(lat_block, lambda j: (*idx(j), 0, COL_S5G)),
    ]
    args = [yg, pv, pv]
    out_specs = [pl.BlockSpec(lat_block, lambda j: (*idx(j), 0, 0))]
    out_shape = [jax.ShapeDtypeStruct((bsz, n_lat, S5_T, S5_WIDTH), F32)]
    if ctx_out:
        pcv = pc.reshape(bsz, n_ctx, S5_T, n_c)
        ctx_block = (bsz, n_ctx, S5_T, 512)
        in_specs += [
            pl.BlockSpec(ctx_block, lambda j: (0, 0, 0, COL_S5U)),
            pl.BlockSpec(ctx_block, lambda j: (0, 0, 0, COL_S5G)),
        ]
        args += [pcv, pcv]
        out_specs.append(pl.BlockSpec(ctx_block, lambda j: (0, 0, 0, 0)))
        out_shape.append(jax.ShapeDtypeStruct((bsz, n_ctx, S5_T, S5_WIDTH), F32))
    in_specs += [
        pl.BlockSpec((1, S5_WIDTH), lambda j: (0, 0)),
        pl.BlockSpec((S5_WIDTH, S5_WIDTH), lambda j: (0, 0)),
        pl.BlockSpec((1, S5_WIDTH), lambda j: (0, 0)),
    ]
    args += [d_skip, w_glu, b_glu]
    outs = pl.pallas_call(
        functools.partial(_s5_out_kernel, ctx_out=ctx_out),
        grid=(bsz * n_half + (1 if ctx_out else 0),),
        in_specs=in_specs,
        out_specs=out_specs,
        out_shape=out_shape,
        compiler_params=_cparams("arbitrary"),
        name="s5_relayout_out",
    )(*args)
    y = outs[0].reshape(bsz, seq, S5_WIDTH)
    yc = outs[1].reshape(bsz, seq_c, S5_WIDTH) if ctx_out else None
    return y, yc


def _s5(p, pc, ops, layer, d_skip, w_glu, b_glu, *, ctx_out):
    ktoep, b_pair_re, b_pair_im, cpad, lam_re, lam_im = ops
    ug = _s5_relayout_in(p, pc)
    vre, vim = _s5_state_in(ug, b_pair_re, b_pair_im, layer)
    xre, xim = _s5_scan(vre, vim, lam_re, lam_im, layer, n_lat=p.shape[1] // S5_T, n_ctx=pc.shape[1] // S5_T)
    yg = _s5_readout(ug, ktoep, cpad, xre, xim, layer)
    return _s5_relayout_out(yg, p, pc, d_skip, w_glu, b_glu, ctx_out=ctx_out)


def _dft_tables():
    n1, n2 = FFT_N1, FFT_N2
    n = n1 * n2
    a = jnp.arange(n1, dtype=jnp.int32)
    ph1 = (a[:, None] * a[None, :]) % n1
    ang1 = ph1.astype(F32) * (2.0 * math.pi / n1)
    c1, s1 = jnp.cos(ang1), -jnp.sin(ang1)
    h = n1 // 2
    m1 = jnp.concatenate([jnp.concatenate([c1[:, :h], -s1[:, :h]], 1),
                          jnp.concatenate([s1[:, :h], c1[:, :h]], 1)], 0)
    m1_real = jnp.concatenate([c1, s1], 0)
    m3 = m1.T / n
    b = jnp.arange(n2, dtype=jnp.int32)
    ang2 = ((b[:, None] * b[None, :]) % n2).astype(F32) * (2.0 * math.pi / n2)
    fr, fi = jnp.cos(ang2), -jnp.sin(ang2)
    f2 = jnp.concatenate([jnp.concatenate([fr, -fi], 1), jnp.concatenate([fi, fr], 1)], 0)
    angt = ((a[:, None] * b[None, :]) % n).astype(F32) * (2.0 * math.pi / n)
    tw = jnp.stack([jnp.cos(angt), -jnp.sin(angt)], axis=1)
    tw = jnp.broadcast_to(tw[..., None], (n1, 2, n2, HY_CT))
    return tuple(_split(t) for t in (m1, m1_real, m3, f2, f2.T)) + (tw,)


def _filter_features(length):
    t = jnp.linspace(0.0, 1.0, length, dtype=F32)[:, None]
    ang = (2.0 * math.pi / length) * jnp.arange(length, dtype=F32)[:, None]
    bands = jnp.linspace(1e-4, HY_BANDS - 1, HY_BANDS, dtype=F32)[None, :]
    feats = jnp.concatenate([t, jnp.cos(bands * ang), -jnp.sin(bands * ang)], axis=-1)
    feats = jnp.pad(feats, ((0, 0), (0, 128 - HY_EMB)))
    rev = jnp.roll(feats[::-1], 1, axis=0)
    return jnp.stack([feats, rev])


def _filter_kernel(f_ref, w1_ref, b1_ref, f1_ref, w2_ref, b2_ref, f2_ref, w3_ref, dl_ref, o_ref):
    half = pl.program_id(1)
    tile = pl.program_id(2)
    x = f_ref[0]
    x_hi, x_lo = _split(x)
    h = jnp.sin(f1_ref[0] * (_mm3(x_hi, x_lo, w1_ref[0]) + b1_ref[0]))
    h_hi, h_lo = _split(h)
    h = jnp.sin(f2_ref[0] * (_mm3(h_hi, h_lo, w2_ref[0]) + b2_ref[0]))
    y = _mm(_bf(h), _bf(w3_ref[0, 0])) * jnp.exp(-x[:, 0:1] * dl_ref[...])
    rows = lax.broadcasted_iota(jnp.int32, y.shape, 0)
    drop = jnp.logical_and(jnp.logical_and(half == 1, tile == 0), rows == 0)
    o_ref[0] = jnp.where(drop, 0.0, y)


def _filter_taps(feats, w1, b1, f1, w2, b2, f2, w3, deltas, tmf):
    depth = w1.shape[0]
    length = feats.shape[1]
    nt = length // tmf
    wide = w3.shape[-1]
    vec = pl.BlockSpec((1, 1, 128), lambda l, s, i: (l, 0, 0))
    sq = pl.BlockSpec((1, 128, 128), lambda l, s, i: (l, 0, 0))
    return pl.pallas_call(
        _filter_kernel,
        grid=(depth, 2, nt),
        in_specs=[
            pl.BlockSpec((1, tmf, 128), lambda l, s, i: (s, i, 0)),
            sq, vec, vec, sq, vec, vec,
            pl.BlockSpec((1, 1, 128, wide), lambda l, s, i: (l, s, 0, 0)),
            pl.BlockSpec((1, wide), lambda l, s, i: (0, 0)),
        ],
        out_specs=pl.BlockSpec((1, tmf, wide), lambda l, s, i: (l, s * nt + i, 0)),
        out_shape=jax.ShapeDtypeStruct((depth, 2 * length, wide), F32),
        compiler_params=_cparams("parallel", "parallel", "parallel"),
        name="hyena_filter",
    )(feats, w1, b1, f1, w2, b2, f2, w3, deltas)


def _parts(pair):
    return tuple(pair) if HY_PASSES == 3 else tuple(pair[:1])


def _mmp(m, b):
    return _mm(m[0], _bf(b)) if len(m) == 1 else _mm3(m[0], m[1], b)


def _blk_rows(n2):
    return pl.ds(pl.multiple_of(n2 * HY_PITCH, 8), 2 * FFT_N1)


def _pad_rows(n2, n):
    return pl.ds(n2, n, stride=HY_PITCH)


def _pad_copy(src, dst, n):
    for n1 in range(n):
        dst[n1 * HY_PITCH:n1 * HY_PITCH + FFT_N2, :] = src[n1 * FFT_N2:(n1 + 1) * FFT_N2, :]


def _dft_stage1(read_rows, m, a_scr):
    ct = a_scr.shape[-1]

    def body(it, carry):
        n2 = it * HY_NB
        a = _mmp(m, jnp.concatenate([read_rows(n2 + j) for j in range(HY_NB)], axis=1))
        for j in range(HY_NB):
            a_scr[_blk_rows(n2 + j), :] = a[:, j * ct:(j + 1) * ct]
        return carry

    lax.fori_loop(0, FFT_N2 // HY_NB, body, 0)


def _stage2_rows(k1):
    return (pl.ds(k1, FFT_N2, stride=HY_PITCH), pl.ds(k1 + FFT_N1, FFT_N2, stride=HY_PITCH))


def _stage2_load(a_scr, tw_ref, kb):
    rows = [_stage2_rows(kb * HY_KB + i) for i in range(HY_KB)]
    vals = []
    for i, (re, im) in enumerate(rows):
        ar, ai = a_scr[re, :], a_scr[im, :]
        tr, ti = tw_ref[kb * HY_KB + i, 0], tw_ref[kb * HY_KB + i, 1]
        vals.append(jnp.concatenate([ar * tr - ai * ti, ar * ti + ai * tr], axis=0))
    return rows, vals


def _spectrum_kernel(*refs):
    np_ = len(_parts((0, 0)))
    m_refs, f_refs, (tw_ref, t_ref, o_ref, a_scr, pad_scr) = refs[:np_], refs[np_:2 * np_], refs[2 * np_:]
    kb = pl.program_id(2)
    ct = t_ref.shape[-1]

    @pl.when(kb == 0)
    def _():
        _pad_copy(t_ref.at[0], pad_scr, FFT_N1)
        _dft_stage1(lambda n2: pad_scr[_pad_rows(n2, FFT_N1), :], [r[...] for r in m_refs], a_scr)

    _, a = _stage2_load(a_scr, tw_ref, kb)
    f2 = [r[...] for r in f_refs]
    for i in range(HY_KB):
        o_ref[0, i] = _bf(_mmp(f2, a[i])).reshape(2, FFT_N2, ct)


def _filter_spectrum(tables, taps):
    m, f2, tw = _parts(tables[1]), _parts(tables[3]), tables[5]
    depth, n, ch = taps.shape
    ct = HY_CT
    const = lambda t: pl.BlockSpec(t.shape, lambda l, c, k: (0,) * t.ndim, pipeline_mode=pl.Buffered(1))
    return pl.pallas_call(
        _spectrum_kernel,
        grid=(depth, ch // ct, FFT_N1 // HY_KB),
        in_specs=[const(t) for t in (*m, *f2, tw)] + [pl.BlockSpec((1, n, ct), lambda l, c, k: (l, 0, c))],
        out_specs=pl.BlockSpec((1, HY_KB, 2, FFT_N2, ct), lambda l, c, k: (l, k, 0, 0, c)),
        out_shape=jax.ShapeDtypeStruct((depth, FFT_N1, 2, FFT_N2, ch), BF16),
        scratch_shapes=[pltpu.VMEM((HY_PITCH * FFT_N2, ct), F32), pltpu.VMEM((HY_PITCH * FFT_N1, ct), F32)],
        compiler_params=_cparams("parallel", "parallel", "arbitrary"),
        name="hyena_spectrum",
    )(*m, *f2, tw, taps)


def _hy_pre_kernel(pv_ref, p1_ref, p2_ref, pg_ref, wv_ref, w1_ref, w2_ref, bv_ref, b1_ref, b2_ref,
                   v_ref, x1_ref, x2_ref):
    n = pv_ref.shape[1]
    rows = lax.broadcasted_iota(jnp.int32, (n, 128), 0)

    def conv(p, w_ref, b_ref):
        up = jnp.where(rows == 0, 0.0, pltpu.roll(p, 1, 0))
        dn = jnp.where(rows == n - 1, 0.0, pltpu.roll(p, n - 1, 0))
        return up * w_ref[0:1, :] + p * w_ref[1:2, :] + dn * w_ref[2:3, :] + b_ref[...]

    v_ref[0] = conv(pv_ref[0], wv_ref, bv_ref)
    x1_ref[0] = conv(p1_ref[0], w1_ref, b1_ref)
    gate = pg_ref[0]
    x2_ref[0] = conv(p2_ref[0], w2_ref, b2_ref) * (gate * jax.nn.sigmoid(gate))


def _hy_pre(p, conv_w, conv_b):
    bsz, seq, _ = p.shape
    def pspec(off):
        return pl.BlockSpec((1, seq, 128), lambda b, j: (b, 0, off + j))
    def wspec(off):
        return pl.BlockSpec((3, 128), lambda b, j: (0, off + j))
    def bspec(off):
        return pl.BlockSpec((1, 128), lambda b, j: (0, off + j))
    out = jax.ShapeDtypeStruct((bsz, seq, HY_WIDTH), F32)
    ospec = pl.BlockSpec((1, seq, 128), lambda b, j: (b, 0, j))
    return pl.pallas_call(
        _hy_pre_kernel,
        grid=(bsz, HY_WIDTH // 128),
        in_specs=[pspec(COL_HY), pspec(COL_HY + 4), pspec(COL_HY + 8), pspec(COL_HY + 12),
                  wspec(0), wspec(4), wspec(8), bspec(0), bspec(4), bspec(8)],
        out_specs=[ospec, ospec, ospec],
        out_shape=[out, out, out],
        compiler_params=_cparams("parallel", "parallel"),
        name="hyena_pre",
    )(p, p, p, p, conv_w, conv_w, conv_w, conv_b, conv_b, conv_b)


def _hy_conv_kernel(*refs):
    np_ = len(_parts((0, 0)))
    m1_refs, m3_refs, f_refs, ft_refs = (refs[i * np_:(i + 1) * np_] for i in range(4))
    tw_ref, u_ref, gate_ref, h_ref, d_ref, o_ref, a_scr, pad_scr = refs[4 * np_:]
    kb = pl.program_id(1)
    ct = u_ref.shape[-1]
    half = FFT_N1 // 2

    @pl.when(kb == 0)
    def _():
        for b in range(2):
            _pad_copy(u_ref.at[b], pad_scr.at[b], half)
        _dft_stage1(lambda n2: jnp.concatenate([pad_scr[0, _pad_rows(n2, half), :],
                                                pad_scr[1, _pad_rows(n2, half), :]], axis=0),
                    [r[...] for r in m1_refs], a_scr)

    srows, a = _stage2_load(a_scr, tw_ref, kb)
    f2, f2t = [r[...] for r in f_refs], [r[...] for r in ft_refs]
    z = [_mmp(f2, a[i]) for i in range(HY_KB)]
    z2 = []
    for i in range(HY_KB):
        zr, zi = z[i][:FFT_N2], z[i][FFT_N2:]
        hr, hi = h_ref[0, i, 0].astype(F32), h_ref[0, i, 1].astype(F32)
        z2.append(jnp.concatenate([zr * hr - zi * hi, zr * hi + zi * hr], axis=0))
    a2 = [_mmp(f2t, z2[i]) for i in range(HY_KB)]
    for i in range(HY_KB):
        cr, ci = a2[i][:FFT_N2], a2[i][FFT_N2:]
        tr, ti = tw_ref[kb * HY_KB + i, 0], tw_ref[kb * HY_KB + i, 1]
        a_scr[srows[i][0], :] = cr * tr + ci * ti
        a_scr[srows[i][1], :] = ci * tr - cr * ti

    @pl.when(kb == pl.num_programs(1) - 1)
    def _():
        m3 = [r[...] for r in m3_refs]
        d = d_ref[...]

        def body(it, carry):
            n2 = it * HY_NB
            y = _mmp(m3, jnp.concatenate([a_scr[_blk_rows(n2 + j), :] for j in range(HY_NB)], axis=1))
            for j in range(HY_NB):
                for b in range(2):
                    pad_scr[b, _pad_rows(n2 + j, half), :] = y[b * half:(b + 1) * half, j * ct:(j + 1) * ct]
            return carry

        lax.fori_loop(0, FFT_N2 // HY_NB, body, 0)
        for b in range(2):
            for n1 in range(half):
                r = slice(n1 * FFT_N2, (n1 + 1) * FFT_N2)
                conv = pad_scr[b, n1 * HY_PITCH:n1 * HY_PITCH + FFT_N2, :]
                o_ref[b, r, :] = gate_ref[b, r, :] * (conv + d * u_ref[b, r, :])


def _hy_long_conv(tables, spec, layer, order, u, gate, d):
    m1, _, m3, f2, f2t = (_parts(t) for t in tables[:5])
    bsz, seq, ch = u.shape
    ct = HY_CT
    const = lambda t: pl.BlockSpec(t.shape, lambda c, k: (0,) * t.ndim, pipeline_mode=pl.Buffered(1))
    uspec = pl.BlockSpec((bsz, seq, ct), lambda c, k: (0, 0, c))
    n_ct = ch // ct
    mats = (*m1, *m3, *f2, *f2t, tables[5])
    return pl.pallas_call(
        _hy_conv_kernel,
        grid=(n_ct, FFT_N1 // HY_KB),
        in_specs=[const(t) for t in mats] + [
            uspec, uspec,
            pl.BlockSpec((1, HY_KB, 2, FFT_N2, ct), lambda c, k: (layer, k, 0, 0, order * n_ct + c)),
            pl.BlockSpec((1, ct), lambda c, k: (0, c))],
        out_specs=uspec,
        out_shape=jax.ShapeDtypeStruct(u.shape, F32),
        scratch_shapes=[pltpu.VMEM((HY_PITCH * FFT_N2, ct), F32),
                        pltpu.VMEM((bsz, HY_PITCH * FFT_N1 // 2, ct), F32)],
        compiler_params=_cparams("parallel", "arbitrary"),
        name="hyena_conv",
    )(*mats, u, gate, spec, d.reshape(1, ch))


def _ctx_dft_tables(length):
    n = 2 * length
    k = jnp.arange(n, dtype=jnp.int32)
    ang = ((k[:, None] * k[None, :]) % n).astype(F32) * (2.0 * math.pi / n)
    fr, fi = jnp.cos(ang), -jnp.sin(ang)
    fwd = jnp.concatenate([jnp.concatenate([fr[:, :length], -fi[:, :length]], 1),
                           jnp.concatenate([fi[:, :length], fr[:, :length]], 1)], 0)
    real = jnp.concatenate([fr, fi], 0)
    inv = fwd.T / n
    return tuple(_split(t) for t in (fwd, real, inv))


def _hyc_kernel(fh_ref, fl_ref, rh_ref, rl_ref, ih_ref, il_ref, taps_ref, u_ref, g_ref, d_ref, o_ref):
    n = u_ref.shape[1]
    x = jnp.concatenate([u_ref[0], u_ref[1]], axis=0)
    z = _mm3(fh_ref[...], fl_ref[...], x)
    h = _mm3(rh_ref[...], rl_ref[...], taps_ref[0])
    m = 2 * n
    zr, zi, hr, hi = z[:m], z[m:], h[:m], h[m:]
    z2 = jnp.concatenate([zr * hr - zi * hi, zr * hi + zi * hr], axis=0)
    y = _mm3(ih_ref[...], il_ref[...], z2)
    for b in range(2):
        o_ref[b] = g_ref[b] * (y[b * n:(b + 1) * n] + d_ref[...] * u_ref[b])


def _hyc_long_conv(ctabs, taps, layer, order, u, gate, d):
    (f_hi, f_lo), (r_hi, r_lo), (i_hi, i_lo) = ctabs
    bsz, n, ch = u.shape
    full = lambda t: pl.BlockSpec(t.shape, lambda i: (0,) * t.ndim)
    uspec = pl.BlockSpec((bsz, n, ch), lambda i: (0, 0, 0))
    return pl.pallas_call(
        _hyc_kernel,
        grid=(1,),
        in_specs=[full(f_hi), full(f_lo), full(r_hi), full(r_lo), full(i_hi), full(i_lo),
                  pl.BlockSpec((1, 2 * n, ch), lambda i: (layer, 0, order)),
                  uspec, uspec, pl.BlockSpec((1, ch), lambda i: (0, 0))],
        out_specs=uspec,
        out_shape=jax.ShapeDtypeStruct((bsz, n, ch), F32),
        compiler_params=_cparams("arbitrary"),
        name="hyena_ctx",
    )(f_hi, f_lo, r_hi, r_lo, i_hi, i_lo, taps, u, gate, d.reshape(1, ch))


def _outproj_kernel(*refs, colmajor, tm):
    if colmajor:
        x_ref, ya_ref, yb_ref, yc_ref, w_ref, gp_ref, gate_ref, perm_ref, o_ref = refs
        ld = lambda r: _bf(_mm(perm_ref[...], _bf(r[0])))
    else:
        x_ref, ya_ref, yb_ref, yc_ref, w_ref, gp_ref, gate_ref, o_ref = refs
        ld = lambda r: _bf(r[0])
    na, nb = ya_ref.shape[-1], yb_ref.shape[-1]
    acc = (_mm(ld(ya_ref), w_ref[0, 0:na, :]) + _mm(ld(yb_ref), w_ref[0, na:na + nb, :])
           + _mm(ld(yc_ref), w_ref[0, na + nb:, :]))
    ms = jnp.mean(acc * acc, axis=-1, keepdims=True)
    upd = gate_ref[0] * (acc * lax.rsqrt(ms + EPS) * gp_ref[...])
    o_ref[0] = x_ref[0] + upd.reshape(x_ref.shape[1:])


def _outproj(x, ya, yb, yc, w_out, layer, g_post, gate, *, colmajor, tm):
    bsz, seq, d = x.shape
    extra_specs, extra_args = [], []
    if colmajor:
        cols = tm // GRID_W
        x_in = x.reshape(bsz, GRID_W, seq // GRID_W, d)
        x_spec = pl.BlockSpec((1, GRID_W, cols, d), lambda b, m: (b, 0, m, 0))
        r_old = jnp.arange(tm, dtype=jnp.int32)[:, None]
        r_new = jnp.arange(tm, dtype=jnp.int32)[None, :]
        perm = _bf((r_new == (r_old % cols) * GRID_W + r_old // cols).astype(F32))
        extra_specs, extra_args = [pl.BlockSpec((tm, tm), lambda b, m: (0, 0))], [perm]
    else:
        x_in = x
        x_spec = pl.BlockSpec((1, tm, d), lambda b, m: (b, m, 0))
    def yspec(t):
        return pl.BlockSpec((1, tm, t.shape[-1]), lambda b, m: (b, m, 0))
    out = pl.pallas_call(
        functools.partial(_outproj_kernel, colmajor=colmajor, tm=tm),
        grid=(bsz, seq // tm),
        in_specs=[x_spec, yspec(ya), yspec(yb), yspec(yc),
                  pl.BlockSpec((1,) + w_out.shape[1:], lambda b, m: (layer, 0, 0)),
                  pl.BlockSpec((1, d), lambda b, m: (0, 0)),
                  pl.BlockSpec((1, 1, d), lambda b, m: (b, 0, 0))] + extra_specs,
        out_specs=x_spec,
        out_shape=jax.ShapeDtypeStruct(x_in.shape, F32),
        compiler_params=_cparams("parallel", "parallel"),
        name="outproj",
    )(x_in, ya, yb, yc, w_out, g_post, gate, *extra_args)
    return out.reshape(bsz, seq, d)


def kernel(x, c, ctx, c_ctx, w_mod, b_mod, g_pre, g_post, w_in, w_out, gla_w_gate, gla_b_gate, gla_norm, s5_lam_re, s5_lam_im, s5_log_step, s5_b_re, s5_b_im, s5_c_re, s5_c_im, s5_d, s5_w_glu, s5_b_glu, hy_conv_w, hy_conv_b, hy_w1, hy_b1, hy_f1, hy_w2, hy_b2, hy_f2, hy_w3, hy_d):
    bsz, seq, d = x.shape
    seq_c = ctx.shape[1]
    depth = w_in.shape[0]
    assert (bsz, d, seq // GRID_W) == (2, D_MODEL, GRID_W) and seq == FFT_N1 * FFT_N2 // 2

    cvec = jnp.concatenate([c, c_ctx[None], jnp.zeros((8 - bsz - 1, d), F32)], axis=0)
    mod = _modulation(cvec, w_mod, b_mod)
    shift, scale, gate = mod[..., :d], mod[..., d:2 * d], mod[..., 2 * d:]

    w_t = _bf(jnp.swapaxes(w_in, 1, 2))
    w_lr = w_t[:, N_QKV:N_QKV + N_LR]
    w_lr3 = jnp.concatenate([w_lr, w_lr, w_lr, jnp.zeros_like(w_lr)], axis=1)
    w_out_b = _bf(w_out)
    w_glu_b = _bf(s5_w_glu)

    wg = gla_w_gate.reshape(depth, 2, 16, GLA_HEADS, GLA_DK).transpose(0, 1, 3, 2, 4)
    wg_pad = jnp.stack([jnp.pad(wg[:, 0], ((0, 0), (0, 0), (0, 112), (0, 0))),
                        jnp.pad(wg[:, 1], ((0, 0), (0, 0), (16, 96), (0, 0)))], axis=1)
    wg_hi, wg_lo = _split(wg_pad[..., :N_LR, :])
    wg3 = jnp.concatenate([wg_hi, wg_hi, wg_lo, jnp.zeros_like(wg_hi)], axis=-2)
    bg = gla_b_gate.reshape(depth, 2, GLA_HEADS, 1, GLA_DK)

    tables = _dft_tables()
    deltas = jnp.abs(jnp.linspace(HY_MIN_DECAY, HY_MAX_DECAY, HY_WIDTH, dtype=F32))
    deltas2 = jnp.tile(deltas, 2).reshape(1, 2 * HY_WIDTH)
    pad_k = lambda w: jnp.pad(w, ((0, 0), (0, 128 - w.shape[1]), (0, 128 - w.shape[2])))
    pad_v = lambda v: jnp.pad(v, ((0, 0), (0, 128 - v.shape[1]))).reshape(depth, 1, 128)
    w3 = hy_w3.reshape(depth, HY_FFN, 2, 2, HY_WIDTH).transpose(0, 3, 1, 2, 4)
    w3 = jnp.pad(w3.reshape(depth, 2, HY_FFN, 2 * HY_WIDTH), ((0, 0), (0, 0), (0, 128 - HY_FFN), (0, 0)))
    filt_w = (pad_k(hy_w1), pad_v(hy_b1), pad_v(hy_f1), pad_k(hy_w2), pad_v(hy_b2), pad_v(hy_f2), w3, deltas2)
    taps = _filter_taps(_filter_features(seq), *filt_w, tmf=512)
    spec = _filter_spectrum(tables, taps)
    taps_c = _filter_taps(_filter_features(seq_c), *filt_w, tmf=seq_c)
    ctabs = _ctx_dft_tables(seq_c)

    s5_ops = _s5_operators(s5_lam_re, s5_lam_im, s5_log_step, s5_b_re, s5_b_im, s5_c_re, s5_c_im)

    xc = ctx
    for l in range(depth):
        last = l == depth - 1
        ctx_out = not last
        colmajor = l % 2 == 1
        pre = g_pre[l][None, :] * (1.0 + scale[l])
        sc_l, sh_l = pre[:bsz, None, :], shift[l, :bsz, None, :]
        sc_c = jnp.broadcast_to(pre[bsz][None, None, :], (bsz, 1, d))
        sh_c = jnp.broadcast_to(shift[l, bsz][None, None, :], (bsz, 1, d))
        qkv, p, lr = _inproj(x, sc_l, sh_l, w_t, w_lr3, l, n=N_MAIN, colmajor=colmajor, tm=1024, tn=512)
        qkvc, pc, lrc = _inproj(xc, sc_c, sh_c, w_t, w_lr3, l, n=N_MAIN if ctx_out else N_STATE,
                                colmajor=False, tm=seq_c, tn=512)

        y_gla, yc_gla = _gla(qkv, p, lr, qkvc, pc, lrc, wg3[l], bg[l], gla_norm[l][None, :],
                             ctx_out=ctx_out)

        y_s5, yc_s5 = _s5(p, pc, s5_ops, l, s5_d[l][None, :], w_glu_b[l], s5_b_glu[l][None, :], ctx_out=ctx_out)

        v, x1, x2 = _hy_pre(p, hy_conv_w[l], hy_conv_b[l][None, :])
        z = _hy_long_conv(tables, spec, l, 0, v, x1, hy_d[l, 0])
        y_hy = _hy_long_conv(tables, spec, l, 1, z, x2, hy_d[l, 1])

        x_new = _outproj(x, y_gla, y_s5, y_hy, w_out_b, l, g_post[l][None, :], gate[l, :bsz, None, :],
                         colmajor=colmajor, tm=512)
        if ctx_out:
            vc, x1c, x2c = _hy_pre(pc, hy_conv_w[l], hy_conv_b[l][None, :])
            zc = _hyc_long_conv(ctabs, taps_c, l, 0, vc, x1c, hy_d[l, 0])
            yc_hy = _hyc_long_conv(ctabs, taps_c, l, 1, zc, x2c, hy_d[l, 1])
            gate_c = jnp.broadcast_to(gate[l, bsz][None, None, :], (bsz, 1, d))
            xc = _outproj(xc, yc_gla, yc_s5, yc_hy, w_out_b, l, g_post[l][None, :], gate_c,
                          colmajor=False, tm=seq_c)
        x = x_new
    return x
```

```python
import functools
import math

import numpy as np
import jax
import jax.numpy as jnp
from jax import lax
from jax.experimental import pallas as pl
from jax.experimental.pallas import tpu as pltpu

F32 = jnp.float32
BF16 = jnp.bfloat16

D_MODEL = 2048
GRID_W = 64
EPS = 1e-6

GLA_HEADS = 4
GLA_DK = 128
GLA_DV = 256
GLA_TAU = 16.0
GLA_CHUNK = 64
GLA_BLOCK = 16

S5_WIDTH = 512
S5_GROUP = 16
S5_GROUPS = 32
S5_STATE = 64
S5_T = 16
S5_COLS = 640
S5_CB = 128

HY_WIDTH = 512
HY_EMB = 33
HY_BANDS = 16
HY_FFN = 64
HY_MIN_DECAY = math.log(1e-2) / 0.3
HY_MAX_DECAY = math.log(1e-2) / 1.5
FFT_N1 = 64
FFT_N2 = 128
HY_CT = 128
HY_KB = 8
HY_NB = 32
HY_PITCH = 136
HY_PASSES = 1

N_MAIN = 6144
N_STATE = 2560
N_QKV = 2048
N_LR = 32
COL_S5U, COL_S5G = 0, 3
COL_GLA_GATE = 2
COL_HY = 16

VMEM_LIMIT_BYTES = 56 * 1024 * 1024


def _cparams(*sem):
    return pltpu.CompilerParams(dimension_semantics=sem, vmem_limit_bytes=VMEM_LIMIT_BYTES)


def _bf(x):
    return x.astype(BF16)


def _split(x):
    hi = _bf(x)
    return hi, _bf(x - hi.astype(F32))


def _split3(x):
    a = _bf(x)
    r = x - a.astype(F32)
    b = _bf(r)
    return a, b, _bf(r - b.astype(F32))


_NN = (((1,), (0,)), ((), ()))
_NT = (((1,), (1,)), ((), ()))
_TN = (((0,), (0,)), ((), ()))


def _mm(a, b, dims=_NN):
    return lax.dot_general(a, b, dims, preferred_element_type=F32)


def _mm3(a_hi, a_lo, b, dims=_NN):
    b_hi, b_lo = _split(b)
    return _mm(a_hi, b_hi, dims) + _mm(a_hi, b_lo, dims) + _mm(a_lo, b_hi, dims)


def _mod_kernel(s_ref, w_ref, b_ref, o_ref):
    s = s_ref[...]
    s = s * jax.nn.sigmoid(s)
    s_hi, s_lo = _split(s)
    o_ref[0] = _mm3(s_hi, s_lo, w_ref[0]) + b_ref[0]


def _modulation(cvec, w_mod, b_mod):
    depth, d, n = w_mod.shape
    tn = 512
    return pl.pallas_call(
        _mod_kernel,
        grid=(depth, n // tn),
        in_specs=[
            pl.BlockSpec((8, d), lambda l, j: (0, 0)),
            pl.BlockSpec((1, d, tn), lambda l, j: (l, 0, j)),
            pl.BlockSpec((1, 1, tn), lambda l, j: (l, 0, j)),
        ],
        out_specs=pl.BlockSpec((1, 8, tn), lambda l, j: (l, 0, j)),
        out_shape=jax.ShapeDtypeStruct((depth, 8, n), F32),
        compiler_params=_cparams("parallel", "parallel"),
        name="modulation",
    )(cvec, w_mod, b_mod.reshape(depth, 1, n))


def _inproj_kernel(*refs, colmajor, tm, tn):
    if colmajor:
        x_ref, sc_ref, sh_ref, w_ref, wlr_ref, perm_ref, qkv_ref, p_ref, lr_ref, h_scr = refs
    else:
        x_ref, sc_ref, sh_ref, w_ref, wlr_ref, qkv_ref, p_ref, lr_ref, h_scr = refs
    j = pl.program_id(2)

    @pl.when(j == 0)
    def _():
        def norm(xb):
            ms = jnp.mean(xb * xb, axis=-1, keepdims=True)
            return _bf(xb * lax.rsqrt(ms + EPS) * sc_ref[0] + sh_ref[0])

        if colmajor:
            h = norm(x_ref[0].reshape(tm, D_MODEL))
            for c in range(0, D_MODEL, 512):
                h_scr[:, c:c + 512] = _bf(_mm(perm_ref[...], h[:, c:c + 512]))
        else:
            h_scr[...] = norm(x_ref[0])
        lr_ref[0] = _mm(h_scr[...], wlr_ref[0], _NT)

    res = _mm(h_scr[...], w_ref[...], _NT)
    n_qkv = N_QKV // tn

    @pl.when(j < n_qkv)
    def _():
        qkv_ref[0] = _bf(res)

    @pl.when(j >= n_qkv)
    def _():
        p_ref[0] = res


def _inproj(x, scale, shift, w_t, w_lr3, layer, *, n, colmajor, tm, tn):
    bsz, seq, d = x.shape
    n_qkv = N_QKV // tn
    extra_specs, extra_args = [], []
    if colmajor:
        cols = tm // GRID_W
        x_in = x.reshape(bsz, GRID_W, seq // GRID_W, d)
        x_spec = pl.BlockSpec((1, GRID_W, cols, d), lambda b, m, j: (b, 0, m, 0))
        r_new = jnp.arange(tm, dtype=jnp.int32)[:, None]
        r_old = jnp.arange(tm, dtype=jnp.int32)[None, :]
        perm = _bf((r_old == (r_new % GRID_W) * cols + r_new // GRID_W).astype(F32))
        extra_specs, extra_args = [pl.BlockSpec((tm, tm), lambda b, m, j: (0, 0))], [perm]
    else:
        x_in = x
        x_spec = pl.BlockSpec((1, tm, d), lambda b, m, j: (b, m, 0))
    return pl.pallas_call(
        functools.partial(_inproj_kernel, colmajor=colmajor, tm=tm, tn=tn),
        grid=(bsz, seq // tm, n // tn),
        in_specs=[
            x_spec,
            pl.BlockSpec((1, 1, d), lambda b, m, j: (b, 0, 0)),
            pl.BlockSpec((1, 1, d), lambda b, m, j: (b, 0, 0)),
            pl.BlockSpec((pl.Squeezed(), pl.Element(tn), pl.Element(d)),
                         lambda b, m, j: (layer, pl.multiple_of(j * tn + jnp.where(j >= n_qkv, N_LR, 0), N_LR), 0)),
            pl.BlockSpec((1, 128, d), lambda b, m, j: (layer, 0, 0)),
        ] + extra_specs,
        out_specs=[
            pl.BlockSpec((1, tm, tn), lambda b, m, j: (b, m, jnp.minimum(j, n_qkv - 1))),
            pl.BlockSpec((1, tm, tn), lambda b, m, j: (b, m, jnp.maximum(j - n_qkv, 0))),
            pl.BlockSpec((1, tm, 128), lambda b, m, j: (b, m, 0)),
        ],
        out_shape=[jax.ShapeDtypeStruct((bsz, seq, N_QKV), BF16),
                   jax.ShapeDtypeStruct((bsz, seq, n - N_QKV), F32),
                   jax.ShapeDtypeStruct((bsz, seq, 128), F32)],
        scratch_shapes=[pltpu.VMEM((tm, d), BF16)],
        compiler_params=_cparams("parallel", "parallel", "arbitrary"),
        name="inproj",
    )(x_in, scale, shift, w_t, w_lr3, *extra_args)


def _gla_block(q, k, v, lr, st, wg, bg, direction, mask, rowc, hi_lanes, nb):
    c = GLA_CHUNK
    n = c * nb
    lr_hi, lr_lo = _split(lr)
    z = _mm(jnp.where(hi_lanes, lr_hi, lr_lo), wg) + bg
    g = (jnp.minimum(z, 0.0) - jnp.log(1.0 + jnp.exp(-jnp.abs(z)))) * (1.0 / GLA_TAU)
    cum = g
    s = 1
    while s < c:
        if direction == 0:
            cum = cum + jnp.where(rowc >= s, pltpu.roll(cum, s, 0), 0.0)
        else:
            cum = cum + jnp.where(rowc < c - s, pltpu.roll(cum, n - s, 0), 0.0)
        s *= 2
    cum3 = cum.reshape(nb, c, GLA_DK)
    tot3 = cum3[:, c - 1:c, :] if direction == 0 else cum3[:, 0:1, :]
    q3 = q.reshape(nb, c, GLA_DK)
    k3 = k.reshape(nb, c, GLA_DK)
    qg3 = _bf(q3 * jnp.exp(cum3))
    kg3 = _bf(k3 * jnp.exp(-cum3))
    kd3 = _bf(k3 * jnp.exp(tot3 - cum3))
    a3 = jnp.exp(tot3)
    vb3 = _bf(v).reshape(nb, c, GLA_DV)
    att = [jnp.where(mask, _mm(qg3[i], kg3[i], _NT), 0.0) for i in range(nb)]
    o_in = [_mm(_bf(att[i]), vb3[i]) for i in range(nb)]
    d_st = [_mm(vb3[i], kd3[i], _TN) for i in range(nb)]
    outs = [None] * nb
    for i in (range(nb) if direction == 0 else range(nb - 1, -1, -1)):
        outs[i] = o_in[i] + _mm(qg3[i], _bf(st), _NT)
        st = st * a3[i] + d_st[i]
    return jnp.concatenate(outs, axis=0), st


def _gla_kernel(*refs, ctx_out, seq, seq_c):
    if ctx_out:
        (q_ref, k_ref, v_ref, gt_ref, lr_ref, qc_ref, kc_ref, vc_ref, gtc_ref, lrc_ref,
         wg_ref, bg_ref, gn_ref, y_ref, yc_ref) = refs
    else:
        (q_ref, k_ref, v_ref, gt_ref, lr_ref, qc_ref, kc_ref, vc_ref, lrc_ref,
         wg_ref, bg_ref, gn_ref, y_ref) = refs
        gtc_ref = yc_ref = None
    c = GLA_CHUNK
    scale = GLA_DK ** -0.5
    row = lax.broadcasted_iota(jnp.int32, (c, c), 0)
    col = lax.broadcasted_iota(jnp.int32, (c, c), 1)
    masks = (col <= row, col >= row)
    gn = gn_ref[...]

    def run(refs4, n_rows, st, direction, store):
        qr, kr, vr, lrr = refs4
        nb = min(GLA_BLOCK, n_rows // c)
        n = c * nb
        rowc = lax.broadcasted_iota(jnp.int32, (n, GLA_DK), 0) % c
        hi_lanes = (lax.broadcasted_iota(jnp.int32, (n, 128), 1) // N_LR) % 2 == 0
        wg = wg_ref[direction, 0]
        bg = bg_ref[direction, 0]
        n_blocks = n_rows // n

        def body(i, st):
            bi = i if direction == 0 else n_blocks - 1 - i
            rows = pl.ds(pl.multiple_of(bi * n, n), n)
            q = qr[0, rows, :].astype(F32) * scale
            k = kr[0, rows, :].astype(F32)
            v = vr[0, rows, :].astype(F32)
            o, st = _gla_block(q, k, v, lrr[0, rows, :], st, wg, bg, direction,
                               masks[direction], rowc, hi_lanes, nb)
            store(rows, q, k, v, o)
            return st

        return lax.fori_loop(0, n_blocks, body, st)

    def store_fwd(out_ref):
        def f(rows, q, k, v, o):
            if out_ref is not None:
                out_ref[0, rows, :] = o
        return f

    def store_bwd(out_ref, gate_ref):
        def f(rows, q, k, v, o):
            if out_ref is None:
                return
            y = out_ref[0, rows, :] + o - jnp.sum(q * k, axis=-1, keepdims=True) * v
            ms = jnp.mean(y * y, axis=-1, keepdims=True)
            y = y * lax.rsqrt(ms + EPS) * gn
            gt = gate_ref[0, rows, :]
            out_ref[0, rows, :] = y * (gt * jax.nn.sigmoid(gt))
        return f

    lat = (q_ref, k_ref, v_ref, lr_ref)
    ctx = (qc_ref, kc_ref, vc_ref, lrc_ref)
    zero = jnp.zeros((GLA_DV, GLA_DK), F32)
    st = run(ctx, seq_c, zero, 0, store_fwd(yc_ref))
    run(lat, seq, st, 0, store_fwd(y_ref))
    st = run(ctx, seq_c, zero, 1, store_bwd(yc_ref, gtc_ref))
    run(lat, seq, st, 1, store_bwd(y_ref, gt_ref))


def _gla(qkv, p, lr, qkvc, pc, lrc, wg, bg, gnorm, *, ctx_out):
    bsz, seq, _ = qkv.shape
    seq_c = qkvc.shape[1]
    h = GLA_HEADS

    def specs(n):
        return [
            pl.BlockSpec((1, n, 128), lambda b, i: (b, 0, i)),
            pl.BlockSpec((1, n, 128), lambda b, i: (b, 0, 4 + i)),
            pl.BlockSpec((1, n, 256), lambda b, i: (b, 0, 4 + i)),
            pl.BlockSpec((1, n, 256), lambda b, i: (b, 0, COL_GLA_GATE + i)),
            pl.BlockSpec((1, n, 128), lambda b, i: (b, 0, 0)),
        ]

    lat_specs = specs(seq)
    ctx_specs = specs(seq_c)
    lat_args = [qkv, qkv, qkv, p, lr]
    ctx_args = [qkvc, qkvc, qkvc, pc, lrc]
    if not ctx_out:
        del ctx_specs[3], ctx_args[3]
    w_specs = [
        pl.BlockSpec((2, 1, 128, 128), lambda b, i: (0, i, 0, 0)),
        pl.BlockSpec((2, 1, 1, 128), lambda b, i: (0, i, 0, 0)),
        pl.BlockSpec((1, GLA_DV), lambda b, i: (0, 0)),
    ]
    out_specs = [pl.BlockSpec((1, seq, 256), lambda b, i: (b, 0, i))]
    out_shape = [jax.ShapeDtypeStruct((bsz, seq, h * GLA_DV), F32)]
    if ctx_out:
        out_specs.append(pl.BlockSpec((1, seq_c, 256), lambda b, i: (b, 0, i)))
        out_shape.append(jax.ShapeDtypeStruct((bsz, seq_c, h * GLA_DV), F32))
    outs = pl.pallas_call(
        functools.partial(_gla_kernel, ctx_out=ctx_out, seq=seq, seq_c=seq_c),
        grid=(bsz, h),
        in_specs=lat_specs + ctx_specs + w_specs,
        out_specs=out_specs,
        out_shape=out_shape,
        compiler_params=_cparams("parallel", "parallel"),
        name="gla",
    )(*lat_args, *ctx_args, wg, bg, gnorm)
    return (outs[0], outs[1]) if ctx_out else (outs[0], None)


def _mm3f(a, b, dims=_NN):
    a_hi, a_lo = _split(a)
    return _mm3(a_hi, a_lo, b, dims)


def _s5_ops_kernel(a_ref, b_ref, lr_ref, li_ref, btr_ref, bti_ref, cr_ref, ci_ref,
                   kt_ref, bpr_ref, bpi_ref, cp_ref, ltr_ref, lti_ref):
    t_len, h_n, p_n = S5_T, S5_GROUP, S5_STATE
    rows = t_len * h_n
    lane = lax.broadcasted_iota(jnp.int32, (rows, 2 * p_n), 1)
    own = (lane < p_n, lane >= p_n)
    lane_blk = lax.broadcasted_iota(jnp.int32, (rows, rows), 1) // h_n
    kk = lax.broadcasted_iota(jnp.int32, (24, 2 * p_n), 0).astype(F32)
    for d in range(2):
        a, b = a_ref[0, d, 0], b_ref[0, d, 0]
        lr, li = lr_ref[0, d, 0], li_ref[0, d, 0]
        mag = jnp.exp(kk * a)
        pwr, pwi = mag * jnp.cos(kk * b), mag * jnp.sin(kk * b)

        def times_pw(k, mr, mi):
            pr, pi = pwr[k:k + 1], pwi[k:k + 1]
            return pr * mr - pi * mi, pr * mi + pi * mr

        x, y = pwr[1:2] - 1.0, pwi[1:2]
        den = lr * lr + li * li
        cfr, cfi = (x * lr + y * li) / den, (y * lr - x * li) / den
        btr, bti = btr_ref[0, d, 0], bti_ref[0, d, 0]
        bbr, bbi = cfr * btr - cfi * bti, cfr * bti + cfi * btr
        cr, ci = cr_ref[0, d, 0], ci_ref[0, d, 0]

        def stack(power_of_t, mr, mi):
            parts = [times_pw(power_of_t(t), mr, mi) for t in range(t_len)]
            return (jnp.concatenate([p[0] for p in parts], axis=0),
                    jnp.concatenate([p[1] for p in parts], axis=0))

        bpr, bpi = stack((lambda t: t_len - 1 - t) if d == 0 else (lambda t: t), bbr, bbi)
        for src, dst in ((bpr, bpr_ref), (bpi, bpi_ref)):
            z = jnp.concatenate([jnp.where(own[0], src, 0.0), jnp.where(own[1], src, 0.0)], axis=0)
            dst[0, d, 0] = _bf(z.T)
        cpr, cpi = stack((lambda t: t + 1) if d == 0 else (lambda t: t_len - t), cr, ci)
        clr, cli = stack((lambda j: j) if d == 0 else (lambda j: t_len - 1 - j), cr, ci)
        bbr_t = jnp.concatenate([bbr] * t_len, axis=0)
        bbi_t = jnp.concatenate([bbi] * t_len, axis=0)
        for e in range(2):
            cp_ref[0, d, e] = _bf(jnp.concatenate(
                [jnp.where(own[e], cpr, 0.0), jnp.where(own[e], -cpi, 0.0)], axis=1))
            kc = (_mm3f(jnp.where(own[e], clr, 0.0), bbr_t, _NT)
                  - _mm3f(jnp.where(own[e], cli, 0.0), bbi_t, _NT))
            acc = jnp.zeros((rows, rows), F32)
            for t in range(t_len):
                if d == 0:
                    pieces = [jnp.zeros((h_n * t, rows), F32), kc[:rows - h_n * t]]
                else:
                    s = h_n * (t_len - 1 - t)
                    pieces = [kc[s:], jnp.zeros((s, rows), F32)]
                shifted = jnp.concatenate([p for p in pieces if p.shape[0]], axis=0)
                acc = jnp.where(lane_blk == t, shifted, acc)
            kt_ref[0, d, e] = _bf(acc)
        ltr_ref[0, d, 0] = pwr[t_len:t_len + 1]
        lti_ref[0, d, 0] = pwi[t_len:t_len + 1]


def _s5_operators(lam_re, lam_im, log_step, b_re, b_im, c_re, c_im):
    depth = lam_re.shape[0]
    g_n, p_n, h_n = S5_GROUPS, S5_STATE, S5_GROUP
    pairs, rows = g_n // 2, S5_T * h_n
    dt = jnp.exp(log_step)[..., None]
    vec = lambda v: v.reshape(depth, 2, pairs, 1, 2 * p_n)
    bt = lambda m: m.reshape(depth, 2, pairs, 2, p_n, h_n).transpose(0, 1, 2, 5, 3, 4).reshape(depth, 2, pairs, h_n, 2 * p_n)
    ct = lambda m: m.reshape(depth, 2, pairs, 2, h_n, p_n).transpose(0, 1, 2, 4, 3, 5).reshape(depth, 2, pairs, h_n, 2 * p_n)
    vspec = pl.BlockSpec((1, 2, 1, 1, 2 * p_n), lambda l, q: (l, 0, q, 0, 0))
    mspec = pl.BlockSpec((1, 2, 1, h_n, 2 * p_n), lambda l, q: (l, 0, q, 0, 0))
    sq_spec = pl.BlockSpec((1, 2, 2, rows, rows), lambda l, q: (l, 0, q, 0, 0))
    bp_spec = pl.BlockSpec((1, 2, 1, 2 * p_n, 2 * rows), lambda l, q: (l, 0, q, 0, 0))
    sq = jax.ShapeDtypeStruct((depth, 2, g_n, rows, rows), BF16)
    bp = jax.ShapeDtypeStruct((depth, 2, pairs, 2 * p_n, 2 * rows), BF16)
    lt = jax.ShapeDtypeStruct((depth, 2, pairs, 1, 2 * p_n), F32)
    ktoep, bpr, bpi, cpad, ltr, lti = pl.pallas_call(
        _s5_ops_kernel,
        grid=(depth, pairs),
        in_specs=[vspec] * 4 + [mspec] * 4,
        out_specs=[sq_spec, bp_spec, bp_spec, sq_spec, vspec, vspec],
        out_shape=[sq, bp, bp, sq, lt, lt],
        compiler_params=_cparams("parallel", "parallel"),
        name="s5_operators",
    )(vec(lam_re * dt), vec(lam_im * dt), vec(lam_re), vec(lam_im), bt(b_re), bt(b_im), ct(c_re), ct(c_im))
    flat = lambda v: v.reshape(depth, 2, 1, g_n * p_n)
    return ktoep, bpr, bpi, cpad, flat(ltr), flat(lti)


def _s5_in_kernel(u_ref, uc_ref, o_ref):
    j = pl.program_id(0)
    last = pl.num_programs(0) - 1

    def emit(slab_of_t):
        for t in range(S5_T):
            o_ref[:, t * S5_GROUP:(t + 1) * S5_GROUP, :] = _bf(
                slab_of_t(t).T.reshape(S5_GROUPS, S5_GROUP, S5_CB))

    @pl.when(j < last)
    def _():
        emit(lambda t: u_ref[0, :, t, :])

    @pl.when(j == last)
    def _():
        pad = jnp.zeros((S5_CB - 2 * uc_ref.shape[1], S5_WIDTH), F32)
        emit(lambda t: jnp.concatenate([uc_ref[0, :, t, :], uc_ref[1, :, t, :], pad], axis=0))


def _s5_lat_index(n_half):
    def index(j):
        jj = jnp.minimum(j, 2 * n_half - 1)
        return jj // n_half, jj % n_half
    return index


def _s5_relayout_in(p, pc):
    bsz, seq, n = p.shape
    seq_c, n_c = pc.shape[1], pc.shape[2]
    n_lat, n_ctx = seq // S5_T, seq_c // S5_T
    n_half = n_lat // S5_CB
    idx = _s5_lat_index(n_half)
    return pl.pallas_call(
        _s5_in_kernel,
        grid=(bsz * n_half + 1,),
        in_specs=[
            pl.BlockSpec((1, S5_CB, S5_T, 512), lambda j: (*idx(j), 0, COL_S5U)),
            pl.BlockSpec((bsz, n_ctx, S5_T, 512), lambda j: (0, 0, 0, COL_S5U)),
        ],
        out_specs=pl.BlockSpec((S5_GROUPS, S5_T * S5_GROUP, S5_CB), lambda j: (0, 0, j)),
        out_shape=jax.ShapeDtypeStruct((S5_GROUPS, S5_T * S5_GROUP, S5_COLS), BF16),
        compiler_params=_cparams("arbitrary"),
        name="s5_relayout_in",
    )(p.reshape(bsz, n_lat, S5_T, n), pc.reshape(bsz, n_ctx, S5_T, n_c))


def _s5_state_in_kernel(ug_ref, bre_ref, bim_ref, vre_ref, vim_ref):
    u = ug_ref[...].reshape(2 * S5_T * S5_GROUP, S5_COLS)
    for direction in range(2):
        vre_ref[direction] = _mm(bre_ref[0, direction, 0], u)
        vim_ref[direction] = _mm(bim_ref[0, direction, 0], u)


def _s5_state_in(ug, b_pair_re, b_pair_im, layer):
    pairs = S5_GROUPS // 2
    rows = S5_T * S5_GROUP
    lanes = S5_GROUPS * S5_STATE
    out = jax.ShapeDtypeStruct((2, lanes, S5_COLS), F32)
    return pl.pallas_call(
        _s5_state_in_kernel,
        grid=(pairs,),
        in_specs=[
            pl.BlockSpec((2, rows, S5_COLS), lambda q: (q, 0, 0)),
            pl.BlockSpec((1, 2, 1, 128, 2 * rows), lambda q: (layer, 0, q, 0, 0)),
            pl.BlockSpec((1, 2, 1, 128, 2 * rows), lambda q: (layer, 0, q, 0, 0)),
        ],
        out_specs=[pl.BlockSpec((2, 128, S5_COLS), lambda q: (0, q, 0))] * 2,
        out_shape=[out, out],
        compiler_params=_cparams("parallel"),
        name="s5_state_in",
    )(ug, b_pair_re, b_pair_im)


def _s5_scan_kernel(vre_ref, vim_ref, lre_ref, lim_ref, xre_ref, xim_ref, tre_scr, tim_scr, *, n_lat, n_ctx):
    lanes = lre_ref.shape[-1]
    for d in range(2):
        tre_scr[d] = vre_ref[d].T
        tim_scr[d] = vim_ref[d].T
    a_re = [lre_ref[0, d] for d in range(2)]
    a_im = [lim_ref[0, d] for d in range(2)]

    def step(d, row, xr, xi):
        sl = pl.ds(row, 1)
        xre_ref[d, sl, :] = xr
        xim_ref[d, sl, :] = xi
        return (a_re[d] * xr - a_im[d] * xi + tre_scr[d, sl, :],
                a_re[d] * xi + a_im[d] * xr + tim_scr[d, sl, :])

    def run(bases, n, carry):
        def body(i, carry):
            out = []
            for d in range(2):
                j = i if d == 0 else n - 1 - i
                for b in range(2):
                    xr, xi = carry[2 * d + b]
                    out.append(step(d, bases[b] + j, xr, xi))
            return tuple(out)
        return lax.fori_loop(0, n, body, carry)

    zero = jnp.zeros((1, lanes), F32)
    carry = run((2 * n_lat, 2 * n_lat + n_ctx), n_ctx, ((zero, zero),) * 4)
    run((0, n_lat), n_lat, carry)
    pad0 = 2 * n_lat + 2 * n_ctx
    for d in range(2):
        xre_ref[d, pad0:, :] = jnp.zeros((S5_COLS - pad0, lanes), F32)
        xim_ref[d, pad0:, :] = jnp.zeros((S5_COLS - pad0, lanes), F32)


def _s5_scan(vre, vim, lam_re, lam_im, layer, *, n_lat, n_ctx):
    lanes = vre.shape[1]
    tl = 512
    vspec = pl.BlockSpec((2, tl, S5_COLS), lambda j: (0, j, 0))
    xspec = pl.BlockSpec((2, S5_COLS, tl), lambda j: (0, 0, j))
    lspec = pl.BlockSpec((1, 2, 1, tl), lambda j: (layer, 0, 0, j))
    out = jax.ShapeDtypeStruct((2, S5_COLS, lanes), F32)
    return pl.pallas_call(
        functools.partial(_s5_scan_kernel, n_lat=n_lat, n_ctx=n_ctx),
        grid=(lanes // tl,),
        in_specs=[vspec, vspec, lspec, lspec],
        out_specs=[xspec, xspec],
        out_shape=[out, out],
        scratch_shapes=[pltpu.VMEM((2, S5_COLS, tl), F32), pltpu.VMEM((2, S5_COLS, tl), F32)],
        compiler_params=_cparams("parallel"),
        name="s5_scan",
    )(vre, vim, lam_re, lam_im)


def _s5_readout_kernel(ug_ref, kt_ref, cp_ref, xre_ref, xim_ref, y_ref):
    for e in range(2):
        u = ug_ref[e]
        acc = None
        for direction in range(2):
            xcat = _bf(jnp.concatenate([xre_ref[direction], xim_ref[direction]], axis=1))
            term = _mm(kt_ref[0, direction, e], u) + _mm(cp_ref[0, direction, e], xcat, _NT)
            acc = term if acc is None else acc + term
        y_ref[e] = acc


def _s5_readout(ug, ktoep, cpad, xre, xim, layer):
    pairs = S5_GROUPS // 2
    rows = S5_T * S5_GROUP
    return pl.pallas_call(
        _s5_readout_kernel,
        grid=(pairs,),
        in_specs=[
            pl.BlockSpec((2, rows, S5_COLS), lambda q: (q, 0, 0)),
            pl.BlockSpec((1, 2, 2, rows, rows), lambda q: (layer, 0, q, 0, 0)),
            pl.BlockSpec((1, 2, 2, rows, rows), lambda q: (layer, 0, q, 0, 0)),
            pl.BlockSpec((2, S5_COLS, 128), lambda q: (0, 0, q)),
            pl.BlockSpec((2, S5_COLS, 128), lambda q: (0, 0, q)),
        ],
        out_specs=pl.BlockSpec((2, rows, S5_COLS), lambda q: (q, 0, 0)),
        out_shape=jax.ShapeDtypeStruct((S5_GROUPS, rows, S5_COLS), F32),
        compiler_params=_cparams("parallel"),
        name="s5_readout",
    )(ug, ktoep, cpad, xre, xim)


def _s5_out_kernel(*refs, ctx_out):
    if ctx_out:
        yg_ref, u_ref, g_ref, uc_ref, gc_ref, d_ref, w_ref, b_ref, o_ref, oc_ref = refs
    else:
        yg_ref, u_ref, g_ref, d_ref, w_ref, b_ref, o_ref = refs
    d = d_ref[...]
    w = w_ref[...]
    bias = b_ref[...]

    def finish(yy, u, gate):
        yy = yy + d * u
        yg = jax.nn.gelu(yy)
        out = yg * jax.nn.sigmoid(_mm(_bf(yg), w) + bias)
        return out * (gate * jax.nn.sigmoid(gate))

    def y_of(t):
        return yg_ref[:, t * S5_GROUP:(t + 1) * S5_GROUP, :].reshape(S5_WIDTH, S5_CB).T

    def lat():
        for t in range(S5_T):
            o_ref[0, :, t, :] = finish(y_of(t), u_ref[0, :, t, :], g_ref[0, :, t, :])

    if not ctx_out:
        lat()
        return
    j = pl.program_id(0)
    last = pl.num_programs(0) - 1
    pl.when(j < last)(lat)

    @pl.when(j == last)
    def _():
        n_ctx = uc_ref.shape[1]
        for t in range(S5_T):
            y = y_of(t)
            for b in range(2):
                oc_ref[b, :, t, :] = finish(y[b * n_ctx:(b + 1) * n_ctx], uc_ref[b, :, t, :], gc_ref[b, :, t, :])


def _s5_relayout_out(yg, p, pc, d_skip, w_glu, b_glu, *, ctx_out):
    bsz, seq, n = p.shape
    seq_c, n_c = pc.shape[1], pc.shape[2]
    n_lat, n_ctx = seq // S5_T, seq_c // S5_T
    n_half = n_lat // S5_CB
    idx = _s5_lat_index(n_half)
    pv = p.reshape(bsz, n_lat, S5_T, n)
    lat_block = (1, S5_CB, S5_T, 512)
    in_specs = [
        pl.BlockSpec((S5_GROUPS, S5_T * S5_GROUP, S5_CB), lambda j: (0, 0, j)),
        pl.BlockSpec(lat_block, lambda j: (*idx(j), 0, COL_S5U)),
        pl.BlockSpec(lat_block, lambda j: (*idx(j), 0, COL_S5G)),
    ]
    args = [yg, pv, pv]
    out_specs = [pl.BlockSpec(lat_block, lambda j: (*idx(j), 0, 0))]
    out_shape = [jax.ShapeDtypeStruct((bsz, n_lat, S5_T, S5_WIDTH), F32)]
    if ctx_out:
        pcv = pc.reshape(bsz, n_ctx, S5_T, n_c)
        ctx_block = (bsz, n_ctx, S5_T, 512)
        in_specs += [
            pl.BlockSpec(ctx_block, lambda j: (0, 0, 0, COL_S5U)),
            pl.BlockSpec(ctx_block, lambda j: (0, 0, 0, COL_S5G)),
        ]
        args += [pcv, pcv]
        out_specs.append(pl.BlockSpec(ctx_block, lambda j: (0, 0, 0, 0)))
        out_shape.append(jax.ShapeDtypeStruct((bsz, n_ctx, S5_T, S5_WIDTH), F32))
    in_specs += [
        pl.BlockSpec((1, S5_WIDTH), lambda j: (0, 0)),
        pl.BlockSpec((S5_WIDTH, S5_WIDTH), lambda j: (0, 0)),
        pl.BlockSpec((1, S5_WIDTH), lambda j: (0, 0)),
    ]
    args += [d_skip, w_glu, b_glu]
    outs = pl.pallas_call(
        functools.partial(_s5_out_kernel, ctx_out=ctx_out),
        grid=(bsz * n_half + (1 if ctx_out else 0),),
        in_specs=in_specs,
        out_specs=out_specs,
        out_shape=out_shape,
        compiler_params=_cparams("arbitrary"),
        name="s5_relayout_out",
    )(*args)
    y = outs[0].reshape(bsz, seq, S5_WIDTH)
    yc = outs[1].reshape(bsz, seq_c, S5_WIDTH) if ctx_out else None
    return y, yc


def _s5(p, pc, ops, layer, d_skip, w_glu, b_glu, *, ctx_out):
    ktoep, b_pair_re, b_pair_im, cpad, lam_re, lam_im = ops
    ug = _s5_relayout_in(p, pc)
    vre, vim = _s5_state_in(ug, b_pair_re, b_pair_im, layer)
    xre, xim = _s5_scan(vre, vim, lam_re, lam_im, layer, n_lat=p.shape[1] // S5_T, n_ctx=pc.shape[1] // S5_T)
    yg = _s5_readout(ug, ktoep, cpad, xre, xim, layer)
    return _s5_relayout_out(yg, p, pc, d_skip, w_glu, b_glu, ctx_out=ctx_out)


def _dft_tables():
    n1, n2 = FFT_N1, FFT_N2
    n = n1 * n2
    a = jnp.arange(n1, dtype=jnp.int32)
    ph1 = (a[:, None] * a[None, :]) % n1
    ang1 = ph1.astype(F32) * (2.0 * math.pi / n1)
    c1, s1 = jnp.cos(ang1), -jnp.sin(ang1)
    h = n1 // 2
    m1 = jnp.concatenate([jnp.concatenate([c1[:, :h], -s1[:, :h]], 1),
                          jnp.concatenate([s1[:, :h], c1[:, :h]], 1)], 0)
    m1_real = jnp.concatenate([c1, s1], 0)
    m3 = m1.T / n
    b = jnp.arange(n2, dtype=jnp.int32)
    ang2 = ((b[:, None] * b[None, :]) % n2).astype(F32) * (2.0 * math.pi / n2)
    fr, fi = jnp.cos(ang2), -jnp.sin(ang2)
    f2 = jnp.concatenate([jnp.concatenate([fr, -fi], 1), jnp.concatenate([fi, fr], 1)], 0)
    angt = ((a[:, None] * b[None, :]) % n).astype(F32) * (2.0 * math.pi / n)
    tw = jnp.stack([jnp.cos(angt), -jnp.sin(angt)], axis=1)
    tw = jnp.broadcast_to(tw[..., None], (n1, 2, n2, HY_CT))
    return tuple(_split(t) for t in (m1, m1_real, m3, f2, f2.T)) + (tw,)


def _filter_features(length):
    t = jnp.linspace(0.0, 1.0, length, dtype=F32)[:, None]
    ang = (2.0 * math.pi / length) * jnp.arange(length, dtype=F32)[:, None]
    bands = jnp.linspace(1e-4, HY_BANDS - 1, HY_BANDS, dtype=F32)[None, :]
    feats = jnp.concatenate([t, jnp.cos(bands * ang), -jnp.sin(bands * ang)], axis=-1)
    feats = jnp.pad(feats, ((0, 0), (0, 128 - HY_EMB)))
    rev = jnp.roll(feats[::-1], 1, axis=0)
    return jnp.stack([feats, rev])


def _filter_kernel(f_ref, w1_ref, b1_ref, f1_ref, w2_ref, b2_ref, f2_ref, w3_ref, dl_ref, o_ref):
    half = pl.program_id(1)
    tile = pl.program_id(2)
    x = f_ref[0]
    x_hi, x_lo = _split(x)
    h = jnp.sin(f1_ref[0] * (_mm3(x_hi, x_lo, w1_ref[0]) + b1_ref[0]))
    h_hi, h_lo = _split(h)
    h = jnp.sin(f2_ref[0] * (_mm3(h_hi, h_lo, w2_ref[0]) + b2_ref[0]))
    y = _mm(_bf(h), _bf(w3_ref[0, 0])) * jnp.exp(-x[:, 0:1] * dl_ref[...])
    rows = lax.broadcasted_iota(jnp.int32, y.shape, 0)
    drop = jnp.logical_and(jnp.logical_and(half == 1, tile == 0), rows == 0)
    o_ref[0] = jnp.where(drop, 0.0, y)


def _filter_taps(feats, w1, b1, f1, w2, b2, f2, w3, deltas, tmf):
    depth = w1.shape[0]
    length = feats.shape[1]
    nt = length // tmf
    wide = w3.shape[-1]
    vec = pl.BlockSpec((1, 1, 128), lambda l, s, i: (l, 0, 0))
    sq = pl.BlockSpec((1, 128, 128), lambda l, s, i: (l, 0, 0))
    return pl.pallas_call(
        _filter_kernel,
        grid=(depth, 2, nt),
        in_specs=[
            pl.BlockSpec((1, tmf, 128), lambda l, s, i: (s, i, 0)),
            sq, vec, vec, sq, vec, vec,
            pl.BlockSpec((1, 1, 128, wide), lambda l, s, i: (l, s, 0, 0)),
            pl.BlockSpec((1, wide), lambda l, s, i: (0, 0)),
        ],
        out_specs=pl.BlockSpec((1, tmf, wide), lambda l, s, i: (l, s * nt + i, 0)),
        out_shape=jax.ShapeDtypeStruct((depth, 2 * length, wide), F32),
        compiler_params=_cparams("parallel", "parallel", "parallel"),
        name="hyena_filter",
    )(feats, w1, b1, f1, w2, b2, f2, w3, deltas)


def _parts(pair):
    return tuple(pair) if HY_PASSES == 3 else tuple(pair[:1])


def _mmp(m, b):
    return _mm(m[0], _bf(b)) if len(m) == 1 else _mm3(m[0], m[1], b)


def _blk_rows(n2):
    return pl.ds(pl.multiple_of(n2 * HY_PITCH, 8), 2 * FFT_N1)


def _pad_rows(n2, n):
    return pl.ds(n2, n, stride=HY_PITCH)


def _pad_copy(src, dst, n):
    for n1 in range(n):
        dst[n1 * HY_PITCH:n1 * HY_PITCH + FFT_N2, :] = src[n1 * FFT_N2:(n1 + 1) * FFT_N2, :]


def _dft_stage1(read_rows, m, a_scr):
    ct = a_scr.shape[-1]

    def body(it, carry):
        n2 = it * HY_NB
        a = _mmp(m, jnp.concatenate([read_rows(n2 + j) for j in range(HY_NB)], axis=1))
        for j in range(HY_NB):
            a_scr[_blk_rows(n2 + j), :] = a[:, j * ct:(j + 1) * ct]
        return carry

    lax.fori_loop(0, FFT_N2 // HY_NB, body, 0)


def _stage2_rows(k1):
    return (pl.ds(k1, FFT_N2, stride=HY_PITCH), pl.ds(k1 + FFT_N1, FFT_N2, stride=HY_PITCH))


def _stage2_load(a_scr, tw_ref, kb):
    rows = [_stage2_rows(kb * HY_KB + i) for i in range(HY_KB)]
    vals = []
    for i, (re, im) in enumerate(rows):
        ar, ai = a_scr[re, :], a_scr[im, :]
        tr, ti = tw_ref[kb * HY_KB + i, 0], tw_ref[kb * HY_KB + i, 1]
        vals.append(jnp.concatenate([ar * tr - ai * ti, ar * ti + ai * tr], axis=0))
    return rows, vals


def _spectrum_kernel(*refs):
    np_ = len(_parts((0, 0)))
    m_refs, f_refs, (tw_ref, t_ref, o_ref, a_scr, pad_scr) = refs[:np_], refs[np_:2 * np_], refs[2 * np_:]
    kb = pl.program_id(2)
    ct = t_ref.shape[-1]

    @pl.when(kb == 0)
    def _():
        _pad_copy(t_ref.at[0], pad_scr, FFT_N1)
        _dft_stage1(lambda n2: pad_scr[_pad_rows(n2, FFT_N1), :], [r[...] for r in m_refs], a_scr)

    _, a = _stage2_load(a_scr, tw_ref, kb)
    f2 = [r[...] for r in f_refs]
    for i in range(HY_KB):
        o_ref[0, i] = _bf(_mmp(f2, a[i])).reshape(2, FFT_N2, ct)


def _filter_spectrum(tables, taps):
    m, f2, tw = _parts(tables[1]), _parts(tables[3]), tables[5]
    depth, n, ch = taps.shape
    ct = HY_CT
    const = lambda t: pl.BlockSpec(t.shape, lambda l, c, k: (0,) * t.ndim, pipeline_mode=pl.Buffered(1))
    return pl.pallas_call(
        _spectrum_kernel,
        grid=(depth, ch // ct, FFT_N1 // HY_KB),
        in_specs=[const(t) for t in (*m, *f2, tw)] + [pl.BlockSpec((1, n, ct), lambda l, c, k: (l, 0, c))],
        out_specs=pl.BlockSpec((1, HY_KB, 2, FFT_N2, ct), lambda l, c, k: (l, k, 0, 0, c)),
        out_shape=jax.ShapeDtypeStruct((depth, FFT_N1, 2, FFT_N2, ch), BF16),
        scratch_shapes=[pltpu.VMEM((HY_PITCH * FFT_N2, ct), F32), pltpu.VMEM((HY_PITCH * FFT_N1, ct), F32)],
        compiler_params=_cparams("parallel", "parallel", "arbitrary"),
        name="hyena_spectrum",
    )(*m, *f2, tw, taps)


def _hy_pre_kernel(pv_ref, p1_ref, p2_ref, pg_ref, wv_ref, w1_ref, w2_ref, bv_ref, b1_ref, b2_ref,
                   v_ref, x1_ref, x2_ref):
    n = pv_ref.shape[1]
    rows = lax.broadcasted_iota(jnp.int32, (n, 128), 0)

    def conv(p, w_ref, b_ref):
        up = jnp.where(rows == 0, 0.0, pltpu.roll(p, 1, 0))
        dn = jnp.where(rows == n - 1, 0.0, pltpu.roll(p, n - 1, 0))
        return up * w_ref[0:1, :] + p * w_ref[1:2, :] + dn * w_ref[2:3, :] + b_ref[...]

    v_ref[0] = conv(pv_ref[0], wv_ref, bv_ref)
    x1_ref[0] = conv(p1_ref[0], w1_ref, b1_ref)
    gate = pg_ref[0]
    x2_ref[0] = conv(p2_ref[0], w2_ref, b2_ref) * (gate * jax.nn.sigmoid(gate))


def _hy_pre(p, conv_w, conv_b):
    bsz, seq, _ = p.shape
    def pspec(off):
        return pl.BlockSpec((1, seq, 128), lambda b, j: (b, 0, off + j))
    def wspec(off):
        return pl.BlockSpec((3, 128), lambda b, j: (0, off + j))
    def bspec(off):
        return pl.BlockSpec((1, 128), lambda b, j: (0, off + j))
    out = jax.ShapeDtypeStruct((bsz, seq, HY_WIDTH), F32)
    ospec = pl.BlockSpec((1, seq, 128), lambda b, j: (b, 0, j))
    return pl.pallas_call(
        _hy_pre_kernel,
        grid=(bsz, HY_WIDTH // 128),
        in_specs=[pspec(COL_HY), pspec(COL_HY + 4), pspec(COL_HY + 8), pspec(COL_HY + 12),
                  wspec(0), wspec(4), wspec(8), bspec(0), bspec(4), bspec(8)],
        out_specs=[ospec, ospec, ospec],
        out_shape=[out, out, out],
        compiler_params=_cparams("parallel", "parallel"),
        name="hyena_pre",
    )(p, p, p, p, conv_w, conv_w, conv_w, conv_b, conv_b, conv_b)


def _hy_conv_kernel(*refs):
    np_ = len(_parts((0, 0)))
    m1_refs, m3_refs, f_refs, ft_refs = (refs[i * np_:(i + 1) * np_] for i in range(4))
    tw_ref, u_ref, gate_ref, h_ref, d_ref, o_ref, a_scr, pad_scr = refs[4 * np_:]
    kb = pl.program_id(1)
    ct = u_ref.shape[-1]
    half = FFT_N1 // 2

    @pl.when(kb == 0)
    def _():
        for b in range(2):
            _pad_copy(u_ref.at[b], pad_scr.at[b], half)
        _dft_stage1(lambda n2: jnp.concatenate([pad_scr[0, _pad_rows(n2, half), :],
                                                pad_scr[1, _pad_rows(n2, half), :]], axis=0),
                    [r[...] for r in m1_refs], a_scr)

    srows, a = _stage2_load(a_scr, tw_ref, kb)
    f2, f2t = [r[...] for r in f_refs], [r[...] for r in ft_refs]
    z = [_mmp(f2, a[i]) for i in range(HY_KB)]
    z2 = []
    for i in range(HY_KB):
        zr, zi = z[i][:FFT_N2], z[i][FFT_N2:]
        hr, hi = h_ref[0, i, 0].astype(F32), h_ref[0, i, 1].astype(F32)
        z2.append(jnp.concatenate([zr * hr - zi * hi, zr * hi + zi * hr], axis=0))
    a2 = [_mmp(f2t, z2[i]) for i in range(HY_KB)]
    for i in range(HY_KB):
        cr, ci = a2[i][:FFT_N2], a2[i][FFT_N2:]
        tr, ti = tw_ref[kb * HY_KB + i, 0], tw_ref[kb * HY_KB + i, 1]
        a_scr[srows[i][0], :] = cr * tr + ci * ti
        a_scr[srows[i][1], :] = ci * tr - cr * ti

    @pl.when(kb == pl.num_programs(1) - 1)
    def _():
        m3 = [r[...] for r in m3_refs]
        d = d_ref[...]

        def body(it, carry):
            n2 = it * HY_NB
            y = _mmp(m3, jnp.concatenate([a_scr[_blk_rows(n2 + j), :] for j in range(HY_NB)], axis=1))
            for j in range(HY_NB):
                for b in range(2):
                    pad_scr[b, _pad_rows(n2 + j, half), :] = y[b * half:(b + 1) * half, j * ct:(j + 1) * ct]
            return carry

        lax.fori_loop(0, FFT_N2 // HY_NB, body, 0)
        for b in range(2):
            for n1 in range(half):
                r = slice(n1 * FFT_N2, (n1 + 1) * FFT_N2)
                conv = pad_scr[b, n1 * HY_PITCH:n1 * HY_PITCH + FFT_N2, :]
                o_ref[b, r, :] = gate_ref[b, r, :] * (conv + d * u_ref[b, r, :])


def _hy_long_conv(tables, spec, layer, order, u, gate, d):
    m1, _, m3, f2, f2t = (_parts(t) for t in tables[:5])
    bsz, seq, ch = u.shape
    ct = HY_CT
    const = lambda t: pl.BlockSpec(t.shape, lambda c, k: (0,) * t.ndim, pipeline_mode=pl.Buffered(1))
    uspec = pl.BlockSpec((bsz, seq, ct), lambda c, k: (0, 0, c))
    n_ct = ch // ct
    mats = (*m1, *m3, *f2, *f2t, tables[5])
    return pl.pallas_call(
        _hy_conv_kernel,
        grid=(n_ct, FFT_N1 // HY_KB),
        in_specs=[const(t) for t in mats] + [
            uspec, uspec,
            pl.BlockSpec((1, HY_KB, 2, FFT_N2, ct), lambda c, k: (layer, k, 0, 0, order * n_ct + c)),
            pl.BlockSpec((1, ct), lambda c, k: (0, c))],
        out_specs=uspec,
        out_shape=jax.ShapeDtypeStruct(u.shape, F32),
        scratch_shapes=[pltpu.VMEM((HY_PITCH * FFT_N2, ct), F32),
                        pltpu.VMEM((bsz, HY_PITCH * FFT_N1 // 2, ct), F32)],
        compiler_params=_cparams("parallel", "arbitrary"),
        name="hyena_conv",
    )(*mats, u, gate, spec, d.reshape(1, ch))


def _ctx_dft_tables(length):
    n = 2 * length
    k = jnp.arange(n, dtype=jnp.int32)
    ang = ((k[:, None] * k[None, :]) % n).astype(F32) * (2.0 * math.pi / n)
    fr, fi = jnp.cos(ang), -jnp.sin(ang)
    fwd = jnp.concatenate([jnp.concatenate([fr[:, :length], -fi[:, :length]], 1),
                           jnp.concatenate([fi[:, :length], fr[:, :length]], 1)], 0)
    real = jnp.concatenate([fr, fi], 0)
    inv = fwd.T / n
    return tuple(_split(t) for t in (fwd, real, inv))


def _hyc_kernel(fh_ref, fl_ref, rh_ref, rl_ref, ih_ref, il_ref, taps_ref, u_ref, g_ref, d_ref, o_ref):
    n = u_ref.shape[1]
    x = jnp.concatenate([u_ref[0], u_ref[1]], axis=0)
    z = _mm3(fh_ref[...], fl_ref[...], x)
    h = _mm3(rh_ref[...], rl_ref[...], taps_ref[0])
    m = 2 * n
    zr, zi, hr, hi = z[:m], z[m:], h[:m], h[m:]
    z2 = jnp.concatenate([zr * hr - zi * hi, zr * hi + zi * hr], axis=0)
    y = _mm3(ih_ref[...], il_ref[...], z2)
    for b in range(2):
        o_ref[b] = g_ref[b] * (y[b * n:(b + 1) * n] + d_ref[...] * u_ref[b])


def _hyc_long_conv(ctabs, taps, layer, order, u, gate, d):
    (f_hi, f_lo), (r_hi, r_lo), (i_hi, i_lo) = ctabs
    bsz, n, ch = u.shape
    full = lambda t: pl.BlockSpec(t.shape, lambda i: (0,) * t.ndim)
    uspec = pl.BlockSpec((bsz, n, ch), lambda i: (0, 0, 0))
    return pl.pallas_call(
        _hyc_kernel,
        grid=(1,),
        in_specs=[full(f_hi), full(f_lo), full(r_hi), full(r_lo), full(i_hi), full(i_lo),
                  pl.BlockSpec((1, 2 * n, ch), lambda i: (layer, 0, order)),
                  uspec, uspec, pl.BlockSpec((1, ch), lambda i: (0, 0))],
        out_specs=uspec,
        out_shape=jax.ShapeDtypeStruct((bsz, n, ch), F32),
        compiler_params=_cparams("arbitrary"),
        name="hyena_ctx",
    )(f_hi, f_lo, r_hi, r_lo, i_hi, i_lo, taps, u, gate, d.reshape(1, ch))


def _outproj_kernel(*refs, colmajor, tm):
    if colmajor:
        x_ref, ya_ref, yb_ref, yc_ref, w_ref, gp_ref, gate_ref, perm_ref, o_ref = refs
        ld = lambda r: _bf(_mm(perm_ref[...], _bf(r[0])))
    else:
        x_ref, ya_ref, yb_ref, yc_ref, w_ref, gp_ref, gate_ref, o_ref = refs
        ld = lambda r: _bf(r[0])
    na, nb = ya_ref.shape[-1], yb_ref.shape[-1]
    acc = (_mm(ld(ya_ref), w_ref[0, 0:na, :]) + _mm(ld(yb_ref), w_ref[0, na:na + nb, :])
           + _mm(ld(yc_ref), w_ref[0, na + nb:, :]))
    ms = jnp.mean(acc * acc, axis=-1, keepdims=True)
    upd = gate_ref[0] * (acc * lax.rsqrt(ms + EPS) * gp_ref[...])
    o_ref[0] = x_ref[0] + upd.reshape(x_ref.shape[1:])


def _outproj(x, ya, yb, yc, w_out, layer, g_post, gate, *, colmajor, tm):
    bsz, seq, d = x.shape
    extra_specs, extra_args = [], []
    if colmajor:
        cols = tm // GRID_W
        x_in = x.reshape(bsz, GRID_W, seq // GRID_W, d)
        x_spec = pl.BlockSpec((1, GRID_W, cols, d), lambda b, m: (b, 0, m, 0))
        r_old = jnp.arange(tm, dtype=jnp.int32)[:, None]
        r_new = jnp.arange(tm, dtype=jnp.int32)[None, :]
        perm = _bf((r_new == (r_old % cols) * GRID_W + r_old // cols).astype(F32))
        extra_specs, extra_args = [pl.BlockSpec((tm, tm), lambda b, m: (0, 0))], [perm]
    else:
        x_in = x
        x_spec = pl.BlockSpec((1, tm, d), lambda b, m: (b, m, 0))
    def yspec(t):
        return pl.BlockSpec((1, tm, t.shape[-1]), lambda b, m: (b, m, 0))
    out = pl.pallas_call(
        functools.partial(_outproj_kernel, colmajor=colmajor, tm=tm),
        grid=(bsz, seq // tm),
        in_specs=[x_spec, yspec(ya), yspec(yb), yspec(yc),
                  pl.BlockSpec((1,) + w_out.shape[1:], lambda b, m: (layer, 0, 0)),
                  pl.BlockSpec((1, d), lambda b, m: (0, 0)),
                  pl.BlockSpec((1, 1, d), lambda b, m: (b, 0, 0))] + extra_specs,
        out_specs=x_spec,
        out_shape=jax.ShapeDtypeStruct(x_in.shape, F32),
        compiler_params=_cparams("parallel", "parallel"),
        name="outproj",
    )(x_in, ya, yb, yc, w_out, g_post, gate, *extra_args)
    return out.reshape(bsz, seq, d)


def kernel(x, c, ctx, c_ctx, w_mod, b_mod, g_pre, g_post, w_in, w_out, gla_w_gate, gla_b_gate, gla_norm, s5_lam_re, s5_lam_im, s5_log_step, s5_b_re, s5_b_im, s5_c_re, s5_c_im, s5_d, s5_w_glu, s5_b_glu, hy_conv_w, hy_conv_b, hy_w1, hy_b1, hy_f1, hy_w2, hy_b2, hy_f2, hy_w3, hy_d):
    bsz, seq, d = x.shape
    seq_c = ctx.shape[1]
    depth = w_in.shape[0]
    assert (bsz, d, seq // GRID_W) == (2, D_MODEL, GRID_W) and seq == FFT_N1 * FFT_N2 // 2

    cvec = jnp.concatenate([c, c_ctx[None], jnp.zeros((8 - bsz - 1, d), F32)], axis=0)
    mod = _modulation(cvec, w_mod, b_mod)
    shift, scale, gate = mod[..., :d], mod[..., d:2 * d], mod[..., 2 * d:]

    w_t = _bf(jnp.swapaxes(w_in, 1, 2))
    w_lr = w_t[:, N_QKV:N_QKV + N_LR]
    w_lr3 = jnp.concatenate([w_lr, w_lr, w_lr, jnp.zeros_like(w_lr)], axis=1)
    w_out_b = _bf(w_out)
    w_glu_b = _bf(s5_w_glu)

    wg = gla_w_gate.reshape(depth, 2, 16, GLA_HEADS, GLA_DK).transpose(0, 1, 3, 2, 4)
    wg_pad = jnp.stack([jnp.pad(wg[:, 0], ((0, 0), (0, 0), (0, 112), (0, 0))),
                        jnp.pad(wg[:, 1], ((0, 0), (0, 0), (16, 96), (0, 0)))], axis=1)
    wg_hi, wg_lo = _split(wg_pad[..., :N_LR, :])
    wg3 = jnp.concatenate([wg_hi, wg_hi, wg_lo, jnp.zeros_like(wg_hi)], axis=-2)
    bg = gla_b_gate.reshape(depth, 2, GLA_HEADS, 1, GLA_DK)

    tables = _dft_tables()
    deltas = jnp.abs(jnp.linspace(HY_MIN_DECAY, HY_MAX_DECAY, HY_WIDTH, dtype=F32))
    deltas2 = jnp.tile(deltas, 2).reshape(1, 2 * HY_WIDTH)
    pad_k = lambda w: jnp.pad(w, ((0, 0), (0, 128 - w.shape[1]), (0, 128 - w.shape[2])))
    pad_v = lambda v: jnp.pad(v, ((0, 0), (0, 128 - v.shape[1]))).reshape(depth, 1, 128)
    w3 = hy_w3.reshape(depth, HY_FFN, 2, 2, HY_WIDTH).transpose(0, 3, 1, 2, 4)
    w3 = jnp.pad(w3.reshape(depth, 2, HY_FFN, 2 * HY_WIDTH), ((0, 0), (0, 0), (0, 128 - HY_FFN), (0, 0)))
    filt_w = (pad_k(hy_w1), pad_v(hy_b1), pad_v(hy_f1), pad_k(hy_w2), pad_v(hy_b2), pad_v(hy_f2), w3, deltas2)
    taps = _filter_taps(_filter_features(seq), *filt_w, tmf=512)
    spec = _filter_spectrum(tables, taps)
    taps_c = _filter_taps(_filter_features(seq_c), *filt_w, tmf=seq_c)
    ctabs = _ctx_dft_tables(seq_c)

    s5_ops = _s5_operators(s5_lam_re, s5_lam_im, s5_log_step, s5_b_re, s5_b_im, s5_c_re, s5_c_im)

    xc = ctx
    for l in range(depth):
        last = l == depth - 1
        ctx_out = not last
        colmajor = l % 2 == 1
        pre = g_pre[l][None, :] * (1.0 + scale[l])
        sc_l, sh_l = pre[:bsz, None, :], shift[l, :bsz, None, :]
        sc_c = jnp.broadcast_to(pre[bsz][None, None, :], (bsz, 1, d))
        sh_c = jnp.broadcast_to(shift[l, bsz][None, None, :], (bsz, 1, d))
        qkv, p, lr = _inproj(x, sc_l, sh_l, w_t, w_lr3, l, n=N_MAIN, colmajor=colmajor, tm=1024, tn=512)
        qkvc, pc, lrc = _inproj(xc, sc_c, sh_c, w_t, w_lr3, l, n=N_MAIN if ctx_out else N_STATE,
                                colmajor=False, tm=seq_c, tn=512)

        y_gla, yc_gla = _gla(qkv, p, lr, qkvc, pc, lrc, wg3[l], bg[l], gla_norm[l][None, :],
                             ctx_out=ctx_out)

        y_s5, yc_s5 = _s5(p, pc, s5_ops, l, s5_d[l][None, :], w_glu_b[l], s5_b_glu[l][None, :], ctx_out=ctx_out)

        v, x1, x2 = _hy_pre(p, hy_conv_w[l], hy_conv_b[l][None, :])
        z = _hy_long_conv(tables, spec, l, 0, v, x1, hy_d[l, 0])
        y_hy = _hy_long_conv(tables, spec, l, 1, z, x2, hy_d[l, 1])

        x_new = _outproj(x, y_gla, y_s5, y_hy, w_out_b, l, g_post[l][None, :], gate[l, :bsz, None, :],
                         colmajor=colmajor, tm=512)
        if ctx_out:
            vc, x1c, x2c = _hy_pre(pc, hy_conv_w[l], hy_conv_b[l][None, :])
            zc = _hyc_long_conv(ctabs, taps_c, l, 0, vc, x1c, hy_d[l, 0])
            yc_hy = _hyc_long_conv(ctabs, taps_c, l, 1, zc, x2c, hy_d[l, 1])
            gate_c = jnp.broadcast_to(gate[l, bsz][None, None, :], (bsz, 1, d))
            xc = _outproj(xc, yc_gla, yc_s5, yc_hy, w_out_b, l, g_post[l][None, :], gate_c,
                          colmajor=False, tm=seq_c)
        x = x_new
    return x
```

```python
import functools
import math

import numpy as np
import jax
import jax.numpy as jnp
from jax import lax
from jax.experimental import pallas as pl
from jax.experimental.pallas import tpu as pltpu

F32 = jnp.float32
BF16 = jnp.bfloat16

D_MODEL = 2048
GRID_W = 64
EPS = 1e-6

GLA_HEADS = 4
GLA_DK = 128
GLA_DV = 256
GLA_TAU = 16.0
GLA_CHUNK = 64
GLA_BLOCK = 16

S5_WIDTH = 512
S5_GROUP = 16
S5_GROUPS = 32
S5_STATE = 64
S5_T = 16
S5_COLS = 640
S5_CB = 128

HY_WIDTH = 512
HY_EMB = 33
HY_BANDS = 16
HY_FFN = 64
HY_MIN_DECAY = math.log(1e-2) / 0.3
HY_MAX_DECAY = math.log(1e-2) / 1.5
FFT_N1 = 64
FFT_N2 = 128
HY_CT = 128
HY_KB = 8
HY_NB = 32
HY_PITCH = 136
HY_PASSES = 1

N_MAIN = 6144
N_STATE = 2560
N_QKV = 2048
N_LR = 32
COL_S5U, COL_S5G = 0, 3
COL_GLA_GATE = 2
COL_HY = 16

VMEM_LIMIT_BYTES = 56 * 1024 * 1024


def _cparams(*sem):
    return pltpu.CompilerParams(dimension_semantics=sem, vmem_limit_bytes=VMEM_LIMIT_BYTES)


def _bf(x):
    return x.astype(BF16)


def _split(x):
    hi = _bf(x)
    return hi, _bf(x - hi.astype(F32))


def _split3(x):
    a = _bf(x)
    r = x - a.astype(F32)
    b = _bf(r)
    return a, b, _bf(r - b.astype(F32))


_NN = (((1,), (0,)), ((), ()))
_NT = (((1,), (1,)), ((), ()))
_TN = (((0,), (0,)), ((), ()))


def _mm(a, b, dims=_NN):
    return lax.dot_general(a, b, dims, preferred_element_type=F32)


def _mm3(a_hi, a_lo, b, dims=_NN):
    b_hi, b_lo = _split(b)
    return _mm(a_hi, b_hi, dims) + _mm(a_hi, b_lo, dims) + _mm(a_lo, b_hi, dims)


def _mod_kernel(s_ref, w_ref, b_ref, o_ref):
    s = s_ref[...]
    s = s * jax.nn.sigmoid(s)
    s_hi, s_lo = _split(s)
    o_ref[0] = _mm3(s_hi, s_lo, w_ref[0]) + b_ref[0]


def _modulation(cvec, w_mod, b_mod):
    depth, d, n = w_mod.shape
    tn = 512
    return pl.pallas_call(
        _mod_kernel,
        grid=(depth, n // tn),
        in_specs=[
            pl.BlockSpec((8, d), lambda l, j: (0, 0)),
            pl.BlockSpec((1, d, tn), lambda l, j: (l, 0, j)),
            pl.BlockSpec((1, 1, tn), lambda l, j: (l, 0, j)),
        ],
        out_specs=pl.BlockSpec((1, 8, tn), lambda l, j: (l, 0, j)),
        out_shape=jax.ShapeDtypeStruct((depth, 8, n), F32),
        compiler_params=_cparams("parallel", "parallel"),
        name="modulation",
    )(cvec, w_mod, b_mod.reshape(depth, 1, n))


def _inproj_kernel(*refs, colmajor, tm, tn):
    if colmajor:
        x_ref, sc_ref, sh_ref, w_ref, wlr_ref, perm_ref, qkv_ref, p_ref, lr_ref, h_scr = refs
    else:
        x_ref, sc_ref, sh_ref, w_ref, wlr_ref, qkv_ref, p_ref, lr_ref, h_scr = refs
    j = pl.program_id(2)

    @pl.when(j == 0)
    def _():
        def norm(xb):
            ms = jnp.mean(xb * xb, axis=-1, keepdims=True)
            return _bf(xb * lax.rsqrt(ms + EPS) * sc_ref[0] + sh_ref[0])

        if colmajor:
            h = norm(x_ref[0].reshape(tm, D_MODEL))
            for c in range(0, D_MODEL, 512):
                h_scr[:, c:c + 512] = _bf(_mm(perm_ref[...], h[:, c:c + 512]))
        else:
            h_scr[...] = norm(x_ref[0])
        lr_ref[0] = _mm(h_scr[...], wlr_ref[0], _NT)

    res = _mm(h_scr[...], w_ref[...], _NT)
    n_qkv = N_QKV // tn

    @pl.when(j < n_qkv)
    def _():
        qkv_ref[0] = _bf(res)

    @pl.when(j >= n_qkv)
    def _():
        p_ref[0] = res


def _inproj(x, scale, shift, w_t, w_lr3, layer, *, n, colmajor, tm, tn):
    bsz, seq, d = x.shape
    n_qkv = N_QKV // tn
    extra_specs, extra_args = [], []
    if colmajor:
        cols = tm // GRID_W
        x_in = x.reshape(bsz, GRID_W, seq // GRID_W, d)
        x_spec = pl.BlockSpec((1, GRID_W, cols, d), lambda b, m, j: (b, 0, m, 0))
        r_new = jnp.arange(tm, dtype=jnp.int32)[:, None]
        r_old = jnp.arange(tm, dtype=jnp.int32)[None, :]
        perm = _bf((r_old == (r_new % GRID_W) * cols + r_new // GRID_W).astype(F32))
        extra_specs, extra_args = [pl.BlockSpec((tm, tm), lambda b, m, j: (0, 0))], [perm]
    else:
        x_in = x
        x_spec = pl.BlockSpec((1, tm, d), lambda b, m, j: (b, m, 0))
    return pl.pallas_call(
        functools.partial(_inproj_kernel, colmajor=colmajor, tm=tm, tn=tn),
        grid=(bsz, seq // tm, n // tn),
        in_specs=[
            x_spec,
            pl.BlockSpec((1, 1, d), lambda b, m, j: (b, 0, 0)),
            pl.BlockSpec((1, 1, d), lambda b, m, j: (b, 0, 0)),
            pl.BlockSpec((pl.Squeezed(), pl.Element(tn), pl.Element(d)),
                         lambda b, m, j: (layer, pl.multiple_of(j * tn + jnp.where(j >= n_qkv, N_LR, 0), N_LR), 0)),
            pl.BlockSpec((1, 128, d), lambda b, m, j: (layer, 0, 0)),
        ] + extra_specs,
        out_specs=[
            pl.BlockSpec((1, tm, tn), lambda b, m, j: (b, m, jnp.minimum(j, n_qkv - 1))),
            pl.BlockSpec((1, tm, tn), lambda b, m, j: (b, m, jnp.maximum(j - n_qkv, 0))),
            pl.BlockSpec((1, tm, 128), lambda b, m, j: (b, m, 0)),
        ],
        out_shape=[jax.ShapeDtypeStruct((bsz, seq, N_QKV), BF16),
                   jax.ShapeDtypeStruct((bsz, seq, n - N_QKV), F32),
                   jax.ShapeDtypeStruct((bsz, seq, 128), F32)],
        scratch_shapes=[pltpu.VMEM((tm, d), BF16)],
        compiler_params=_cparams("parallel", "parallel", "arbitrary"),
        name="inproj",
    )(x_in, scale, shift, w_t, w_lr3, *extra_args)


def _gla_block(q, k, v, lr, st, wg, bg, direction, mask, rowc, hi_lanes, nb):
    c = GLA_CHUNK
    n = c * nb
    lr_hi, lr_lo = _split(lr)
    z = _mm(jnp.where(hi_lanes, lr_hi, lr_lo), wg) + bg
    g = (jnp.minimum(z, 0.0) - jnp.log(1.0 + jnp.exp(-jnp.abs(z)))) * (1.0 / GLA_TAU)
    cum = g
    s = 1
    while s < c:
        if direction == 0:
            cum = cum + jnp.where(rowc >= s, pltpu.roll(cum, s, 0), 0.0)
        else:
            cum = cum + jnp.where(rowc < c - s, pltpu.roll(cum, n - s, 0), 0.0)
        s *= 2
    cum3 = cum.reshape(nb, c, GLA_DK)
    tot3 = cum3[:, c - 1:c, :] if direction == 0 else cum3[:, 0:1, :]
    q3 = q.reshape(nb, c, GLA_DK)
    k3 = k.reshape(nb, c, GLA_DK)
    qg3 = _bf(q3 * jnp.exp(cum3))
    kg3 = _bf(k3 * jnp.exp(-cum3))
    kd3 = _bf(k3 * jnp.exp(tot3 - cum3))
    a3 = jnp.exp(tot3)
    vb3 = _bf(v).reshape(nb, c, GLA_DV)
    att = [jnp.where(mask, _mm(qg3[i], kg3[i], _NT), 0.0) for i in range(nb)]
    o_in = [_mm(_bf(att[i]), vb3[i]) for i in range(nb)]
    d_st = [_mm(vb3[i], kd3[i], _TN) for i in range(nb)]
    outs = [None] * nb
    for i in (range(nb) if direction == 0 else range(nb - 1, -1, -1)):
        outs[i] = o_in[i] + _mm(qg3[i], _bf(st), _NT)
        st = st * a3[i] + d_st[i]
    return jnp.concatenate(outs, axis=0), st


def _gla_kernel(*refs, ctx_out, seq, seq_c):
    if ctx_out:
        (q_ref, k_ref, v_ref, gt_ref, lr_ref, qc_ref, kc_ref, vc_ref, gtc_ref, lrc_ref,
         wg_ref, bg_ref, gn_ref, y_ref, yc_ref) = refs
    else:
        (q_ref, k_ref, v_ref, gt_ref, lr_ref, qc_ref, kc_ref, vc_ref, lrc_ref,
         wg_ref, bg_ref, gn_ref, y_ref) = refs
        gtc_ref = yc_ref = None
    c = GLA_CHUNK
    scale = GLA_DK ** -0.5
    row = lax.broadcasted_iota(jnp.int32, (c, c), 0)
    col = lax.broadcasted_iota(jnp.int32, (c, c), 1)
    masks = (col <= row, col >= row)
    gn = gn_ref[...]

    def run(refs4, n_rows, st, direction, store):
        qr, kr, vr, lrr = refs4
        nb = min(GLA_BLOCK, n_rows // c)
        n = c * nb
        rowc = lax.broadcasted_iota(jnp.int32, (n, GLA_DK), 0) % c
        hi_lanes = (lax.broadcasted_iota(jnp.int32, (n, 128), 1) // N_LR) % 2 == 0
        wg = wg_ref[direction, 0]
        bg = bg_ref[direction, 0]
        n_blocks = n_rows // n

        def body(i, st):
            bi = i if direction == 0 else n_blocks - 1 - i
            rows = pl.ds(pl.multiple_of(bi * n, n), n)
            q = qr[0, rows, :].astype(F32) * scale
            k = kr[0, rows, :].astype(F32)
            v = vr[0, rows, :].astype(F32)
            o, st = _gla_block(q, k, v, lrr[0, rows, :], st, wg, bg, direction,
                               masks[direction], rowc, hi_lanes, nb)
            store(rows, q, k, v, o)
            return st

        return lax.fori_loop(0, n_blocks, body, st)

    def store_fwd(out_ref):
        def f(rows, q, k, v, o):
            if out_ref is not None:
                out_ref[0, rows, :] = o
        return f

    def store_bwd(out_ref, gate_ref):
        def f(rows, q, k, v, o):
            if out_ref is None:
                return
            y = out_ref[0, rows, :] + o - jnp.sum(q * k, axis=-1, keepdims=True) * v
            ms = jnp.mean(y * y, axis=-1, keepdims=True)
            y = y * lax.rsqrt(ms + EPS) * gn
            gt = gate_ref[0, rows, :]
            out_ref[0, rows, :] = y * (gt * jax.nn.sigmoid(gt))
        return f

    lat = (q_ref, k_ref, v_ref, lr_ref)
    ctx = (qc_ref, kc_ref, vc_ref, lrc_ref)
    zero = jnp.zeros((GLA_DV, GLA_DK), F32)
    st = run(ctx, seq_c, zero, 0, store_fwd(yc_ref))
    run(lat, seq, st, 0, store_fwd(y_ref))
    st = run(ctx, seq_c, zero, 1, store_bwd(yc_ref, gtc_ref))
    run(lat, seq, st, 1, store_bwd(y_ref, gt_ref))


def _gla(qkv, p, lr, qkvc, pc, lrc, wg, bg, gnorm, *, ctx_out):
    bsz, seq, _ = qkv.shape
    seq_c = qkvc.shape[1]
    h = GLA_HEADS

    def specs(n):
        return [
            pl.BlockSpec((1, n, 128), lambda b, i: (b, 0, i)),
            pl.BlockSpec((1, n, 128), lambda b, i: (b, 0, 4 + i)),
            pl.BlockSpec((1, n, 256), lambda b, i: (b, 0, 4 + i)),
            pl.BlockSpec((1, n, 256), lambda b, i: (b, 0, COL_GLA_GATE + i)),
            pl.BlockSpec((1, n, 128), lambda b, i: (b, 0, 0)),
        ]

    lat_specs = specs(seq)
    ctx_specs = specs(seq_c)
    lat_args = [qkv, qkv, qkv, p, lr]
    ctx_args = [qkvc, qkvc, qkvc, pc, lrc]
    if not ctx_out:
        del ctx_specs[3], ctx_args[3]
    w_specs = [
        pl.BlockSpec((2, 1, 128, 128), lambda b, i: (0, i, 0, 0)),
        pl.BlockSpec((2, 1, 1, 128), lambda b, i: (0, i, 0, 0)),
        pl.BlockSpec((1, GLA_DV), lambda b, i: (0, 0)),
    ]
    out_specs = [pl.BlockSpec((1, seq, 256), lambda b, i: (b, 0, i))]
    out_shape = [jax.ShapeDtypeStruct((bsz, seq, h * GLA_DV), F32)]
    if ctx_out:
        out_specs.append(pl.BlockSpec((1, seq_c, 256), lambda b, i: (b, 0, i)))
        out_shape.append(jax.ShapeDtypeStruct((bsz, seq_c, h * GLA_DV), F32))
    outs = pl.pallas_call(
        functools.partial(_gla_kernel, ctx_out=ctx_out, seq=seq, seq_c=seq_c),
        grid=(bsz, h),
        in_specs=lat_specs + ctx_specs + w_specs,
        out_specs=out_specs,
        out_shape=out_shape,
        compiler_params=_cparams("parallel", "parallel"),
        name="gla",
    )(*lat_args, *ctx_args, wg, bg, gnorm)
    return (outs[0], outs[1]) if ctx_out else (outs[0], None)


def _mm3f(a, b, dims=_NN):
    a_hi, a_lo = _split(a)
    return _mm3(a_hi, a_lo, b, dims)


def _s5_ops_kernel(a_ref, b_ref, lr_ref, li_ref, btr_ref, bti_ref, cr_ref, ci_ref,
                   kt_ref, bpr_ref, bpi_ref, cp_ref, ltr_ref, lti_ref):
    t_len, h_n, p_n = S5_T, S5_GROUP, S5_STATE
    rows = t_len * h_n
    lane = lax.broadcasted_iota(jnp.int32, (rows, 2 * p_n), 1)
    own = (lane < p_n, lane >= p_n)
    lane_blk = lax.broadcasted_iota(jnp.int32, (rows, rows), 1) // h_n
    kk = lax.broadcasted_iota(jnp.int32, (24, 2 * p_n), 0).astype(F32)
    for d in range(2):
        a, b = a_ref[0, d, 0], b_ref[0, d, 0]
        lr, li = lr_ref[0, d, 0], li_ref[0, d, 0]
        mag = jnp.exp(kk * a)
        pwr, pwi = mag * jnp.cos(kk * b), mag * jnp.sin(kk * b)

        def times_pw(k, mr, mi):
            pr, pi = pwr[k:k + 1], pwi[k:k + 1]
            return pr * mr - pi * mi, pr * mi + pi * mr

        x, y = pwr[1:2] - 1.0, pwi[1:2]
        den = lr * lr + li * li
        cfr, cfi = (x * lr + y * li) / den, (y * lr - x * li) / den
        btr, bti = btr_ref[0, d, 0], bti_ref[0, d, 0]
        bbr, bbi = cfr * btr - cfi * bti, cfr * bti + cfi * btr
        cr, ci = cr_ref[0, d, 0], ci_ref[0, d, 0]

        def stack(power_of_t, mr, mi):
            parts = [times_pw(power_of_t(t), mr, mi) for t in range(t_len)]
            return (jnp.concatenate([p[0] for p in parts], axis=0),
                    jnp.concatenate([p[1] for p in parts], axis=0))

        bpr, bpi = stack((lambda t: t_len - 1 - t) if d == 0 else (lambda t: t), bbr, bbi)
        for src, dst in ((bpr, bpr_ref), (bpi, bpi_ref)):
            z = jnp.concatenate([jnp.where(own[0], src, 0.0), jnp.where(own[1], src, 0.0)], axis=0)
            dst[0, d, 0] = _bf(z.T)
        cpr, cpi = stack((lambda t: t + 1) if d == 0 else (lambda t: t_len - t), cr, ci)
        clr, cli = stack((lambda j: j) if d == 0 else (lambda j: t_len - 1 - j), cr, ci)
        bbr_t = jnp.concatenate([bbr] * t_len, axis=0)
        bbi_t = jnp.concatenate([bbi] * t_len, axis=0)
        for e in range(2):
            cp_ref[0, d, e] = _bf(jnp.concatenate(
                [jnp.where(own[e], cpr, 0.0), jnp.where(own[e], -cpi, 0.0)], axis=1))
            kc = (_mm3f(jnp.where(own[e], clr, 0.0), bbr_t, _NT)
                  - _mm3f(jnp.where(own[e], cli, 0.0), bbi_t, _NT))
            acc = jnp.zeros((rows, rows), F32)
            for t in range(t_len):
                if d == 0:
                    pieces = [jnp.zeros((h_n * t, rows), F32), kc[:rows - h_n * t]]
                else:
                    s = h_n * (t_len - 1 - t)
                    pieces = [kc[s:], jnp.zeros((s, rows), F32)]
                shifted = jnp.concatenate([p for p in pieces if p.shape[0]], axis=0)
                acc = jnp.where(lane_blk == t, shifted, acc)
            kt_ref[0, d, e] = _bf(acc)
        ltr_ref[0, d, 0] = pwr[t_len:t_len + 1]
        lti_ref[0, d, 0] = pwi[t_len:t_len + 1]


def _s5_operators(lam_re, lam_im, log_step, b_re, b_im, c_re, c_im):
    depth = lam_re.shape[0]
    g_n, p_n, h_n = S5_GROUPS, S5_STATE, S5_GROUP
    pairs, rows = g_n // 2, S5_T * h_n
    dt = jnp.exp(log_step)[..., None]
    vec = lambda v: v.reshape(depth, 2, pairs, 1, 2 * p_n)
    bt = lambda m: m.reshape(depth, 2, pairs, 2, p_n, h_n).transpose(0, 1, 2, 5, 3, 4).reshape(depth, 2, pairs, h_n, 2 * p_n)
    ct = lambda m: m.reshape(depth, 2, pairs, 2, h_n, p_n).transpose(0, 1, 2, 4, 3, 5).reshape(depth, 2, pairs, h_n, 2 * p_n)
    vspec = pl.BlockSpec((1, 2, 1, 1, 2 * p_n), lambda l, q: (l, 0, q, 0, 0))
    mspec = pl.BlockSpec((1, 2, 1, h_n, 2 * p_n), lambda l, q: (l, 0, q, 0, 0))
    sq_spec = pl.BlockSpec((1, 2, 2, rows, rows), lambda l, q: (l, 0, q, 0, 0))
    bp_spec = pl.BlockSpec((1, 2, 1, 2 * p_n, 2 * rows), lambda l, q: (l, 0, q, 0, 0))
    sq = jax.ShapeDtypeStruct((depth, 2, g_n, rows, rows), BF16)
    bp = jax.ShapeDtypeStruct((depth, 2, pairs, 2 * p_n, 2 * rows), BF16)
    lt = jax.ShapeDtypeStruct((depth, 2, pairs, 1, 2 * p_n), F32)
    ktoep, bpr, bpi, cpad, ltr, lti = pl.pallas_call(
        _s5_ops_kernel,
        grid=(depth, pairs),
        in_specs=[vspec] * 4 + [mspec] * 4,
        out_specs=[sq_spec, bp_spec, bp_spec, sq_spec, vspec, vspec],
        out_shape=[sq, bp, bp, sq, lt, lt],
        compiler_params=_cparams("parallel", "parallel"),
        name="s5_operators",
    )(vec(lam_re * dt), vec(lam_im * dt), vec(lam_re), vec(lam_im), bt(b_re), bt(b_im), ct(c_re), ct(c_im))
    flat = lambda v: v.reshape(depth, 2, 1, g_n * p_n)
    return ktoep, bpr, bpi, cpad, flat(ltr), flat(lti)


def _s5_in_kernel(u_ref, uc_ref, o_ref):
    j = pl.program_id(0)
    last = pl.num_programs(0) - 1

    def emit(slab_of_t):
        for t in range(S5_T):
            o_ref[:, t * S5_GROUP:(t + 1) * S5_GROUP, :] = _bf(
                slab_of_t(t).T.reshape(S5_GROUPS, S5_GROUP, S5_CB))

    @pl.when(j < last)
    def _():
        emit(lambda t: u_ref[0, :, t, :])

    @pl.when(j == last)
    def _():
        pad = jnp.zeros((S5_CB - 2 * uc_ref.shape[1], S5_WIDTH), F32)
        emit(lambda t: jnp.concatenate([uc_ref[0, :, t, :], uc_ref[1, :, t, :], pad], axis=0))


def _s5_lat_index(n_half):
    def index(j):
        jj = jnp.minimum(j, 2 * n_half - 1)
        return jj // n_half, jj % n_half
    return index


def _s5_relayout_in(p, pc):
    bsz, seq, n = p.shape
    seq_c, n_c = pc.shape[1], pc.shape[2]
    n_lat, n_ctx = seq // S5_T, seq_c // S5_T
    n_half = n_lat // S5_CB
    idx = _s5_lat_index(n_half)
    return pl.pallas_call(
        _s5_in_kernel,
        grid=(bsz * n_half + 1,),
        in_specs=[
            pl.BlockSpec((1, S5_CB, S5_T, 512), lambda j: (*idx(j), 0, COL_S5U)),
            pl.BlockSpec((bsz, n_ctx, S5_T, 512), lambda j: (0, 0, 0, COL_S5U)),
        ],
        out_specs=pl.BlockSpec((S5_GROUPS, S5_T * S5_GROUP, S5_CB), lambda j: (0, 0, j)),
        out_shape=jax.ShapeDtypeStruct((S5_GROUPS, S5_T * S5_GROUP, S5_COLS), BF16),
        compiler_params=_cparams("arbitrary"),
        name="s5_relayout_in",
    )(p.reshape(bsz, n_lat, S5_T, n), pc.reshape(bsz, n_ctx, S5_T, n_c))


def _s5_state_in_kernel(ug_ref, bre_ref, bim_ref, vre_ref, vim_ref):
    u = ug_ref[...].reshape(2 * S5_T * S5_GROUP, S5_COLS)
    for direction in range(2):
        vre_ref[direction] = _mm(bre_ref[0, direction, 0], u)
        vim_ref[direction] = _mm(bim_ref[0, direction, 0], u)


def _s5_state_in(ug, b_pair_re, b_pair_im, layer):
    pairs = S5_GROUPS // 2
    rows = S5_T * S5_GROUP
    lanes = S5_GROUPS * S5_STATE
    out = jax.ShapeDtypeStruct((2, lanes, S5_COLS), F32)
    return pl.pallas_call(
        _s5_state_in_kernel,
        grid=(pairs,),
        in_specs=[
            pl.BlockSpec((2, rows, S5_COLS), lambda q: (q, 0, 0)),
            pl.BlockSpec((1, 2, 1, 128, 2 * rows), lambda q: (layer, 0, q, 0, 0)),
            pl.BlockSpec((1, 2, 1, 128, 2 * rows), lambda q: (layer, 0, q, 0, 0)),
        ],
        out_specs=[pl.BlockSpec((2, 128, S5_COLS), lambda q: (0, q, 0))] * 2,
        out_shape=[out, out],
        compiler_params=_cparams("parallel"),
        name="s5_state_in",
    )(ug, b_pair_re, b_pair_im)


def _s5_scan_kernel(vre_ref, vim_ref, lre_ref, lim_ref, xre_ref, xim_ref, tre_scr, tim_scr, *, n_lat, n_ctx):
    lanes = lre_ref.shape[-1]
    for d in range(2):
        tre_scr[d] = vre_ref[d].T
        tim_scr[d] = vim_ref[d].T
    a_re = [lre_ref[0, d] for d in range(2)]
    a_im = [lim_ref[0, d] for d in range(2)]

    def step(d, row, xr, xi):
        sl = pl.ds(row, 1)
        xre_ref[d, sl, :] = xr
        xim_ref[d, sl, :] = xi
        return (a_re[d] * xr - a_im[d] * xi + tre_scr[d, sl, :],
                a_re[d] * xi + a_im[d] * xr + tim_scr[d, sl, :])

    def run(bases, n, carry):
        def body(i, carry):
            out = []
            for d in range(2):
                j = i if d == 0 else n - 1 - i
                for b in range(2):
                    xr, xi = carry[2 * d + b]
                    out.append(step(d, bases[b] + j, xr, xi))
            return tuple(out)
        return lax.fori_loop(0, n, body, carry)

    zero = jnp.zeros((1, lanes), F32)
    carry = run((2 * n_lat, 2 * n_lat + n_ctx), n_ctx, ((zero, zero),) * 4)
    run((0, n_lat), n_lat, carry)
    pad0 = 2 * n_lat + 2 * n_ctx
    for d in range(2):
        xre_ref[d, pad0:, :] = jnp.zeros((S5_COLS - pad0, lanes), F32)
        xim_ref[d, pad0:, :] = jnp.zeros((S5_COLS - pad0, lanes), F32)


def _s5_scan(vre, vim, lam_re, lam_im, layer, *, n_lat, n_ctx):
    lanes = vre.shape[1]
    tl = 512
    vspec = pl.BlockSpec((2, tl, S5_COLS), lambda j: (0, j, 0))
    xspec = pl.BlockSpec((2, S5_COLS, tl), lambda j: (0, 0, j))
    lspec = pl.BlockSpec((1, 2, 1, tl), lambda j: (layer, 0, 0, j))
    out = jax.ShapeDtypeStruct((2, S5_COLS, lanes), F32)
    return pl.pallas_call(
        functools.partial(_s5_scan_kernel, n_lat=n_lat, n_ctx=n_ctx),
        grid=(lanes // tl,),
        in_specs=[vspec, vspec, lspec, lspec],
        out_specs=[xspec, xspec],
        out_shape=[out, out],
        scratch_shapes=[pltpu.VMEM((2, S5_COLS, tl), F32), pltpu.VMEM((2, S5_COLS, tl), F32)],
        compiler_params=_cparams("parallel"),
        name="s5_scan",
    )(vre, vim, lam_re, lam_im)


def _s5_readout_kernel(ug_ref, kt_ref, cp_ref, xre_ref, xim_ref, y_ref):
    for e in range(2):
        u = ug_ref[e]
        acc = None
        for direction in range(2):
            xcat = _bf(jnp.concatenate([xre_ref[direction], xim_ref[direction]], axis=1))
            term = _mm(kt_ref[0, direction, e], u) + _mm(cp_ref[0, direction, e], xcat, _NT)
            acc = term if acc is None else acc + term
        y_ref[e] = acc


def _s5_readout(ug, ktoep, cpad, xre, xim, layer):
    pairs = S5_GROUPS // 2
    rows = S5_T * S5_GROUP
    return pl.pallas_call(
        _s5_readout_kernel,
        grid=(pairs,),
        in_specs=[
            pl.BlockSpec((2, rows, S5_COLS), lambda q: (q, 0, 0)),
            pl.BlockSpec((1, 2, 2, rows, rows), lambda q: (layer, 0, q, 0, 0)),
            pl.BlockSpec((1, 2, 2, rows, rows), lambda q: (layer, 0, q, 0, 0)),
            pl.BlockSpec((2, S5_COLS, 128), lambda q: (0, 0, q)),
            pl.BlockSpec((2, S5_COLS, 128), lambda q: (0, 0, q)),
        ],
        out_specs=pl.BlockSpec((2, rows, S5_COLS), lambda q: (q, 0, 0)),
        out_shape=jax.ShapeDtypeStruct((S5_GROUPS, rows, S5_COLS), F32),
        compiler_params=_cparams("parallel"),
        name="s5_readout",
    )(ug, ktoep, cpad, xre, xim)


def _s5_out_kernel(*refs, ctx_out):
    if ctx_out:
        yg_ref, u_ref, g_ref, uc_ref, gc_ref, d_ref, w_ref, b_ref, o_ref, oc_ref = refs
    else:
        yg_ref, u_ref, g_ref, d_ref, w_ref, b_ref, o_ref = refs
    d = d_ref[...]
    w = w_ref[...]
    bias = b_ref[...]

    def finish(yy, u, gate):
        yy = yy + d * u
        yg = jax.nn.gelu(yy)
        out = yg * jax.nn.sigmoid(_mm(_bf(yg), w) + bias)
        return out * (gate * jax.nn.sigmoid(gate))

    def y_of(t):
        return yg_ref[:, t * S5_GROUP:(t + 1) * S5_GROUP, :].reshape(S5_WIDTH, S5_CB).T

    def lat():
        for t in range(S5_T):
            o_ref[0, :, t, :] = finish(y_of(t), u_ref[0, :, t, :], g_ref[0, :, t, :])

    if not ctx_out:
        lat()
        return
    j = pl.program_id(0)
    last = pl.num_programs(0) - 1
    pl.when(j < last)(lat)

    @pl.when(j == last)
    def _():
        n_ctx = uc_ref.shape[1]
        for t in range(S5_T):
            y = y_of(t)
            for b in range(2):
                oc_ref[b, :, t, :] = finish(y[b * n_ctx:(b + 1) * n_ctx], uc_ref[b, :, t, :], gc_ref[b, :, t, :])


def _s5_relayout_out(yg, p, pc, d_skip, w_glu, b_glu, *, ctx_out):
    bsz, seq, n = p.shape
    seq_c, n_c = pc.shape[1], pc.shape[2]
    n_lat, n_ctx = seq // S5_T, seq_c // S5_T
    n_half = n_lat // S5_CB
    idx = _s5_lat_index(n_half)
    pv = p.reshape(bsz, n_lat, S5_T, n)
    lat_block = (1, S5_CB, S5_T, 512)
    in_specs = [
        pl.BlockSpec((S5_GROUPS, S5_T * S5_GROUP, S5_CB), lambda j: (0, 0, j)),
        pl.BlockSpec(lat_block, lambda j: (*idx(j), 0, COL_S5U)),
        pl.BlockSpec(lat_block, lambda j: (*idx(j), 0, COL_S5G)),
    ]
    args = [yg, pv, pv]
    out_specs = [pl.BlockSpec(lat_block, lambda j: (*idx(j), 0, 0))]
    out_shape = [jax.ShapeDtypeStruct((bsz, n_lat, S5_T, S5_WIDTH), F32)]
    if ctx_out:
        pcv = pc.reshape(bsz, n_ctx, S5_T, n_c)
        ctx_block = (bsz, n_ctx, S5_T, 512)
        in_specs += [
            pl.BlockSpec(ctx_block, lambda j: (0, 0, 0, COL_S5U)),
            pl.BlockSpec(ctx_block, lambda j: (0, 0, 0, COL_S5G)),
        ]
        args += [pcv, pcv]
        out_specs.append(pl.BlockSpec(ctx_block, lambda j: (0, 0, 0, 0)))
        out_shape.append(jax.ShapeDtypeStruct((bsz, n_ctx, S5_T, S5_WIDTH), F32))
    in_specs += [
        pl.BlockSpec((1, S5_WIDTH), lambda j: (0, 0)),
        pl.BlockSpec((S5_WIDTH, S5_WIDTH), lambda j: (0, 0)),
        pl.BlockSpec((1, S5_WIDTH), lambda j: (0, 0)),
    ]
    args += [d_skip, w_glu, b_glu]
    outs = pl.pallas_call(
        functools.partial(_s5_out_kernel, ctx_out=ctx_out),
        grid=(bsz * n_half + (1 if ctx_out else 0),),
        in_specs=in_specs,
        out_specs=out_specs,
        out_shape=out_shape,
        compiler_params=_cparams("arbitrary"),
        name="s5_relayout_out",
    )(*args)
    y = outs[0].reshape(bsz, seq, S5_WIDTH)
    yc = outs[1].reshape(bsz, seq_c, S5_WIDTH) if ctx_out else None
    return y, yc


def _s5(p, pc, ops, layer, d_skip, w_glu, b_glu, *, ctx_out):
    ktoep, b_pair_re, b_pair_im, cpad, lam_re, lam_im = ops
    ug = _s5_relayout_in(p, pc)
    vre, vim = _s5_state_in(ug, b_pair_re, b_pair_im, layer)
    xre, xim = _s5_scan(vre, vim, lam_re, lam_im, layer, n_lat=p.shape[1] // S5_T, n_ctx=pc.shape[1] // S5_T)
    yg = _s5_readout(ug, ktoep, cpad, xre, xim, layer)
    return _s5_relayout_out(yg, p, pc, d_skip, w_glu, b_glu, ctx_out=ctx_out)


def _dft_tables():
    n1, n2 = FFT_N1, FFT_N2
    n = n1 * n2
    a = jnp.arange(n1, dtype=jnp.int32)
    ph1 = (a[:, None] * a[None, :]) % n1
    ang1 = ph1.astype(F32) * (2.0 * math.pi / n1)
    c1, s1 = jnp.cos(ang1), -jnp.sin(ang1)
    h = n1 // 2
    m1 = jnp.concatenate([jnp.concatenate([c1[:, :h], -s1[:, :h]], 1),
                          jnp.concatenate([s1[:, :h], c1[:, :h]], 1)], 0)
    m1_real = jnp.concatenate([c1, s1], 0)
    m3 = m1.T / n
    b = jnp.arange(n2, dtype=jnp.int32)
    ang2 = ((b[:, None] * b[None, :]) % n2).astype(F32) * (2.0 * math.pi / n2)
    fr, fi = jnp.cos(ang2), -jnp.sin(ang2)
    f2 = jnp.concatenate([jnp.concatenate([fr, -fi], 1), jnp.concatenate([fi, fr], 1)], 0)
    angt = ((a[:, None] * b[None, :]) % n).astype(F32) * (2.0 * math.pi / n)
    tw = jnp.stack([jnp.cos(angt), -jnp.sin(angt)], axis=1)
    tw = jnp.broadcast_to(tw[..., None], (n1, 2, n2, HY_CT))
    return tuple(_split(t) for t in (m1, m1_real, m3, f2, f2.T)) + (tw,)


def _filter_features(length):
    t = jnp.linspace(0.0, 1.0, length, dtype=F32)[:, None]
    ang = (2.0 * math.pi / length) * jnp.arange(length, dtype=F32)[:, None]
    bands = jnp.linspace(1e-4, HY_BANDS - 1, HY_BANDS, dtype=F32)[None, :]
    feats = jnp.concatenate([t, jnp.cos(bands * ang), -jnp.sin(bands * ang)], axis=-1)
    feats = jnp.pad(feats, ((0, 0), (0, 128 - HY_EMB)))
    rev = jnp.roll(feats[::-1], 1, axis=0)
    return jnp.stack([feats, rev])


def _filter_kernel(f_ref, w1a_ref, w1b_ref, b1_ref, f1_ref, w2_ref, b2_ref, f2_ref, w3a_ref, w3b_ref, dl_ref,
                   o_ref):
    half = pl.program_id(1)
    tile = pl.program_id(2)
    x = f_ref[0]
    hh = x.shape[0] // 2
    h = jnp.sin(f1_ref[0] * (_mm3f(x[:hh], w1a_ref[0]) + _mm3f(x[hh:], w1b_ref[0]) + b1_ref[0]))
    h = jnp.sin(f2_ref[0] * (_mm3f(h, w2_ref[0]) + b2_ref[0]))
    hb = _bf(h)
    y = jnp.concatenate([_mm(hb, _bf(w3a_ref[0, 0])), _mm(hb, _bf(w3b_ref[0, 0]))], axis=0)
    y = y * jnp.exp(-x[:, 0:1] * dl_ref[...])
    rows = lax.broadcasted_iota(jnp.int32, y.shape, 0)
    drop = jnp.logical_and(jnp.logical_and(half == 1, tile == 0), rows == 0)
    o_ref[0] = jnp.where(drop, 0.0, y)


def _filter_taps(feats, w1a, w1b, b1, f1, w2, b2, f2, w3a, w3b, deltas, tmf):
    depth = w1a.shape[0]
    length = feats.shape[1]
    nt = length // tmf
    wide = w3a.shape[-1]
    vec = pl.BlockSpec((1, 1, 128), lambda l, s, i: (l, 0, 0))
    sq = pl.BlockSpec((1, 128, 128), lambda l, s, i: (l, 0, 0))
    w3spec = pl.BlockSpec((1, 1, 128, wide), lambda l, s, i: (l, s, 0, 0))
    return pl.pallas_call(
        _filter_kernel,
        grid=(depth, 2, nt),
        in_specs=[
            pl.BlockSpec((1, tmf, 128), lambda l, s, i: (s, i, 0)),
            sq, sq, vec, vec, sq, vec, vec, w3spec, w3spec,
            pl.BlockSpec((1, wide), lambda l, s, i: (0, 0)),
        ],
        out_specs=pl.BlockSpec((1, tmf, wide), lambda l, s, i: (l, s * nt + i, 0)),
        out_shape=jax.ShapeDtypeStruct((depth, 2 * length, wide), F32),
        compiler_params=_cparams("parallel", "parallel", "parallel"),
        name="hyena_filter",
    )(feats, w1a, w1b, b1, f1, w2, b2, f2, w3a, w3b, deltas)


def _parts(pair):
    return tuple(pair) if HY_PASSES == 3 else tuple(pair[:1])


def _mmp(m, b):
    return _mm(m[0], _bf(b)) if len(m) == 1 else _mm3(m[0], m[1], b)


def _blk_rows(n2):
    return pl.ds(pl.multiple_of(n2 * HY_PITCH, 8), 2 * FFT_N1)


def _pad_rows(n2, n):
    return pl.ds(n2, n, stride=HY_PITCH)


def _pad_copy(src, dst, n):
    for n1 in range(n):
        dst[n1 * HY_PITCH:n1 * HY_PITCH + FFT_N2, :] = src[n1 * FFT_N2:(n1 + 1) * FFT_N2, :]


def _dft_stage1(read_rows, m, a_scr):
    ct = a_scr.shape[-1]

    def body(it, carry):
        n2 = it * HY_NB
        a = _mmp(m, jnp.concatenate([read_rows(n2 + j) for j in range(HY_NB)], axis=1))
        for j in range(HY_NB):
            a_scr[_blk_rows(n2 + j), :] = a[:, j * ct:(j + 1) * ct]
        return carry

    lax.fori_loop(0, FFT_N2 // HY_NB, body, 0)


def _stage2_rows(k1):
    return (pl.ds(k1, FFT_N2, stride=HY_PITCH), pl.ds(k1 + FFT_N1, FFT_N2, stride=HY_PITCH))


def _stage2_load(a_scr, tw_ref, kb):
    rows = [_stage2_rows(kb * HY_KB + i) for i in range(HY_KB)]
    vals = []
    for i, (re, im) in enumerate(rows):
        ar, ai = a_scr[re, :], a_scr[im, :]
        tr, ti = tw_ref[kb * HY_KB + i, 0], tw_ref[kb * HY_KB + i, 1]
        vals.append(jnp.concatenate([ar * tr - ai * ti, ar * ti + ai * tr], axis=0))
    return rows, vals


def _spectrum_kernel(*refs):
    np_ = len(_parts((0, 0)))
    m_refs, f_refs, (tw_ref, t_ref, o_ref, a_scr, pad_scr) = refs[:np_], refs[np_:2 * np_], refs[2 * np_:]
    kb = pl.program_id(2)
    ct = t_ref.shape[-1]

    @pl.when(kb == 0)
    def _():
        _pad_copy(t_ref.at[0], pad_scr, FFT_N1)
        _dft_stage1(lambda n2: pad_scr[_pad_rows(n2, FFT_N1), :], [r[...] for r in m_refs], a_scr)

    _, a = _stage2_load(a_scr, tw_ref, kb)
    f2 = [r[...] for r in f_refs]
    for i in range(HY_KB):
        o_ref[0, i] = _bf(_mmp(f2, a[i])).reshape(2, FFT_N2, ct)


def _filter_spectrum(tables, taps):
    m, f2, tw = _parts(tables[1]), _parts(tables[3]), tables[5]
    depth, n, ch = taps.shape
    ct = HY_CT
    const = lambda t: pl.BlockSpec(t.shape, lambda l, c, k: (0,) * t.ndim, pipeline_mode=pl.Buffered(1))
    return pl.pallas_call(
        _spectrum_kernel,
        grid=(depth, ch // ct, FFT_N1 // HY_KB),
        in_specs=[const(t) for t in (*m, *f2, tw)] + [pl.BlockSpec((1, n, ct), lambda l, c, k: (l, 0, c))],
        out_specs=pl.BlockSpec((1, HY_KB, 2, FFT_N2, ct), lambda l, c, k: (l, k, 0, 0, c)),
        out_shape=jax.ShapeDtypeStruct((depth, FFT_N1, 2, FFT_N2, ch), BF16),
        scratch_shapes=[pltpu.VMEM((HY_PITCH * FFT_N2, ct), F32), pltpu.VMEM((HY_PITCH * FFT_N1, ct), F32)],
        compiler_params=_cparams("parallel", "parallel", "arbitrary"),
        name="hyena_spectrum",
    )(*m, *f2, tw, taps)


def _hy_pre_kernel(pv_ref, p1_ref, p2_ref, pg_ref, wv_ref, w1_ref, w2_ref, bv_ref, b1_ref, b2_ref,
                   v_ref, x1_ref, x2_ref):
    n = pv_ref.shape[1]
    rows = lax.broadcasted_iota(jnp.int32, (n, 128), 0)

    def conv(p, w_ref, b_ref):
        up = jnp.where(rows == 0, 0.0, pltpu.roll(p, 1, 0))
        dn = jnp.where(rows == n - 1, 0.0, pltpu.roll(p, n - 1, 0))
        return up * w_ref[0:1, :] + p * w_ref[1:2, :] + dn * w_ref[2:3, :] + b_ref[...]

    v_ref[0] = conv(pv_ref[0], wv_ref, bv_ref)
    x1_ref[0] = conv(p1_ref[0], w1_ref, b1_ref)
    gate = pg_ref[0]
    x2_ref[0] = conv(p2_ref[0], w2_ref, b2_ref) * (gate * jax.nn.sigmoid(gate))


def _hy_pre(p, conv_w, conv_b):
    bsz, seq, _ = p.shape
    def pspec(off):
        return pl.BlockSpec((1, seq, 128), lambda b, j: (b, 0, off + j))
    def wspec(off):
        return pl.BlockSpec((3, 128), lambda b, j: (0, off + j))
    def bspec(off):
        return pl.BlockSpec((1, 128), lambda b, j: (0, off + j))
    out = jax.ShapeDtypeStruct((bsz, seq, HY_WIDTH), F32)
    ospec = pl.BlockSpec((1, seq, 128), lambda b, j: (b, 0, j))
    return pl.pallas_call(
        _hy_pre_kernel,
        grid=(bsz, HY_WIDTH // 128),
        in_specs=[pspec(COL_HY), pspec(COL_HY + 4), pspec(COL_HY + 8), pspec(COL_HY + 12),
                  wspec(0), wspec(4), wspec(8), bspec(0), bspec(4), bspec(8)],
        out_specs=[ospec, ospec, ospec],
        out_shape=[out, out, out],
        compiler_params=_cparams("parallel", "parallel"),
        name="hyena_pre",
    )(p, p, p, p, conv_w, conv_w, conv_w, conv_b, conv_b, conv_b)


def _hy_conv_kernel(*refs):
    np_ = len(_parts((0, 0)))
    m1_refs, m3_refs, f_refs, ft_refs = (refs[i * np_:(i + 1) * np_] for i in range(4))
    tw_ref, u_ref, gate_ref, h_ref, d_ref, o_ref, a_scr, pad_scr = refs[4 * np_:]
    kb = pl.program_id(1)
    ct = u_ref.shape[-1]
    half = FFT_N1 // 2

    @pl.when(kb == 0)
    def _():
        for b in range(2):
            _pad_copy(u_ref.at[b], pad_scr.at[b], half)
        _dft_stage1(lambda n2: jnp.concatenate([pad_scr[0, _pad_rows(n2, half), :],
                                                pad_scr[1, _pad_rows(n2, half), :]], axis=0),
                    [r[...] for r in m1_refs], a_scr)

    srows, a = _stage2_load(a_scr, tw_ref, kb)
    f2, f2t = [r[...] for r in f_refs], [r[...] for r in ft_refs]
    z = [_mmp(f2, a[i]) for i in range(HY_KB)]
    z2 = []
    for i in range(HY_KB):
        zr, zi = z[i][:FFT_N2], z[i][FFT_N2:]
        hr, hi = h_ref[0, i, 0].astype(F32), h_ref[0, i, 1].astype(F32)
        z2.append(jnp.concatenate([zr * hr - zi * hi, zr * hi + zi * hr], axis=0))
    a2 = [_mmp(f2t, z2[i]) for i in range(HY_KB)]
    for i in range(HY_KB):
        cr, ci = a2[i][:FFT_N2], a2[i][FFT_N2:]
        tr, ti = tw_ref[kb * HY_KB + i, 0], tw_ref[kb * HY_KB + i, 1]
        a_scr[srows[i][0], :] = cr * tr + ci * ti
        a_scr[srows[i][1], :] = ci * tr - cr * ti

    @pl.when(kb == pl.num_programs(1) - 1)
    def _():
        m3 = [r[...] for r in m3_refs]
        d = d_ref[...]

        def body(it, carry):
            n2 = it * HY_NB
            y = _mmp(m3, jnp.concatenate([a_scr[_blk_rows(n2 + j), :] for j in range(HY_NB)], axis=1))
            for j in range(HY_NB):
                for b in range(2):
                    pad_scr[b, _pad_rows(n2 + j, half), :] = y[b * half:(b + 1) * half, j * ct:(j + 1) * ct]
            return carry

        lax.fori_loop(0, FFT_N2 // HY_NB, body, 0)
        for b in range(2):
            for n1 in range(half):
                r = slice(n1 * FFT_N2, (n1 + 1) * FFT_N2)
                conv = pad_scr[b, n1 * HY_PITCH:n1 * HY_PITCH + FFT_N2, :]
                o_ref[b, r, :] = gate_ref[b, r, :] * (conv + d * u_ref[b, r, :])


def _hy_long_conv(tables, spec, layer, order, u, gate, d):
    m1, _, m3, f2, f2t = (_parts(t) for t in tables[:5])
    bsz, seq, ch = u.shape
    ct = HY_CT
    const = lambda t: pl.BlockSpec(t.shape, lambda c, k: (0,) * t.ndim, pipeline_mode=pl.Buffered(1))
    uspec = pl.BlockSpec((bsz, seq, ct), lambda c, k: (0, 0, c))
    n_ct = ch // ct
    mats = (*m1, *m3, *f2, *f2t, tables[5])
    return pl.pallas_call(
        _hy_conv_kernel,
        grid=(n_ct, FFT_N1 // HY_KB),
        in_specs=[const(t) for t in mats] + [
            uspec, uspec,
            pl.BlockSpec((1, HY_KB, 2, FFT_N2, ct), lambda c, k: (layer, k, 0, 0, order * n_ct + c)),
            pl.BlockSpec((1, ct), lambda c, k: (0, c))],
        out_specs=uspec,
        out_shape=jax.ShapeDtypeStruct(u.shape, F32),
        scratch_shapes=[pltpu.VMEM((HY_PITCH * FFT_N2, ct), F32),
                        pltpu.VMEM((bsz, HY_PITCH * FFT_N1 // 2, ct), F32)],
        compiler_params=_cparams("parallel", "arbitrary"),
        name="hyena_conv",
    )(*mats, u, gate, spec, d.reshape(1, ch))


def _ctx_dft_tables(length):
    n = 2 * length
    k = jnp.arange(n, dtype=jnp.int32)
    ang = ((k[:, None] * k[None, :]) % n).astype(F32) * (2.0 * math.pi / n)
    fr, fi = jnp.cos(ang), -jnp.sin(ang)
    fwd = jnp.concatenate([jnp.concatenate([fr[:, :length], -fi[:, :length]], 1),
                           jnp.concatenate([fi[:, :length], fr[:, :length]], 1)], 0)
    real = jnp.concatenate([fr, fi], 0)
    inv = fwd.T / n
    return tuple(_split(t) for t in (fwd, real, inv))


def _hyc_kernel(fh_ref, fl_ref, rh_ref, rl_ref, ih_ref, il_ref, taps_ref, u_ref, g_ref, d_ref, o_ref):
    n = u_ref.shape[1]
    x = jnp.concatenate([u_ref[0], u_ref[1]], axis=0)
    z = _mm3(fh_ref[...], fl_ref[...], x)
    h = _mm3(rh_ref[...], rl_ref[...], taps_ref[0])
    m = 2 * n
    zr, zi, hr, hi = z[:m], z[m:], h[:m], h[m:]
    z2 = jnp.concatenate([zr * hr - zi * hi, zr * hi + zi * hr], axis=0)
    y = _mm3(ih_ref[...], il_ref[...], z2)
    for b in range(2):
        o_ref[b] = g_ref[b] * (y[b * n:(b + 1) * n] + d_ref[...] * u_ref[b])


def _hyc_long_conv(ctabs, taps, layer, order, u, gate, d):
    (f_hi, f_lo), (r_hi, r_lo), (i_hi, i_lo) = ctabs
    bsz, n, ch = u.shape
    full = lambda t: pl.BlockSpec(t.shape, lambda i: (0,) * t.ndim)
    uspec = pl.BlockSpec((bsz, n, ch), lambda i: (0, 0, 0))
    return pl.pallas_call(
        _hyc_kernel,
        grid=(1,),
        in_specs=[full(f_hi), full(f_lo), full(r_hi), full(r_lo), full(i_hi), full(i_lo),
                  pl.BlockSpec((1, 2 * n, ch), lambda i: (layer, 0, order)),
                  uspec, uspec, pl.BlockSpec((1, ch), lambda i: (0, 0))],
        out_specs=uspec,
        out_shape=jax.ShapeDtypeStruct((bsz, n, ch), F32),
        compiler_params=_cparams("arbitrary"),
        name="hyena_ctx",
    )(f_hi, f_lo, r_hi, r_lo, i_hi, i_lo, taps, u, gate, d.reshape(1, ch))


def _outproj_kernel(*refs, colmajor, tm):
    if colmajor:
        x_ref, ya_ref, yb_ref, yc_ref, w_ref, gp_ref, gate_ref, perm_ref, o_ref = refs
        ld = lambda r: _bf(_mm(perm_ref[...], _bf(r[0])))
    else:
        x_ref, ya_ref, yb_ref, yc_ref, w_ref, gp_ref, gate_ref, o_ref = refs
        ld = lambda r: _bf(r[0])
    na, nb = ya_ref.shape[-1], yb_ref.shape[-1]
    acc = (_mm(ld(ya_ref), w_ref[0, 0:na, :]) + _mm(ld(yb_ref), w_ref[0, na:na + nb, :])
           + _mm(ld(yc_ref), w_ref[0, na + nb:, :]))
    ms = jnp.mean(acc * acc, axis=-1, keepdims=True)
    upd = gate_ref[0] * (acc * lax.rsqrt(ms + EPS) * gp_ref[...])
    o_ref[0] = x_ref[0] + upd.reshape(x_ref.shape[1:])


def _outproj(x, ya, yb, yc, w_out, layer, g_post, gate, *, colmajor, tm):
    bsz, seq, d = x.shape
    extra_specs, extra_args = [], []
    if colmajor:
        cols = tm // GRID_W
        x_in = x.reshape(bsz, GRID_W, seq // GRID_W, d)
        x_spec = pl.BlockSpec((1, GRID_W, cols, d), lambda b, m: (b, 0, m, 0))
        r_old = jnp.arange(tm, dtype=jnp.int32)[:, None]
        r_new = jnp.arange(tm, dtype=jnp.int32)[None, :]
        perm = _bf((r_new == (r_old % cols) * GRID_W + r_old // cols).astype(F32))
        extra_specs, extra_args = [pl.BlockSpec((tm, tm), lambda b, m: (0, 0))], [perm]
    else:
        x_in = x
        x_spec = pl.BlockSpec((1, tm, d), lambda b, m: (b, m, 0))
    def yspec(t):
        return pl.BlockSpec((1, tm, t.shape[-1]), lambda b, m: (b, m, 0))
    out = pl.pallas_call(
        functools.partial(_outproj_kernel, colmajor=colmajor, tm=tm),
        grid=(bsz, seq // tm),
        in_specs=[x_spec, yspec(ya), yspec(yb), yspec(yc),
                  pl.BlockSpec((1,) + w_out.shape[1:], lambda b, m: (layer, 0, 0)),
                  pl.BlockSpec((1, d), lambda b, m: (0, 0)),
                  pl.BlockSpec((1, 1, d), lambda b, m: (b, 0, 0))] + extra_specs,
        out_specs=x_spec,
        out_shape=jax.ShapeDtypeStruct(x_in.shape, F32),
        compiler_params=_cparams("parallel", "parallel"),
        name="outproj",
    )(x_in, ya, yb, yc, w_out, g_post, gate, *extra_args)
    return out.reshape(bsz, seq, d)


def kernel(x, c, ctx, c_ctx, w_mod, b_mod, g_pre, g_post, w_in, w_out, gla_w_gate, gla_b_gate, gla_norm, s5_lam_re, s5_lam_im, s5_log_step, s5_b_re, s5_b_im, s5_c_re, s5_c_im, s5_d, s5_w_glu, s5_b_glu, hy_conv_w, hy_conv_b, hy_w1, hy_b1, hy_f1, hy_w2, hy_b2, hy_f2, hy_w3, hy_d):
    bsz, seq, d = x.shape
    seq_c = ctx.shape[1]
    depth = w_in.shape[0]
    assert (bsz, d, seq // GRID_W) == (2, D_MODEL, GRID_W) and seq == FFT_N1 * FFT_N2 // 2

    cvec = jnp.concatenate([c, c_ctx[None], jnp.zeros((8 - bsz - 1, d), F32)], axis=0)
    mod = _modulation(cvec, w_mod, b_mod)
    shift, scale, gate = mod[..., :d], mod[..., d:2 * d], mod[..., 2 * d:]

    w_t = _bf(jnp.swapaxes(w_in, 1, 2))
    w_lr = w_t[:, N_QKV:N_QKV + N_LR]
    w_lr3 = jnp.concatenate([w_lr, w_lr, w_lr, jnp.zeros_like(w_lr)], axis=1)
    w_out_b = _bf(w_out)
    w_glu_b = _bf(s5_w_glu)

    wg = gla_w_gate.reshape(depth, 2, 16, GLA_HEADS, GLA_DK).transpose(0, 1, 3, 2, 4)
    wg_pad = jnp.stack([jnp.pad(wg[:, 0], ((0, 0), (0, 0), (0, 112), (0, 0))),
                        jnp.pad(wg[:, 1], ((0, 0), (0, 0), (16, 96), (0, 0)))], axis=1)
    wg_hi, wg_lo = _split(wg_pad[..., :N_LR, :])
    wg3 = jnp.concatenate([wg_hi, wg_hi, wg_lo, jnp.zeros_like(wg_hi)], axis=-2)
    bg = gla_b_gate.reshape(depth, 2, GLA_HEADS, 1, GLA_DK)

    tables = _dft_tables()
    deltas = jnp.abs(jnp.linspace(HY_MIN_DECAY, HY_MAX_DECAY, HY_WIDTH, dtype=F32))
    deltas2 = jnp.tile(deltas, 2).reshape(1, 2 * HY_WIDTH)
    fh = HY_FFN
    lo_k = lambda w: jnp.pad(w, ((0, 0), (0, 128 - w.shape[1]), (0, fh)))
    hi_k = lambda w: jnp.pad(w, ((0, 0), (0, 128 - w.shape[1]), (fh, 0)))
    dup_v = lambda v: jnp.concatenate([v, v], axis=1).reshape(depth, 1, 128)
    w2_bd = jnp.pad(hy_w2, ((0, 0), (0, fh), (0, fh))) + jnp.pad(hy_w2, ((0, 0), (fh, 0), (fh, 0)))
    w3 = hy_w3.reshape(depth, HY_FFN, 2, 2, HY_WIDTH).transpose(0, 3, 1, 2, 4)
    w3 = w3.reshape(depth, 2, HY_FFN, 2 * HY_WIDTH)
    w3a = jnp.pad(w3, ((0, 0), (0, 0), (0, fh), (0, 0)))
    w3b = jnp.pad(w3, ((0, 0), (0, 0), (fh, 0), (0, 0)))
    filt_w = (lo_k(hy_w1), hi_k(hy_w1), dup_v(hy_b1), dup_v(hy_f1), w2_bd, dup_v(hy_b2), dup_v(hy_f2),
              w3a, w3b, deltas2)
    taps = _filter_taps(_filter_features(seq), *filt_w, tmf=512)
    spec = _filter_spectrum(tables, taps)
    taps_c = _filter_taps(_filter_features(seq_c), *filt_w, tmf=seq_c)
    ctabs = _ctx_dft_tables(seq_c)

    s5_ops = _s5_operators(s5_lam_re, s5_lam_im, s5_log_step, s5_b_re, s5_b_im, s5_c_re, s5_c_im)

    xc = ctx
    for l in range(depth):
        last = l == depth - 1
        ctx_out = not last
        colmajor = l % 2 == 1
        pre = g_pre[l][None, :] * (1.0 + scale[l])
        sc_l, sh_l = pre[:bsz, None, :], shift[l, :bsz, None, :]
        sc_c = jnp.broadcast_to(pre[bsz][None, None, :], (bsz, 1, d))
        sh_c = jnp.broadcast_to(shift[l, bsz][None, None, :], (bsz, 1, d))
        qkv, p, lr = _inproj(x, sc_l, sh_l, w_t, w_lr3, l, n=N_MAIN, colmajor=colmajor, tm=1024, tn=512)
        qkvc, pc, lrc = _inproj(xc, sc_c, sh_c, w_t, w_lr3, l, n=N_MAIN if ctx_out else N_STATE,
                                colmajor=False, tm=seq_c, tn=512)

        y_gla, yc_gla = _gla(qkv, p, lr, qkvc, pc, lrc, wg3[l], bg[l], gla_norm[l][None, :],
                             ctx_out=ctx_out)

        y_s5, yc_s5 = _s5(p, pc, s5_ops, l, s5_d[l][None, :], w_glu_b[l], s5_b_glu[l][None, :], ctx_out=ctx_out)

        v, x1, x2 = _hy_pre(p, hy_conv_w[l], hy_conv_b[l][None, :])
        z = _hy_long_conv(tables, spec, l, 0, v, x1, hy_d[l, 0])
        y_hy = _hy_long_conv(tables, spec, l, 1, z, x2, hy_d[l, 1])

        x_new = _outproj(x, y_gla, y_s5, y_hy, w_out_b, l, g_post[l][None, :], gate[l, :bsz, None, :],
                         colmajor=colmajor, tm=512)
        if ctx_out:
            vc, x1c, x2c = _hy_pre(pc, hy_conv_w[l], hy_conv_b[l][None, :])
            zc = _hyc_long_conv(ctabs, taps_c, l, 0, vc, x1c, hy_d[l, 0])
            yc_hy = _hyc_long_conv(ctabs, taps_c, l, 1, zc, x2c, hy_d[l, 1])
            gate_c = jnp.broadcast_to(gate[l, bsz][None, None, :], (bsz, 1, d))
            xc = _outproj(xc, yc_gla, yc_s5, yc_hy, w_out_b, l, g_post[l][None, :], gate_c,
                          colmajor=False, tm=seq_c)
        x = x_new
    return x
```
